```python
import jax
import jax.numpy as jnp
from jax import lax
import numpy as np

D_MODEL = 2048
BATCH = 4
SEQ = 2048
DEPTH = 2

GRID_W = 64
CTX_LEN = 256
Q_BLOCK = 128
SCAN_CHUNK = 16
ROPE_THETA = 10000.0
NORM_EPS = 1e-6
A_HEADS = 8
A_DK = 128
A_DV = 128
B_HEADS = 8
B_Q_LORA = 512
B_KV_LORA = 256
B_NOPE = 128
B_ROPE = 64
B_DV = 128
C_HEADS = 8
C_KV_HEADS = 2
C_DH = 128
D_HEADS = 4
D_DK = 128
D_DV = 256
D_GATE_RANK = 16
GLA_TAU = 16.0
N_EXPERTS = 16
N_GROUPS = 4
TOP_K = 2
EXPERT_DIM = 512
N_AB = (DEPTH + 1) // 2
N_CD = DEPTH // 2

kernel_name = 'hybrid_hgrn2_mla_gqa_gla_moe_dit'


def rms_norm(x, g):
    xf = x.astype(jnp.float32)
    y = xf * lax.rsqrt(jnp.mean(xf * xf, axis=-1, keepdims=True) + NORM_EPS)
    return (y * g.astype(jnp.float32)).astype(x.dtype)


def modulate(h, shift, scale):
    return h * (1 + scale) + shift


def split_cols(p, sizes):
    cuts = np.cumsum(np.array(sizes))[:-1].tolist()
    return jnp.split(p, cuts, axis=-1)


def heads(x, h):
    b, t, w = x.shape
    return x.reshape(b, t, h, w // h).transpose(0, 2, 1, 3)


def merge(x):
    b, h, t, d = x.shape
    return x.transpose(0, 2, 1, 3).reshape(b, t, h * d)


def axial_angles(t_len, d_rope):
    rows = t_len // GRID_W
    quarter = d_rope // 4
    freqs = ROPE_THETA ** (-jnp.arange(quarter, dtype=jnp.float32) / quarter)
    row = jnp.repeat(jnp.arange(rows, dtype=jnp.float32), GRID_W)
    col = jnp.tile(jnp.arange(GRID_W, dtype=jnp.float32), rows)
    return jnp.concatenate([row[:, None] * freqs, col[:, None] * freqs], axis=-1)


def apply_rope(x, ang):
    half = x.shape[-1] // 2
    x1, x2 = x[..., :half], x[..., half:]
    cos, sin = jnp.cos(ang), jnp.sin(ang)
    return jnp.concatenate([x1 * cos - x2 * sin, x2 * cos + x1 * sin], axis=-1).astype(x.dtype)


def block_attention(q, k, v, scale):
    b, hk, g, t, dq = q.shape
    dv = v.shape[-1]
    nb = t // Q_BLOCK
    qb = jnp.moveaxis(q.reshape(b, hk, g, nb, Q_BLOCK, dq), 3, 0)

    def one_block(qi):
        s = jnp.einsum('bhgtd,bhsd->bhgts', qi, k, preferred_element_type=jnp.float32) * scale
        p = jax.nn.softmax(s, axis=-1)
        return jnp.einsum('bhgts,bhsv->bhgtv', p.astype(v.dtype), v)

    o = lax.map(one_block, qb)
    return jnp.moveaxis(o, 0, 3).reshape(b, hk * g, t, dv)


def chunked_gated_scan(q, k, v, g, s0):
    b, h, t, dk = q.shape
    n = t // SCAN_CHUNK

    def to_chunks(a):
        return a.astype(jnp.float32).reshape(b, h, n, SCAN_CHUNK, a.shape[-1]).transpose(2, 0, 1, 3, 4)

    mask = jnp.tril(jnp.ones((SCAN_CHUNK, SCAN_CHUNK), dtype=bool))[:, :, None]

    def step(state, inp):
        qc, kc, vc, gc = inp
        cum = jnp.cumsum(gc, axis=-2)
        rel = jnp.where(mask, cum[..., :, None, :] - cum[..., None, :, :], -jnp.inf)
        scores = jnp.einsum('bhtd,bhsd,bhtsd->bhts', qc, kc, jnp.exp(rel))
        o = jnp.einsum('bhts,bhsv->bhtv', scores, vc) + jnp.einsum('bhtd,bhdv->bhtv', qc * jnp.exp(cum), state)
        last = cum[..., -1:, :]
        state = state * jnp.exp(last)[..., 0, :, None] + jnp.einsum('bhsd,bhsv->bhdv', kc * jnp.exp(last - cum), vc)
        return state, o

    s_final, o = lax.scan(step, s0.astype(jnp.float32), (to_chunks(q), to_chunks(k), to_chunks(v), to_chunks(g)))
    o = o.transpose(1, 2, 0, 3, 4).reshape(b, h, t, v.shape[-1])
    return o.astype(v.dtype), s_final


def scan_ctx_then_lat(c_args, x_args, reverse):
    if reverse:
        c_args = tuple(jnp.flip(a, axis=2) for a in c_args)
        x_args = tuple(jnp.flip(a, axis=2) for a in x_args)
    b, h, _, dk = c_args[0].shape
    dv = c_args[2].shape[-1]
    oc, s_ctx = chunked_gated_scan(*c_args, jnp.zeros((b, h, dk, dv), jnp.float32))
    ox, _ = chunked_gated_scan(*x_args, s_ctx)
    if reverse:
        oc, ox = jnp.flip(oc, axis=2), jnp.flip(ox, axis=2)
    return oc, ox


def mixer_ab(hc, hx, w_in, w_out, lb_f, lb_b, hg_norm_g, mq_norm_g, w_uq, mkv_norm_g, w_ukv, ang, need_ctx):
    sizes = (A_HEADS * A_DK,) * 3 + (A_HEADS * A_DV,) * 2 + (B_Q_LORA, B_KV_LORA, B_ROPE)
    pc = split_cols(hc @ w_in, sizes)
    px = split_cols(hx @ w_in, sizes)

    def hgrn_args(p):
        q = heads(jax.nn.silu(p[0]), A_HEADS) * A_DK ** -0.5
        v = heads(p[3], A_HEADS)

        def direction(f_logit, lb):
            f = lb + (1 - lb) * jax.nn.sigmoid(f_logit.astype(jnp.float32))
            return heads(1 - f, A_HEADS), heads(jnp.log(f), A_HEADS)

        kf, gf = direction(p[1], lb_f)
        kb, gb = direction(p[2], lb_b)
        return (q, kf, v, gf), (q, kb, v, gb)

    c_fwd, c_bwd = hgrn_args(pc)
    x_fwd, x_bwd = hgrn_args(px)
    oc_f, ox_f = scan_ctx_then_lat(c_fwd, x_fwd, False)
    oc_b, ox_b = scan_ctx_then_lat(c_bwd, x_bwd, True)

    def hgrn_out(o, gate):
        return merge(rms_norm(o, hg_norm_g) * jax.nn.sigmoid(heads(gate, A_HEADS)))

    def mla_qkv(p, rotate):
        q = heads(rms_norm(p[5], mq_norm_g) @ w_uq, B_HEADS)
        kv = heads(rms_norm(p[6], mkv_norm_g) @ w_ukv, B_HEADS)
        q_nope, q_rope = q[..., :B_NOPE], q[..., B_NOPE:]
        k_nope, v = kv[..., :B_NOPE], kv[..., B_NOPE:]
        k_rope = p[7][:, None]
        if rotate:
            q_rope, k_rope = apply_rope(q_rope, ang), apply_rope(k_rope, ang)
        k = jnp.concatenate([k_nope, jnp.broadcast_to(k_rope, k_nope.shape[:-1] + (B_ROPE,))], axis=-1)
        return jnp.concatenate([q_nope, q_rope], axis=-1), k, v

    qc, kc, vc = mla_qkv(pc, False)
    qx, kx, vx = mla_qkv(px, True)
    scale = (B_NOPE + B_ROPE) ** -0.5
    ox_mla = block_attention(qx[:, :, None], jnp.concatenate([kc, kx], axis=2), jnp.concatenate([vc, vx], axis=2), scale)
    out_x = jnp.concatenate([hgrn_out(ox_f + ox_b, px[4]), merge(ox_mla)], axis=-1) @ w_out
    if not need_ctx:
        return None, out_x
    oc_mla = block_attention(qc[:, :, None], kc, vc, scale)
    out_c = jnp.concatenate([hgrn_out(oc_f + oc_b, pc[4]), merge(oc_mla)], axis=-1) @ w_out
    return out_c, out_x


def mixer_cd(hc, hx, w_in, w_out, q_norm_g, k_norm_g, w_a2, b_a, gla_norm_g, ang, need_ctx):
    sizes = (C_HEADS * C_DH, C_KV_HEADS * C_DH, C_KV_HEADS * C_DH, D_HEADS * D_DK, D_HEADS * D_DK,
             D_HEADS * D_DV, D_HEADS * D_DV, D_GATE_RANK, D_GATE_RANK)
    pc = split_cols(hc @ w_in, sizes)
    px = split_cols(hx @ w_in, sizes)

    def gqa_qkv(p, rotate):
        q = rms_norm(heads(p[0], C_HEADS), q_norm_g)
        k = rms_norm(heads(p[1], C_KV_HEADS), k_norm_g)
        v = heads(p[2], C_KV_HEADS)
        if rotate:
            q, k = apply_rope(q, ang), apply_rope(k, ang)
        b, _, t, _ = q.shape
        return q.reshape(b, C_KV_HEADS, C_HEADS // C_KV_HEADS, t, C_DH), k, v

    qc, kc, vc = gqa_qkv(pc, False)
    qx, kx, vx = gqa_qkv(px, True)
    scale = C_DH ** -0.5
    ox_att = block_attention(qx, jnp.concatenate([kc, kx], axis=2), jnp.concatenate([vc, vx], axis=2), scale)

    def gla_args(p):
        q = heads(p[3], D_HEADS) * D_DK ** -0.5
        k = heads(p[4], D_HEADS)
        v = heads(p[5], D_HEADS)

        def direction(a, d):
            return heads(jax.nn.log_sigmoid((a @ w_a2[d] + b_a[d]).astype(jnp.float32)) / GLA_TAU, D_HEADS)

        return (q, k, v, direction(p[7], 0)), (q, k, v, direction(p[8], 1))

    c_fwd, c_bwd = gla_args(pc)
    x_fwd, x_bwd = gla_args(px)
    oc_f, ox_f = scan_ctx_then_lat(c_fwd, x_fwd, False)
    oc_b, ox_b = scan_ctx_then_lat(c_bwd, x_bwd, True)

    def gla_out(o, gate):
        return merge(rms_norm(o, gla_norm_g) * jax.nn.silu(heads(gate, D_HEADS)))

    out_x = jnp.concatenate([merge(ox_att), gla_out(ox_f + ox_b, px[6])], axis=-1) @ w_out
    if not need_ctx:
        return None, out_x
    oc_att = block_attention(qc, kc, vc, scale)
    out_c = jnp.concatenate([merge(oc_att), gla_out(oc_f + oc_b, pc[6])], axis=-1) @ w_out
    return out_c, out_x


def moe(h, router_w, router_b, w1, w3, w2):
    shape = h.shape
    hf = h.reshape(-1, shape[-1])
    scores = jax.nn.sigmoid((hf @ router_w).astype(jnp.float32))
    sel = scores + router_b.astype(jnp.float32)
    per_group = N_EXPERTS // N_GROUPS
    grp_top, _ = lax.top_k(sel.reshape(-1, N_GROUPS, per_group), 2)
    g_idx = jnp.argmax(grp_top.sum(-1), axis=-1)
    g_mask = jnp.repeat(jnp.arange(N_GROUPS)[None, :] == g_idx[:, None], per_group, axis=1)
    _, e_idx = lax.top_k(jnp.where(g_mask, sel, -jnp.inf), TOP_K)
    w = jnp.take_along_axis(scores, e_idx, axis=-1)
    w = w / jnp.sum(w, axis=-1, keepdims=True)
    combine = jnp.sum(jax.nn.one_hot(e_idx, N_EXPERTS, dtype=jnp.float32) * w[..., None], axis=1)
    hidden = jax.nn.silu(jnp.einsum('nd,edf->nef', hf, w1)) * jnp.einsum('nd,edf->nef', hf, w3)
    y = jnp.einsum('nef,efd->nd', hidden * combine[..., None].astype(hidden.dtype), w2)
    return y.reshape(shape)


def setup_inputs(seed: int = 0) -> dict:
    key = jax.random.key(seed)
    ks = jax.random.split(key, 29)
    D = D_MODEL
    ab_in = 3 * A_HEADS * A_DK + 2 * A_HEADS * A_DV + B_Q_LORA + B_KV_LORA + B_ROPE
    ab_mix = A_HEADS * A_DV + B_HEADS * B_DV
    cd_in = (C_HEADS + 2 * C_KV_HEADS) * C_DH + 2 * D_HEADS * D_DK + 2 * D_HEADS * D_DV + 2 * D_GATE_RANK
    cd_mix = C_HEADS * C_DH + D_HEADS * D_DV
    mod_init = 0.5

    def nrm(i, shape, scale):
        return jax.random.normal(ks[i], shape, jnp.float32) * scale

    def gain(i, shape):
        return 1.0 + nrm(i, shape, 0.02)

    return {
        'x': nrm(0, (BATCH, SEQ, D), 1.0),
        'c': nrm(1, (BATCH, D), 1.0),
        'ctx': nrm(2, (BATCH, CTX_LEN, D), 1.0),
        'c_ctx': nrm(3, (D,), 1.0),
        'mod_w': nrm(4, (DEPTH, D, 6 * D), mod_init * D ** -0.5),
        'mod_b': nrm(5, (DEPTH, 6 * D), 0.02),
        'norm_attn_g': gain(6, (DEPTH, D)),
        'norm_ffn_g': gain(7, (DEPTH, D)),
        'final_norm_g': gain(8, (D,)),
        'ab_w_in': nrm(9, (N_AB, D, ab_in), D ** -0.5),
        'ab_w_out': nrm(10, (N_AB, ab_mix, D), ab_mix ** -0.5),
        'hgrn_lb_logits': nrm(11, (2, DEPTH + 1, A_HEADS * A_DK), 0.1),
        'hgrn_norm_g': gain(12, (N_AB, A_DV)),
        'mla_q_norm_g': gain(13, (N_AB, B_Q_LORA)),
        'mla_w_uq': nrm(14, (N_AB, B_Q_LORA, B_HEADS * (B_NOPE + B_ROPE)), B_Q_LORA ** -0.5),
        'mla_kv_norm_g': gain(15, (N_AB, B_KV_LORA)),
        'mla_w_ukv': nrm(16, (N_AB, B_KV_LORA, B_HEADS * (B_NOPE + B_DV)), B_KV_LORA ** -0.5),
        'cd_w_in': nrm(17, (N_CD, D, cd_in), D ** -0.5),
        'cd_w_out': nrm(18, (N_CD, cd_mix, D), cd_mix ** -0.5),
        'gqa_q_norm_g': gain(19, (N_CD, C_DH)),
        'gqa_k_norm_g': gain(20, (N_CD, C_DH)),
        'gla_w_a2': nrm(21, (N_CD, 2, D_GATE_RANK, D_HEADS * D_DK), D_GATE_RANK ** -0.5),
        'gla_b_a': nrm(22, (N_CD, 2, D_HEADS * D_DK), 0.1),
        'gla_norm_g': gain(23, (N_CD, D_DV)),
        'router_w': nrm(24, (D, N_EXPERTS), D ** -0.5),
        'router_b': nrm(25, (N_EXPERTS,), 0.01),
        'moe_w1': nrm(26, (DEPTH, N_EXPERTS, D, EXPERT_DIM), D ** -0.5),
        'moe_w3': nrm(27, (DEPTH, N_EXPERTS, D, EXPERT_DIM), D ** -0.5),
        'moe_w2': nrm(28, (DEPTH, N_EXPERTS, EXPERT_DIM, D), EXPERT_DIM ** -0.5),
    }


def reference(x, c, ctx, c_ctx, mod_w, mod_b, norm_attn_g, norm_ffn_g, final_norm_g,
              ab_w_in, ab_w_out, hgrn_lb_logits, hgrn_norm_g, mla_q_norm_g, mla_w_uq, mla_kv_norm_g, mla_w_ukv,
              cd_w_in, cd_w_out, gqa_q_norm_g, gqa_k_norm_g, gla_w_a2, gla_b_a, gla_norm_g,
              router_w, router_b, moe_w1, moe_w3, moe_w2):
    t_len = x.shape[1]
    ang_b = axial_angles(t_len, B_ROPE)
    ang_c = axial_angles(t_len, C_DH)
    lb = jnp.cumsum(jax.nn.softmax(hgrn_lb_logits.astype(jnp.float32), axis=1), axis=1)
    xc = ctx
    for l in range(DEPTH):
        need_ctx = l < DEPTH - 1
        j = l // 2
        mx = (jax.nn.silu(c) @ mod_w[l] + mod_b[l])[:, None, :]
        mc = jax.nn.silu(c_ctx) @ mod_w[l] + mod_b[l]
        sx1, ax1, gx1, sx2, ax2, gx2 = jnp.split(mx, 6, axis=-1)
        sc1, ac1, gc1, sc2, ac2, gc2 = jnp.split(mc, 6, axis=-1)
        hx = modulate(rms_norm(x, norm_attn_g[l]), sx1, ax1)
        hc = modulate(rms_norm(xc, norm_attn_g[l]), sc1, ac1)
        if l % 2 == 0:
            oc, ox = mixer_ab(hc, hx, ab_w_in[j], ab_w_out[j], lb[0, l], lb[1, l], hgrn_norm_g[j],
                              mla_q_norm_g[j], mla_w_uq[j], mla_kv_norm_g[j], mla_w_ukv[j], ang_b, need_ctx)
        else:
            oc, ox = mixer_cd(hc, hx, cd_w_in[j], cd_w_out[j], gqa_q_norm_g[j], gqa_k_norm_g[j],
                              gla_w_a2[j], gla_b_a[j], gla_norm_g[j], ang_c, need_ctx)
        x = x + gx1 * ox
        hx = modulate(rms_norm(x, norm_ffn_g[l]), sx2, ax2)
        x = x + gx2 * moe(hx, router_w, router_b, moe_w1[l], moe_w3[l], moe_w2[l])
        if need_ctx:
            xc = xc + gc1 * oc
            hc = modulate(rms_norm(xc, norm_ffn_g[l]), sc2, ac2)
            xc = xc + gc2 * moe(hc, router_w, router_b, moe_w1[l], moe_w3[l], moe_w2[l])
    return rms_norm(x, final_norm_g)
```

```python
import functools

import jax
import jax.numpy as jnp
from jax import lax
from jax.experimental import pallas as pl
from jax.experimental.pallas import tpu as pltpu

F32 = jnp.float32
BF16 = jnp.bfloat16
HI = lax.Precision.HIGHEST

GRID_W = 64
ROPE_THETA = 10000.0
NORM_EPS = 1e-6
A_HEADS, A_DK, A_DV = 8, 128, 128
B_HEADS, B_Q_LORA, B_KV_LORA, B_NOPE, B_ROPE, B_DV = 8, 512, 256, 128, 64, 128
C_HEADS, C_KV_HEADS, C_DH = 8, 2, 128
D_HEADS, D_DK, D_DV, D_GATE_RANK = 4, 128, 256, 16
GLA_TAU = 16.0
N_EXPERTS, N_GROUPS = 16, 4

TQ = 256
SCAN_C = 64
TE = 256
TROUTE = 512
TCOMB = 256
VMEM_MIB = 56


def _cp(sem):
    return pltpu.CompilerParams(dimension_semantics=sem, vmem_limit_bytes=VMEM_MIB * 1024 * 1024)


def _rms(x, g):
    return x * lax.rsqrt(jnp.mean(x * x, axis=-1, keepdims=True) + NORM_EPS) * g


def _rope(x, cos, sin):
    half = x.shape[-1] // 2
    swapped = jnp.concatenate([x[:, half:], x[:, :half]], axis=-1)
    return x * cos + swapped * sin


def _dot_nt(a, b):
    return lax.dot_general(a, b, (((1,), (1,)), ((), ())), preferred_element_type=F32)


def _dot_tn(a, b):
    return lax.dot_general(a, b, (((0,), (0,)), ((), ())), preferred_element_type=F32)


def _modvec_kernel(c_ref, w_ref, b_ref, o_ref):
    c = c_ref[...]
    a = c * jax.nn.sigmoid(c)
    o_ref[...] = jnp.dot(a, w_ref[...], preferred_element_type=F32, precision=HI) + b_ref[...]


def _modvec(cvec, mod_w, mod_b):
    n_layers, d, n6 = mod_w.shape
    tn = min(1024, n6)
    return pl.pallas_call(
        _modvec_kernel,
        grid=(n_layers, n6 // tn),
        in_specs=[pl.BlockSpec((8, d), lambda l, j: (0, 0)),
                  pl.BlockSpec((None, d, tn), lambda l, j: (l, 0, j)),
                  pl.BlockSpec((None, 1, tn), lambda l, j: (l, 0, j))],
        out_specs=pl.BlockSpec((None, 8, tn), lambda l, j: (l, 0, j)),
        out_shape=jax.ShapeDtypeStruct((n_layers, 8, n6), F32),
        compiler_params=_cp(("parallel", "parallel")),
        name="modvec",
    )(cvec, mod_w, mod_b.reshape(n_layers, 1, n6))


class _Rows:
    def __init__(self, batch, seq, ctx_len, tm):
        assert seq % tm == 0 and (batch * ctx_len) % tm == 0
        self.tm = tm
        self.batch = batch
        self.per_batch = seq // tm
        self.n_lat = batch * seq // tm
        self.n_ctx = batch * ctx_len // tm
        self.n_all = self.n_lat + self.n_ctx

    def mod_row(self, i):
        return jnp.where(i < self.n_lat, i // self.per_batch, self.batch)


def _mod_spec(rows, layer, chunk, d):
    return pl.BlockSpec((None, None, None, 1, d), lambda i, *_: (layer, rows.mod_row(i), chunk, 0, 0))


def _norm_mod_kernel(xl_ref, xc_ref, g_ref, sh_ref, sc_ref, o_ref, *, n_lat):
    i = pl.program_id(0)

    def body(x_ref):
        y = _rms(x_ref[...], g_ref[...])
        o_ref[...] = (y * (1.0 + sc_ref[...]) + sh_ref[...]).astype(o_ref.dtype)

    @pl.when(i < n_lat)
    def _():
        body(xl_ref)

    @pl.when(i >= n_lat)
    def _():
        body(xc_ref)


def _norm_mod(x_lat, x_ctx, ctx_block0, g, mods5, layer, rows):
    d = x_lat.shape[-1]
    tm = rows.tm
    nl = rows.n_lat
    return pl.pallas_call(
        functools.partial(_norm_mod_kernel, n_lat=nl),
        grid=(rows.n_all,),
        in_specs=[pl.BlockSpec((tm, d), lambda i: (jnp.minimum(i, nl - 1), 0)),
                  pl.BlockSpec((tm, d), lambda i: (ctx_block0 + jnp.maximum(i - nl, 0), 0)),
                  pl.BlockSpec((1, d), lambda i: (0, 0)),
                  _mod_spec(rows, layer, 0, d),
                  _mod_spec(rows, layer, 1, d)],
        out_specs=pl.BlockSpec((tm, d), lambda i: (i, 0)),
        out_shape=jax.ShapeDtypeStruct((rows.n_all * tm, d), BF16),
        compiler_params=_cp(("parallel",)),
        name="norm_mod",
    )(x_lat, x_ctx, g.reshape(1, d), mods5, mods5)


def _mm_kernel(a_ref, w_ref, o_ref):
    o_ref[...] = jnp.dot(a_ref[...], w_ref[...].astype(BF16), preferred_element_type=F32).astype(o_ref.dtype)


def _matmul(a, w3, layer, n_cols, tn, tm):
    m, k = a.shape
    return pl.pallas_call(
        _mm_kernel,
        grid=(m // tm, n_cols // tn),
        in_specs=[pl.BlockSpec((tm, k), lambda i, j: (i, 0)),
                  pl.BlockSpec((None, k, tn), lambda i, j: (layer, 0, j))],
        out_specs=pl.BlockSpec((tm, tn), lambda i, j: (i, j)),
        out_shape=jax.ShapeDtypeStruct((m, n_cols), BF16),
        compiler_params=_cp(("parallel", "arbitrary")),
        name="in_proj",
    )(a, w3)


def _out_proj_kernel(ma_ref, mb_ref, wa_ref, wb_ref, xl_ref, xc_ref, g_ref, o_ref, *, n_lat):
    i = pl.program_id(0)
    acc = jnp.dot(ma_ref[...], wa_ref[...].astype(BF16), preferred_element_type=F32)
    acc += jnp.dot(mb_ref[...], wb_ref[...].astype(BF16), preferred_element_type=F32)
    upd = g_ref[...] * acc

    @pl.when(i < n_lat)
    def _():
        o_ref[...] = xl_ref[...] + upd

    @pl.when(i >= n_lat)
    def _():
        o_ref[...] = xc_ref[...] + upd


def _out_proj(mix_a, mix_b, w_out3, widx, x_lat, x_ctx, ctx_block0, mods5, layer, rows, n_tiles):
    d = x_lat.shape[-1]
    ka, kb = mix_a.shape[-1], mix_b.shape[-1]
    assert ka == kb
    tm = rows.tm
    tn = min(512, d)
    nl = rows.n_lat
    return pl.pallas_call(
        functools.partial(_out_proj_kernel, n_lat=nl),
        grid=(n_tiles, d // tn),
        in_specs=[pl.BlockSpec((tm, ka), lambda i, j: (i, 0)),
                  pl.BlockSpec((tm, kb), lambda i, j: (i, 0)),
                  pl.BlockSpec((None, ka, tn), lambda i, j: (2 * widx, 0, j)),
                  pl.BlockSpec((None, kb, tn), lambda i, j: (2 * widx + 1, 0, j)),
                  pl.BlockSpec((tm, tn), lambda i, j: (jnp.minimum(i, nl - 1), j)),
                  pl.BlockSpec((tm, tn), lambda i, j: (ctx_block0 + jnp.maximum(i - nl, 0), j)),
                  pl.BlockSpec((None, None, None, 1, tn), lambda i, j: (layer, rows.mod_row(i), 2, 0, j))],
        out_specs=pl.BlockSpec((tm, tn), lambda i, j: (i, j)),
        out_shape=jax.ShapeDtypeStruct((n_tiles * tm, d), F32),
        compiler_params=_cp(("parallel", "arbitrary")),
        name="out_proj",
    )(mix_a, mix_b, w_out3, w_out3, x_lat, x_ctx, mods5)


def _tri(c, upper):
    r = lax.broadcasted_iota(jnp.int32, (c, c), 0)
    s = lax.broadcasted_iota(jnp.int32, (c, c), 1)
    return (s >= r) if upper else (r >= s)


def _chunk_step(q, k, v, g, st, tri_f32, mask, mid, last):
    cum = jnp.dot(tri_f32, g, preferred_element_type=F32, precision=HI)
    m = cum[mid:mid + 1, :]
    tot = cum[last:last + 1, :]
    qe = (q * jnp.exp(cum - m)).astype(BF16)
    ke = (k * jnp.exp(m - cum)).astype(BF16)
    a = jnp.where(mask, _dot_nt(qe, ke), 0.0).astype(BF16)
    stp = st * jnp.exp(m)
    o = jnp.dot(a, v, preferred_element_type=F32) + _dot_nt(qe, stp.astype(BF16))
    new = (stp + _dot_tn(v, ke)) * jnp.exp(tot - m)
    return o, new


def _scan_segments(segments, prep_f, prep_b, of_ref, ob_ref, dk, dv):
    c = SCAN_C
    low, up = _tri(c, False), _tri(c, True)
    low_f, up_f = low.astype(F32), up.astype(F32)
    carry = (jnp.zeros((dv, dk), F32), jnp.zeros((dv, dk), F32))
    for n, off, seg in segments:
        def body(i, carry, n=n, off=off, seg=seg):
            sf, sb = carry
            rf = pl.multiple_of(i * c, c)
            rb = pl.multiple_of((n - 1 - i) * c, c)
            q, k, v, g = prep_f(seg, rf)
            o, sf = _chunk_step(q, k, v, g, sf, low_f, low, c // 2 - 1, c - 1)
            of_ref[pl.ds(off + rf, c), :] = o
            q, k, v, g = prep_b(seg, rb)
            o, sb = _chunk_step(q, k, v, g, sb, up_f, up, c // 2, 0)
            ob_ref[pl.ds(off + rb, c), :] = o
            return sf, sb

        carry = lax.fori_loop(0, n, body, carry)


def _hgrn_kernel(ql, qc, f1l, f1c, f2l, f2c, vl, vc, gl, gc, lbf, lbb, ng, o_ref, of_s, ob_s, *, ctx_len, seq):
    rt = pl.program_id(2)
    n_ctx_tiles = ctx_len // TQ

    @pl.when(rt == 0)
    def _scan():
        refs = {0: (qc, f1c, f2c, vc), 1: (ql, f1l, f2l, vl)}
        scale = A_DK ** -0.5

        def prep(seg, r, fi, lb_ref):
            x = refs[seg][0][pl.ds(r, SCAN_C), :].astype(F32)
            q = x * jax.nn.sigmoid(x) * scale
            v = refs[seg][3][pl.ds(r, SCAN_C), :]
            lb = lb_ref[...]
            f = lb + (1.0 - lb) * jax.nn.sigmoid(refs[seg][fi][pl.ds(r, SCAN_C), :].astype(F32))
            return q, 1.0 - f, v, jnp.log(f)

        _scan_segments(
            [(ctx_len // SCAN_C, 0, 0), (seq // SCAN_C, ctx_len, 1)],
            lambda seg, r: prep(seg, r, 1, lbf),
            lambda seg, r: prep(seg, r, 2, lbb),
            of_s, ob_s, A_DK, A_DV)

    r0 = pl.multiple_of(rt * TQ, TQ)
    o = of_s[pl.ds(r0, TQ), :] + ob_s[pl.ds(r0, TQ), :]
    y = _rms(o, ng[...])

    @pl.when(rt < n_ctx_tiles)
    def _():
        gate = gc[pl.ds(r0, TQ), :].astype(F32)
        o_ref[...] = (y * jax.nn.sigmoid(gate)).astype(o_ref.dtype)

    @pl.when(rt >= n_ctx_tiles)
    def _():
        gate = gl[pl.ds(pl.multiple_of(r0 - ctx_len, TQ), TQ), :].astype(F32)
        o_ref[...] = (y * jax.nn.sigmoid(gate)).astype(o_ref.dtype)


def _out_row_block(batch, seq, ctx_len):
    nct = ctx_len // TQ
    nlt = seq // TQ

    def f(b, rt):
        return jnp.where(rt < nct, batch * nlt + b * nct + rt, b * nlt + rt - nct)

    return f


def _hgrn(p, lb_f, lb_b, norm_g, batch, seq, ctx_len):
    h, dk, dv = A_HEADS, A_DK, A_DV
    nct, nlt = ctx_len // TQ, seq // TQ
    ctx_blk0 = batch * seq // ctx_len
    row_block = _out_row_block(batch, seq, ctx_len)
    in_specs = []
    for kcol in range(5):
        in_specs.append(pl.BlockSpec((seq, dk), lambda b, hh, rt, kcol=kcol: (b, kcol * h + hh)))
        in_specs.append(pl.BlockSpec((ctx_len, dk), lambda b, hh, rt, kcol=kcol: (ctx_blk0 + b, kcol * h + hh)))
    vec = pl.BlockSpec((1, dk), lambda b, hh, rt: (0, hh))
    in_specs += [vec, vec, pl.BlockSpec((1, dv), lambda b, hh, rt: (0, 0))]
    return pl.pallas_call(
        functools.partial(_hgrn_kernel, ctx_len=ctx_len, seq=seq),
        grid=(batch, h, nct + nlt),
        in_specs=in_specs,
        out_specs=pl.BlockSpec((TQ, dv), lambda b, hh, rt: (row_block(b, rt), hh)),
        out_shape=jax.ShapeDtypeStruct((batch * (seq + ctx_len), h * dv), BF16),
        scratch_shapes=[pltpu.VMEM((seq + ctx_len, dv), F32), pltpu.VMEM((seq + ctx_len, dv), F32)],
        compiler_params=_cp(("parallel", "parallel", "arbitrary")),
        name="hgrn_scan",
    )(*([p] * 10), lb_f.reshape(1, h * dk), lb_b.reshape(1, h * dk), norm_g.reshape(1, dv))


def _gla_kernel(ql, qc, kl, kc, vl, vc, gl, al, ac, wa, ba, ng, o_ref, of_s, ob_s, *, ctx_len, seq):
    rt = pl.program_id(2)

    @pl.when(rt == 0)
    def _scan():
        refs = {0: (qc, kc, vc, ac), 1: (ql, kl, vl, al)}
        scale = D_DK ** -0.5
        r16 = D_GATE_RANK

        def prep(seg, r, d):
            q = refs[seg][0][pl.ds(r, SCAN_C), :].astype(F32) * scale
            k = refs[seg][1][pl.ds(r, SCAN_C), :].astype(F32)
            v = refs[seg][2][pl.ds(r, SCAN_C), :]
            a = refs[seg][3][pl.ds(r, SCAN_C), :].astype(F32)[:, d * r16:(d + 1) * r16]
            z = jnp.dot(a, wa[d], preferred_element_type=F32, precision=HI) + ba[d]
            g = (jnp.minimum(z, 0.0) - jnp.log(1.0 + jnp.exp(-jnp.abs(z)))) * (1.0 / GLA_TAU)
            return q, k, v, g

        _scan_segments(
            [(ctx_len // SCAN_C, 0, 0), (seq // SCAN_C, ctx_len, 1)],
            lambda seg, r: prep(seg, r, 0),
            lambda seg, r: prep(seg, r, 1),
            of_s, ob_s, D_DK, D_DV)

    r0 = pl.multiple_of(rt * TQ, TQ)
    o = of_s[pl.ds(ctx_len + r0, TQ), :] + ob_s[pl.ds(ctx_len + r0, TQ), :]
    gate = gl[pl.ds(r0, TQ), :].astype(F32)
    o_ref[...] = (_rms(o, ng[...]) * gate * jax.nn.sigmoid(gate)).astype(o_ref.dtype)


def _gla(p, ga, w_a2, b_a, norm_g, batch, seq, ctx_len):
    h, dk, dv = D_HEADS, D_DK, D_DV
    nlt = seq // TQ
    ctx_blk0 = batch * seq // ctx_len
    q0 = (C_HEADS + 2 * C_KV_HEADS) * C_DH // dk
    k0 = q0 + h
    v0 = (k0 + h) * dk // dv
    g0 = v0 + h

    def pair(width, blk0):
        return [pl.BlockSpec((seq, width), lambda b, hh, rt: (b, blk0 + hh)),
                pl.BlockSpec((ctx_len, width), lambda b, hh, rt: (ctx_blk0 + b, blk0 + hh))]

    in_specs = pair(dk, q0) + pair(dk, k0) + pair(dv, v0)
    in_specs += [pl.BlockSpec((seq, dv), lambda b, hh, rt: (b, g0 + hh)),
                 pl.BlockSpec((seq, 2 * D_GATE_RANK), lambda b, hh, rt: (b, 0)),
                 pl.BlockSpec((ctx_len, 2 * D_GATE_RANK), lambda b, hh, rt: (ctx_blk0 + b, 0)),
                 pl.BlockSpec((2, D_GATE_RANK, dk), lambda b, hh, rt: (0, 0, hh)),
                 pl.BlockSpec((2, 1, dk), lambda b, hh, rt: (0, 0, hh)),
                 pl.BlockSpec((1, dv), lambda b, hh, rt: (0, 0))]
    return pl.pallas_call(
        functools.partial(_gla_kernel, ctx_len=ctx_len, seq=seq),
        grid=(batch, h, nlt),
        in_specs=in_specs,
        out_specs=pl.BlockSpec((TQ, dv), lambda b, hh, rt: (b * nlt + rt, hh)),
        out_shape=jax.ShapeDtypeStruct((batch * seq, h * dv), BF16),
        scratch_shapes=[pltpu.VMEM((seq + ctx_len, dv), F32), pltpu.VMEM((seq + ctx_len, dv), F32)],
        compiler_params=_cp(("parallel", "parallel", "arbitrary")),
        name="gla_scan",
    )(p, p, p, p, p, p, p, ga, ga, w_a2, b_a.reshape(2, 1, h * dk), norm_g.reshape(1, dv))


def _softmax_pv(s, v):
    m = jnp.max(s, axis=-1, keepdims=True)
    p = jnp.exp(s - m)
    l = jnp.sum(p, axis=-1, keepdims=True)
    return jnp.dot(p.astype(BF16), v, preferred_element_type=F32) / l


def _mla_kernel(ql_ref, kvl_ref, kvc_ref, krl_ref, krc_ref, wq_ref, wkv_ref, gq_ref, gkv_ref,
                cosq_ref, sinq_ref, cosk_ref, sink_ref, o_ref, kn_s, kr_s, v_s, *, ctx_len):
    qt = pl.program_id(2)
    n_ctx_tiles = ctx_len // TQ
    scale = (B_NOPE + B_ROPE) ** -0.5

    @pl.when(qt == 0)
    def _prep():
        wkv = wkv_ref[...].astype(BF16)

        def up(x_ref):
            xn = _rms(x_ref[...].astype(F32), gkv_ref[...])
            return jnp.dot(xn.astype(BF16), wkv, preferred_element_type=F32)

        kvc = up(kvc_ref)
        kn_s[0:ctx_len, :] = kvc[:, :B_NOPE].astype(BF16)
        v_s[0:ctx_len, :] = kvc[:, B_NOPE:].astype(BF16)
        kvl = up(kvl_ref)
        kn_s[ctx_len:, :] = kvl[:, :B_NOPE].astype(BF16)
        v_s[ctx_len:, :] = kvl[:, B_NOPE:].astype(BF16)
        kr_s[0:ctx_len, :] = krc_ref[...]
        kr_s[ctx_len:, :] = _rope(krl_ref[...].astype(F32), cosk_ref[...], sink_ref[...]).astype(BF16)

    xn = _rms(ql_ref[...].astype(F32), gq_ref[...])
    q = jnp.dot(xn.astype(BF16), wq_ref[...].astype(BF16), preferred_element_type=F32) * scale
    qn = q[:, :B_NOPE].astype(BF16)
    qr = q[:, B_NOPE:]

    @pl.when(qt < n_ctx_tiles)
    def _():
        s = _dot_nt(qn, kn_s[0:ctx_len, :]) + _dot_nt(qr.astype(BF16), kr_s[0:ctx_len, :])
        o_ref[...] = _softmax_pv(s, v_s[0:ctx_len, :]).astype(o_ref.dtype)

    @pl.when(qt >= n_ctx_tiles)
    def _():
        qrr = _rope(qr, cosq_ref[...], sinq_ref[...]).astype(BF16)
        s = _dot_nt(qn, kn_s[...]) + _dot_nt(qrr, kr_s[...])
        o_ref[...] = _softmax_pv(s, v_s[...]).astype(o_ref.dtype)


def _mla(p, kr, w_uq_h, w_ukv_h, gq, gkv, cos, sin, batch, seq, ctx_len):
    h = B_HEADS
    nct, nlt = ctx_len // TQ, seq // TQ
    ctx_blk0 = batch * seq // ctx_len
    row_block = _out_row_block(batch, seq, ctx_len)
    ql_blk = 5 * A_HEADS * A_DK // B_Q_LORA
    kv_blk = (5 * A_HEADS * A_DK + B_Q_LORA) // B_KV_LORA
    dq = B_NOPE + B_ROPE
    dkv = B_NOPE + B_DV
    s_all = seq + ctx_len
    in_specs = [
        pl.BlockSpec((TQ, B_Q_LORA), lambda b, hh, qt: (row_block(b, qt), ql_blk)),
        pl.BlockSpec((seq, B_KV_LORA), lambda b, hh, qt: (b, kv_blk)),
        pl.BlockSpec((ctx_len, B_KV_LORA), lambda b, hh, qt: (ctx_blk0 + b, kv_blk)),
        pl.BlockSpec((seq, B_ROPE), lambda b, hh, qt: (b, 0)),
        pl.BlockSpec((ctx_len, B_ROPE), lambda b, hh, qt: (ctx_blk0 + b, 0)),
        pl.BlockSpec((None, B_Q_LORA, dq), lambda b, hh, qt: (hh, 0, 0)),
        pl.BlockSpec((None, B_KV_LORA, dkv), lambda b, hh, qt: (hh, 0, 0)),
        pl.BlockSpec((1, B_Q_LORA), lambda b, hh, qt: (0, 0)),
        pl.BlockSpec((1, B_KV_LORA), lambda b, hh, qt: (0, 0)),
        pl.BlockSpec((TQ, B_ROPE), lambda b, hh, qt: (jnp.maximum(qt - nct, 0), 0)),
        pl.BlockSpec((TQ, B_ROPE), lambda b, hh, qt: (jnp.maximum(qt - nct, 0), 0)),
        pl.BlockSpec((seq, B_ROPE), lambda b, hh, qt: (0, 0)),
        pl.BlockSpec((seq, B_ROPE), lambda b, hh, qt: (0, 0)),
    ]
    return pl.pallas_call(
        functools.partial(_mla_kernel, ctx_len=ctx_len),
        grid=(batch, h, nct + nlt),
        in_specs=in_specs,
        out_specs=pl.BlockSpec((TQ, B_DV), lambda b, hh, qt: (row_block(b, qt), hh)),
        out_shape=jax.ShapeDtypeStruct((batch * s_all, h * B_DV), BF16),
        scratch_shapes=[pltpu.VMEM((s_all, B_NOPE), BF16), pltpu.VMEM((s_all, B_ROPE), BF16),
                        pltpu.VMEM((s_all, B_DV), BF16)],
        compiler_params=_cp(("parallel", "parallel", "arbitrary")),
        name="mla_attn",
    )(p, p, p, kr, kr, w_uq_h, w_ukv_h, gq.reshape(1, -1), gkv.reshape(1, -1), cos, sin, cos, sin)


def _gqa_kernel(q_ref, kl_ref, kc_ref, vl_ref, vc_ref, gq_ref, gk_ref, cosq_ref, sinq_ref, cosk_ref, sink_ref,
                o_ref, k_s, v_s, *, ctx_len):
    qi = pl.program_id(2)
    scale = C_DH ** -0.5

    @pl.when(qi == 0)
    def _prep():
        k_s[0:ctx_len, :] = _rms(kc_ref[...].astype(F32), gk_ref[...]).astype(BF16)
        kl = _rms(kl_ref[...].astype(F32), gk_ref[...])
        k_s[ctx_len:, :] = _rope(kl, cosk_ref[...], sink_ref[...]).astype(BF16)
        v_s[0:ctx_len, :] = vc_ref[...]
        v_s[ctx_len:, :] = vl_ref[...]

    q = _rope(_rms(q_ref[...].astype(F32), gq_ref[...]), cosq_ref[...], sinq_ref[...]) * scale
    s = _dot_nt(q.astype(BF16), k_s[...])
    o_ref[...] = _softmax_pv(s, v_s[...]).astype(o_ref.dtype)


def _gqa(p, gq, gk, cos, sin, batch, seq, ctx_len):
    kvh, grp, dh = C_KV_HEADS, C_HEADS // C_KV_HEADS, C_DH
    nlt = seq // TQ
    ctx_blk0 = batch * seq // ctx_len
    k0 = C_HEADS
    v0 = C_HEADS + C_KV_HEADS
    s_all = seq + ctx_len
    in_specs = [
        pl.BlockSpec((TQ, dh), lambda b, kh, qi: (b * nlt + qi % nlt, kh * grp + qi // nlt)),
        pl.BlockSpec((seq, dh), lambda b, kh, qi: (b, k0 + kh)),
        pl.BlockSpec((ctx_len, dh), lambda b, kh, qi: (ctx_blk0 + b, k0 + kh)),
        pl.BlockSpec((seq, dh), lambda b, kh, qi: (b, v0 + kh)),
        pl.BlockSpec((ctx_len, dh), lambda b, kh, qi: (ctx_blk0 + b, v0 + kh)),
        pl.BlockSpec((1, dh), lambda b, kh, qi: (0, 0)),
        pl.BlockSpec((1, dh), lambda b, kh, qi: (0, 0)),
        pl.BlockSpec((TQ, dh), lambda b, kh, qi: (qi % nlt, 0)),
        pl.BlockSpec((TQ, dh), lambda b, kh, qi: (qi % nlt, 0)),
        pl.BlockSpec((seq, dh), lambda b, kh, qi: (0, 0)),
        pl.BlockSpec((seq, dh), lambda b, kh, qi: (0, 0)),
    ]
    return pl.pallas_call(
        functools.partial(_gqa_kernel, ctx_len=ctx_len),
        grid=(batch, kvh, grp * nlt),
        in_specs=in_specs,
        out_specs=pl.BlockSpec((TQ, dh), lambda b, kh, qi: (b * nlt + qi % nlt, kh * grp + qi // nlt)),
        out_shape=jax.ShapeDtypeStruct((batch * seq, C_HEADS * dh), BF16),
        scratch_shapes=[pltpu.VMEM((s_all, dh), BF16), pltpu.VMEM((s_all, dh), BF16)],
        compiler_params=_cp(("parallel", "parallel", "arbitrary")),
        name="gqa_attn",
    )(p, p, p, p, p, gq.reshape(1, dh), gk.reshape(1, dh), cos, sin, cos, sin)


def _router_kernel(x_ref, g_ref, sh_ref, sc_ref, rw_ref, rb_ref, h_ref, ri_ref, rf_ref, cnt_ref, base_s):
    i = pl.program_id(0)
    tm = x_ref.shape[0]
    ne = N_EXPERTS
    per = ne // N_GROUPS

    @pl.when(i == 0)
    def _():
        base_s[...] = jnp.zeros_like(base_s)

    h = _rms(x_ref[...], g_ref[...]) * (1.0 + sc_ref[...]) + sh_ref[...]
    h_ref[...] = h.astype(h_ref.dtype)
    logits = jnp.dot(h, rw_ref[...], preferred_element_type=F32, precision=HI)
    scores = jax.nn.sigmoid(logits)
    sel = scores + rb_ref[...]
    lane = lax.broadcasted_iota(jnp.int32, (tm, ne), 1).astype(F32)
    neg = -jnp.inf
    big = float(ne)

    def top2(vals):
        m1 = jnp.max(vals, axis=1, keepdims=True)
        i1 = jnp.min(jnp.where(vals == m1, lane, big), axis=1, keepdims=True)
        rest = jnp.where(lane == i1, neg, vals)
        m2 = jnp.max(rest, axis=1, keepdims=True)
        i2 = jnp.min(jnp.where(rest == m2, lane, big), axis=1, keepdims=True)
        return m1 + m2, i1, i2

    best, e1, e2 = None, None, None
    for grp in range(N_GROUPS):
        in_grp = jnp.logical_and(lane >= float(grp * per), lane < float((grp + 1) * per))
        gsum, i1, i2 = top2(jnp.where(in_grp, sel, neg))
        if grp == 0:
            best, e1, e2 = gsum, i1, i2
        else:
            better = gsum > best
            best = jnp.where(better, gsum, best)
            e1 = jnp.where(better, i1, e1)
            e2 = jnp.where(better, i2, e2)

    hot1 = lane == e1
    hot2 = lane == e2
    w1 = jnp.sum(jnp.where(hot1, scores, 0.0), axis=1, keepdims=True)
    w2 = jnp.sum(jnp.where(hot2, scores, 0.0), axis=1, keepdims=True)
    wsum = w1 + w2
    assign = jnp.logical_or(hot1, hot2)
    r = lax.broadcasted_iota(jnp.int32, (tm, tm), 0)
    c = lax.broadcasted_iota(jnp.int32, (tm, tm), 1)
    before = (c < r).astype(BF16)
    excl = jnp.dot(before, assign.astype(BF16), preferred_element_type=F32) + base_s[...]
    rank1 = jnp.sum(jnp.where(hot1, excl, 0.0), axis=1, keepdims=True)
    rank2 = jnp.sum(jnp.where(hot2, excl, 0.0), axis=1, keepdims=True)
    base_s[...] = base_s[...] + jnp.sum(assign.astype(F32), axis=0, keepdims=True)

    l128 = lax.broadcasted_iota(jnp.int32, (tm, 128), 1)
    ri = jnp.where(l128 == 0, e1, jnp.where(l128 == 1, e2, jnp.where(l128 == 2, rank1, jnp.where(l128 == 3, rank2, 0.0))))
    ri_ref[...] = ri.astype(jnp.int32)
    rf_ref[...] = jnp.where(l128 == 0, w1 / wsum, jnp.where(l128 == 1, w2 / wsum, 0.0))
    cnt_ref[...] = jnp.broadcast_to(base_s[...], cnt_ref.shape)


def _router(x_all, g, mods5, layer, router_w, router_b, rows, n_tiles):
    d = x_all.shape[-1]
    tm = rows.tm
    n = n_tiles * tm
    ne = N_EXPERTS
    return pl.pallas_call(
        _router_kernel,
        grid=(n_tiles,),
        in_specs=[pl.BlockSpec((tm, d), lambda i: (i, 0)),
                  pl.BlockSpec((1, d), lambda i: (0, 0)),
                  _mod_spec(rows, layer, 3, d),
                  _mod_spec(rows, layer, 4, d),
                  pl.BlockSpec((d, ne), lambda i: (0, 0)),
                  pl.BlockSpec((1, ne), lambda i: (0, 0))],
        out_specs=[pl.BlockSpec((tm, d), lambda i: (i, 0)),
                   pl.BlockSpec((tm, 128), lambda i: (i, 0)),
                   pl.BlockSpec((tm, 128), lambda i: (i, 0)),
                   pl.BlockSpec((8, ne), lambda i: (0, 0))],
        out_shape=[jax.ShapeDtypeStruct((n, d), F32),
                   jax.ShapeDtypeStruct((n, 128), jnp.int32),
                   jax.ShapeDtypeStruct((n, 128), F32),
                   jax.ShapeDtypeStruct((8, ne), F32)],
        scratch_shapes=[pltpu.VMEM((1, ne), F32)],
        compiler_params=_cp(("arbitrary",)),
        name="moe_router",
    )(x_all, g.reshape(1, d), mods5, mods5, router_w, router_b.reshape(1, ne))


def _dispatch_kernel(dest_ref, h_hbm, xs_in, xs_hbm, sem, *, tm, n_tok):
    del xs_in
    i = pl.program_id(0)

    def row_copy(tok, dst):
        return pltpu.make_async_copy(h_hbm.at[pl.ds(tok, 1)], xs_hbm.at[pl.ds(dst, 1)], sem)

    def start(t, carry):
        tok = i * tm + t
        row_copy(tok, dest_ref[tok]).start()
        row_copy(tok, dest_ref[n_tok + tok]).start()
        return carry

    lax.fori_loop(0, tm, start, 0)

    def wait(t, carry):
        row_copy(0, 0).wait()
        row_copy(0, 0).wait()
        return carry

    lax.fori_loop(0, tm, wait, 0)


def _dispatch(dest, h, n_tok, p_max):
    d = h.shape[-1]
    tm = TCOMB
    zeros = jnp.zeros((p_max, d), h.dtype)
    grid_spec = pltpu.PrefetchScalarGridSpec(
        num_scalar_prefetch=1,
        grid=(n_tok // tm,),
        in_specs=[pl.BlockSpec(memory_space=pl.ANY), pl.BlockSpec(memory_space=pl.ANY)],
        out_specs=pl.BlockSpec(memory_space=pl.ANY),
        scratch_shapes=[pltpu.SemaphoreType.DMA(())],
    )
    return pl.pallas_call(
        functools.partial(_dispatch_kernel, tm=tm, n_tok=n_tok),
        grid_spec=grid_spec,
        out_shape=jax.ShapeDtypeStruct((p_max, d), h.dtype),
        input_output_aliases={2: 0},
        compiler_params=_cp(("arbitrary",)),
        name="moe_dispatch",
    )(dest, h, zeros)


def _expert_kernel(te_ref, nu_ref, x_ref, w1_ref, w3_ref, w2_ref, y_ref, w1_s, w3_s, w2_s):
    r = pl.program_id(0)
    active = r < nu_ref[0]
    changed = jnp.logical_or(r == 0, te_ref[r] != te_ref[jnp.maximum(r - 1, 0)])

    @pl.when(jnp.logical_and(active, changed))
    def _():
        w1_s[...] = w1_ref[...].astype(BF16)
        w3_s[...] = w3_ref[...].astype(BF16)
        w2_s[...] = w2_ref[...].astype(BF16)

    @pl.when(active)
    def _():
        x = x_ref[...].astype(BF16)
        a = jnp.dot(x, w1_s[...], preferred_element_type=F32)
        b = jnp.dot(x, w3_s[...], preferred_element_type=F32)
        hid = (a * jax.nn.sigmoid(a) * b).astype(BF16)
        y_ref[...] = jnp.dot(hid, w2_s[...], preferred_element_type=F32)

    @pl.when(jnp.logical_not(active))
    def _():
        y_ref[...] = jnp.zeros_like(y_ref)


def _experts(tile_expert, n_used, xs, w1, w3, w2, layer):
    p_max, d = xs.shape
    f = w1.shape[-1]
    grid_spec = pltpu.PrefetchScalarGridSpec(
        num_scalar_prefetch=2,
        grid=(p_max // TE,),
        in_specs=[pl.BlockSpec((TE, d), lambda r, te, nu: (r, 0)),
                  pl.BlockSpec((None, None, d, f), lambda r, te, nu: (layer, te[r], 0, 0)),
                  pl.BlockSpec((None, None, d, f), lambda r, te, nu: (layer, te[r], 0, 0)),
                  pl.BlockSpec((None, None, f, d), lambda r, te, nu: (layer, te[r], 0, 0))],
        out_specs=pl.BlockSpec((TE, d), lambda r, te, nu: (r, 0)),
        scratch_shapes=[pltpu.VMEM((d, f), BF16), pltpu.VMEM((d, f), BF16), pltpu.VMEM((f, d), BF16)],
    )
    return pl.pallas_call(
        _expert_kernel,
        grid_spec=grid_spec,
        out_shape=jax.ShapeDtypeStruct((p_max, d), F32),
        compiler_params=_cp(("arbitrary",)),
        name="moe_experts",
    )(tile_expert, n_used, xs, w1, w3, w2)


def _combine_kernel(dest_ref, x_ref, rf_ref, g_ref, fg_ref, ys_hbm, o_ref, buf, sem, *, tm, n_tok, final):
    i = pl.program_id(0)

    def row_copy(src, slot, t):
        return pltpu.make_async_copy(ys_hbm.at[pl.ds(src, 1)], buf.at[slot, pl.ds(t, 1)], sem)

    def start(t, carry):
        tok = i * tm + t
        row_copy(dest_ref[tok], 0, t).start()
        row_copy(dest_ref[n_tok + tok], 1, t).start()
        return carry

    lax.fori_loop(0, tm, start, 0)

    def wait(t, carry):
        row_copy(0, 0, 0).wait()
        row_copy(0, 1, 0).wait()
        return carry

    lax.fori_loop(0, tm, wait, 0)
    w = rf_ref[...]
    y = w[:, 0:1] * buf[0] + w[:, 1:2] * buf[1]
    x2 = x_ref[...] + g_ref[...] * y
    if final:
        x2 = _rms(x2, fg_ref[...])
    o_ref[...] = x2


def _combine(dest, x_all, rf, mods5, layer, final_g, ys, rows, n_tiles, n_tok_total, final):
    d = x_all.shape[-1]
    tm = rows.tm
    grid_spec = pltpu.PrefetchScalarGridSpec(
        num_scalar_prefetch=1,
        grid=(n_tiles,),
        in_specs=[pl.BlockSpec((tm, d), lambda i, dr: (i, 0)),
                  pl.BlockSpec((tm, 128), lambda i, dr: (i, 0)),
                  _mod_spec(rows, layer, 5, d),
                  pl.BlockSpec((1, d), lambda i, dr: (0, 0)),
                  pl.BlockSpec(memory_space=pl.ANY)],
        out_specs=pl.BlockSpec((tm, d), lambda i, dr: (i, 0)),
        scratch_shapes=[pltpu.VMEM((2, tm, d), F32), pltpu.SemaphoreType.DMA(())],
    )
    return pl.pallas_call(
        functools.partial(_combine_kernel, tm=tm, n_tok=n_tok_total, final=final),
        grid_spec=grid_spec,
        out_shape=jax.ShapeDtypeStruct((n_tiles * tm, d), F32),
        compiler_params=_cp(("arbitrary",)),
        name="moe_combine",
    )(dest, x_all, rf, mods5, final_g.reshape(1, d), ys)


def _moe(x_all, n_tok, norm_g, mods5, layer, router_w, router_b, w1, w3, w2, final_g, final, batch, seq, ctx_len):
    rows_r = _Rows(batch, seq, ctx_len, TROUTE)
    rows_c = _Rows(batch, seq, ctx_len, TCOMB)
    h, ri, rf, cnt = _router(x_all, norm_g, mods5, layer, router_w, router_b, rows_r, n_tok // TROUTE)
    counts = cnt[0].astype(jnp.int32)
    padded = ((counts + TE - 1) // TE) * TE
    ends = jnp.cumsum(padded)
    starts = ends - padded
    e1, e2, r1, r2 = ri[:, 0], ri[:, 1], ri[:, 2], ri[:, 3]
    dest = jnp.concatenate([starts[e1] + r1, starts[e2] + r2]).astype(jnp.int32)
    p_max = 2 * n_tok + N_EXPERTS * TE
    n_tiles = p_max // TE
    n_used = (ends[-1] // TE).astype(jnp.int32)
    tile_start = jnp.arange(n_tiles, dtype=jnp.int32) * TE
    tile_expert = jnp.sum((tile_start[:, None] >= ends[None, :]).astype(jnp.int32), axis=1)
    last_expert = jnp.sum((jnp.maximum(ends[-1] - 1, 0) >= ends).astype(jnp.int32))
    tile_expert = jnp.minimum(jnp.where(tile_start < ends[-1], tile_expert, last_expert), N_EXPERTS - 1).astype(jnp.int32)
    xs = _dispatch(dest, h, n_tok, p_max)
    ys = _experts(tile_expert, n_used.reshape(1), xs, w1, w3, w2, layer)
    return _combine(dest, x_all, rf, mods5, layer, final_g, ys, rows_c, n_tok // TCOMB, n_tok, final)


def _rope_tables(t_len, d_rope):
    rows = t_len // GRID_W
    quarter = d_rope // 4
    freqs = ROPE_THETA ** (-jnp.arange(quarter, dtype=F32) / quarter)
    row = jnp.repeat(jnp.arange(rows, dtype=F32), GRID_W)
    col = jnp.tile(jnp.arange(GRID_W, dtype=F32), rows)
    ang = jnp.concatenate([row[:, None] * freqs, col[:, None] * freqs], axis=-1)
    cos, sin = jnp.cos(ang), jnp.sin(ang)
    return jnp.concatenate([cos, cos], axis=-1), jnp.concatenate([-sin, sin], axis=-1)


def kernel(x, c, ctx, c_ctx, mod_w, mod_b, norm_attn_g, norm_ffn_g, final_norm_g, ab_w_in, ab_w_out, hgrn_lb_logits, hgrn_norm_g, mla_q_norm_g, mla_w_uq, mla_kv_norm_g, mla_w_ukv, cd_w_in, cd_w_out, gqa_q_norm_g, gqa_k_norm_g, gla_w_a2, gla_b_a, gla_norm_g, router_w, router_b, moe_w1, moe_w3, moe_w2):
    batch, seq, d = x.shape
    ctx_len = ctx.shape[1]
    n_lat, n_ctx = batch * seq, batch * ctx_len
    assert ctx_len % TQ == 0 and seq % TQ == 0 and seq % ctx_len == 0 and batch < 8
    tm = min(1024, seq, n_ctx)
    rows = _Rows(batch, seq, ctx_len, tm)

    cvec = jnp.concatenate([c, c_ctx[None, :], jnp.zeros((8 - batch - 1, d), F32)], axis=0)
    mods = _modvec(cvec, mod_w, mod_b)
    mods5 = mods.reshape(mods.shape[0], 8, 6, 1, d)

    cos_b, sin_b = _rope_tables(seq, B_ROPE)
    cos_c, sin_c = _rope_tables(seq, C_DH)
    lb = jnp.cumsum(jax.nn.softmax(hgrn_lb_logits.astype(F32), axis=1), axis=1)

    x_lat = x.reshape(n_lat, d)
    x_ctx = ctx.reshape(n_ctx, d)

    h0 = _norm_mod(x_lat, x_ctx, 0, norm_attn_g[0], mods5, 0, rows)
    ab_main = 5 * A_HEADS * A_DK + B_Q_LORA + B_KV_LORA
    p0 = _matmul(h0, ab_w_in, 0, ab_main, 256, tm)
    kr0 = _matmul(h0, ab_w_in[:, :, ab_main:], 0, B_ROPE, B_ROPE, tm)
    mix_a = _hgrn(p0, lb[0, 0], lb[1, 0], hgrn_norm_g[0], batch, seq, ctx_len)
    dq = B_NOPE + B_ROPE
    dkv = B_NOPE + B_DV
    w_uq_h = mla_w_uq[0].reshape(B_Q_LORA, B_HEADS, dq).transpose(1, 0, 2)
    w_ukv_h = mla_w_ukv[0].reshape(B_KV_LORA, B_HEADS, dkv).transpose(1, 0, 2)
    mix_b = _mla(p0, kr0, w_uq_h, w_ukv_h, mla_q_norm_g[0], mla_kv_norm_g[0], cos_b, sin_b, batch, seq, ctx_len)
    half = mix_a.shape[-1]
    w_out0 = ab_w_out.reshape(ab_w_out.shape[0] * 2, half, d)
    x1 = _out_proj(mix_a, mix_b, w_out0, 0, x_lat, x_ctx, 0, mods5, 0, rows, rows.n_all)
    x2 = _moe(x1, n_lat + n_ctx, norm_ffn_g[0], mods5, 0, router_w, router_b, moe_w1, moe_w3, moe_w2,
              final_norm_g, False, batch, seq, ctx_len)

    h1 = _norm_mod(x2, x2, rows.n_lat, norm_attn_g[1], mods5, 1, rows)
    cd_main = (C_HEADS + 2 * C_KV_HEADS) * C_DH + 2 * D_HEADS * D_DK + 2 * D_HEADS * D_DV
    p1 = _matmul(h1, cd_w_in, 0, cd_main, 512, tm)
    ga1 = _matmul(h1, cd_w_in[:, :, cd_main:], 0, 2 * D_GATE_RANK, 2 * D_GATE_RANK, tm)
    mix_c = _gqa(p1, gqa_q_norm_g[0], gqa_k_norm_g[0], cos_c, sin_c, batch, seq, ctx_len)
    mix_d = _gla(p1, ga1, gla_w_a2[0], gla_b_a[0], gla_norm_g[0], batch, seq, ctx_len)
    w_out1 = cd_w_out.reshape(cd_w_out.shape[0] * 2, mix_c.shape[-1], d)
    x3 = _out_proj(mix_c, mix_d, w_out1, 0, x2, x2, rows.n_lat, mods5, 1, rows, rows.n_lat)
    out = _moe(x3, n_lat, norm_ffn_g[1], mods5, 1, router_w, router_b, moe_w1, moe_w3, moe_w2,
               final_norm_g, True, batch, seq, ctx_len)
    return out.reshape(batch, seq, d)
```

```python
import functools

import jax
import jax.numpy as jnp
from jax import lax
from jax.experimental import pallas as pl
from jax.experimental.pallas import tpu as pltpu

F32 = jnp.float32
BF16 = jnp.bfloat16
HI = lax.Precision.HIGHEST

GRID_W = 64
ROPE_THETA = 10000.0
NORM_EPS = 1e-6
A_HEADS, A_DK, A_DV = 8, 128, 128
B_HEADS, B_Q_LORA, B_KV_LORA, B_NOPE, B_ROPE, B_DV = 8, 512, 256, 128, 64, 128
C_HEADS, C_KV_HEADS, C_DH = 8, 2, 128
D_HEADS, D_DK, D_DV, D_GATE_RANK = 4, 128, 256, 16
GLA_TAU = 16.0
N_EXPERTS, N_GROUPS = 16, 4

TQ = 256
SCAN_C = 64
TE = 256
TROUTE = 512
TCOMB = 256
VMEM_MIB = 56


def _cp(sem):
    return pltpu.CompilerParams(dimension_semantics=sem, vmem_limit_bytes=VMEM_MIB * 1024 * 1024)


def _rms(x, g):
    return x * lax.rsqrt(jnp.mean(x * x, axis=-1, keepdims=True) + NORM_EPS) * g


def _rope(x, cos, sin):
    half = x.shape[-1] // 2
    swapped = jnp.concatenate([x[:, half:], x[:, :half]], axis=-1)
    return x * cos + swapped * sin


def _dot_nt(a, b):
    return lax.dot_general(a, b, (((1,), (1,)), ((), ())), preferred_element_type=F32)


def _dot_tn(a, b):
    return lax.dot_general(a, b, (((0,), (0,)), ((), ())), preferred_element_type=F32)


def _modvec_kernel(c_ref, w_ref, b_ref, o_ref):
    c = c_ref[...]
    a = c * jax.nn.sigmoid(c)
    o_ref[...] = jnp.dot(a, w_ref[...], preferred_element_type=F32, precision=HI) + b_ref[...]


def _modvec(cvec, mod_w, mod_b):
    n_layers, d, n6 = mod_w.shape
    tn = min(1024, n6)
    return pl.pallas_call(
        _modvec_kernel,
        grid=(n_layers, n6 // tn),
        in_specs=[pl.BlockSpec((8, d), lambda l, j: (0, 0)),
                  pl.BlockSpec((None, d, tn), lambda l, j: (l, 0, j)),
                  pl.BlockSpec((None, 1, tn), lambda l, j: (l, 0, j))],
        out_specs=pl.BlockSpec((None, 8, tn), lambda l, j: (l, 0, j)),
        out_shape=jax.ShapeDtypeStruct((n_layers, 8, n6), F32),
        compiler_params=_cp(("parallel", "parallel")),
        name="modvec",
    )(cvec, mod_w, mod_b.reshape(n_layers, 1, n6))


class _Rows:
    def __init__(self, batch, seq, ctx_len, tm):
        assert seq % tm == 0 and (batch * ctx_len) % tm == 0
        self.tm = tm
        self.batch = batch
        self.per_batch = seq // tm
        self.n_lat = batch * seq // tm
        self.n_ctx = batch * ctx_len // tm
        self.n_all = self.n_lat + self.n_ctx

    def mod_row(self, i):
        return jnp.where(i < self.n_lat, i // self.per_batch, self.batch)


def _mod_spec(rows, layer, chunk, d):
    return pl.BlockSpec((None, None, None, 1, d), lambda i, *_: (layer, rows.mod_row(i), chunk, 0, 0))


def _norm_mod_kernel(xl_ref, xc_ref, g_ref, sh_ref, sc_ref, o_ref, *, n_lat):
    i = pl.program_id(0)

    def body(x_ref):
        y = _rms(x_ref[...], g_ref[...])
        o_ref[...] = (y * (1.0 + sc_ref[...]) + sh_ref[...]).astype(o_ref.dtype)

    @pl.when(i < n_lat)
    def _():
        body(xl_ref)

    @pl.when(i >= n_lat)
    def _():
        body(xc_ref)


def _norm_mod(x_lat, x_ctx, ctx_block0, g, mods5, layer, rows):
    d = x_lat.shape[-1]
    tm = rows.tm
    nl = rows.n_lat
    return pl.pallas_call(
        functools.partial(_norm_mod_kernel, n_lat=nl),
        grid=(rows.n_all,),
        in_specs=[pl.BlockSpec((tm, d), lambda i: (jnp.minimum(i, nl - 1), 0)),
                  pl.BlockSpec((tm, d), lambda i: (ctx_block0 + jnp.maximum(i - nl, 0), 0)),
                  pl.BlockSpec((1, d), lambda i: (0, 0)),
                  _mod_spec(rows, layer, 0, d),
                  _mod_spec(rows, layer, 1, d)],
        out_specs=pl.BlockSpec((tm, d), lambda i: (i, 0)),
        out_shape=jax.ShapeDtypeStruct((rows.n_all * tm, d), BF16),
        compiler_params=_cp(("parallel",)),
        name="norm_mod",
    )(x_lat, x_ctx, g.reshape(1, d), mods5, mods5)


def _mm_kernel(a_ref, w_ref, o_ref):
    o_ref[...] = jnp.dot(a_ref[...], w_ref[...].astype(BF16), preferred_element_type=F32).astype(o_ref.dtype)


def _matmul(a, w3, layer, n_cols, tn, tm):
    m, k = a.shape
    return pl.pallas_call(
        _mm_kernel,
        grid=(m // tm, n_cols // tn),
        in_specs=[pl.BlockSpec((tm, k), lambda i, j: (i, 0)),
                  pl.BlockSpec((None, k, tn), lambda i, j: (layer, 0, j))],
        out_specs=pl.BlockSpec((tm, tn), lambda i, j: (i, j)),
        out_shape=jax.ShapeDtypeStruct((m, n_cols), BF16),
        compiler_params=_cp(("parallel", "arbitrary")),
        name="in_proj",
    )(a, w3)


def _out_proj_kernel(ma_ref, mb_ref, wa_ref, wb_ref, xl_ref, xc_ref, g_ref, o_ref, *, n_lat):
    i = pl.program_id(0)
    acc = jnp.dot(ma_ref[...], wa_ref[...].astype(BF16), preferred_element_type=F32)
    acc += jnp.dot(mb_ref[...], wb_ref[...].astype(BF16), preferred_element_type=F32)
    upd = g_ref[...] * acc

    @pl.when(i < n_lat)
    def _():
        o_ref[...] = xl_ref[...] + upd

    @pl.when(i >= n_lat)
    def _():
        o_ref[...] = xc_ref[...] + upd


def _out_proj(mix_a, mix_b, w_out3, widx, x_lat, x_ctx, ctx_block0, mods5, layer, rows, n_tiles):
    d = x_lat.shape[-1]
    ka, kb = mix_a.shape[-1], mix_b.shape[-1]
    assert ka == kb
    tm = rows.tm
    tn = min(512, d)
    nl = rows.n_lat
    return pl.pallas_call(
        functools.partial(_out_proj_kernel, n_lat=nl),
        grid=(n_tiles, d // tn),
        in_specs=[pl.BlockSpec((tm, ka), lambda i, j: (i, 0)),
                  pl.BlockSpec((tm, kb), lambda i, j: (i, 0)),
                  pl.BlockSpec((None, ka, tn), lambda i, j: (2 * widx, 0, j)),
                  pl.BlockSpec((None, kb, tn), lambda i, j: (2 * widx + 1, 0, j)),
                  pl.BlockSpec((tm, tn), lambda i, j: (jnp.minimum(i, nl - 1), j)),
                  pl.BlockSpec((tm, tn), lambda i, j: (ctx_block0 + jnp.maximum(i - nl, 0), j)),
                  pl.BlockSpec((None, None, None, 1, tn), lambda i, j: (layer, rows.mod_row(i), 2, 0, j))],
        out_specs=pl.BlockSpec((tm, tn), lambda i, j: (i, j)),
        out_shape=jax.ShapeDtypeStruct((n_tiles * tm, d), F32),
        compiler_params=_cp(("parallel", "arbitrary")),
        name="out_proj",
    )(mix_a, mix_b, w_out3, w_out3, x_lat, x_ctx, mods5)


def _tri(c, upper):
    r = lax.broadcasted_iota(jnp.int32, (c, c), 0)
    s = lax.broadcasted_iota(jnp.int32, (c, c), 1)
    return (s >= r) if upper else (r >= s)


def _chunk_step(q, k, v, g, st, tri_f32, mask, mid, last):
    cum = jnp.dot(tri_f32, g, preferred_element_type=F32, precision=HI)
    m = cum[mid:mid + 1, :]
    tot = cum[last:last + 1, :]
    qe = (q * jnp.exp(cum - m)).astype(BF16)
    ke = (k * jnp.exp(m - cum)).astype(BF16)
    a = jnp.where(mask, _dot_nt(qe, ke), 0.0).astype(BF16)
    stp = st * jnp.exp(m)
    o = jnp.dot(a, v, preferred_element_type=F32) + _dot_nt(qe, stp.astype(BF16))
    new = (stp + _dot_tn(v, ke)) * jnp.exp(tot - m)
    return o, new


def _scan_segments(segments, prep_f, prep_b, of_ref, ob_ref, dk, dv):
    c = SCAN_C
    low, up = _tri(c, False), _tri(c, True)
    low_f, up_f = low.astype(F32), up.astype(F32)
    carry = (jnp.zeros((dv, dk), F32), jnp.zeros((dv, dk), F32))
    for n, off, seg in segments:
        def body(i, carry, n=n, off=off, seg=seg):
            sf, sb = carry
            rf = pl.multiple_of(i * c, c)
            rb = pl.multiple_of((n - 1 - i) * c, c)
            q, k, v, g = prep_f(seg, rf)
            o, sf = _chunk_step(q, k, v, g, sf, low_f, low, c // 2 - 1, c - 1)
            of_ref[pl.ds(off + rf, c), :] = o
            q, k, v, g = prep_b(seg, rb)
            o, sb = _chunk_step(q, k, v, g, sb, up_f, up, c // 2, 0)
            ob_ref[pl.ds(off + rb, c), :] = o
            return sf, sb

        carry = lax.fori_loop(0, n, body, carry)


def _hgrn_kernel(ql, qc, f1l, f1c, f2l, f2c, vl, vc, gl, gc, lbf, lbb, ng, o_ref, of_s, ob_s, *, ctx_len, seq):
    rt = pl.program_id(2)
    n_ctx_tiles = ctx_len // TQ

    @pl.when(rt == 0)
    def _scan():
        refs = {0: (qc, f1c, f2c, vc), 1: (ql, f1l, f2l, vl)}
        scale = A_DK ** -0.5

        def prep(seg, r, fi, lb_ref):
            x = refs[seg][0][pl.ds(r, SCAN_C), :].astype(F32)
            q = x * jax.nn.sigmoid(x) * scale
            v = refs[seg][3][pl.ds(r, SCAN_C), :]
            lb = lb_ref[...]
            f = lb + (1.0 - lb) * jax.nn.sigmoid(refs[seg][fi][pl.ds(r, SCAN_C), :].astype(F32))
            return q, 1.0 - f, v, jnp.log(f)

        _scan_segments(
            [(ctx_len // SCAN_C, 0, 0), (seq // SCAN_C, ctx_len, 1)],
            lambda seg, r: prep(seg, r, 1, lbf),
            lambda seg, r: prep(seg, r, 2, lbb),
            of_s, ob_s, A_DK, A_DV)

    r0 = pl.multiple_of(rt * TQ, TQ)
    o = of_s[pl.ds(r0, TQ), :] + ob_s[pl.ds(r0, TQ), :]
    y = _rms(o, ng[...])

    @pl.when(rt < n_ctx_tiles)
    def _():
        gate = gc[pl.ds(r0, TQ), :].astype(F32)
        o_ref[...] = (y * jax.nn.sigmoid(gate)).astype(o_ref.dtype)

    @pl.when(rt >= n_ctx_tiles)
    def _():
        gate = gl[pl.ds(pl.multiple_of(r0 - ctx_len, TQ), TQ), :].astype(F32)
        o_ref[...] = (y * jax.nn.sigmoid(gate)).astype(o_ref.dtype)


def _out_row_block(batch, seq, ctx_len):
    nct = ctx_len // TQ
    nlt = seq // TQ

    def f(b, rt):
        return jnp.where(rt < nct, batch * nlt + b * nct + rt, b * nlt + rt - nct)

    return f


def _hgrn(p, lb_f, lb_b, norm_g, batch, seq, ctx_len):
    h, dk, dv = A_HEADS, A_DK, A_DV
    nct, nlt = ctx_len // TQ, seq // TQ
    ctx_blk0 = batch * seq // ctx_len
    row_block = _out_row_block(batch, seq, ctx_len)
    in_specs = []
    for kcol in range(5):
        in_specs.append(pl.BlockSpec((seq, dk), lambda b, hh, rt, kcol=kcol: (b, kcol * h + hh)))
        in_specs.append(pl.BlockSpec((ctx_len, dk), lambda b, hh, rt, kcol=kcol: (ctx_blk0 + b, kcol * h + hh)))
    vec = pl.BlockSpec((1, dk), lambda b, hh, rt: (0, hh))
    in_specs += [vec, vec, pl.BlockSpec((1, dv), lambda b, hh, rt: (0, 0))]
    return pl.pallas_call(
        functools.partial(_hgrn_kernel, ctx_len=ctx_len, seq=seq),
        grid=(batch, h, nct + nlt),
        in_specs=in_specs,
        out_specs=pl.BlockSpec((TQ, dv), lambda b, hh, rt: (row_block(b, rt), hh)),
        out_shape=jax.ShapeDtypeStruct((batch * (seq + ctx_len), h * dv), BF16),
        scratch_shapes=[pltpu.VMEM((seq + ctx_len, dv), F32), pltpu.VMEM((seq + ctx_len, dv), F32)],
        compiler_params=_cp(("parallel", "parallel", "arbitrary")),
        name="hgrn_scan",
    )(*([p] * 10), lb_f.reshape(1, h * dk), lb_b.reshape(1, h * dk), norm_g.reshape(1, dv))


def _gla_kernel(ql, qc, kl, kc, vl, vc, gl, al, ac, wa, ba, ng, o_ref, of_s, ob_s, *, ctx_len, seq):
    rt = pl.program_id(2)

    @pl.when(rt == 0)
    def _scan():
        refs = {0: (qc, kc, vc, ac), 1: (ql, kl, vl, al)}
        scale = D_DK ** -0.5
        r16 = D_GATE_RANK

        def prep(seg, r, d):
            q = refs[seg][0][pl.ds(r, SCAN_C), :].astype(F32) * scale
            k = refs[seg][1][pl.ds(r, SCAN_C), :].astype(F32)
            v = refs[seg][2][pl.ds(r, SCAN_C), :]
            a = refs[seg][3][pl.ds(r, SCAN_C), :].astype(F32)[:, d * r16:(d + 1) * r16]
            z = jnp.dot(a, wa[d], preferred_element_type=F32, precision=HI) + ba[d]
            g = (jnp.minimum(z, 0.0) - jnp.log(1.0 + jnp.exp(-jnp.abs(z)))) * (1.0 / GLA_TAU)
            return q, k, v, g

        _scan_segments(
            [(ctx_len // SCAN_C, 0, 0), (seq // SCAN_C, ctx_len, 1)],
            lambda seg, r: prep(seg, r, 0),
            lambda seg, r: prep(seg, r, 1),
            of_s, ob_s, D_DK, D_DV)

    r0 = pl.multiple_of(rt * TQ, TQ)
    o = of_s[pl.ds(ctx_len + r0, TQ), :] + ob_s[pl.ds(ctx_len + r0, TQ), :]
    gate = gl[pl.ds(r0, TQ), :].astype(F32)
    o_ref[...] = (_rms(o, ng[...]) * gate * jax.nn.sigmoid(gate)).astype(o_ref.dtype)


def _gla(p, ga, w_a2, b_a, norm_g, batch, seq, ctx_len):
    h, dk, dv = D_HEADS, D_DK, D_DV
    nlt = seq // TQ
    ctx_blk0 = batch * seq // ctx_len
    q0 = (C_HEADS + 2 * C_KV_HEADS) * C_DH // dk
    k0 = q0 + h
    v0 = (k0 + h) * dk // dv
    g0 = v0 + h

    def pair(width, blk0):
        return [pl.BlockSpec((seq, width), lambda b, hh, rt: (b, blk0 + hh)),
                pl.BlockSpec((ctx_len, width), lambda b, hh, rt: (ctx_blk0 + b, blk0 + hh))]

    in_specs = pair(dk, q0) + pair(dk, k0) + pair(dv, v0)
    in_specs += [pl.BlockSpec((seq, dv), lambda b, hh, rt: (b, g0 + hh)),
                 pl.BlockSpec((seq, 2 * D_GATE_RANK), lambda b, hh, rt: (b, 0)),
                 pl.BlockSpec((ctx_len, 2 * D_GATE_RANK), lambda b, hh, rt: (ctx_blk0 + b, 0)),
                 pl.BlockSpec((2, D_GATE_RANK, dk), lambda b, hh, rt: (0, 0, hh)),
                 pl.BlockSpec((2, 1, dk), lambda b, hh, rt: (0, 0, hh)),
                 pl.BlockSpec((1, dv), lambda b, hh, rt: (0, 0))]
    return pl.pallas_call(
        functools.partial(_gla_kernel, ctx_len=ctx_len, seq=seq),
        grid=(batch, h, nlt),
        in_specs=in_specs,
        out_specs=pl.BlockSpec((TQ, dv), lambda b, hh, rt: (b * nlt + rt, hh)),
        out_shape=jax.ShapeDtypeStruct((batch * seq, h * dv), BF16),
        scratch_shapes=[pltpu.VMEM((seq + ctx_len, dv), F32), pltpu.VMEM((seq + ctx_len, dv), F32)],
        compiler_params=_cp(("parallel", "parallel", "arbitrary")),
        name="gla_scan",
    )(p, p, p, p, p, p, p, ga, ga, w_a2, b_a.reshape(2, 1, h * dk), norm_g.reshape(1, dv))


def _softmax_pv(s, v):
    m = jnp.max(s, axis=-1, keepdims=True)
    p = jnp.exp(s - m)
    l = jnp.sum(p, axis=-1, keepdims=True)
    return jnp.dot(p.astype(BF16), v, preferred_element_type=F32) / l


def _mla_kernel(ql_ref, kvl_ref, kvc_ref, krl_ref, krc_ref, wq_ref, wkv_ref, gq_ref, gkv_ref,
                cosq_ref, sinq_ref, cosk_ref, sink_ref, o_ref, kn_s, kr_s, v_s, *, ctx_len):
    qt = pl.program_id(2)
    n_ctx_tiles = ctx_len // TQ
    scale = (B_NOPE + B_ROPE) ** -0.5

    @pl.when(qt == 0)
    def _prep():
        wkv = wkv_ref[...].astype(BF16)

        def up(x_ref):
            xn = _rms(x_ref[...].astype(F32), gkv_ref[...])
            return jnp.dot(xn.astype(BF16), wkv, preferred_element_type=F32)

        kvc = up(kvc_ref)
        kn_s[0:ctx_len, :] = kvc[:, :B_NOPE].astype(BF16)
        v_s[0:ctx_len, :] = kvc[:, B_NOPE:].astype(BF16)
        kvl = up(kvl_ref)
        kn_s[ctx_len:, :] = kvl[:, :B_NOPE].astype(BF16)
        v_s[ctx_len:, :] = kvl[:, B_NOPE:].astype(BF16)
        kr_s[0:ctx_len, :] = krc_ref[...]
        kr_s[ctx_len:, :] = _rope(krl_ref[...].astype(F32), cosk_ref[...], sink_ref[...]).astype(BF16)

    xn = _rms(ql_ref[...].astype(F32), gq_ref[...])
    q = jnp.dot(xn.astype(BF16), wq_ref[...].astype(BF16), preferred_element_type=F32) * scale
    qn = q[:, :B_NOPE].astype(BF16)
    qr = q[:, B_NOPE:]

    @pl.when(qt < n_ctx_tiles)
    def _():
        s = _dot_nt(qn, kn_s[0:ctx_len, :]) + _dot_nt(qr.astype(BF16), kr_s[0:ctx_len, :])
        o_ref[...] = _softmax_pv(s, v_s[0:ctx_len, :]).astype(o_ref.dtype)

    @pl.when(qt >= n_ctx_tiles)
    def _():
        qrr = _rope(qr, cosq_ref[...], sinq_ref[...]).astype(BF16)
        s = _dot_nt(qn, kn_s[...]) + _dot_nt(qrr, kr_s[...])
        o_ref[...] = _softmax_pv(s, v_s[...]).astype(o_ref.dtype)


def _mla(p, kr, w_uq_h, w_ukv_h, gq, gkv, cos, sin, batch, seq, ctx_len):
    h = B_HEADS
    nct, nlt = ctx_len // TQ, seq // TQ
    ctx_blk0 = batch * seq // ctx_len
    row_block = _out_row_block(batch, seq, ctx_len)
    ql_blk = 5 * A_HEADS * A_DK // B_Q_LORA
    kv_blk = (5 * A_HEADS * A_DK + B_Q_LORA) // B_KV_LORA
    dq = B_NOPE + B_ROPE
    dkv = B_NOPE + B_DV
    s_all = seq + ctx_len
    in_specs = [
        pl.BlockSpec((TQ, B_Q_LORA), lambda b, hh, qt: (row_block(b, qt), ql_blk)),
        pl.BlockSpec((seq, B_KV_LORA), lambda b, hh, qt: (b, kv_blk)),
        pl.BlockSpec((ctx_len, B_KV_LORA), lambda b, hh, qt: (ctx_blk0 + b, kv_blk)),
        pl.BlockSpec((seq, B_ROPE), lambda b, hh, qt: (b, 0)),
        pl.BlockSpec((ctx_len, B_ROPE), lambda b, hh, qt: (ctx_blk0 + b, 0)),
        pl.BlockSpec((None, B_Q_LORA, dq), lambda b, hh, qt: (hh, 0, 0)),
        pl.BlockSpec((None, B_KV_LORA, dkv), lambda b, hh, qt: (hh, 0, 0)),
        pl.BlockSpec((1, B_Q_LORA), lambda b, hh, qt: (0, 0)),
        pl.BlockSpec((1, B_KV_LORA), lambda b, hh, qt: (0, 0)),
        pl.BlockSpec((TQ, B_ROPE), lambda b, hh, qt: (jnp.maximum(qt - nct, 0), 0)),
        pl.BlockSpec((TQ, B_ROPE), lambda b, hh, qt: (jnp.maximum(qt - nct, 0), 0)),
        pl.BlockSpec((seq, B_ROPE), lambda b, hh, qt: (0, 0)),
        pl.BlockSpec((seq, B_ROPE), lambda b, hh, qt: (0, 0)),
    ]
    return pl.pallas_call(
        functools.partial(_mla_kernel, ctx_len=ctx_len),
        grid=(batch, h, nct + nlt),
        in_specs=in_specs,
        out_specs=pl.BlockSpec((TQ, B_DV), lambda b, hh, qt: (row_block(b, qt), hh)),
        out_shape=jax.ShapeDtypeStruct((batch * s_all, h * B_DV), BF16),
        scratch_shapes=[pltpu.VMEM((s_all, B_NOPE), BF16), pltpu.VMEM((s_all, B_ROPE), BF16),
                        pltpu.VMEM((s_all, B_DV), BF16)],
        compiler_params=_cp(("parallel", "parallel", "arbitrary")),
        name="mla_attn",
    )(p, p, p, kr, kr, w_uq_h, w_ukv_h, gq.reshape(1, -1), gkv.reshape(1, -1), cos, sin, cos, sin)


def _gqa_kernel(q_ref, kl_ref, kc_ref, vl_ref, vc_ref, gq_ref, gk_ref, cosq_ref, sinq_ref, cosk_ref, sink_ref,
                o_ref, k_s, v_s, *, ctx_len):
    qi = pl.program_id(2)
    scale = C_DH ** -0.5

    @pl.when(qi == 0)
    def _prep():
        k_s[0:ctx_len, :] = _rms(kc_ref[...].astype(F32), gk_ref[...]).astype(BF16)
        kl = _rms(kl_ref[...].astype(F32), gk_ref[...])
        k_s[ctx_len:, :] = _rope(kl, cosk_ref[...], sink_ref[...]).astype(BF16)
        v_s[0:ctx_len, :] = vc_ref[...]
        v_s[ctx_len:, :] = vl_ref[...]

    q = _rope(_rms(q_ref[...].astype(F32), gq_ref[...]), cosq_ref[...], sinq_ref[...]) * scale
    s = _dot_nt(q.astype(BF16), k_s[...])
    o_ref[...] = _softmax_pv(s, v_s[...]).astype(o_ref.dtype)


def _gqa(p, gq, gk, cos, sin, batch, seq, ctx_len):
    kvh, grp, dh = C_KV_HEADS, C_HEADS // C_KV_HEADS, C_DH
    nlt = seq // TQ
    ctx_blk0 = batch * seq // ctx_len
    k0 = C_HEADS
    v0 = C_HEADS + C_KV_HEADS
    s_all = seq + ctx_len
    in_specs = [
        pl.BlockSpec((TQ, dh), lambda b, kh, qi: (b * nlt + qi % nlt, kh * grp + qi // nlt)),
        pl.BlockSpec((seq, dh), lambda b, kh, qi: (b, k0 + kh)),
        pl.BlockSpec((ctx_len, dh), lambda b, kh, qi: (ctx_blk0 + b, k0 + kh)),
        pl.BlockSpec((seq, dh), lambda b, kh, qi: (b, v0 + kh)),
        pl.BlockSpec((ctx_len, dh), lambda b, kh, qi: (ctx_blk0 + b, v0 + kh)),
        pl.BlockSpec((1, dh), lambda b, kh, qi: (0, 0)),
        pl.BlockSpec((1, dh), lambda b, kh, qi: (0, 0)),
        pl.BlockSpec((TQ, dh), lambda b, kh, qi: (qi % nlt, 0)),
        pl.BlockSpec((TQ, dh), lambda b, kh, qi: (qi % nlt, 0)),
        pl.BlockSpec((seq, dh), lambda b, kh, qi: (0, 0)),
        pl.BlockSpec((seq, dh), lambda b, kh, qi: (0, 0)),
    ]
    return pl.pallas_call(
        functools.partial(_gqa_kernel, ctx_len=ctx_len),
        grid=(batch, kvh, grp * nlt),
        in_specs=in_specs,
        out_specs=pl.BlockSpec((TQ, dh), lambda b, kh, qi: (b * nlt + qi % nlt, kh * grp + qi // nlt)),
        out_shape=jax.ShapeDtypeStruct((batch * seq, C_HEADS * dh), BF16),
        scratch_shapes=[pltpu.VMEM((s_all, dh), BF16), pltpu.VMEM((s_all, dh), BF16)],
        compiler_params=_cp(("parallel", "parallel", "arbitrary")),
        name="gqa_attn",
    )(p, p, p, p, p, gq.reshape(1, dh), gk.reshape(1, dh), cos, sin, cos, sin)


def _router_kernel(x_ref, g_ref, sh_ref, sc_ref, rw_ref, rb_ref, h_ref, ri_ref, rf_ref, cnt_ref, base_s):
    i = pl.program_id(0)
    tm = x_ref.shape[0]
    ne = N_EXPERTS
    per = ne // N_GROUPS

    @pl.when(i == 0)
    def _():
        base_s[...] = jnp.zeros_like(base_s)

    h = _rms(x_ref[...], g_ref[...]) * (1.0 + sc_ref[...]) + sh_ref[...]
    h_ref[...] = h.astype(h_ref.dtype)
    logits = jnp.dot(h, rw_ref[...], preferred_element_type=F32, precision=HI)
    scores = jax.nn.sigmoid(logits)
    sel = scores + rb_ref[...]
    lane = lax.broadcasted_iota(jnp.int32, (tm, ne), 1).astype(F32)
    neg = -jnp.inf
    big = float(ne)

    def top2(vals):
        m1 = jnp.max(vals, axis=1, keepdims=True)
        i1 = jnp.min(jnp.where(vals == m1, lane, big), axis=1, keepdims=True)
        rest = jnp.where(lane == i1, neg, vals)
        m2 = jnp.max(rest, axis=1, keepdims=True)
        i2 = jnp.min(jnp.where(rest == m2, lane, big), axis=1, keepdims=True)
        return m1 + m2, i1, i2

    best, e1, e2 = None, None, None
    for grp in range(N_GROUPS):
        in_grp = jnp.logical_and(lane >= float(grp * per), lane < float((grp + 1) * per))
        gsum, i1, i2 = top2(jnp.where(in_grp, sel, neg))
        if grp == 0:
            best, e1, e2 = gsum, i1, i2
        else:
            better = gsum > best
            best = jnp.where(better, gsum, best)
            e1 = jnp.where(better, i1, e1)
            e2 = jnp.where(better, i2, e2)

    hot1 = lane == e1
    hot2 = lane == e2
    w1 = jnp.sum(jnp.where(hot1, scores, 0.0), axis=1, keepdims=True)
    w2 = jnp.sum(jnp.where(hot2, scores, 0.0), axis=1, keepdims=True)
    wsum = w1 + w2
    assign = jnp.logical_or(hot1, hot2)
    r = lax.broadcasted_iota(jnp.int32, (tm, tm), 0)
    c = lax.broadcasted_iota(jnp.int32, (tm, tm), 1)
    before = (c < r).astype(BF16)
    excl = jnp.dot(before, assign.astype(BF16), preferred_element_type=F32) + base_s[...]
    rank1 = jnp.sum(jnp.where(hot1, excl, 0.0), axis=1, keepdims=True)
    rank2 = jnp.sum(jnp.where(hot2, excl, 0.0), axis=1, keepdims=True)
    base_s[...] = base_s[...] + jnp.sum(assign.astype(F32), axis=0, keepdims=True)

    l128 = lax.broadcasted_iota(jnp.int32, (tm, 128), 1)
    ri = jnp.where(l128 == 0, e1, jnp.where(l128 == 1, e2, jnp.where(l128 == 2, rank1, jnp.where(l128 == 3, rank2, 0.0))))
    ri_ref[...] = ri.astype(jnp.int32)
    rf_ref[...] = jnp.where(l128 == 0, w1 / wsum, jnp.where(l128 == 1, w2 / wsum, 0.0))
    cnt_ref[...] = jnp.broadcast_to(base_s[...], cnt_ref.shape)


def _router(x_all, g, mods5, layer, router_w, router_b, rows, n_tiles):
    d = x_all.shape[-1]
    tm = rows.tm
    n = n_tiles * tm
    ne = N_EXPERTS
    return pl.pallas_call(
        _router_kernel,
        grid=(n_tiles,),
        in_specs=[pl.BlockSpec((tm, d), lambda i: (i, 0)),
                  pl.BlockSpec((1, d), lambda i: (0, 0)),
                  _mod_spec(rows, layer, 3, d),
                  _mod_spec(rows, layer, 4, d),
                  pl.BlockSpec((d, ne), lambda i: (0, 0)),
                  pl.BlockSpec((1, ne), lambda i: (0, 0))],
        out_specs=[pl.BlockSpec((tm, d), lambda i: (i, 0)),
                   pl.BlockSpec((tm, 128), lambda i: (i, 0)),
                   pl.BlockSpec((tm, 128), lambda i: (i, 0)),
                   pl.BlockSpec((8, ne), lambda i: (0, 0))],
        out_shape=[jax.ShapeDtypeStruct((n, d), F32),
                   jax.ShapeDtypeStruct((n, 128), jnp.int32),
                   jax.ShapeDtypeStruct((n, 128), F32),
                   jax.ShapeDtypeStruct((8, ne), F32)],
        scratch_shapes=[pltpu.VMEM((1, ne), F32)],
        compiler_params=_cp(("arbitrary",)),
        name="moe_router",
    )(x_all, g.reshape(1, d), mods5, mods5, router_w, router_b.reshape(1, ne))


def _dispatch_kernel(dest_ref, h_ref, xs_in, xs_hbm, sem, *, tm, n_tok):
    del xs_in
    i = pl.program_id(0)

    def row_copy(t, dst):
        return pltpu.make_async_copy(h_ref.at[pl.ds(t, 1)], xs_hbm.at[pl.ds(dst, 1)], sem)

    def start(t, carry):
        tok = i * tm + t
        row_copy(t, dest_ref[tok]).start()
        row_copy(t, dest_ref[n_tok + tok]).start()
        return carry

    lax.fori_loop(0, tm, start, 0)

    def wait(t, carry):
        row_copy(0, 0).wait()
        row_copy(0, 0).wait()
        return carry

    lax.fori_loop(0, tm, wait, 0)


def _dispatch(dest, h, n_tok, p_max):
    d = h.shape[-1]
    tm = TCOMB
    zeros = jnp.zeros((p_max, d), h.dtype)
    grid_spec = pltpu.PrefetchScalarGridSpec(
        num_scalar_prefetch=1,
        grid=(n_tok // tm,),
        in_specs=[pl.BlockSpec((tm, d), lambda i, dr: (i, 0)), pl.BlockSpec(memory_space=pl.ANY)],
        out_specs=pl.BlockSpec(memory_space=pl.ANY),
        scratch_shapes=[pltpu.SemaphoreType.DMA(())],
    )
    return pl.pallas_call(
        functools.partial(_dispatch_kernel, tm=tm, n_tok=n_tok),
        grid_spec=grid_spec,
        out_shape=jax.ShapeDtypeStruct((p_max, d), h.dtype),
        input_output_aliases={2: 0},
        compiler_params=_cp(("arbitrary",)),
        name="moe_dispatch",
    )(dest, h, zeros)


def _expert_kernel(te_ref, nu_ref, x_ref, w1_ref, w3_ref, w2_ref, y_ref, w1_s, w3_s, w2_s):
    r = pl.program_id(0)
    active = r < nu_ref[0]
    changed = jnp.logical_or(r == 0, te_ref[r] != te_ref[jnp.maximum(r - 1, 0)])

    @pl.when(jnp.logical_and(active, changed))
    def _():
        w1_s[...] = w1_ref[...].astype(BF16)
        w3_s[...] = w3_ref[...].astype(BF16)
        w2_s[...] = w2_ref[...].astype(BF16)

    @pl.when(active)
    def _():
        x = x_ref[...].astype(BF16)
        a = jnp.dot(x, w1_s[...], preferred_element_type=F32)
        b = jnp.dot(x, w3_s[...], preferred_element_type=F32)
        hid = (a * jax.nn.sigmoid(a) * b).astype(BF16)
        y_ref[...] = jnp.dot(hid, w2_s[...], preferred_element_type=F32)

    @pl.when(jnp.logical_not(active))
    def _():
        y_ref[...] = jnp.zeros_like(y_ref)


def _experts(tile_expert, n_used, xs, w1, w3, w2, layer):
    p_max, d = xs.shape
    f = w1.shape[-1]
    grid_spec = pltpu.PrefetchScalarGridSpec(
        num_scalar_prefetch=2,
        grid=(p_max // TE,),
        in_specs=[pl.BlockSpec((TE, d), lambda r, te, nu: (r, 0)),
                  pl.BlockSpec((None, None, d, f), lambda r, te, nu: (layer, te[r], 0, 0)),
                  pl.BlockSpec((None, None, d, f), lambda r, te, nu: (layer, te[r], 0, 0)),
                  pl.BlockSpec((None, None, f, d), lambda r, te, nu: (layer, te[r], 0, 0))],
        out_specs=pl.BlockSpec((TE, d), lambda r, te, nu: (r, 0)),
        scratch_shapes=[pltpu.VMEM((d, f), BF16), pltpu.VMEM((d, f), BF16), pltpu.VMEM((f, d), BF16)],
    )
    return pl.pallas_call(
        _expert_kernel,
        grid_spec=grid_spec,
        out_shape=jax.ShapeDtypeStruct((p_max, d), F32),
        compiler_params=_cp(("arbitrary",)),
        name="moe_experts",
    )(tile_expert, n_used, xs, w1, w3, w2)


def _combine_kernel(dest_ref, x_ref, rf_ref, g_ref, fg_ref, ys_hbm, o_ref, buf, sem, *, tm, n_tok, final):
    i = pl.program_id(0)

    def row_copy(src, slot, t):
        return pltpu.make_async_copy(ys_hbm.at[pl.ds(src, 1)], buf.at[slot, pl.ds(t, 1)], sem)

    def start(t, carry):
        tok = i * tm + t
        row_copy(dest_ref[tok], 0, t).start()
        row_copy(dest_ref[n_tok + tok], 1, t).start()
        return carry

    lax.fori_loop(0, tm, start, 0)

    def wait(t, carry):
        row_copy(0, 0, 0).wait()
        row_copy(0, 1, 0).wait()
        return carry

    lax.fori_loop(0, tm, wait, 0)
    w = rf_ref[...]
    y = w[:, 0:1] * buf[0] + w[:, 1:2] * buf[1]
    x2 = x_ref[...] + g_ref[...] * y
    if final:
        x2 = _rms(x2, fg_ref[...])
    o_ref[...] = x2


def _combine(dest, x_all, rf, mods5, layer, final_g, ys, rows, n_tiles, n_tok_total, final):
    d = x_all.shape[-1]
    tm = rows.tm
    grid_spec = pltpu.PrefetchScalarGridSpec(
        num_scalar_prefetch=1,
        grid=(n_tiles,),
        in_specs=[pl.BlockSpec((tm, d), lambda i, dr: (i, 0)),
                  pl.BlockSpec((tm, 128), lambda i, dr: (i, 0)),
                  _mod_spec(rows, layer, 5, d),
                  pl.BlockSpec((1, d), lambda i, dr: (0, 0)),
                  pl.BlockSpec(memory_space=pl.ANY)],
        out_specs=pl.BlockSpec((tm, d), lambda i, dr: (i, 0)),
        scratch_shapes=[pltpu.VMEM((2, tm, d), F32), pltpu.SemaphoreType.DMA(())],
    )
    return pl.pallas_call(
        functools.partial(_combine_kernel, tm=tm, n_tok=n_tok_total, final=final),
        grid_spec=grid_spec,
        out_shape=jax.ShapeDtypeStruct((n_tiles * tm, d), F32),
        compiler_params=_cp(("arbitrary",)),
        name="moe_combine",
    )(dest, x_all, rf, mods5, final_g.reshape(1, d), ys)


def _moe(x_all, n_tok, norm_g, mods5, layer, router_w, router_b, w1, w3, w2, final_g, final, batch, seq, ctx_len):
    rows_r = _Rows(batch, seq, ctx_len, TROUTE)
    rows_c = _Rows(batch, seq, ctx_len, TCOMB)
    h, ri, rf, cnt = _router(x_all, norm_g, mods5, layer, router_w, router_b, rows_r, n_tok // TROUTE)
    counts = cnt[0].astype(jnp.int32)
    padded = ((counts + TE - 1) // TE) * TE
    ends = jnp.cumsum(padded)
    starts = ends - padded
    e1, e2, r1, r2 = ri[:, 0], ri[:, 1], ri[:, 2], ri[:, 3]
    dest = jnp.concatenate([starts[e1] + r1, starts[e2] + r2]).astype(jnp.int32)
    p_max = 2 * n_tok + N_EXPERTS * TE
    n_tiles = p_max // TE
    n_used = (ends[-1] // TE).astype(jnp.int32)
    tile_start = jnp.arange(n_tiles, dtype=jnp.int32) * TE
    tile_expert = jnp.sum((tile_start[:, None] >= ends[None, :]).astype(jnp.int32), axis=1)
    last_expert = jnp.sum((jnp.maximum(ends[-1] - 1, 0) >= ends).astype(jnp.int32))
    tile_expert = jnp.minimum(jnp.where(tile_start < ends[-1], tile_expert, last_expert), N_EXPERTS - 1).astype(jnp.int32)
    xs = _dispatch(dest, h, n_tok, p_max)
    ys = _experts(tile_expert, n_used.reshape(1), xs, w1, w3, w2, layer)
    return _combine(dest, x_all, rf, mods5, layer, final_g, ys, rows_c, n_tok // TCOMB, n_tok, final)


def _rope_tables(t_len, d_rope):
    rows = t_len // GRID_W
    quarter = d_rope // 4
    freqs = ROPE_THETA ** (-jnp.arange(quarter, dtype=F32) / quarter)
    row = jnp.repeat(jnp.arange(rows, dtype=F32), GRID_W)
    col = jnp.tile(jnp.arange(GRID_W, dtype=F32), rows)
    ang = jnp.concatenate([row[:, None] * freqs, col[:, None] * freqs], axis=-1)
    cos, sin = jnp.cos(ang), jnp.sin(ang)
    return jnp.concatenate([cos, cos], axis=-1), jnp.concatenate([-sin, sin], axis=-1)


def kernel(x, c, ctx, c_ctx, mod_w, mod_b, norm_attn_g, norm_ffn_g, final_norm_g, ab_w_in, ab_w_out, hgrn_lb_logits, hgrn_norm_g, mla_q_norm_g, mla_w_uq, mla_kv_norm_g, mla_w_ukv, cd_w_in, cd_w_out, gqa_q_norm_g, gqa_k_norm_g, gla_w_a2, gla_b_a, gla_norm_g, router_w, router_b, moe_w1, moe_w3, moe_w2):
    batch, seq, d = x.shape
    ctx_len = ctx.shape[1]
    n_lat, n_ctx = batch * seq, batch * ctx_len
    assert ctx_len % TQ == 0 and seq % TQ == 0 and seq % ctx_len == 0 and batch < 8
    tm = min(1024, seq, n_ctx)
    rows = _Rows(batch, seq, ctx_len, tm)

    cvec = jnp.concatenate([c, c_ctx[None, :], jnp.zeros((8 - batch - 1, d), F32)], axis=0)
    mods = _modvec(cvec, mod_w, mod_b)
    mods5 = mods.reshape(mods.shape[0], 8, 6, 1, d)

    cos_b, sin_b = _rope_tables(seq, B_ROPE)
    cos_c, sin_c = _rope_tables(seq, C_DH)
    lb = jnp.cumsum(jax.nn.softmax(hgrn_lb_logits.astype(F32), axis=1), axis=1)

    x_lat = x.reshape(n_lat, d)
    x_ctx = ctx.reshape(n_ctx, d)

    h0 = _norm_mod(x_lat, x_ctx, 0, norm_attn_g[0], mods5, 0, rows)
    ab_main = 5 * A_HEADS * A_DK + B_Q_LORA + B_KV_LORA
    p0 = _matmul(h0, ab_w_in, 0, ab_main, 256, tm)
    kr0 = _matmul(h0, ab_w_in[:, :, ab_main:], 0, B_ROPE, B_ROPE, tm)
    mix_a = _hgrn(p0, lb[0, 0], lb[1, 0], hgrn_norm_g[0], batch, seq, ctx_len)
    dq = B_NOPE + B_ROPE
    dkv = B_NOPE + B_DV
    w_uq_h = mla_w_uq[0].reshape(B_Q_LORA, B_HEADS, dq).transpose(1, 0, 2)
    w_ukv_h = mla_w_ukv[0].reshape(B_KV_LORA, B_HEADS, dkv).transpose(1, 0, 2)
    mix_b = _mla(p0, kr0, w_uq_h, w_ukv_h, mla_q_norm_g[0], mla_kv_norm_g[0], cos_b, sin_b, batch, seq, ctx_len)
    half = mix_a.shape[-1]
    w_out0 = ab_w_out.reshape(ab_w_out.shape[0] * 2, half, d)
    x1 = _out_proj(mix_a, mix_b, w_out0, 0, x_lat, x_ctx, 0, mods5, 0, rows, rows.n_all)
    x2 = _moe(x1, n_lat + n_ctx, norm_ffn_g[0], mods5, 0, router_w, router_b, moe_w1, moe_w3, moe_w2,
              final_norm_g, False, batch, seq, ctx_len)

    h1 = _norm_mod(x2, x2, rows.n_lat, norm_attn_g[1], mods5, 1, rows)
    cd_main = (C_HEADS + 2 * C_KV_HEADS) * C_DH + 2 * D_HEADS * D_DK + 2 * D_HEADS * D_DV
    p1 = _matmul(h1, cd_w_in, 0, cd_main, 512, tm)
    ga1 = _matmul(h1, cd_w_in[:, :, cd_main:], 0, 2 * D_GATE_RANK, 2 * D_GATE_RANK, tm)
    mix_c = _gqa(p1, gqa_q_norm_g[0], gqa_k_norm_g[0], cos_c, sin_c, batch, seq, ctx_len)
    mix_d = _gla(p1, ga1, gla_w_a2[0], gla_b_a[0], gla_norm_g[0], batch, seq, ctx_len)
    w_out1 = cd_w_out.reshape(cd_w_out.shape[0] * 2, mix_c.shape[-1], d)
    x3 = _out_proj(mix_c, mix_d, w_out1, 0, x2, x2, rows.n_lat, mods5, 1, rows, rows.n_lat)
    out = _moe(x3, n_lat, norm_ffn_g[1], mods5, 1, router_w, router_b, moe_w1, moe_w3, moe_w2,
               final_norm_g, True, batch, seq, ctx_len)
    return out.reshape(batch, seq, d)
```

```python
import functools

import jax
import jax.numpy as jnp
from jax import lax
from jax.experimental import pallas as pl
from jax.experimental.pallas import tpu as pltpu

F32 = jnp.float32
BF16 = jnp.bfloat16
HI = lax.Precision.HIGHEST

GRID_W = 64
ROPE_THETA = 10000.0
NORM_EPS = 1e-6
A_HEADS, A_DK, A_DV = 8, 128, 128
B_HEADS, B_Q_LORA, B_KV_LORA, B_NOPE, B_ROPE, B_DV = 8, 512, 256, 128, 64, 128
C_HEADS, C_KV_HEADS, C_DH = 8, 2, 128
D_HEADS, D_DK, D_DV, D_GATE_RANK = 4, 128, 256, 16
GLA_TAU = 16.0
N_EXPERTS, N_GROUPS = 16, 4

TQ = 256
SCAN_C = 64
SCAN_UNROLL = 4
TE = 256
TROUTE = 512
TCOMB = 256
DMA_UNROLL = 8
VMEM_MIB = 56


def _cp(sem):
    return pltpu.CompilerParams(dimension_semantics=sem, vmem_limit_bytes=VMEM_MIB * 1024 * 1024)


def _rms(x, g):
    return x * lax.rsqrt(jnp.mean(x * x, axis=-1, keepdims=True) + NORM_EPS) * g


def _rope(x, cos, sin):
    half = x.shape[-1] // 2
    swapped = jnp.concatenate([x[:, half:], x[:, :half]], axis=-1)
    return x * cos + swapped * sin


def _dot_nt(a, b):
    return lax.dot_general(a, b, (((1,), (1,)), ((), ())), preferred_element_type=F32)


def _dot_tn(a, b):
    return lax.dot_general(a, b, (((0,), (0,)), ((), ())), preferred_element_type=F32)


def _modvec_kernel(c_ref, w_ref, b_ref, o_ref):
    c = c_ref[...]
    a = c * jax.nn.sigmoid(c)
    o_ref[...] = jnp.dot(a, w_ref[...], preferred_element_type=F32, precision=HI) + b_ref[...]


def _modvec(cvec, mod_w, mod_b):
    n_layers, d, n6 = mod_w.shape
    tn = min(1024, n6)
    return pl.pallas_call(
        _modvec_kernel,
        grid=(n_layers, n6 // tn),
        in_specs=[pl.BlockSpec((8, d), lambda l, j: (0, 0)),
                  pl.BlockSpec((None, d, tn), lambda l, j: (l, 0, j)),
                  pl.BlockSpec((None, 1, tn), lambda l, j: (l, 0, j))],
        out_specs=pl.BlockSpec((None, 8, tn), lambda l, j: (l, 0, j)),
        out_shape=jax.ShapeDtypeStruct((n_layers, 8, n6), F32),
        compiler_params=_cp(("parallel", "parallel")),
        name="modvec",
    )(cvec, mod_w, mod_b.reshape(n_layers, 1, n6))


class _Rows:
    def __init__(self, batch, seq, ctx_len, tm):
        assert seq % tm == 0 and (batch * ctx_len) % tm == 0
        self.tm = tm
        self.batch = batch
        self.per_batch = seq // tm
        self.n_lat = batch * seq // tm
        self.n_ctx = batch * ctx_len // tm
        self.n_all = self.n_lat + self.n_ctx

    def mod_row(self, i):
        return jnp.where(i < self.n_lat, i // self.per_batch, self.batch)


def _mod_spec(rows, layer, chunk, d):
    return pl.BlockSpec((None, None, None, 1, d), lambda i, *_: (layer, rows.mod_row(i), chunk, 0, 0))


def _norm_mod_kernel(xl_ref, xc_ref, g_ref, sh_ref, sc_ref, o_ref, *, n_lat):
    i = pl.program_id(0)

    def body(x_ref):
        y = _rms(x_ref[...], g_ref[...])
        o_ref[...] = (y * (1.0 + sc_ref[...]) + sh_ref[...]).astype(o_ref.dtype)

    @pl.when(i < n_lat)
    def _():
        body(xl_ref)

    @pl.when(i >= n_lat)
    def _():
        body(xc_ref)


def _norm_mod(x_lat, x_ctx, ctx_block0, g, mods5, layer, rows):
    d = x_lat.shape[-1]
    tm = rows.tm
    nl = rows.n_lat
    return pl.pallas_call(
        functools.partial(_norm_mod_kernel, n_lat=nl),
        grid=(rows.n_all,),
        in_specs=[pl.BlockSpec((tm, d), lambda i: (jnp.minimum(i, nl - 1), 0)),
                  pl.BlockSpec((tm, d), lambda i: (ctx_block0 + jnp.maximum(i - nl, 0), 0)),
                  pl.BlockSpec((1, d), lambda i: (0, 0)),
                  _mod_spec(rows, layer, 0, d),
                  _mod_spec(rows, layer, 1, d)],
        out_specs=pl.BlockSpec((tm, d), lambda i: (i, 0)),
        out_shape=jax.ShapeDtypeStruct((rows.n_all * tm, d), BF16),
        compiler_params=_cp(("parallel",)),
        name="norm_mod",
    )(x_lat, x_ctx, g.reshape(1, d), mods5, mods5)


def _mm_kernel(a_ref, w_ref, o_ref):
    o_ref[...] = jnp.dot(a_ref[...], w_ref[...].astype(BF16), preferred_element_type=F32).astype(o_ref.dtype)


def _matmul(a, w3, layer, n_cols, tn, tm):
    m, k = a.shape
    return pl.pallas_call(
        _mm_kernel,
        grid=(m // tm, n_cols // tn),
        in_specs=[pl.BlockSpec((tm, k), lambda i, j: (i, 0)),
                  pl.BlockSpec((None, k, tn), lambda i, j: (layer, 0, j))],
        out_specs=pl.BlockSpec((tm, tn), lambda i, j: (i, j)),
        out_shape=jax.ShapeDtypeStruct((m, n_cols), BF16),
        compiler_params=_cp(("parallel", "arbitrary")),
        name="in_proj",
    )(a, w3)


def _out_proj_kernel(ma_ref, mb_ref, wa_ref, wb_ref, xl_ref, xc_ref, g_ref, o_ref, *, n_lat):
    i = pl.program_id(0)
    acc = jnp.dot(ma_ref[...], wa_ref[...].astype(BF16), preferred_element_type=F32)
    acc += jnp.dot(mb_ref[...], wb_ref[...].astype(BF16), preferred_element_type=F32)
    upd = g_ref[...] * acc

    @pl.when(i < n_lat)
    def _():
        o_ref[...] = xl_ref[...] + upd

    @pl.when(i >= n_lat)
    def _():
        o_ref[...] = xc_ref[...] + upd


def _out_proj(mix_a, mix_b, w_out3, widx, x_lat, x_ctx, ctx_block0, mods5, layer, rows, n_tiles):
    d = x_lat.shape[-1]
    ka, kb = mix_a.shape[-1], mix_b.shape[-1]
    assert ka == kb
    tm = rows.tm
    tn = min(512, d)
    nl = rows.n_lat
    return pl.pallas_call(
        functools.partial(_out_proj_kernel, n_lat=nl),
        grid=(n_tiles, d // tn),
        in_specs=[pl.BlockSpec((tm, ka), lambda i, j: (i, 0)),
                  pl.BlockSpec((tm, kb), lambda i, j: (i, 0)),
                  pl.BlockSpec((None, ka, tn), lambda i, j: (2 * widx, 0, j)),
                  pl.BlockSpec((None, kb, tn), lambda i, j: (2 * widx + 1, 0, j)),
                  pl.BlockSpec((tm, tn), lambda i, j: (jnp.minimum(i, nl - 1), j)),
                  pl.BlockSpec((tm, tn), lambda i, j: (ctx_block0 + jnp.maximum(i - nl, 0), j)),
                  pl.BlockSpec((None, None, None, 1, tn), lambda i, j: (layer, rows.mod_row(i), 2, 0, j))],
        out_specs=pl.BlockSpec((tm, tn), lambda i, j: (i, j)),
        out_shape=jax.ShapeDtypeStruct((n_tiles * tm, d), F32),
        compiler_params=_cp(("parallel", "arbitrary")),
        name="out_proj",
    )(mix_a, mix_b, w_out3, w_out3, x_lat, x_ctx, mods5)


def _tri(c, upper):
    r = lax.broadcasted_iota(jnp.int32, (c, c), 0)
    s = lax.broadcasted_iota(jnp.int32, (c, c), 1)
    return (s >= r) if upper else (r >= s)


def _chunk_step(q, k, v, g, st, tri_bf, mask, mid, last):
    g_hi = g.astype(BF16)
    g_lo = (g - g_hi.astype(F32)).astype(BF16)
    cum = jnp.dot(tri_bf, g_hi, preferred_element_type=F32) + jnp.dot(tri_bf, g_lo, preferred_element_type=F32)
    m = cum[mid:mid + 1, :]
    tot = cum[last:last + 1, :]
    qe = (q * jnp.exp(cum - m)).astype(BF16)
    ke = (k * jnp.exp(m - cum)).astype(BF16)
    a = jnp.where(mask, _dot_nt(qe, ke), 0.0).astype(BF16)
    stp = st * jnp.exp(m)
    o = jnp.dot(a, v, preferred_element_type=F32) + _dot_nt(qe, stp.astype(BF16))
    new = (stp + _dot_tn(v, ke)) * jnp.exp(tot - m)
    return o, new


def _scan_segments(segments, prep_f, prep_b, of_ref, ob_ref, dk, dv):
    c = SCAN_C
    low, up = _tri(c, False), _tri(c, True)
    low_f, up_f = low.astype(BF16), up.astype(BF16)
    carry = (jnp.zeros((dv, dk), F32), jnp.zeros((dv, dk), F32))
    for n, off, seg in segments:
        def body(i, carry, n=n, off=off, seg=seg):
            sf, sb = carry
            rf = pl.multiple_of(i * c, c)
            rb = pl.multiple_of((n - 1 - i) * c, c)
            q, k, v, g = prep_f(seg, rf)
            o, sf = _chunk_step(q, k, v, g, sf, low_f, low, c // 2 - 1, c - 1)
            of_ref[pl.ds(off + rf, c), :] = o
            q, k, v, g = prep_b(seg, rb)
            o, sb = _chunk_step(q, k, v, g, sb, up_f, up, c // 2, 0)
            ob_ref[pl.ds(off + rb, c), :] = o
            return sf, sb

        carry = lax.fori_loop(0, n, body, carry, unroll=SCAN_UNROLL)


def _hgrn_kernel(ql, qc, f1l, f1c, f2l, f2c, vl, vc, gl, gc, lbf, lbb, ng, o_ref, of_s, ob_s, *, ctx_len, seq):
    rt = pl.program_id(2)
    n_ctx_tiles = ctx_len // TQ

    @pl.when(rt == 0)
    def _scan():
        refs = {0: (qc, f1c, f2c, vc), 1: (ql, f1l, f2l, vl)}
        scale = A_DK ** -0.5

        def prep(seg, r, fi, lb_ref):
            x = refs[seg][0][pl.ds(r, SCAN_C), :].astype(F32)
            q = x * jax.nn.sigmoid(x) * scale
            v = refs[seg][3][pl.ds(r, SCAN_C), :]
            lb = lb_ref[...]
            f = lb + (1.0 - lb) * jax.nn.sigmoid(refs[seg][fi][pl.ds(r, SCAN_C), :].astype(F32))
            return q, 1.0 - f, v, jnp.log(f)

        _scan_segments(
            [(ctx_len // SCAN_C, 0, 0), (seq // SCAN_C, ctx_len, 1)],
            lambda seg, r: prep(seg, r, 1, lbf),
            lambda seg, r: prep(seg, r, 2, lbb),
            of_s, ob_s, A_DK, A_DV)

    r0 = pl.multiple_of(rt * TQ, TQ)
    o = of_s[pl.ds(r0, TQ), :] + ob_s[pl.ds(r0, TQ), :]
    y = _rms(o, ng[...])

    @pl.when(rt < n_ctx_tiles)
    def _():
        gate = gc[pl.ds(r0, TQ), :].astype(F32)
        o_ref[...] = (y * jax.nn.sigmoid(gate)).astype(o_ref.dtype)

    @pl.when(rt >= n_ctx_tiles)
    def _():
        gate = gl[pl.ds(pl.multiple_of(r0 - ctx_len, TQ), TQ), :].astype(F32)
        o_ref[...] = (y * jax.nn.sigmoid(gate)).astype(o_ref.dtype)


def _out_row_block(batch, seq, ctx_len):
    nct = ctx_len // TQ
    nlt = seq // TQ

    def f(b, rt):
        return jnp.where(rt < nct, batch * nlt + b * nct + rt, b * nlt + rt - nct)

    return f


def _hgrn(p, lb_f, lb_b, norm_g, batch, seq, ctx_len):
    h, dk, dv = A_HEADS, A_DK, A_DV
    nct, nlt = ctx_len // TQ, seq // TQ
    ctx_blk0 = batch * seq // ctx_len
    row_block = _out_row_block(batch, seq, ctx_len)
    in_specs = []
    for kcol in range(5):
        in_specs.append(pl.BlockSpec((seq, dk), lambda b, hh, rt, kcol=kcol: (b, kcol * h + hh)))
        in_specs.append(pl.BlockSpec((ctx_len, dk), lambda b, hh, rt, kcol=kcol: (ctx_blk0 + b, kcol * h + hh)))
    vec = pl.BlockSpec((1, dk), lambda b, hh, rt: (0, hh))
    in_specs += [vec, vec, pl.BlockSpec((1, dv), lambda b, hh, rt: (0, 0))]
    return pl.pallas_call(
        functools.partial(_hgrn_kernel, ctx_len=ctx_len, seq=seq),
        grid=(batch, h, nct + nlt),
        in_specs=in_specs,
        out_specs=pl.BlockSpec((TQ, dv), lambda b, hh, rt: (row_block(b, rt), hh)),
        out_shape=jax.ShapeDtypeStruct((batch * (seq + ctx_len), h * dv), BF16),
        scratch_shapes=[pltpu.VMEM((seq + ctx_len, dv), F32), pltpu.VMEM((seq + ctx_len, dv), F32)],
        compiler_params=_cp(("parallel", "parallel", "arbitrary")),
        name="hgrn_scan",
    )(*([p] * 10), lb_f.reshape(1, h * dk), lb_b.reshape(1, h * dk), norm_g.reshape(1, dv))


def _gla_kernel(ql, qc, kl, kc, vl, vc, gl, al, ac, wa, ba, ng, o_ref, of_s, ob_s, *, ctx_len, seq):
    rt = pl.program_id(2)

    @pl.when(rt == 0)
    def _scan():
        refs = {0: (qc, kc, vc, ac), 1: (ql, kl, vl, al)}
        scale = D_DK ** -0.5
        r16 = D_GATE_RANK

        def prep(seg, r, d):
            q = refs[seg][0][pl.ds(r, SCAN_C), :].astype(F32) * scale
            k = refs[seg][1][pl.ds(r, SCAN_C), :].astype(F32)
            v = refs[seg][2][pl.ds(r, SCAN_C), :]
            a = refs[seg][3][pl.ds(r, SCAN_C), :].astype(F32)[:, d * r16:(d + 1) * r16]
            z = jnp.dot(a, wa[d], preferred_element_type=F32, precision=HI) + ba[d]
            g = (jnp.minimum(z, 0.0) - jnp.log(1.0 + jnp.exp(-jnp.abs(z)))) * (1.0 / GLA_TAU)
            return q, k, v, g

        _scan_segments(
            [(ctx_len // SCAN_C, 0, 0), (seq // SCAN_C, ctx_len, 1)],
            lambda seg, r: prep(seg, r, 0),
            lambda seg, r: prep(seg, r, 1),
            of_s, ob_s, D_DK, D_DV)

    r0 = pl.multiple_of(rt * TQ, TQ)
    o = of_s[pl.ds(ctx_len + r0, TQ), :] + ob_s[pl.ds(ctx_len + r0, TQ), :]
    gate = gl[pl.ds(r0, TQ), :].astype(F32)
    o_ref[...] = (_rms(o, ng[...]) * gate * jax.nn.sigmoid(gate)).astype(o_ref.dtype)


def _gla(p, ga, w_a2, b_a, norm_g, batch, seq, ctx_len):
    h, dk, dv = D_HEADS, D_DK, D_DV
    nlt = seq // TQ
    ctx_blk0 = batch * seq // ctx_len
    q0 = (C_HEADS + 2 * C_KV_HEADS) * C_DH // dk
    k0 = q0 + h
    v0 = (k0 + h) * dk // dv
    g0 = v0 + h

    def pair(width, blk0):
        return [pl.BlockSpec((seq, width), lambda b, hh, rt: (b, blk0 + hh)),
                pl.BlockSpec((ctx_len, width), lambda b, hh, rt: (ctx_blk0 + b, blk0 + hh))]

    in_specs = pair(dk, q0) + pair(dk, k0) + pair(dv, v0)
    in_specs += [pl.BlockSpec((seq, dv), lambda b, hh, rt: (b, g0 + hh)),
                 pl.BlockSpec((seq, 2 * D_GATE_RANK), lambda b, hh, rt: (b, 0)),
                 pl.BlockSpec((ctx_len, 2 * D_GATE_RANK), lambda b, hh, rt: (ctx_blk0 + b, 0)),
                 pl.BlockSpec((2, D_GATE_RANK, dk), lambda b, hh, rt: (0, 0, hh)),
                 pl.BlockSpec((2, 1, dk), lambda b, hh, rt: (0, 0, hh)),
                 pl.BlockSpec((1, dv), lambda b, hh, rt: (0, 0))]
    return pl.pallas_call(
        functools.partial(_gla_kernel, ctx_len=ctx_len, seq=seq),
        grid=(batch, h, nlt),
        in_specs=in_specs,
        out_specs=pl.BlockSpec((TQ, dv), lambda b, hh, rt: (b * nlt + rt, hh)),
        out_shape=jax.ShapeDtypeStruct((batch * seq, h * dv), BF16),
        scratch_shapes=[pltpu.VMEM((seq + ctx_len, dv), F32), pltpu.VMEM((seq + ctx_len, dv), F32)],
        compiler_params=_cp(("parallel", "parallel", "arbitrary")),
        name="gla_scan",
    )(p, p, p, p, p, p, p, ga, ga, w_a2, b_a.reshape(2, 1, h * dk), norm_g.reshape(1, dv))


def _softmax_pv(s, v):
    m = jnp.max(s, axis=-1, keepdims=True)
    p = jnp.exp(s - m)
    l = jnp.sum(p, axis=-1, keepdims=True)
    return jnp.dot(p.astype(BF16), v, preferred_element_type=F32) / l


def _mla_kernel(ql_ref, kvl_ref, kvc_ref, krl_ref, krc_ref, wq_ref, wkv_ref, gq_ref, gkv_ref,
                cosq_ref, sinq_ref, cosk_ref, sink_ref, o_ref, kn_s, kr_s, v_s, *, ctx_len):
    qt = pl.program_id(2)
    n_ctx_tiles = ctx_len // TQ
    scale = (B_NOPE + B_ROPE) ** -0.5

    @pl.when(qt == 0)
    def _prep():
        wkv = wkv_ref[...].astype(BF16)

        def up(x_ref):
            xn = _rms(x_ref[...].astype(F32), gkv_ref[...])
            return jnp.dot(xn.astype(BF16), wkv, preferred_element_type=F32)

        kvc = up(kvc_ref)
        kn_s[0:ctx_len, :] = kvc[:, :B_NOPE].astype(BF16)
        v_s[0:ctx_len, :] = kvc[:, B_NOPE:].astype(BF16)
        kvl = up(kvl_ref)
        kn_s[ctx_len:, :] = kvl[:, :B_NOPE].astype(BF16)
        v_s[ctx_len:, :] = kvl[:, B_NOPE:].astype(BF16)
        kr_s[0:ctx_len, :] = krc_ref[...]
        kr_s[ctx_len:, :] = _rope(krl_ref[...].astype(F32), cosk_ref[...], sink_ref[...]).astype(BF16)

    xn = _rms(ql_ref[...].astype(F32), gq_ref[...])
    q = jnp.dot(xn.astype(BF16), wq_ref[...].astype(BF16), preferred_element_type=F32) * scale
    qn = q[:, :B_NOPE].astype(BF16)
    qr = q[:, B_NOPE:]

    @pl.when(qt < n_ctx_tiles)
    def _():
        s = _dot_nt(qn, kn_s[0:ctx_len, :]) + _dot_nt(qr.astype(BF16), kr_s[0:ctx_len, :])
        o_ref[...] = _softmax_pv(s, v_s[0:ctx_len, :]).astype(o_ref.dtype)

    @pl.when(qt >= n_ctx_tiles)
    def _():
        qrr = _rope(qr, cosq_ref[...], sinq_ref[...]).astype(BF16)
        s = _dot_nt(qn, kn_s[...]) + _dot_nt(qrr, kr_s[...])
        o_ref[...] = _softmax_pv(s, v_s[...]).astype(o_ref.dtype)


def _mla(p, kr, w_uq_h, w_ukv_h, gq, gkv, cos, sin, batch, seq, ctx_len):
    h = B_HEADS
    nct, nlt = ctx_len // TQ, seq // TQ
    ctx_blk0 = batch * seq // ctx_len
    row_block = _out_row_block(batch, seq, ctx_len)
    ql_blk = 5 * A_HEADS * A_DK // B_Q_LORA
    kv_blk = (5 * A_HEADS * A_DK + B_Q_LORA) // B_KV_LORA
    dq = B_NOPE + B_ROPE
    dkv = B_NOPE + B_DV
    s_all = seq + ctx_len
    in_specs = [
        pl.BlockSpec((TQ, B_Q_LORA), lambda b, hh, qt: (row_block(b, qt), ql_blk)),
        pl.BlockSpec((seq, B_KV_LORA), lambda b, hh, qt: (b, kv_blk)),
        pl.BlockSpec((ctx_len, B_KV_LORA), lambda b, hh, qt: (ctx_blk0 + b, kv_blk)),
        pl.BlockSpec((seq, B_ROPE), lambda b, hh, qt: (b, 0)),
        pl.BlockSpec((ctx_len, B_ROPE), lambda b, hh, qt: (ctx_blk0 + b, 0)),
        pl.BlockSpec((None, B_Q_LORA, dq), lambda b, hh, qt: (hh, 0, 0)),
        pl.BlockSpec((None, B_KV_LORA, dkv), lambda b, hh, qt: (hh, 0, 0)),
        pl.BlockSpec((1, B_Q_LORA), lambda b, hh, qt: (0, 0)),
        pl.BlockSpec((1, B_KV_LORA), lambda b, hh, qt: (0, 0)),
        pl.BlockSpec((TQ, B_ROPE), lambda b, hh, qt: (jnp.maximum(qt - nct, 0), 0)),
        pl.BlockSpec((TQ, B_ROPE), lambda b, hh, qt: (jnp.maximum(qt - nct, 0), 0)),
        pl.BlockSpec((seq, B_ROPE), lambda b, hh, qt: (0, 0)),
        pl.BlockSpec((seq, B_ROPE), lambda b, hh, qt: (0, 0)),
    ]
    return pl.pallas_call(
        functools.partial(_mla_kernel, ctx_len=ctx_len),
        grid=(batch, h, nct + nlt),
        in_specs=in_specs,
        out_specs=pl.BlockSpec((TQ, B_DV), lambda b, hh, qt: (row_block(b, qt), hh)),
        out_shape=jax.ShapeDtypeStruct((batch * s_all, h * B_DV), BF16),
        scratch_shapes=[pltpu.VMEM((s_all, B_NOPE), BF16), pltpu.VMEM((s_all, B_ROPE), BF16),
                        pltpu.VMEM((s_all, B_DV), BF16)],
        compiler_params=_cp(("parallel", "parallel", "arbitrary")),
        name="mla_attn",
    )(p, p, p, kr, kr, w_uq_h, w_ukv_h, gq.reshape(1, -1), gkv.reshape(1, -1), cos, sin, cos, sin)


def _gqa_kernel(q_ref, kl_ref, kc_ref, vl_ref, vc_ref, gq_ref, gk_ref, cosq_ref, sinq_ref, cosk_ref, sink_ref,
                o_ref, k_s, v_s, *, ctx_len):
    qi = pl.program_id(2)
    scale = C_DH ** -0.5

    @pl.when(qi == 0)
    def _prep():
        k_s[0:ctx_len, :] = _rms(kc_ref[...].astype(F32), gk_ref[...]).astype(BF16)
        kl = _rms(kl_ref[...].astype(F32), gk_ref[...])
        k_s[ctx_len:, :] = _rope(kl, cosk_ref[...], sink_ref[...]).astype(BF16)
        v_s[0:ctx_len, :] = vc_ref[...]
        v_s[ctx_len:, :] = vl_ref[...]

    q = _rope(_rms(q_ref[...].astype(F32), gq_ref[...]), cosq_ref[...], sinq_ref[...]) * scale
    s = _dot_nt(q.astype(BF16), k_s[...])
    o_ref[...] = _softmax_pv(s, v_s[...]).astype(o_ref.dtype)


def _gqa(p, gq, gk, cos, sin, batch, seq, ctx_len):
    kvh, grp, dh = C_KV_HEADS, C_HEADS // C_KV_HEADS, C_DH
    nlt = seq // TQ
    ctx_blk0 = batch * seq // ctx_len
    k0 = C_HEADS
    v0 = C_HEADS + C_KV_HEADS
    s_all = seq + ctx_len
    in_specs = [
        pl.BlockSpec((TQ, dh), lambda b, kh, qi: (b * nlt + qi % nlt, kh * grp + qi // nlt)),
        pl.BlockSpec((seq, dh), lambda b, kh, qi: (b, k0 + kh)),
        pl.BlockSpec((ctx_len, dh), lambda b, kh, qi: (ctx_blk0 + b, k0 + kh)),
        pl.BlockSpec((seq, dh), lambda b, kh, qi: (b, v0 + kh)),
        pl.BlockSpec((ctx_len, dh), lambda b, kh, qi: (ctx_blk0 + b, v0 + kh)),
        pl.BlockSpec((1, dh), lambda b, kh, qi: (0, 0)),
        pl.BlockSpec((1, dh), lambda b, kh, qi: (0, 0)),
        pl.BlockSpec((TQ, dh), lambda b, kh, qi: (qi % nlt, 0)),
        pl.BlockSpec((TQ, dh), lambda b, kh, qi: (qi % nlt, 0)),
        pl.BlockSpec((seq, dh), lambda b, kh, qi: (0, 0)),
        pl.BlockSpec((seq, dh), lambda b, kh, qi: (0, 0)),
    ]
    return pl.pallas_call(
        functools.partial(_gqa_kernel, ctx_len=ctx_len),
        grid=(batch, kvh, grp * nlt),
        in_specs=in_specs,
        out_specs=pl.BlockSpec((TQ, dh), lambda b, kh, qi: (b * nlt + qi % nlt, kh * grp + qi // nlt)),
        out_shape=jax.ShapeDtypeStruct((batch * seq, C_HEADS * dh), BF16),
        scratch_shapes=[pltpu.VMEM((s_all, dh), BF16), pltpu.VMEM((s_all, dh), BF16)],
        compiler_params=_cp(("parallel", "parallel", "arbitrary")),
        name="gqa_attn",
    )(p, p, p, p, p, gq.reshape(1, dh), gk.reshape(1, dh), cos, sin, cos, sin)


def _router_kernel(x_ref, g_ref, sh_ref, sc_ref, rw_ref, rb_ref, h_ref, ri_ref, rf_ref, cnt_ref, base_s):
    i = pl.program_id(0)
    tm = x_ref.shape[0]
    ne = N_EXPERTS
    per = ne // N_GROUPS

    @pl.when(i == 0)
    def _():
        base_s[...] = jnp.zeros_like(base_s)

    h = _rms(x_ref[...], g_ref[...]) * (1.0 + sc_ref[...]) + sh_ref[...]
    h_ref[...] = h.astype(h_ref.dtype)
    logits = jnp.dot(h, rw_ref[...], preferred_element_type=F32, precision=HI)
    scores = jax.nn.sigmoid(logits)
    sel = scores + rb_ref[...]
    lane = lax.broadcasted_iota(jnp.int32, (tm, ne), 1).astype(F32)
    neg = -jnp.inf
    big = float(ne)

    def top2(vals):
        m1 = jnp.max(vals, axis=1, keepdims=True)
        i1 = jnp.min(jnp.where(vals == m1, lane, big), axis=1, keepdims=True)
        rest = jnp.where(lane == i1, neg, vals)
        m2 = jnp.max(rest, axis=1, keepdims=True)
        i2 = jnp.min(jnp.where(rest == m2, lane, big), axis=1, keepdims=True)
        return m1 + m2, i1, i2

    best, e1, e2 = None, None, None
    for grp in range(N_GROUPS):
        in_grp = jnp.logical_and(lane >= float(grp * per), lane < float((grp + 1) * per))
        gsum, i1, i2 = top2(jnp.where(in_grp, sel, neg))
        if grp == 0:
            best, e1, e2 = gsum, i1, i2
        else:
            better = gsum > best
            best = jnp.where(better, gsum, best)
            e1 = jnp.where(better, i1, e1)
            e2 = jnp.where(better, i2, e2)

    hot1 = lane == e1
    hot2 = lane == e2
    w1 = jnp.sum(jnp.where(hot1, scores, 0.0), axis=1, keepdims=True)
    w2 = jnp.sum(jnp.where(hot2, scores, 0.0), axis=1, keepdims=True)
    wsum = w1 + w2
    assign = jnp.logical_or(hot1, hot2)
    r = lax.broadcasted_iota(jnp.int32, (tm, tm), 0)
    c = lax.broadcasted_iota(jnp.int32, (tm, tm), 1)
    before = (c < r).astype(BF16)
    excl = jnp.dot(before, assign.astype(BF16), preferred_element_type=F32) + base_s[...]
    rank1 = jnp.sum(jnp.where(hot1, excl, 0.0), axis=1, keepdims=True)
    rank2 = jnp.sum(jnp.where(hot2, excl, 0.0), axis=1, keepdims=True)
    base_s[...] = base_s[...] + jnp.sum(assign.astype(F32), axis=0, keepdims=True)

    l128 = lax.broadcasted_iota(jnp.int32, (tm, 128), 1)
    ri = jnp.where(l128 == 0, e1, jnp.where(l128 == 1, e2, jnp.where(l128 == 2, rank1, jnp.where(l128 == 3, rank2, 0.0))))
    ri_ref[...] = ri.astype(jnp.int32)
    rf_ref[...] = jnp.where(l128 == 0, w1 / wsum, jnp.where(l128 == 1, w2 / wsum, 0.0))
    cnt_ref[...] = jnp.broadcast_to(base_s[...], cnt_ref.shape)


def _router(x_all, g, mods5, layer, router_w, router_b, rows, n_tiles):
    d = x_all.shape[-1]
    tm = rows.tm
    n = n_tiles * tm
    ne = N_EXPERTS
    return pl.pallas_call(
        _router_kernel,
        grid=(n_tiles,),
        in_specs=[pl.BlockSpec((tm, d), lambda i: (i, 0)),
                  pl.BlockSpec((1, d), lambda i: (0, 0)),
                  _mod_spec(rows, layer, 3, d),
                  _mod_spec(rows, layer, 4, d),
                  pl.BlockSpec((d, ne), lambda i: (0, 0)),
                  pl.BlockSpec((1, ne), lambda i: (0, 0))],
        out_specs=[pl.BlockSpec((tm, d), lambda i: (i, 0)),
                   pl.BlockSpec((tm, 128), lambda i: (i, 0)),
                   pl.BlockSpec((tm, 128), lambda i: (i, 0)),
                   pl.BlockSpec((8, ne), lambda i: (0, 0))],
        out_shape=[jax.ShapeDtypeStruct((n, d), F32),
                   jax.ShapeDtypeStruct((n, 128), jnp.int32),
                   jax.ShapeDtypeStruct((n, 128), F32),
                   jax.ShapeDtypeStruct((8, ne), F32)],
        scratch_shapes=[pltpu.VMEM((1, ne), F32)],
        compiler_params=_cp(("arbitrary",)),
        name="moe_router",
    )(x_all, g.reshape(1, d), mods5, mods5, router_w, router_b.reshape(1, ne))


def _dispatch_kernel(dest_ref, h_ref, xs_in, xs_hbm, sem, *, tm, n_tok):
    del xs_in
    i = pl.program_id(0)

    def row_copy(t, dst):
        return pltpu.make_async_copy(h_ref.at[pl.ds(t, 1)], xs_hbm.at[pl.ds(dst, 1)], sem)

    def start(t, carry):
        tok = i * tm + t
        row_copy(t, dest_ref[tok]).start()
        row_copy(t, dest_ref[n_tok + tok]).start()
        return carry

    lax.fori_loop(0, tm, start, 0, unroll=DMA_UNROLL)
    tile_wait = pltpu.make_async_copy(h_ref, xs_hbm.at[pl.ds(0, tm)], sem)
    tile_wait.wait()
    tile_wait.wait()


def _dispatch(dest, h, n_tok, p_max):
    d = h.shape[-1]
    tm = TCOMB
    zeros = jnp.zeros((p_max, d), h.dtype)
    grid_spec = pltpu.PrefetchScalarGridSpec(
        num_scalar_prefetch=1,
        grid=(n_tok // tm,),
        in_specs=[pl.BlockSpec((tm, d), lambda i, dr: (i, 0)), pl.BlockSpec(memory_space=pl.ANY)],
        out_specs=pl.BlockSpec(memory_space=pl.ANY),
        scratch_shapes=[pltpu.SemaphoreType.DMA(())],
    )
    return pl.pallas_call(
        functools.partial(_dispatch_kernel, tm=tm, n_tok=n_tok),
        grid_spec=grid_spec,
        out_shape=jax.ShapeDtypeStruct((p_max, d), h.dtype),
        input_output_aliases={2: 0},
        compiler_params=_cp(("arbitrary",)),
        name="moe_dispatch",
    )(dest, h, zeros)


def _expert_kernel(te_ref, nu_ref, x_ref, w1_ref, w3_ref, w2_ref, y_ref, w1_s, w3_s, w2_s):
    r = pl.program_id(0)
    active = r < nu_ref[0]
    changed = jnp.logical_or(r == 0, te_ref[r] != te_ref[jnp.maximum(r - 1, 0)])

    @pl.when(jnp.logical_and(active, changed))
    def _():
        w1_s[...] = w1_ref[...].astype(BF16)
        w3_s[...] = w3_ref[...].astype(BF16)
        w2_s[...] = w2_ref[...].astype(BF16)

    @pl.when(active)
    def _():
        x = x_ref[...].astype(BF16)
        a = jnp.dot(x, w1_s[...], preferred_element_type=F32)
        b = jnp.dot(x, w3_s[...], preferred_element_type=F32)
        hid = (a * jax.nn.sigmoid(a) * b).astype(BF16)
        y_ref[...] = jnp.dot(hid, w2_s[...], preferred_element_type=F32)

    @pl.when(jnp.logical_not(active))
    def _():
        y_ref[...] = jnp.zeros_like(y_ref)


def _experts(tile_expert, n_used, xs, w1, w3, w2, layer):
    p_max, d = xs.shape
    f = w1.shape[-1]
    grid_spec = pltpu.PrefetchScalarGridSpec(
        num_scalar_prefetch=2,
        grid=(p_max // TE,),
        in_specs=[pl.BlockSpec((TE, d), lambda r, te, nu: (r, 0)),
                  pl.BlockSpec((None, None, d, f), lambda r, te, nu: (layer, te[r], 0, 0)),
                  pl.BlockSpec((None, None, d, f), lambda r, te, nu: (layer, te[r], 0, 0)),
                  pl.BlockSpec((None, None, f, d), lambda r, te, nu: (layer, te[r], 0, 0))],
        out_specs=pl.BlockSpec((TE, d), lambda r, te, nu: (r, 0)),
        scratch_shapes=[pltpu.VMEM((d, f), BF16), pltpu.VMEM((d, f), BF16), pltpu.VMEM((f, d), BF16)],
    )
    return pl.pallas_call(
        _expert_kernel,
        grid_spec=grid_spec,
        out_shape=jax.ShapeDtypeStruct((p_max, d), F32),
        compiler_params=_cp(("arbitrary",)),
        name="moe_experts",
    )(tile_expert, n_used, xs, w1, w3, w2)


def _combine_kernel(dest_ref, x_ref, rf_ref, g_ref, fg_ref, ys_hbm, o_ref, buf, sem, *, tm, n_tok, final):
    i = pl.program_id(0)

    def row_copy(src, slot, t):
        return pltpu.make_async_copy(ys_hbm.at[pl.ds(src, 1)], buf.at[slot, pl.ds(t, 1)], sem)

    def start(t, carry):
        tok = i * tm + t
        row_copy(dest_ref[tok], 0, t).start()
        row_copy(dest_ref[n_tok + tok], 1, t).start()
        return carry

    lax.fori_loop(0, tm, start, 0, unroll=DMA_UNROLL)
    pltpu.make_async_copy(ys_hbm.at[pl.ds(0, tm)], buf.at[0], sem).wait()
    pltpu.make_async_copy(ys_hbm.at[pl.ds(0, tm)], buf.at[1], sem).wait()
    w = rf_ref[...]
    y = w[:, 0:1] * buf[0] + w[:, 1:2] * buf[1]
    x2 = x_ref[...] + g_ref[...] * y
    if final:
        x2 = _rms(x2, fg_ref[...])
    o_ref[...] = x2


def _combine(dest, x_all, rf, mods5, layer, final_g, ys, rows, n_tiles, n_tok_total, final):
    d = x_all.shape[-1]
    tm = rows.tm
    grid_spec = pltpu.PrefetchScalarGridSpec(
        num_scalar_prefetch=1,
        grid=(n_tiles,),
        in_specs=[pl.BlockSpec((tm, d), lambda i, dr: (i, 0)),
                  pl.BlockSpec((tm, 128), lambda i, dr: (i, 0)),
                  _mod_spec(rows, layer, 5, d),
                  pl.BlockSpec((1, d), lambda i, dr: (0, 0)),
                  pl.BlockSpec(memory_space=pl.ANY)],
        out_specs=pl.BlockSpec((tm, d), lambda i, dr: (i, 0)),
        scratch_shapes=[pltpu.VMEM((2, tm, d), F32), pltpu.SemaphoreType.DMA(())],
    )
    return pl.pallas_call(
        functools.partial(_combine_kernel, tm=tm, n_tok=n_tok_total, final=final),
        grid_spec=grid_spec,
        out_shape=jax.ShapeDtypeStruct((n_tiles * tm, d), F32),
        compiler_params=_cp(("arbitrary",)),
        name="moe_combine",
    )(dest, x_all, rf, mods5, final_g.reshape(1, d), ys)


def _moe(x_all, n_tok, norm_g, mods5, layer, router_w, router_b, w1, w3, w2, final_g, final, batch, seq, ctx_len):
    rows_r = _Rows(batch, seq, ctx_len, TROUTE)
    rows_c = _Rows(batch, seq, ctx_len, TCOMB)
    h, ri, rf, cnt = _router(x_all, norm_g, mods5, layer, router_w, router_b, rows_r, n_tok // TROUTE)
    counts = cnt[0].astype(jnp.int32)
    padded = ((counts + TE - 1) // TE) * TE
    ends = jnp.cumsum(padded)
    starts = ends - padded
    e1, e2, r1, r2 = ri[:, 0], ri[:, 1], ri[:, 2], ri[:, 3]
    dest = jnp.concatenate([starts[e1] + r1, starts[e2] + r2]).astype(jnp.int32)
    p_max = 2 * n_tok + N_EXPERTS * TE
    n_tiles = p_max // TE
    n_used = (ends[-1] // TE).astype(jnp.int32)
    tile_start = jnp.arange(n_tiles, dtype=jnp.int32) * TE
    tile_expert = jnp.sum((tile_start[:, None] >= ends[None, :]).astype(jnp.int32), axis=1)
    last_expert = jnp.sum((jnp.maximum(ends[-1] - 1, 0) >= ends).astype(jnp.int32))
    tile_expert = jnp.minimum(jnp.where(tile_start < ends[-1], tile_expert, last_expert), N_EXPERTS - 1).astype(jnp.int32)
    xs = _dispatch(dest, h, n_tok, p_max)
    ys = _experts(tile_expert, n_used.reshape(1), xs, w1, w3, w2, layer)
    return _combine(dest, x_all, rf, mods5, layer, final_g, ys, rows_c, n_tok // TCOMB, n_tok, final)


def _rope_tables(t_len, d_rope):
    rows = t_len // GRID_W
    quarter = d_rope // 4
    freqs = ROPE_THETA ** (-jnp.arange(quarter, dtype=F32) / quarter)
    row = jnp.repeat(jnp.arange(rows, dtype=F32), GRID_W)
    col = jnp.tile(jnp.arange(GRID_W, dtype=F32), rows)
    ang = jnp.concatenate([row[:, None] * freqs, col[:, None] * freqs], axis=-1)
    cos, sin = jnp.cos(ang), jnp.sin(ang)
    return jnp.concatenate([cos, cos], axis=-1), jnp.concatenate([-sin, sin], axis=-1)


def kernel(x, c, ctx, c_ctx, mod_w, mod_b, norm_attn_g, norm_ffn_g, final_norm_g, ab_w_in, ab_w_out, hgrn_lb_logits, hgrn_norm_g, mla_q_norm_g, mla_w_uq, mla_kv_norm_g, mla_w_ukv, cd_w_in, cd_w_out, gqa_q_norm_g, gqa_k_norm_g, gla_w_a2, gla_b_a, gla_norm_g, router_w, router_b, moe_w1, moe_w3, moe_w2):
    batch, seq, d = x.shape
    ctx_len = ctx.shape[1]
    n_lat, n_ctx = batch * seq, batch * ctx_len
    assert ctx_len % TQ == 0 and seq % TQ == 0 and seq % ctx_len == 0 and batch < 8
    tm = min(1024, seq, n_ctx)
    rows = _Rows(batch, seq, ctx_len, tm)

    cvec = jnp.concatenate([c, c_ctx[None, :], jnp.zeros((8 - batch - 1, d), F32)], axis=0)
    mods = _modvec(cvec, mod_w, mod_b)
    mods5 = mods.reshape(mods.shape[0], 8, 6, 1, d)

    cos_b, sin_b = _rope_tables(seq, B_ROPE)
    cos_c, sin_c = _rope_tables(seq, C_DH)
    lb = jnp.cumsum(jax.nn.softmax(hgrn_lb_logits.astype(F32), axis=1), axis=1)

    x_lat = x.reshape(n_lat, d)
    x_ctx = ctx.reshape(n_ctx, d)

    h0 = _norm_mod(x_lat, x_ctx, 0, norm_attn_g[0], mods5, 0, rows)
    ab_main = 5 * A_HEADS * A_DK + B_Q_LORA + B_KV_LORA
    p0 = _matmul(h0, ab_w_in, 0, ab_main, 256, tm)
    kr0 = _matmul(h0, ab_w_in[:, :, ab_main:], 0, B_ROPE, B_ROPE, tm)
    mix_a = _hgrn(p0, lb[0, 0], lb[1, 0], hgrn_norm_g[0], batch, seq, ctx_len)
    dq = B_NOPE + B_ROPE
    dkv = B_NOPE + B_DV
    w_uq_h = mla_w_uq[0].reshape(B_Q_LORA, B_HEADS, dq).transpose(1, 0, 2)
    w_ukv_h = mla_w_ukv[0].reshape(B_KV_LORA, B_HEADS, dkv).transpose(1, 0, 2)
    mix_b = _mla(p0, kr0, w_uq_h, w_ukv_h, mla_q_norm_g[0], mla_kv_norm_g[0], cos_b, sin_b, batch, seq, ctx_len)
    half = mix_a.shape[-1]
    w_out0 = ab_w_out.reshape(ab_w_out.shape[0] * 2, half, d)
    x1 = _out_proj(mix_a, mix_b, w_out0, 0, x_lat, x_ctx, 0, mods5, 0, rows, rows.n_all)
    x2 = _moe(x1, n_lat + n_ctx, norm_ffn_g[0], mods5, 0, router_w, router_b, moe_w1, moe_w3, moe_w2,
              final_norm_g, False, batch, seq, ctx_len)

    h1 = _norm_mod(x2, x2, rows.n_lat, norm_attn_g[1], mods5, 1, rows)
    cd_main = (C_HEADS + 2 * C_KV_HEADS) * C_DH + 2 * D_HEADS * D_DK + 2 * D_HEADS * D_DV
    p1 = _matmul(h1, cd_w_in, 0, cd_main, 512, tm)
    ga1 = _matmul(h1, cd_w_in[:, :, cd_main:], 0, 2 * D_GATE_RANK, 2 * D_GATE_RANK, tm)
    mix_c = _gqa(p1, gqa_q_norm_g[0], gqa_k_norm_g[0], cos_c, sin_c, batch, seq, ctx_len)
    mix_d = _gla(p1, ga1, gla_w_a2[0], gla_b_a[0], gla_norm_g[0], batch, seq, ctx_len)
    w_out1 = cd_w_out.reshape(cd_w_out.shape[0] * 2, mix_c.shape[-1], d)
    x3 = _out_proj(mix_c, mix_d, w_out1, 0, x2, x2, rows.n_lat, mods5, 1, rows, rows.n_lat)
    out = _moe(x3, n_lat, norm_ffn_g[1], mods5, 1, router_w, router_b, moe_w1, moe_w3, moe_w2,
               final_norm_g, True, batch, seq, ctx_len)
    return out.reshape(batch, seq, d)
```

```python
import functools

import jax
import jax.numpy as jnp
from jax import lax
from jax.experimental import pallas as pl
from jax.experimental.pallas import tpu as pltpu

F32 = jnp.float32
BF16 = jnp.bfloat16
HI = lax.Precision.HIGHEST

GRID_W = 64
ROPE_THETA = 10000.0
NORM_EPS = 1e-6
A_HEADS, A_DK, A_DV = 8, 128, 128
B_HEADS, B_Q_LORA, B_KV_LORA, B_NOPE, B_ROPE, B_DV = 8, 512, 256, 128, 64, 128
C_HEADS, C_KV_HEADS, C_DH = 8, 2, 128
D_HEADS, D_DK, D_DV, D_GATE_RANK = 4, 128, 256, 16
GLA_TAU = 16.0
N_EXPERTS, N_GROUPS = 16, 4

TQ = 256
SCAN_C = 64
SCAN_BLOCK = 512
TE = 256
TROUTE = 512
TCOMB = 256
DMA_UNROLL = 8
VMEM_MIB = 56


def _cp(sem):
    return pltpu.CompilerParams(dimension_semantics=sem, vmem_limit_bytes=VMEM_MIB * 1024 * 1024)


def _rms(x, g):
    return x * lax.rsqrt(jnp.mean(x * x, axis=-1, keepdims=True) + NORM_EPS) * g


def _rope(x, cos, sin):
    half = x.shape[-1] // 2
    swapped = jnp.concatenate([x[:, half:], x[:, :half]], axis=-1)
    return x * cos + swapped * sin


def _dot_nt(a, b):
    return lax.dot_general(a, b, (((1,), (1,)), ((), ())), preferred_element_type=F32)


def _dot_tn(a, b):
    return lax.dot_general(a, b, (((0,), (0,)), ((), ())), preferred_element_type=F32)


def _modvec_kernel(c_ref, w_ref, b_ref, o_ref):
    c = c_ref[...]
    a = c * jax.nn.sigmoid(c)
    o_ref[...] = jnp.dot(a, w_ref[...], preferred_element_type=F32, precision=HI) + b_ref[...]


def _modvec(cvec, mod_w, mod_b):
    n_layers, d, n6 = mod_w.shape
    tn = min(1024, n6)
    return pl.pallas_call(
        _modvec_kernel,
        grid=(n_layers, n6 // tn),
        in_specs=[pl.BlockSpec((8, d), lambda l, j: (0, 0)),
                  pl.BlockSpec((None, d, tn), lambda l, j: (l, 0, j)),
                  pl.BlockSpec((None, 1, tn), lambda l, j: (l, 0, j))],
        out_specs=pl.BlockSpec((None, 8, tn), lambda l, j: (l, 0, j)),
        out_shape=jax.ShapeDtypeStruct((n_layers, 8, n6), F32),
        compiler_params=_cp(("parallel", "parallel")),
        name="modvec",
    )(cvec, mod_w, mod_b.reshape(n_layers, 1, n6))


class _Rows:
    def __init__(self, batch, seq, ctx_len, tm):
        assert seq % tm == 0 and (batch * ctx_len) % tm == 0
        self.tm = tm
        self.batch = batch
        self.per_batch = seq // tm
        self.n_lat = batch * seq // tm
        self.n_ctx = batch * ctx_len // tm
        self.n_all = self.n_lat + self.n_ctx

    def mod_row(self, i):
        return jnp.where(i < self.n_lat, i // self.per_batch, self.batch)


def _mod_spec(rows, layer, chunk, d):
    return pl.BlockSpec((None, None, None, 1, d), lambda i, *_: (layer, rows.mod_row(i), chunk, 0, 0))


def _norm_mod_kernel(xl_ref, xc_ref, g_ref, sh_ref, sc_ref, o_ref, *, n_lat):
    i = pl.program_id(0)

    def body(x_ref):
        y = _rms(x_ref[...], g_ref[...])
        o_ref[...] = (y * (1.0 + sc_ref[...]) + sh_ref[...]).astype(o_ref.dtype)

    @pl.when(i < n_lat)
    def _():
        body(xl_ref)

    @pl.when(i >= n_lat)
    def _():
        body(xc_ref)


def _norm_mod(x_lat, x_ctx, ctx_block0, g, mods5, layer, rows):
    d = x_lat.shape[-1]
    tm = rows.tm
    nl = rows.n_lat
    return pl.pallas_call(
        functools.partial(_norm_mod_kernel, n_lat=nl),
        grid=(rows.n_all,),
        in_specs=[pl.BlockSpec((tm, d), lambda i: (jnp.minimum(i, nl - 1), 0)),
                  pl.BlockSpec((tm, d), lambda i: (ctx_block0 + jnp.maximum(i - nl, 0), 0)),
                  pl.BlockSpec((1, d), lambda i: (0, 0)),
                  _mod_spec(rows, layer, 0, d),
                  _mod_spec(rows, layer, 1, d)],
        out_specs=pl.BlockSpec((tm, d), lambda i: (i, 0)),
        out_shape=jax.ShapeDtypeStruct((rows.n_all * tm, d), BF16),
        compiler_params=_cp(("parallel",)),
        name="norm_mod",
    )(x_lat, x_ctx, g.reshape(1, d), mods5, mods5)


def _mm_kernel(a_ref, w_ref, o_ref):
    o_ref[...] = jnp.dot(a_ref[...], w_ref[...].astype(BF16), preferred_element_type=F32).astype(o_ref.dtype)


def _matmul(a, w3, layer, n_cols, tn, tm):
    m, k = a.shape
    return pl.pallas_call(
        _mm_kernel,
        grid=(m // tm, n_cols // tn),
        in_specs=[pl.BlockSpec((tm, k), lambda i, j: (i, 0)),
                  pl.BlockSpec((None, k, tn), lambda i, j: (layer, 0, j))],
        out_specs=pl.BlockSpec((tm, tn), lambda i, j: (i, j)),
        out_shape=jax.ShapeDtypeStruct((m, n_cols), BF16),
        compiler_params=_cp(("parallel", "arbitrary")),
        name="in_proj",
    )(a, w3)


def _out_proj_kernel(ma_ref, mb_ref, wa_ref, wb_ref, xl_ref, xc_ref, g_ref, o_ref, *, n_lat):
    i = pl.program_id(0)
    acc = jnp.dot(ma_ref[...], wa_ref[...].astype(BF16), preferred_element_type=F32)
    acc += jnp.dot(mb_ref[...], wb_ref[...].astype(BF16), preferred_element_type=F32)
    upd = g_ref[...] * acc

    @pl.when(i < n_lat)
    def _():
        o_ref[...] = xl_ref[...] + upd

    @pl.when(i >= n_lat)
    def _():
        o_ref[...] = xc_ref[...] + upd


def _out_proj(mix_a, mix_b, w_out3, widx, x_lat, x_ctx, ctx_block0, mods5, layer, rows, n_tiles):
    d = x_lat.shape[-1]
    ka, kb = mix_a.shape[-1], mix_b.shape[-1]
    assert ka == kb
    tm = rows.tm
    tn = min(512, d)
    nl = rows.n_lat
    return pl.pallas_call(
        functools.partial(_out_proj_kernel, n_lat=nl),
        grid=(n_tiles, d // tn),
        in_specs=[pl.BlockSpec((tm, ka), lambda i, j: (i, 0)),
                  pl.BlockSpec((tm, kb), lambda i, j: (i, 0)),
                  pl.BlockSpec((None, ka, tn), lambda i, j: (2 * widx, 0, j)),
                  pl.BlockSpec((None, kb, tn), lambda i, j: (2 * widx + 1, 0, j)),
                  pl.BlockSpec((tm, tn), lambda i, j: (jnp.minimum(i, nl - 1), j)),
                  pl.BlockSpec((tm, tn), lambda i, j: (ctx_block0 + jnp.maximum(i - nl, 0), j)),
                  pl.BlockSpec((None, None, None, 1, tn), lambda i, j: (layer, rows.mod_row(i), 2, 0, j))],
        out_specs=pl.BlockSpec((tm, tn), lambda i, j: (i, j)),
        out_shape=jax.ShapeDtypeStruct((n_tiles * tm, d), F32),
        compiler_params=_cp(("parallel", "arbitrary")),
        name="out_proj",
    )(mix_a, mix_b, w_out3, w_out3, x_lat, x_ctx, mods5)


def _tri(c, upper):
    r = lax.broadcasted_iota(jnp.int32, (c, c), 0)
    s = lax.broadcasted_iota(jnp.int32, (c, c), 1)
    return (s >= r) if upper else (r >= s)


def _scan_block(q, k, v, g, st, mask, forward):
    c = SCAN_C
    dk, dv = q.shape[-1], v.shape[-1]
    n = q.shape[0] // c
    mid, last = (c // 2 - 1, c - 1) if forward else (c // 2, 0)
    tri = jnp.broadcast_to(mask.astype(BF16)[None], (n, c, c))
    g3 = g.reshape(n, c, dk)
    g_hi = g3.astype(BF16)
    g_lo = (g3 - g_hi.astype(F32)).astype(BF16)
    cum = (jnp.einsum('cts,csd->ctd', tri, g_hi, preferred_element_type=F32)
           + jnp.einsum('cts,csd->ctd', tri, g_lo, preferred_element_type=F32))
    m = cum[:, mid:mid + 1, :]
    tot = cum[:, last:last + 1, :]
    qe = (q.reshape(n, c, dk) * jnp.exp(cum - m)).astype(BF16)
    ke = (k.reshape(n, c, dk) * jnp.exp(m - cum)).astype(BF16)
    a = jnp.einsum('ctd,csd->cts', qe, ke, preferred_element_type=F32)
    a = jnp.where(mask[None], a, 0.0).astype(BF16)
    v3 = v.reshape(n, c, dv)
    o = jnp.einsum('cts,csv->ctv', a, v3, preferred_element_type=F32)
    u = jnp.einsum('csv,csd->cvd', v3, ke, preferred_element_type=F32)
    em = jnp.exp(m)
    et = jnp.exp(tot - m)
    states = [None] * n
    for ci in (range(n) if forward else reversed(range(n))):
        stp = st * em[ci]
        states[ci] = stp.astype(BF16)
        st = (stp + u[ci]) * et[ci]
    o = o + jnp.einsum('ctd,cvd->ctv', qe, jnp.stack(states), preferred_element_type=F32)
    return o.reshape(n * c, dv), st


def _scan_segments(segments, prep_f, prep_b, of_ref, ob_ref, dk, dv):
    low, up = _tri(SCAN_C, False), _tri(SCAN_C, True)
    carry = (jnp.zeros((dv, dk), F32), jnp.zeros((dv, dk), F32))
    for rows, off, seg in segments:
        rb_ = min(SCAN_BLOCK, rows)
        n = rows // rb_

        def body(i, carry, n=n, off=off, seg=seg, rb_=rb_):
            sf, sb = carry
            rf = pl.multiple_of(i * rb_, rb_)
            rb = pl.multiple_of((n - 1 - i) * rb_, rb_)
            q, k, v, g = prep_f(seg, rf, rb_)
            o, sf = _scan_block(q, k, v, g, sf, low, True)
            of_ref[pl.ds(off + rf, rb_), :] = o
            q, k, v, g = prep_b(seg, rb, rb_)
            o, sb = _scan_block(q, k, v, g, sb, up, False)
            ob_ref[pl.ds(off + rb, rb_), :] = o
            return sf, sb

        carry = lax.fori_loop(0, n, body, carry)


def _hgrn_kernel(ql, qc, f1l, f1c, f2l, f2c, vl, vc, gl, gc, lbf, lbb, ng, o_ref, of_s, ob_s, *, ctx_len, seq):
    rt = pl.program_id(2)
    n_ctx_tiles = ctx_len // TQ

    @pl.when(rt == 0)
    def _scan():
        refs = {0: (qc, f1c, f2c, vc), 1: (ql, f1l, f2l, vl)}
        scale = A_DK ** -0.5

        def prep(seg, r, nr, fi, lb_ref):
            x = refs[seg][0][pl.ds(r, nr), :].astype(F32)
            q = x * jax.nn.sigmoid(x) * scale
            v = refs[seg][3][pl.ds(r, nr), :]
            lb = lb_ref[...]
            f = lb + (1.0 - lb) * jax.nn.sigmoid(refs[seg][fi][pl.ds(r, nr), :].astype(F32))
            return q, 1.0 - f, v, jnp.log(f)

        _scan_segments(
            [(ctx_len, 0, 0), (seq, ctx_len, 1)],
            lambda seg, r, nr: prep(seg, r, nr, 1, lbf),
            lambda seg, r, nr: prep(seg, r, nr, 2, lbb),
            of_s, ob_s, A_DK, A_DV)

    r0 = pl.multiple_of(rt * TQ, TQ)
    o = of_s[pl.ds(r0, TQ), :] + ob_s[pl.ds(r0, TQ), :]
    y = _rms(o, ng[...])

    @pl.when(rt < n_ctx_tiles)
    def _():
        gate = gc[pl.ds(r0, TQ), :].astype(F32)
        o_ref[...] = (y * jax.nn.sigmoid(gate)).astype(o_ref.dtype)

    @pl.when(rt >= n_ctx_tiles)
    def _():
        gate = gl[pl.ds(pl.multiple_of(r0 - ctx_len, TQ), TQ), :].astype(F32)
        o_ref[...] = (y * jax.nn.sigmoid(gate)).astype(o_ref.dtype)


def _out_row_block(batch, seq, ctx_len):
    nct = ctx_len // TQ
    nlt = seq // TQ

    def f(b, rt):
        return jnp.where(rt < nct, batch * nlt + b * nct + rt, b * nlt + rt - nct)

    return f


def _hgrn(p, lb_f, lb_b, norm_g, batch, seq, ctx_len):
    h, dk, dv = A_HEADS, A_DK, A_DV
    nct, nlt = ctx_len // TQ, seq // TQ
    ctx_blk0 = batch * seq // ctx_len
    row_block = _out_row_block(batch, seq, ctx_len)
    in_specs = []
    for kcol in range(5):
        in_specs.append(pl.BlockSpec((seq, dk), lambda b, hh, rt, kcol=kcol: (b, kcol * h + hh)))
        in_specs.append(pl.BlockSpec((ctx_len, dk), lambda b, hh, rt, kcol=kcol: (ctx_blk0 + b, kcol * h + hh)))
    vec = pl.BlockSpec((1, dk), lambda b, hh, rt: (0, hh))
    in_specs += [vec, vec, pl.BlockSpec((1, dv), lambda b, hh, rt: (0, 0))]
    return pl.pallas_call(
        functools.partial(_hgrn_kernel, ctx_len=ctx_len, seq=seq),
        grid=(batch, h, nct + nlt),
        in_specs=in_specs,
        out_specs=pl.BlockSpec((TQ, dv), lambda b, hh, rt: (row_block(b, rt), hh)),
        out_shape=jax.ShapeDtypeStruct((batch * (seq + ctx_len), h * dv), BF16),
        scratch_shapes=[pltpu.VMEM((seq + ctx_len, dv), F32), pltpu.VMEM((seq + ctx_len, dv), F32)],
        compiler_params=_cp(("parallel", "parallel", "arbitrary")),
        name="hgrn_scan",
    )(*([p] * 10), lb_f.reshape(1, h * dk), lb_b.reshape(1, h * dk), norm_g.reshape(1, dv))


def _gla_kernel(ql, qc, kl, kc, vl, vc, gl, al, ac, wa, ba, ng, o_ref, of_s, ob_s, *, ctx_len, seq):
    rt = pl.program_id(2)

    @pl.when(rt == 0)
    def _scan():
        refs = {0: (qc, kc, vc, ac), 1: (ql, kl, vl, al)}
        scale = D_DK ** -0.5
        r16 = D_GATE_RANK

        def prep(seg, r, nr, d):
            q = refs[seg][0][pl.ds(r, nr), :].astype(F32) * scale
            k = refs[seg][1][pl.ds(r, nr), :].astype(F32)
            v = refs[seg][2][pl.ds(r, nr), :]
            a = refs[seg][3][pl.ds(r, nr), :].astype(F32)[:, d * r16:(d + 1) * r16]
            z = jnp.dot(a, wa[d], preferred_element_type=F32, precision=HI) + ba[d]
            g = (jnp.minimum(z, 0.0) - jnp.log(1.0 + jnp.exp(-jnp.abs(z)))) * (1.0 / GLA_TAU)
            return q, k, v, g

        _scan_segments(
            [(ctx_len, 0, 0), (seq, ctx_len, 1)],
            lambda seg, r, nr: prep(seg, r, nr, 0),
            lambda seg, r, nr: prep(seg, r, nr, 1),
            of_s, ob_s, D_DK, D_DV)

    r0 = pl.multiple_of(rt * TQ, TQ)
    o = of_s[pl.ds(ctx_len + r0, TQ), :] + ob_s[pl.ds(ctx_len + r0, TQ), :]
    gate = gl[pl.ds(r0, TQ), :].astype(F32)
    o_ref[...] = (_rms(o, ng[...]) * gate * jax.nn.sigmoid(gate)).astype(o_ref.dtype)


def _gla(p, ga, w_a2, b_a, norm_g, batch, seq, ctx_len):
    h, dk, dv = D_HEADS, D_DK, D_DV
    nlt = seq // TQ
    ctx_blk0 = batch * seq // ctx_len
    q0 = (C_HEADS + 2 * C_KV_HEADS) * C_DH // dk
    k0 = q0 + h
    v0 = (k0 + h) * dk // dv
    g0 = v0 + h

    def pair(width, blk0):
        return [pl.BlockSpec((seq, width), lambda b, hh, rt: (b, blk0 + hh)),
                pl.BlockSpec((ctx_len, width), lambda b, hh, rt: (ctx_blk0 + b, blk0 + hh))]

    in_specs = pair(dk, q0) + pair(dk, k0) + pair(dv, v0)
    in_specs += [pl.BlockSpec((seq, dv), lambda b, hh, rt: (b, g0 + hh)),
                 pl.BlockSpec((seq, 2 * D_GATE_RANK), lambda b, hh, rt: (b, 0)),
                 pl.BlockSpec((ctx_len, 2 * D_GATE_RANK), lambda b, hh, rt: (ctx_blk0 + b, 0)),
                 pl.BlockSpec((2, D_GATE_RANK, dk), lambda b, hh, rt: (0, 0, hh)),
                 pl.BlockSpec((2, 1, dk), lambda b, hh, rt: (0, 0, hh)),
                 pl.BlockSpec((1, dv), lambda b, hh, rt: (0, 0))]
    return pl.pallas_call(
        functools.partial(_gla_kernel, ctx_len=ctx_len, seq=seq),
        grid=(batch, h, nlt),
        in_specs=in_specs,
        out_specs=pl.BlockSpec((TQ, dv), lambda b, hh, rt: (b * nlt + rt, hh)),
        out_shape=jax.ShapeDtypeStruct((batch * seq, h * dv), BF16),
        scratch_shapes=[pltpu.VMEM((seq + ctx_len, dv), F32), pltpu.VMEM((seq + ctx_len, dv), F32)],
        compiler_params=_cp(("parallel", "parallel", "arbitrary")),
        name="gla_scan",
    )(p, p, p, p, p, p, p, ga, ga, w_a2, b_a.reshape(2, 1, h * dk), norm_g.reshape(1, dv))


def _softmax_pv(s, v):
    m = jnp.max(s, axis=-1, keepdims=True)
    p = jnp.exp(s - m)
    l = jnp.sum(p, axis=-1, keepdims=True)
    return jnp.dot(p.astype(BF16), v, preferred_element_type=F32) / l


def _mla_kernel(ql_ref, kvl_ref, kvc_ref, krl_ref, krc_ref, wq_ref, wkv_ref, gq_ref, gkv_ref,
                cosq_ref, sinq_ref, cosk_ref, sink_ref, o_ref, kn_s, kr_s, v_s, *, ctx_len):
    qt = pl.program_id(2)
    n_ctx_tiles = ctx_len // TQ
    scale = (B_NOPE + B_ROPE) ** -0.5

    @pl.when(qt == 0)
    def _prep():
        wkv = wkv_ref[...].astype(BF16)

        def up(x_ref):
            xn = _rms(x_ref[...].astype(F32), gkv_ref[...])
            return jnp.dot(xn.astype(BF16), wkv, preferred_element_type=F32)

        kvc = up(kvc_ref)
        kn_s[0:ctx_len, :] = kvc[:, :B_NOPE].astype(BF16)
        v_s[0:ctx_len, :] = kvc[:, B_NOPE:].astype(BF16)
        kvl = up(kvl_ref)
        kn_s[ctx_len:, :] = kvl[:, :B_NOPE].astype(BF16)
        v_s[ctx_len:, :] = kvl[:, B_NOPE:].astype(BF16)
        kr_s[0:ctx_len, :] = krc_ref[...]
        kr_s[ctx_len:, :] = _rope(krl_ref[...].astype(F32), cosk_ref[...], sink_ref[...]).astype(BF16)

    xn = _rms(ql_ref[...].astype(F32), gq_ref[...])
    q = jnp.dot(xn.astype(BF16), wq_ref[...].astype(BF16), preferred_element_type=F32) * scale
    qn = q[:, :B_NOPE].astype(BF16)
    qr = q[:, B_NOPE:]

    @pl.when(qt < n_ctx_tiles)
    def _():
        s = _dot_nt(qn, kn_s[0:ctx_len, :]) + _dot_nt(qr.astype(BF16), kr_s[0:ctx_len, :])
        o_ref[...] = _softmax_pv(s, v_s[0:ctx_len, :]).astype(o_ref.dtype)

    @pl.when(qt >= n_ctx_tiles)
    def _():
        qrr = _rope(qr, cosq_ref[...], sinq_ref[...]).astype(BF16)
        s = _dot_nt(qn, kn_s[...]) + _dot_nt(qrr, kr_s[...])
        o_ref[...] = _softmax_pv(s, v_s[...]).astype(o_ref.dtype)


def _mla(p, kr, w_uq_h, w_ukv_h, gq, gkv, cos, sin, batch, seq, ctx_len):
    h = B_HEADS
    nct, nlt = ctx_len // TQ, seq // TQ
    ctx_blk0 = batch * seq // ctx_len
    row_block = _out_row_block(batch, seq, ctx_len)
    ql_blk = 5 * A_HEADS * A_DK // B_Q_LORA
    kv_blk = (5 * A_HEADS * A_DK + B_Q_LORA) // B_KV_LORA
    dq = B_NOPE + B_ROPE
    dkv = B_NOPE + B_DV
    s_all = seq + ctx_len
    in_specs = [
        pl.BlockSpec((TQ, B_Q_LORA), lambda b, hh, qt: (row_block(b, qt), ql_blk)),
        pl.BlockSpec((seq, B_KV_LORA), lambda b, hh, qt: (b, kv_blk)),
        pl.BlockSpec((ctx_len, B_KV_LORA), lambda b, hh, qt: (ctx_blk0 + b, kv_blk)),
        pl.BlockSpec((seq, B_ROPE), lambda b, hh, qt: (b, 0)),
        pl.BlockSpec((ctx_len, B_ROPE), lambda b, hh, qt: (ctx_blk0 + b, 0)),
        pl.BlockSpec((None, B_Q_LORA, dq), lambda b, hh, qt: (hh, 0, 0)),
        pl.BlockSpec((None, B_KV_LORA, dkv), lambda b, hh, qt: (hh, 0, 0)),
        pl.BlockSpec((1, B_Q_LORA), lambda b, hh, qt: (0, 0)),
        pl.BlockSpec((1, B_KV_LORA), lambda b, hh, qt: (0, 0)),
        pl.BlockSpec((TQ, B_ROPE), lambda b, hh, qt: (jnp.maximum(qt - nct, 0), 0)),
        pl.BlockSpec((TQ, B_ROPE), lambda b, hh, qt: (jnp.maximum(qt - nct, 0), 0)),
        pl.BlockSpec((seq, B_ROPE), lambda b, hh, qt: (0, 0)),
        pl.BlockSpec((seq, B_ROPE), lambda b, hh, qt: (0, 0)),
    ]
    return pl.pallas_call(
        functools.partial(_mla_kernel, ctx_len=ctx_len),
        grid=(batch, h, nct + nlt),
        in_specs=in_specs,
        out_specs=pl.BlockSpec((TQ, B_DV), lambda b, hh, qt: (row_block(b, qt), hh)),
        out_shape=jax.ShapeDtypeStruct((batch * s_all, h * B_DV), BF16),
        scratch_shapes=[pltpu.VMEM((s_all, B_NOPE), BF16), pltpu.VMEM((s_all, B_ROPE), BF16),
                        pltpu.VMEM((s_all, B_DV), BF16)],
        compiler_params=_cp(("parallel", "parallel", "arbitrary")),
        name="mla_attn",
    )(p, p, p, kr, kr, w_uq_h, w_ukv_h, gq.reshape(1, -1), gkv.reshape(1, -1), cos, sin, cos, sin)


def _gqa_kernel(q_ref, kl_ref, kc_ref, vl_ref, vc_ref, gq_ref, gk_ref, cosq_ref, sinq_ref, cosk_ref, sink_ref,
                o_ref, k_s, v_s, *, ctx_len):
    qi = pl.program_id(2)
    scale = C_DH ** -0.5

    @pl.when(qi == 0)
    def _prep():
        k_s[0:ctx_len, :] = _rms(kc_ref[...].astype(F32), gk_ref[...]).astype(BF16)
        kl = _rms(kl_ref[...].astype(F32), gk_ref[...])
        k_s[ctx_len:, :] = _rope(kl, cosk_ref[...], sink_ref[...]).astype(BF16)
        v_s[0:ctx_len, :] = vc_ref[...]
        v_s[ctx_len:, :] = vl_ref[...]

    q = _rope(_rms(q_ref[...].astype(F32), gq_ref[...]), cosq_ref[...], sinq_ref[...]) * scale
    s = _dot_nt(q.astype(BF16), k_s[...])
    o_ref[...] = _softmax_pv(s, v_s[...]).astype(o_ref.dtype)


def _gqa(p, gq, gk, cos, sin, batch, seq, ctx_len):
    kvh, grp, dh = C_KV_HEADS, C_HEADS // C_KV_HEADS, C_DH
    nlt = seq // TQ
    ctx_blk0 = batch * seq // ctx_len
    k0 = C_HEADS
    v0 = C_HEADS + C_KV_HEADS
    s_all = seq + ctx_len
    in_specs = [
        pl.BlockSpec((TQ, dh), lambda b, kh, qi: (b * nlt + qi % nlt, kh * grp + qi // nlt)),
        pl.BlockSpec((seq, dh), lambda b, kh, qi: (b, k0 + kh)),
        pl.BlockSpec((ctx_len, dh), lambda b, kh, qi: (ctx_blk0 + b, k0 + kh)),
        pl.BlockSpec((seq, dh), lambda b, kh, qi: (b, v0 + kh)),
        pl.BlockSpec((ctx_len, dh), lambda b, kh, qi: (ctx_blk0 + b, v0 + kh)),
        pl.BlockSpec((1, dh), lambda b, kh, qi: (0, 0)),
        pl.BlockSpec((1, dh), lambda b, kh, qi: (0, 0)),
        pl.BlockSpec((TQ, dh), lambda b, kh, qi: (qi % nlt, 0)),
        pl.BlockSpec((TQ, dh), lambda b, kh, qi: (qi % nlt, 0)),
        pl.BlockSpec((seq, dh), lambda b, kh, qi: (0, 0)),
        pl.BlockSpec((seq, dh), lambda b, kh, qi: (0, 0)),
    ]
    return pl.pallas_call(
        functools.partial(_gqa_kernel, ctx_len=ctx_len),
        grid=(batch, kvh, grp * nlt),
        in_specs=in_specs,
        out_specs=pl.BlockSpec((TQ, dh), lambda b, kh, qi: (b * nlt + qi % nlt, kh * grp + qi // nlt)),
        out_shape=jax.ShapeDtypeStruct((batch * seq, C_HEADS * dh), BF16),
        scratch_shapes=[pltpu.VMEM((s_all, dh), BF16), pltpu.VMEM((s_all, dh), BF16)],
        compiler_params=_cp(("parallel", "parallel", "arbitrary")),
        name="gqa_attn",
    )(p, p, p, p, p, gq.reshape(1, dh), gk.reshape(1, dh), cos, sin, cos, sin)


def _router_kernel(x_ref, g_ref, sh_ref, sc_ref, rw_ref, rb_ref, h_ref, ri_ref, rf_ref, cnt_ref, base_s):
    i = pl.program_id(0)
    tm = x_ref.shape[0]
    ne = N_EXPERTS
    per = ne // N_GROUPS

    @pl.when(i == 0)
    def _():
        base_s[...] = jnp.zeros_like(base_s)

    h = _rms(x_ref[...], g_ref[...]) * (1.0 + sc_ref[...]) + sh_ref[...]
    h_ref[...] = h
    logits = jnp.dot(h, rw_ref[...], preferred_element_type=F32, precision=HI)
    scores = jax.nn.sigmoid(logits)
    sel = scores + rb_ref[...]
    lane = lax.broadcasted_iota(jnp.int32, (tm, ne), 1).astype(F32)
    neg = -jnp.inf
    big = float(ne)

    def top2(vals):
        m1 = jnp.max(vals, axis=1, keepdims=True)
        i1 = jnp.min(jnp.where(vals == m1, lane, big), axis=1, keepdims=True)
        rest = jnp.where(lane == i1, neg, vals)
        m2 = jnp.max(rest, axis=1, keepdims=True)
        i2 = jnp.min(jnp.where(rest == m2, lane, big), axis=1, keepdims=True)
        return m1 + m2, i1, i2

    best, e1, e2 = None, None, None
    for grp in range(N_GROUPS):
        in_grp = jnp.logical_and(lane >= float(grp * per), lane < float((grp + 1) * per))
        gsum, i1, i2 = top2(jnp.where(in_grp, sel, neg))
        if grp == 0:
            best, e1, e2 = gsum, i1, i2
        else:
            better = gsum > best
            best = jnp.where(better, gsum, best)
            e1 = jnp.where(better, i1, e1)
            e2 = jnp.where(better, i2, e2)

    hot1 = lane == e1
    hot2 = lane == e2
    w1 = jnp.sum(jnp.where(hot1, scores, 0.0), axis=1, keepdims=True)
    w2 = jnp.sum(jnp.where(hot2, scores, 0.0), axis=1, keepdims=True)
    wsum = w1 + w2
    assign = jnp.logical_or(hot1, hot2)
    r = lax.broadcasted_iota(jnp.int32, (tm, tm), 0)
    c = lax.broadcasted_iota(jnp.int32, (tm, tm), 1)
    before = (c < r).astype(BF16)
    excl = jnp.dot(before, assign.astype(BF16), preferred_element_type=F32) + base_s[...]
    rank1 = jnp.sum(jnp.where(hot1, excl, 0.0), axis=1, keepdims=True)
    rank2 = jnp.sum(jnp.where(hot2, excl, 0.0), axis=1, keepdims=True)
    base_s[...] = base_s[...] + jnp.sum(assign.astype(F32), axis=0, keepdims=True)

    l128 = lax.broadcasted_iota(jnp.int32, (tm, 128), 1)
    ri = jnp.where(l128 == 0, e1, jnp.where(l128 == 1, e2, jnp.where(l128 == 2, rank1, jnp.where(l128 == 3, rank2, 0.0))))
    ri_ref[...] = ri.astype(jnp.int32)
    rf_ref[...] = jnp.where(l128 == 0, w1 / wsum, jnp.where(l128 == 1, w2 / wsum, 0.0))
    cnt_ref[...] = jnp.broadcast_to(base_s[...], cnt_ref.shape)


def _router(x_all, g, mods5, layer, router_w, router_b, rows, n_tiles):
    d = x_all.shape[-1]
    tm = rows.tm
    n = n_tiles * tm
    ne = N_EXPERTS
    return pl.pallas_call(
        _router_kernel,
        grid=(n_tiles,),
        in_specs=[pl.BlockSpec((tm, d), lambda i: (i, 0)),
                  pl.BlockSpec((1, d), lambda i: (0, 0)),
                  _mod_spec(rows, layer, 3, d),
                  _mod_spec(rows, layer, 4, d),
                  pl.BlockSpec((d, ne), lambda i: (0, 0)),
                  pl.BlockSpec((1, ne), lambda i: (0, 0))],
        out_specs=[pl.BlockSpec((tm, d), lambda i: (i, 0)),
                   pl.BlockSpec((tm, 128), lambda i: (i, 0)),
                   pl.BlockSpec((tm, 128), lambda i: (i, 0)),
                   pl.BlockSpec((8, ne), lambda i: (0, 0))],
        out_shape=[jax.ShapeDtypeStruct((n, d), F32),
                   jax.ShapeDtypeStruct((n, 128), jnp.int32),
                   jax.ShapeDtypeStruct((n, 128), F32),
                   jax.ShapeDtypeStruct((8, ne), F32)],
        scratch_shapes=[pltpu.VMEM((1, ne), F32)],
        compiler_params=_cp(("arbitrary",)),
        name="moe_router",
    )(x_all, g.reshape(1, d), mods5, mods5, router_w, router_b.reshape(1, ne))


def _expert_kernel(te_ref, nu_ref, src_ref, h_hbm, w1_ref, w3_ref, w2_ref, y_ref, xbuf, sem, w1_s, w3_s, w2_s):
    r = pl.program_id(0)
    n_used = nu_ref[0]
    active = r < n_used
    changed = jnp.logical_or(r == 0, te_ref[r] != te_ref[jnp.maximum(r - 1, 0)])
    slot = lax.rem(r, 2)

    def gather(tile, dst_slot):
        base = tile * TE
        for t in range(TE):
            pltpu.make_async_copy(h_hbm.at[pl.ds(src_ref[base + t], 1)], xbuf.at[dst_slot, pl.ds(t, 1)],
                                  sem.at[dst_slot]).start()

    @pl.when(jnp.logical_and(r == 0, active))
    def _():
        gather(0, 0)

    @pl.when(r + 1 < n_used)
    def _():
        gather(r + 1, 1 - slot)

    @pl.when(jnp.logical_and(active, changed))
    def _():
        w1_s[...] = w1_ref[...].astype(BF16)
        w3_s[...] = w3_ref[...].astype(BF16)
        w2_s[...] = w2_ref[...].astype(BF16)

    @pl.when(active)
    def _():
        pltpu.make_async_copy(h_hbm.at[pl.ds(0, TE)], xbuf.at[slot], sem.at[slot]).wait()
        x = xbuf[slot].astype(BF16)
        a = jnp.dot(x, w1_s[...], preferred_element_type=F32)
        b = jnp.dot(x, w3_s[...], preferred_element_type=F32)
        hid = (a * jax.nn.sigmoid(a) * b).astype(BF16)
        y_ref[...] = jnp.dot(hid, w2_s[...], preferred_element_type=F32)

    @pl.when(jnp.logical_not(active))
    def _():
        y_ref[...] = jnp.zeros_like(y_ref)


def _experts(tile_expert, n_used, src, hp, w1, w3, w2, layer):
    p_max = src.shape[0]
    d = hp.shape[-1]
    f = w1.shape[-1]
    grid_spec = pltpu.PrefetchScalarGridSpec(
        num_scalar_prefetch=3,
        grid=(p_max // TE,),
        in_specs=[pl.BlockSpec(memory_space=pl.ANY),
                  pl.BlockSpec((None, None, d, f), lambda r, te, nu, sr: (layer, te[r], 0, 0)),
                  pl.BlockSpec((None, None, d, f), lambda r, te, nu, sr: (layer, te[r], 0, 0)),
                  pl.BlockSpec((None, None, f, d), lambda r, te, nu, sr: (layer, te[r], 0, 0))],
        out_specs=pl.BlockSpec((TE, d), lambda r, te, nu, sr: (r, 0)),
        scratch_shapes=[pltpu.VMEM((2, TE, d), F32), pltpu.SemaphoreType.DMA((2,)),
                        pltpu.VMEM((d, f), BF16), pltpu.VMEM((d, f), BF16), pltpu.VMEM((f, d), BF16)],
    )
    return pl.pallas_call(
        _expert_kernel,
        grid_spec=grid_spec,
        out_shape=jax.ShapeDtypeStruct((p_max, d), F32),
        compiler_params=_cp(("arbitrary",)),
        name="moe_experts",
    )(tile_expert, n_used, src, hp, w1, w3, w2)


def _combine_kernel(dest_ref, x_ref, rf_ref, g_ref, fg_ref, ys_hbm, o_ref, buf, sem, *, tm, n_tok, n_tiles, final):
    i = pl.program_id(0)
    slot = lax.rem(i, 2)

    def gather(tile, dst_slot):
        base = tile * tm
        for t in range(tm):
            pltpu.make_async_copy(ys_hbm.at[pl.ds(dest_ref[base + t], 1)], buf.at[dst_slot, 0, pl.ds(t, 1)],
                                  sem.at[dst_slot]).start()
            pltpu.make_async_copy(ys_hbm.at[pl.ds(dest_ref[n_tok + base + t], 1)], buf.at[dst_slot, 1, pl.ds(t, 1)],
                                  sem.at[dst_slot]).start()

    @pl.when(i == 0)
    def _():
        gather(0, 0)

    @pl.when(i + 1 < n_tiles)
    def _():
        gather(i + 1, 1 - slot)

    pltpu.make_async_copy(ys_hbm.at[pl.ds(0, tm)], buf.at[slot, 0], sem.at[slot]).wait()
    pltpu.make_async_copy(ys_hbm.at[pl.ds(0, tm)], buf.at[slot, 1], sem.at[slot]).wait()
    w = rf_ref[...]
    y = w[:, 0:1] * buf[slot, 0] + w[:, 1:2] * buf[slot, 1]
    x2 = x_ref[...] + g_ref[...] * y
    if final:
        x2 = _rms(x2, fg_ref[...])
    o_ref[...] = x2


def _combine(dest, x_all, rf, mods5, layer, final_g, ys, rows, n_tiles, n_tok_total, final):
    d = x_all.shape[-1]
    tm = rows.tm
    grid_spec = pltpu.PrefetchScalarGridSpec(
        num_scalar_prefetch=1,
        grid=(n_tiles,),
        in_specs=[pl.BlockSpec((tm, d), lambda i, dr: (i, 0)),
                  pl.BlockSpec((tm, 128), lambda i, dr: (i, 0)),
                  _mod_spec(rows, layer, 5, d),
                  pl.BlockSpec((1, d), lambda i, dr: (0, 0)),
                  pl.BlockSpec(memory_space=pl.ANY)],
        out_specs=pl.BlockSpec((tm, d), lambda i, dr: (i, 0)),
        scratch_shapes=[pltpu.VMEM((2, 2, tm, d), F32), pltpu.SemaphoreType.DMA((2,))],
    )
    return pl.pallas_call(
        functools.partial(_combine_kernel, tm=tm, n_tok=n_tok_total, n_tiles=n_tiles, final=final),
        grid_spec=grid_spec,
        out_shape=jax.ShapeDtypeStruct((n_tiles * tm, d), F32),
        compiler_params=_cp(("arbitrary",)),
        name="moe_combine",
    )(dest, x_all, rf, mods5, final_g.reshape(1, d), ys)


def _moe(x_all, n_tok, norm_g, mods5, layer, router_w, router_b, w1, w3, w2, final_g, final, batch, seq, ctx_len):
    rows_r = _Rows(batch, seq, ctx_len, TROUTE)
    rows_c = _Rows(batch, seq, ctx_len, TCOMB)
    h, ri, rf, cnt = _router(x_all, norm_g, mods5, layer, router_w, router_b, rows_r, n_tok // TROUTE)
    counts = cnt[0].astype(jnp.int32)
    padded = ((counts + TE - 1) // TE) * TE
    ends = jnp.cumsum(padded)
    starts = ends - padded
    e1, e2, r1, r2 = ri[:, 0], ri[:, 1], ri[:, 2], ri[:, 3]
    dest = jnp.concatenate([starts[e1] + r1, starts[e2] + r2]).astype(jnp.int32)
    p_max = 2 * n_tok + N_EXPERTS * TE
    n_tiles = p_max // TE
    n_used = (ends[-1] // TE).astype(jnp.int32)
    tile_start = jnp.arange(n_tiles, dtype=jnp.int32) * TE
    tile_expert = jnp.sum((tile_start[:, None] >= ends[None, :]).astype(jnp.int32), axis=1)
    last_expert = jnp.sum((jnp.maximum(ends[-1] - 1, 0) >= ends).astype(jnp.int32))
    tile_expert = jnp.minimum(jnp.where(tile_start < ends[-1], tile_expert, last_expert), N_EXPERTS - 1).astype(jnp.int32)
    tok = jnp.arange(n_tok, dtype=jnp.int32)
    src = jnp.zeros((p_max,), jnp.int32).at[dest].set(jnp.concatenate([tok, tok]))
    ys = _experts(tile_expert, n_used.reshape(1), src, h, w1, w3, w2, layer)
    return _combine(dest, x_all, rf, mods5, layer, final_g, ys, rows_c, n_tok // TCOMB, n_tok, final)


def _rope_tables(t_len, d_rope):
    rows = t_len // GRID_W
    quarter = d_rope // 4
    freqs = ROPE_THETA ** (-jnp.arange(quarter, dtype=F32) / quarter)
    row = jnp.repeat(jnp.arange(rows, dtype=F32), GRID_W)
    col = jnp.tile(jnp.arange(GRID_W, dtype=F32), rows)
    ang = jnp.concatenate([row[:, None] * freqs, col[:, None] * freqs], axis=-1)
    cos, sin = jnp.cos(ang), jnp.sin(ang)
    return jnp.concatenate([cos, cos], axis=-1), jnp.concatenate([-sin, sin], axis=-1)


def kernel(x, c, ctx, c_ctx, mod_w, mod_b, norm_attn_g, norm_ffn_g, final_norm_g, ab_w_in, ab_w_out, hgrn_lb_logits, hgrn_norm_g, mla_q_norm_g, mla_w_uq, mla_kv_norm_g, mla_w_ukv, cd_w_in, cd_w_out, gqa_q_norm_g, gqa_k_norm_g, gla_w_a2, gla_b_a, gla_norm_g, router_w, router_b, moe_w1, moe_w3, moe_w2):
    batch, seq, d = x.shape
    ctx_len = ctx.shape[1]
    n_lat, n_ctx = batch * seq, batch * ctx_len
    assert ctx_len % TQ == 0 and seq % TQ == 0 and seq % ctx_len == 0 and batch < 8
    tm = min(1024, seq, n_ctx)
    rows = _Rows(batch, seq, ctx_len, tm)

    cvec = jnp.concatenate([c, c_ctx[None, :], jnp.zeros((8 - batch - 1, d), F32)], axis=0)
    mods = _modvec(cvec, mod_w, mod_b)
    mods5 = mods.reshape(mods.shape[0], 8, 6, 1, d)

    cos_b, sin_b = _rope_tables(seq, B_ROPE)
    cos_c, sin_c = _rope_tables(seq, C_DH)
    lb = jnp.cumsum(jax.nn.softmax(hgrn_lb_logits.astype(F32), axis=1), axis=1)

    x_lat = x.reshape(n_lat, d)
    x_ctx = ctx.reshape(n_ctx, d)

    h0 = _norm_mod(x_lat, x_ctx, 0, norm_attn_g[0], mods5, 0, rows)
    ab_main = 5 * A_HEADS * A_DK + B_Q_LORA + B_KV_LORA
    p0 = _matmul(h0, ab_w_in, 0, ab_main, 256, tm)
    kr0 = _matmul(h0, ab_w_in[:, :, ab_main:], 0, B_ROPE, B_ROPE, tm)
    mix_a = _hgrn(p0, lb[0, 0], lb[1, 0], hgrn_norm_g[0], batch, seq, ctx_len)
    dq = B_NOPE + B_ROPE
    dkv = B_NOPE + B_DV
    w_uq_h = mla_w_uq[0].reshape(B_Q_LORA, B_HEADS, dq).transpose(1, 0, 2)
    w_ukv_h = mla_w_ukv[0].reshape(B_KV_LORA, B_HEADS, dkv).transpose(1, 0, 2)
    mix_b = _mla(p0, kr0, w_uq_h, w_ukv_h, mla_q_norm_g[0], mla_kv_norm_g[0], cos_b, sin_b, batch, seq, ctx_len)
    half = mix_a.shape[-1]
    w_out0 = ab_w_out.reshape(ab_w_out.shape[0] * 2, half, d)
    x1 = _out_proj(mix_a, mix_b, w_out0, 0, x_lat, x_ctx, 0, mods5, 0, rows, rows.n_all)
    x2 = _moe(x1, n_lat + n_ctx, norm_ffn_g[0], mods5, 0, router_w, router_b, moe_w1, moe_w3, moe_w2,
              final_norm_g, False, batch, seq, ctx_len)

    h1 = _norm_mod(x2, x2, rows.n_lat, norm_attn_g[1], mods5, 1, rows)
    cd_main = (C_HEADS + 2 * C_KV_HEADS) * C_DH + 2 * D_HEADS * D_DK + 2 * D_HEADS * D_DV
    p1 = _matmul(h1, cd_w_in, 0, cd_main, 512, tm)
    ga1 = _matmul(h1, cd_w_in[:, :, cd_main:], 0, 2 * D_GATE_RANK, 2 * D_GATE_RANK, tm)
    mix_c = _gqa(p1, gqa_q_norm_g[0], gqa_k_norm_g[0], cos_c, sin_c, batch, seq, ctx_len)
    mix_d = _gla(p1, ga1, gla_w_a2[0], gla_b_a[0], gla_norm_g[0], batch, seq, ctx_len)
    w_out1 = cd_w_out.reshape(cd_w_out.shape[0] * 2, mix_c.shape[-1], d)
    x3 = _out_proj(mix_c, mix_d, w_out1, 0, x2, x2, rows.n_lat, mods5, 1, rows, rows.n_lat)
    out = _moe(x3, n_lat, norm_ffn_g[1], mods5, 1, router_w, router_b, moe_w1, moe_w3, moe_w2,
               final_norm_g, True, batch, seq, ctx_len)
    return out.reshape(batch, seq, d)
```

```python
import functools

import jax
import jax.numpy as jnp
from jax import lax
from jax.experimental import pallas as pl
from jax.experimental.pallas import tpu as pltpu

F32 = jnp.float32
BF16 = jnp.bfloat16
HI = lax.Precision.HIGHEST

GRID_W = 64
ROPE_THETA = 10000.0
NORM_EPS = 1e-6
A_HEADS, A_DK, A_DV = 8, 128, 128
B_HEADS, B_Q_LORA, B_KV_LORA, B_NOPE, B_ROPE, B_DV = 8, 512, 256, 128, 64, 128
C_HEADS, C_KV_HEADS, C_DH = 8, 2, 128
D_HEADS, D_DK, D_DV, D_GATE_RANK = 4, 128, 256, 16
GLA_TAU = 16.0
N_EXPERTS, N_GROUPS = 16, 4

TQ = 256
SCAN_C = 64
SCAN_BLOCK = 512
TE = 256
TROUTE = 512
TCOMB = 256
DMA_UNROLL = 8
VMEM_MIB = 56


def _cp(sem):
    return pltpu.CompilerParams(dimension_semantics=sem, vmem_limit_bytes=VMEM_MIB * 1024 * 1024)


def _rms(x, g):
    return x * lax.rsqrt(jnp.mean(x * x, axis=-1, keepdims=True) + NORM_EPS) * g


def _rope(x, cos, sin):
    half = x.shape[-1] // 2
    swapped = jnp.concatenate([x[:, half:], x[:, :half]], axis=-1)
    return x * cos + swapped * sin


def _dot_nt(a, b):
    return lax.dot_general(a, b, (((1,), (1,)), ((), ())), preferred_element_type=F32)


def _dot_tn(a, b):
    return lax.dot_general(a, b, (((0,), (0,)), ((), ())), preferred_element_type=F32)


def _modvec_kernel(c_ref, w_ref, b_ref, o_ref):
    c = c_ref[...]
    a = c * jax.nn.sigmoid(c)
    o_ref[...] = jnp.dot(a, w_ref[...], preferred_element_type=F32, precision=HI) + b_ref[...]


def _modvec(cvec, mod_w, mod_b):
    n_layers, d, n6 = mod_w.shape
    tn = min(1024, n6)
    return pl.pallas_call(
        _modvec_kernel,
        grid=(n_layers, n6 // tn),
        in_specs=[pl.BlockSpec((8, d), lambda l, j: (0, 0)),
                  pl.BlockSpec((None, d, tn), lambda l, j: (l, 0, j)),
                  pl.BlockSpec((None, 1, tn), lambda l, j: (l, 0, j))],
        out_specs=pl.BlockSpec((None, 8, tn), lambda l, j: (l, 0, j)),
        out_shape=jax.ShapeDtypeStruct((n_layers, 8, n6), F32),
        compiler_params=_cp(("parallel", "parallel")),
        name="modvec",
    )(cvec, mod_w, mod_b.reshape(n_layers, 1, n6))


class _Rows:
    def __init__(self, batch, seq, ctx_len, tm):
        assert seq % tm == 0 and (batch * ctx_len) % tm == 0
        self.tm = tm
        self.batch = batch
        self.per_batch = seq // tm
        self.n_lat = batch * seq // tm
        self.n_ctx = batch * ctx_len // tm
        self.n_all = self.n_lat + self.n_ctx

    def mod_row(self, i):
        return jnp.where(i < self.n_lat, i // self.per_batch, self.batch)


def _mod_spec(rows, layer, chunk, d):
    return pl.BlockSpec((None, None, None, 1, d), lambda i, *_: (layer, rows.mod_row(i), chunk, 0, 0))


def _norm_mod_kernel(xl_ref, xc_ref, g_ref, sh_ref, sc_ref, o_ref, *, n_lat):
    i = pl.program_id(0)

    def body(x_ref):
        y = _rms(x_ref[...], g_ref[...])
        o_ref[...] = (y * (1.0 + sc_ref[...]) + sh_ref[...]).astype(o_ref.dtype)

    @pl.when(i < n_lat)
    def _():
        body(xl_ref)

    @pl.when(i >= n_lat)
    def _():
        body(xc_ref)


def _norm_mod(x_lat, x_ctx, ctx_block0, g, mods5, layer, rows):
    d = x_lat.shape[-1]
    tm = rows.tm
    nl = rows.n_lat
    return pl.pallas_call(
        functools.partial(_norm_mod_kernel, n_lat=nl),
        grid=(rows.n_all,),
        in_specs=[pl.BlockSpec((tm, d), lambda i: (jnp.minimum(i, nl - 1), 0)),
                  pl.BlockSpec((tm, d), lambda i: (ctx_block0 + jnp.maximum(i - nl, 0), 0)),
                  pl.BlockSpec((1, d), lambda i: (0, 0)),
                  _mod_spec(rows, layer, 0, d),
                  _mod_spec(rows, layer, 1, d)],
        out_specs=pl.BlockSpec((tm, d), lambda i: (i, 0)),
        out_shape=jax.ShapeDtypeStruct((rows.n_all * tm, d), BF16),
        compiler_params=_cp(("parallel",)),
        name="norm_mod",
    )(x_lat, x_ctx, g.reshape(1, d), mods5, mods5)


def _mm_kernel(a_ref, w_ref, o_ref):
    o_ref[...] = jnp.dot(a_ref[...], w_ref[...].astype(BF16), preferred_element_type=F32).astype(o_ref.dtype)


def _matmul(a, w3, layer, n_cols, tn, tm):
    m, k = a.shape
    return pl.pallas_call(
        _mm_kernel,
        grid=(m // tm, n_cols // tn),
        in_specs=[pl.BlockSpec((tm, k), lambda i, j: (i, 0)),
                  pl.BlockSpec((None, k, tn), lambda i, j: (layer, 0, j))],
        out_specs=pl.BlockSpec((tm, tn), lambda i, j: (i, j)),
        out_shape=jax.ShapeDtypeStruct((m, n_cols), BF16),
        compiler_params=_cp(("parallel", "arbitrary")),
        name="in_proj",
    )(a, w3)


def _out_proj_kernel(ma_ref, mb_ref, wa_ref, wb_ref, xl_ref, xc_ref, g_ref, o_ref, *, n_lat):
    i = pl.program_id(0)
    acc = jnp.dot(ma_ref[...], wa_ref[...].astype(BF16), preferred_element_type=F32)
    acc += jnp.dot(mb_ref[...], wb_ref[...].astype(BF16), preferred_element_type=F32)
    upd = g_ref[...] * acc

    @pl.when(i < n_lat)
    def _():
        o_ref[...] = xl_ref[...] + upd

    @pl.when(i >= n_lat)
    def _():
        o_ref[...] = xc_ref[...] + upd


def _out_proj(mix_a, mix_b, w_out3, widx, x_lat, x_ctx, ctx_block0, mods5, layer, rows, n_tiles):
    d = x_lat.shape[-1]
    ka, kb = mix_a.shape[-1], mix_b.shape[-1]
    assert ka == kb
    tm = rows.tm
    tn = min(512, d)
    nl = rows.n_lat
    return pl.pallas_call(
        functools.partial(_out_proj_kernel, n_lat=nl),
        grid=(n_tiles, d // tn),
        in_specs=[pl.BlockSpec((tm, ka), lambda i, j: (i, 0)),
                  pl.BlockSpec((tm, kb), lambda i, j: (i, 0)),
                  pl.BlockSpec((None, ka, tn), lambda i, j: (2 * widx, 0, j)),
                  pl.BlockSpec((None, kb, tn), lambda i, j: (2 * widx + 1, 0, j)),
                  pl.BlockSpec((tm, tn), lambda i, j: (jnp.minimum(i, nl - 1), j)),
                  pl.BlockSpec((tm, tn), lambda i, j: (ctx_block0 + jnp.maximum(i - nl, 0), j)),
                  pl.BlockSpec((None, None, None, 1, tn), lambda i, j: (layer, rows.mod_row(i), 2, 0, j))],
        out_specs=pl.BlockSpec((tm, tn), lambda i, j: (i, j)),
        out_shape=jax.ShapeDtypeStruct((n_tiles * tm, d), F32),
        compiler_params=_cp(("parallel", "arbitrary")),
        name="out_proj",
    )(mix_a, mix_b, w_out3, w_out3, x_lat, x_ctx, mods5)


def _tri(c, upper):
    r = lax.broadcasted_iota(jnp.int32, (c, c), 0)
    s = lax.broadcasted_iota(jnp.int32, (c, c), 1)
    return (s >= r) if upper else (r >= s)


def _scan_block(q, k, v, g, st, mask, forward):
    c = SCAN_C
    dk, dv = q.shape[-1], v.shape[-1]
    n = q.shape[0] // c
    mid, last = (c // 2 - 1, c - 1) if forward else (c // 2, 0)
    tri = jnp.broadcast_to(mask.astype(BF16)[None], (n, c, c))
    g3 = g.reshape(n, c, dk)
    g_hi = g3.astype(BF16)
    g_lo = (g3 - g_hi.astype(F32)).astype(BF16)
    cum = (jnp.einsum('cts,csd->ctd', tri, g_hi, preferred_element_type=F32)
           + jnp.einsum('cts,csd->ctd', tri, g_lo, preferred_element_type=F32))
    m = cum[:, mid:mid + 1, :]
    tot = cum[:, last:last + 1, :]
    qe = (q.reshape(n, c, dk) * jnp.exp(cum - m)).astype(BF16)
    ke = (k.reshape(n, c, dk) * jnp.exp(m - cum)).astype(BF16)
    a = jnp.einsum('ctd,csd->cts', qe, ke, preferred_element_type=F32)
    a = jnp.where(mask[None], a, 0.0).astype(BF16)
    v3 = v.reshape(n, c, dv)
    o = jnp.einsum('cts,csv->ctv', a, v3, preferred_element_type=F32)
    u = jnp.einsum('csv,csd->cvd', v3, ke, preferred_element_type=F32)
    em = jnp.exp(m)
    et = jnp.exp(tot - m)
    states = [None] * n
    for ci in (range(n) if forward else reversed(range(n))):
        stp = st * em[ci]
        states[ci] = stp.astype(BF16)
        st = (stp + u[ci]) * et[ci]
    o = o + jnp.einsum('ctd,cvd->ctv', qe, jnp.stack(states), preferred_element_type=F32)
    return o.reshape(n * c, dv), st


def _scan_segments(segments, prep_f, prep_b, of_ref, ob_ref, dk, dv):
    low, up = _tri(SCAN_C, False), _tri(SCAN_C, True)
    carry = (jnp.zeros((dv, dk), F32), jnp.zeros((dv, dk), F32))
    for rows, off, seg in segments:
        rb_ = min(SCAN_BLOCK, rows)
        n = rows // rb_

        def body(i, carry, n=n, off=off, seg=seg, rb_=rb_):
            sf, sb = carry
            rf = pl.multiple_of(i * rb_, rb_)
            rb = pl.multiple_of((n - 1 - i) * rb_, rb_)
            q, k, v, g = prep_f(seg, rf, rb_)
            o, sf = _scan_block(q, k, v, g, sf, low, True)
            of_ref[pl.ds(off + rf, rb_), :] = o
            q, k, v, g = prep_b(seg, rb, rb_)
            o, sb = _scan_block(q, k, v, g, sb, up, False)
            ob_ref[pl.ds(off + rb, rb_), :] = o
            return sf, sb

        carry = lax.fori_loop(0, n, body, carry)


def _hgrn_kernel(ql, qc, f1l, f1c, f2l, f2c, vl, vc, gl, gc, lbf, lbb, ng, o_ref, of_s, ob_s, *, ctx_len, seq):
    rt = pl.program_id(2)
    n_ctx_tiles = ctx_len // TQ

    @pl.when(rt == 0)
    def _scan():
        refs = {0: (qc, f1c, f2c, vc), 1: (ql, f1l, f2l, vl)}
        scale = A_DK ** -0.5

        def prep(seg, r, nr, fi, lb_ref):
            x = refs[seg][0][pl.ds(r, nr), :].astype(F32)
            q = x * jax.nn.sigmoid(x) * scale
            v = refs[seg][3][pl.ds(r, nr), :]
            lb = lb_ref[...]
            f = lb + (1.0 - lb) * jax.nn.sigmoid(refs[seg][fi][pl.ds(r, nr), :].astype(F32))
            return q, 1.0 - f, v, jnp.log(f)

        _scan_segments(
            [(ctx_len, 0, 0), (seq, ctx_len, 1)],
            lambda seg, r, nr: prep(seg, r, nr, 1, lbf),
            lambda seg, r, nr: prep(seg, r, nr, 2, lbb),
            of_s, ob_s, A_DK, A_DV)

    r0 = pl.multiple_of(rt * TQ, TQ)
    o = of_s[pl.ds(r0, TQ), :] + ob_s[pl.ds(r0, TQ), :]
    y = _rms(o, ng[...])

    @pl.when(rt < n_ctx_tiles)
    def _():
        gate = gc[pl.ds(r0, TQ), :].astype(F32)
        o_ref[...] = (y * jax.nn.sigmoid(gate)).astype(o_ref.dtype)

    @pl.when(rt >= n_ctx_tiles)
    def _():
        gate = gl[pl.ds(pl.multiple_of(r0 - ctx_len, TQ), TQ), :].astype(F32)
        o_ref[...] = (y * jax.nn.sigmoid(gate)).astype(o_ref.dtype)


def _out_row_block(batch, seq, ctx_len):
    nct = ctx_len // TQ
    nlt = seq // TQ

    def f(b, rt):
        return jnp.where(rt < nct, batch * nlt + b * nct + rt, b * nlt + rt - nct)

    return f


def _hgrn(p, lb_f, lb_b, norm_g, batch, seq, ctx_len):
    h, dk, dv = A_HEADS, A_DK, A_DV
    nct, nlt = ctx_len // TQ, seq // TQ
    ctx_blk0 = batch * seq // ctx_len
    row_block = _out_row_block(batch, seq, ctx_len)
    in_specs = []
    for kcol in range(5):
        in_specs.append(pl.BlockSpec((seq, dk), lambda b, hh, rt, kcol=kcol: (b, kcol * h + hh)))
        in_specs.append(pl.BlockSpec((ctx_len, dk), lambda b, hh, rt, kcol=kcol: (ctx_blk0 + b, kcol * h + hh)))
    vec = pl.BlockSpec((1, dk), lambda b, hh, rt: (0, hh))
    in_specs += [vec, vec, pl.BlockSpec((1, dv), lambda b, hh, rt: (0, 0))]
    return pl.pallas_call(
        functools.partial(_hgrn_kernel, ctx_len=ctx_len, seq=seq),
        grid=(batch, h, nct + nlt),
        in_specs=in_specs,
        out_specs=pl.BlockSpec((TQ, dv), lambda b, hh, rt: (row_block(b, rt), hh)),
        out_shape=jax.ShapeDtypeStruct((batch * (seq + ctx_len), h * dv), BF16),
        scratch_shapes=[pltpu.VMEM((seq + ctx_len, dv), F32), pltpu.VMEM((seq + ctx_len, dv), F32)],
        compiler_params=_cp(("parallel", "parallel", "arbitrary")),
        name="hgrn_scan",
    )(*([p] * 10), lb_f.reshape(1, h * dk), lb_b.reshape(1, h * dk), norm_g.reshape(1, dv))


def _gla_kernel(ql, qc, kl, kc, vl, vc, gl, al, ac, wa, ba, ng, o_ref, of_s, ob_s, *, ctx_len, seq):
    rt = pl.program_id(2)

    @pl.when(rt == 0)
    def _scan():
        refs = {0: (qc, kc, vc, ac), 1: (ql, kl, vl, al)}
        scale = D_DK ** -0.5
        r16 = D_GATE_RANK

        def prep(seg, r, nr, d):
            q = refs[seg][0][pl.ds(r, nr), :].astype(F32) * scale
            k = refs[seg][1][pl.ds(r, nr), :].astype(F32)
            v = refs[seg][2][pl.ds(r, nr), :]
            a = refs[seg][3][pl.ds(r, nr), :].astype(F32)[:, d * r16:(d + 1) * r16]
            z = jnp.dot(a, wa[d], preferred_element_type=F32, precision=HI) + ba[d]
            g = (jnp.minimum(z, 0.0) - jnp.log(1.0 + jnp.exp(-jnp.abs(z)))) * (1.0 / GLA_TAU)
            return q, k, v, g

        _scan_segments(
            [(ctx_len, 0, 0), (seq, ctx_len, 1)],
            lambda seg, r, nr: prep(seg, r, nr, 0),
            lambda seg, r, nr: prep(seg, r, nr, 1),
            of_s, ob_s, D_DK, D_DV)

    r0 = pl.multiple_of(rt * TQ, TQ)
    o = of_s[pl.ds(ctx_len + r0, TQ), :] + ob_s[pl.ds(ctx_len + r0, TQ), :]
    gate = gl[pl.ds(r0, TQ), :].astype(F32)
    o_ref[...] = (_rms(o, ng[...]) * gate * jax.nn.sigmoid(gate)).astype(o_ref.dtype)


def _gla(p, ga, w_a2, b_a, norm_g, batch, seq, ctx_len):
    h, dk, dv = D_HEADS, D_DK, D_DV
    nlt = seq // TQ
    ctx_blk0 = batch * seq // ctx_len
    q0 = (C_HEADS + 2 * C_KV_HEADS) * C_DH // dk
    k0 = q0 + h
    v0 = (k0 + h) * dk // dv
    g0 = v0 + h

    def pair(width, blk0):
        return [pl.BlockSpec((seq, width), lambda b, hh, rt: (b, blk0 + hh)),
                pl.BlockSpec((ctx_len, width), lambda b, hh, rt: (ctx_blk0 + b, blk0 + hh))]

    in_specs = pair(dk, q0) + pair(dk, k0) + pair(dv, v0)
    in_specs += [pl.BlockSpec((seq, dv), lambda b, hh, rt: (b, g0 + hh)),
                 pl.BlockSpec((seq, 2 * D_GATE_RANK), lambda b, hh, rt: (b, 0)),
                 pl.BlockSpec((ctx_len, 2 * D_GATE_RANK), lambda b, hh, rt: (ctx_blk0 + b, 0)),
                 pl.BlockSpec((2, D_GATE_RANK, dk), lambda b, hh, rt: (0, 0, hh)),
                 pl.BlockSpec((2, 1, dk), lambda b, hh, rt: (0, 0, hh)),
                 pl.BlockSpec((1, dv), lambda b, hh, rt: (0, 0))]
    return pl.pallas_call(
        functools.partial(_gla_kernel, ctx_len=ctx_len, seq=seq),
        grid=(batch, h, nlt),
        in_specs=in_specs,
        out_specs=pl.BlockSpec((TQ, dv), lambda b, hh, rt: (b * nlt + rt, hh)),
        out_shape=jax.ShapeDtypeStruct((batch * seq, h * dv), BF16),
        scratch_shapes=[pltpu.VMEM((seq + ctx_len, dv), F32), pltpu.VMEM((seq + ctx_len, dv), F32)],
        compiler_params=_cp(("parallel", "parallel", "arbitrary")),
        name="gla_scan",
    )(p, p, p, p, p, p, p, ga, ga, w_a2, b_a.reshape(2, 1, h * dk), norm_g.reshape(1, dv))


def _softmax_pv(s, v):
    m = jnp.max(s, axis=-1, keepdims=True)
    p = jnp.exp(s - m)
    l = jnp.sum(p, axis=-1, keepdims=True)
    return jnp.dot(p.astype(BF16), v, preferred_element_type=F32) / l


def _mla_kernel(ql_ref, kvl_ref, kvc_ref, krl_ref, krc_ref, wqn_ref, wqr_ref, wkv_ref, gq_ref, gkv_ref,
                cosq_ref, sinq_ref, cosk_ref, sink_ref, o_ref, kn_s, kr_s, v_s, *, ctx_len):
    qt = pl.program_id(1)
    n_ctx_tiles = ctx_len // TQ
    scale = (B_NOPE + B_ROPE) ** -0.5
    dkv = B_NOPE + B_DV

    @pl.when(qt == 0)
    def _prep():
        kvc = _rms(kvc_ref[...].astype(F32), gkv_ref[...]).astype(BF16)
        kvl = _rms(kvl_ref[...].astype(F32), gkv_ref[...]).astype(BF16)
        for h in range(B_HEADS):
            w = wkv_ref[:, h * dkv:(h + 1) * dkv].astype(BF16)
            up_c = jnp.dot(kvc, w, preferred_element_type=F32)
            kn_s[h, 0:ctx_len, :] = up_c[:, :B_NOPE].astype(BF16)
            v_s[h, 0:ctx_len, :] = up_c[:, B_NOPE:].astype(BF16)
            up_l = jnp.dot(kvl, w, preferred_element_type=F32)
            kn_s[h, ctx_len:, :] = up_l[:, :B_NOPE].astype(BF16)
            v_s[h, ctx_len:, :] = up_l[:, B_NOPE:].astype(BF16)
        kr_s[0:ctx_len, :] = krc_ref[...]
        kr_s[ctx_len:, :] = _rope(krl_ref[...].astype(F32), cosk_ref[...], sink_ref[...]).astype(BF16)

    xn = _rms(ql_ref[...].astype(F32), gq_ref[...]).astype(BF16)
    qn_all = jnp.dot(xn, wqn_ref[...].astype(BF16), preferred_element_type=F32) * scale
    qr_all = jnp.dot(xn, wqr_ref[...].astype(BF16), preferred_element_type=F32) * scale

    def heads(n_keys, rotate):
        outs = []
        for h in range(B_HEADS):
            qn = qn_all[:, h * B_NOPE:(h + 1) * B_NOPE].astype(BF16)
            qr = qr_all[:, h * B_ROPE:(h + 1) * B_ROPE]
            if rotate:
                qr = _rope(qr, cosq_ref[...], sinq_ref[...])
            s = _dot_nt(qn, kn_s[h, 0:n_keys, :]) + _dot_nt(qr.astype(BF16), kr_s[0:n_keys, :])
            outs.append(_softmax_pv(s, v_s[h, 0:n_keys, :]).astype(o_ref.dtype))
        o_ref[...] = jnp.concatenate(outs, axis=-1)

    @pl.when(qt < n_ctx_tiles)
    def _():
        heads(ctx_len, False)

    @pl.when(qt >= n_ctx_tiles)
    def _():
        heads(kr_s.shape[0], True)


def _mla(p, kr, w_uq, w_ukv, gq, gkv, cos, sin, batch, seq, ctx_len):
    h = B_HEADS
    nct, nlt = ctx_len // TQ, seq // TQ
    ctx_blk0 = batch * seq // ctx_len
    row_block = _out_row_block(batch, seq, ctx_len)
    ql_blk = 5 * A_HEADS * A_DK // B_Q_LORA
    kv_blk = (5 * A_HEADS * A_DK + B_Q_LORA) // B_KV_LORA
    dq = B_NOPE + B_ROPE
    s_all = seq + ctx_len
    w3 = w_uq.reshape(B_Q_LORA, h, dq)
    wq_n = w3[:, :, :B_NOPE].reshape(B_Q_LORA, h * B_NOPE)
    wq_r = w3[:, :, B_NOPE:].reshape(B_Q_LORA, h * B_ROPE)
    in_specs = [
        pl.BlockSpec((TQ, B_Q_LORA), lambda b, qt: (row_block(b, qt), ql_blk)),
        pl.BlockSpec((seq, B_KV_LORA), lambda b, qt: (b, kv_blk)),
        pl.BlockSpec((ctx_len, B_KV_LORA), lambda b, qt: (ctx_blk0 + b, kv_blk)),
        pl.BlockSpec((seq, B_ROPE), lambda b, qt: (b, 0)),
        pl.BlockSpec((ctx_len, B_ROPE), lambda b, qt: (ctx_blk0 + b, 0)),
        pl.BlockSpec((B_Q_LORA, h * B_NOPE), lambda b, qt: (0, 0)),
        pl.BlockSpec((B_Q_LORA, h * B_ROPE), lambda b, qt: (0, 0)),
        pl.BlockSpec((B_KV_LORA, h * (B_NOPE + B_DV)), lambda b, qt: (0, 0)),
        pl.BlockSpec((1, B_Q_LORA), lambda b, qt: (0, 0)),
        pl.BlockSpec((1, B_KV_LORA), lambda b, qt: (0, 0)),
        pl.BlockSpec((TQ, B_ROPE), lambda b, qt: (jnp.maximum(qt - nct, 0), 0)),
        pl.BlockSpec((TQ, B_ROPE), lambda b, qt: (jnp.maximum(qt - nct, 0), 0)),
        pl.BlockSpec((seq, B_ROPE), lambda b, qt: (0, 0)),
        pl.BlockSpec((seq, B_ROPE), lambda b, qt: (0, 0)),
    ]
    return pl.pallas_call(
        functools.partial(_mla_kernel, ctx_len=ctx_len),
        grid=(batch, nct + nlt),
        in_specs=in_specs,
        out_specs=pl.BlockSpec((TQ, h * B_DV), lambda b, qt: (row_block(b, qt), 0)),
        out_shape=jax.ShapeDtypeStruct((batch * s_all, h * B_DV), BF16),
        scratch_shapes=[pltpu.VMEM((h, s_all, B_NOPE), BF16), pltpu.VMEM((s_all, B_ROPE), BF16),
                        pltpu.VMEM((h, s_all, B_DV), BF16)],
        compiler_params=_cp(("parallel", "arbitrary")),
        name="mla_attn",
    )(p, p, p, kr, kr, wq_n, wq_r, w_ukv, gq.reshape(1, -1), gkv.reshape(1, -1), cos, sin, cos, sin)


def _gqa_kernel(q_ref, kl_ref, kc_ref, vl_ref, vc_ref, gq_ref, gk_ref, cosq_ref, sinq_ref, cosk_ref, sink_ref,
                o_ref, k_s, v_s, *, ctx_len):
    qt = pl.program_id(2)
    scale = C_DH ** -0.5
    dh = C_DH

    @pl.when(qt == 0)
    def _prep():
        k_s[0:ctx_len, :] = _rms(kc_ref[...].astype(F32), gk_ref[...]).astype(BF16)
        kl = _rms(kl_ref[...].astype(F32), gk_ref[...])
        k_s[ctx_len:, :] = _rope(kl, cosk_ref[...], sink_ref[...]).astype(BF16)
        v_s[0:ctx_len, :] = vc_ref[...]
        v_s[ctx_len:, :] = vl_ref[...]

    outs = []
    for g in range(C_HEADS // C_KV_HEADS):
        q = _rms(q_ref[:, g * dh:(g + 1) * dh].astype(F32), gq_ref[...])
        q = _rope(q, cosq_ref[...], sinq_ref[...]) * scale
        s = _dot_nt(q.astype(BF16), k_s[...])
        outs.append(_softmax_pv(s, v_s[...]).astype(o_ref.dtype))
    o_ref[...] = jnp.concatenate(outs, axis=-1)


def _gqa(p, gq, gk, cos, sin, batch, seq, ctx_len):
    kvh, grp, dh = C_KV_HEADS, C_HEADS // C_KV_HEADS, C_DH
    nlt = seq // TQ
    ctx_blk0 = batch * seq // ctx_len
    k0 = C_HEADS
    v0 = C_HEADS + C_KV_HEADS
    s_all = seq + ctx_len
    in_specs = [
        pl.BlockSpec((TQ, grp * dh), lambda b, kh, qt: (b * nlt + qt, kh)),
        pl.BlockSpec((seq, dh), lambda b, kh, qt: (b, k0 + kh)),
        pl.BlockSpec((ctx_len, dh), lambda b, kh, qt: (ctx_blk0 + b, k0 + kh)),
        pl.BlockSpec((seq, dh), lambda b, kh, qt: (b, v0 + kh)),
        pl.BlockSpec((ctx_len, dh), lambda b, kh, qt: (ctx_blk0 + b, v0 + kh)),
        pl.BlockSpec((1, dh), lambda b, kh, qt: (0, 0)),
        pl.BlockSpec((1, dh), lambda b, kh, qt: (0, 0)),
        pl.BlockSpec((TQ, dh), lambda b, kh, qt: (qt, 0)),
        pl.BlockSpec((TQ, dh), lambda b, kh, qt: (qt, 0)),
        pl.BlockSpec((seq, dh), lambda b, kh, qt: (0, 0)),
        pl.BlockSpec((seq, dh), lambda b, kh, qt: (0, 0)),
    ]
    return pl.pallas_call(
        functools.partial(_gqa_kernel, ctx_len=ctx_len),
        grid=(batch, kvh, nlt),
        in_specs=in_specs,
        out_specs=pl.BlockSpec((TQ, grp * dh), lambda b, kh, qt: (b * nlt + qt, kh)),
        out_shape=jax.ShapeDtypeStruct((batch * seq, C_HEADS * dh), BF16),
        scratch_shapes=[pltpu.VMEM((s_all, dh), BF16), pltpu.VMEM((s_all, dh), BF16)],
        compiler_params=_cp(("parallel", "parallel", "arbitrary")),
        name="gqa_attn",
    )(p, p, p, p, p, gq.reshape(1, dh), gk.reshape(1, dh), cos, sin, cos, sin)


def _router_kernel(x_ref, g_ref, sh_ref, sc_ref, rw_ref, rb_ref, h_ref, ri_ref, rf_ref, cnt_ref, base_s):
    i = pl.program_id(0)
    tm = x_ref.shape[0]
    ne = N_EXPERTS
    per = ne // N_GROUPS

    @pl.when(i == 0)
    def _():
        base_s[...] = jnp.zeros_like(base_s)

    h = _rms(x_ref[...], g_ref[...]) * (1.0 + sc_ref[...]) + sh_ref[...]
    h_ref[...] = h
    logits = jnp.dot(h, rw_ref[...], preferred_element_type=F32, precision=HI)
    scores = jax.nn.sigmoid(logits)
    sel = scores + rb_ref[...]
    lane = lax.broadcasted_iota(jnp.int32, (tm, ne), 1).astype(F32)
    neg = -jnp.inf
    big = float(ne)

    def top2(vals):
        m1 = jnp.max(vals, axis=1, keepdims=True)
        i1 = jnp.min(jnp.where(vals == m1, lane, big), axis=1, keepdims=True)
        rest = jnp.where(lane == i1, neg, vals)
        m2 = jnp.max(rest, axis=1, keepdims=True)
        i2 = jnp.min(jnp.where(rest == m2, lane, big), axis=1, keepdims=True)
        return m1 + m2, i1, i2

    best, e1, e2 = None, None, None
    for grp in range(N_GROUPS):
        in_grp = jnp.logical_and(lane >= float(grp * per), lane < float((grp + 1) * per))
        gsum, i1, i2 = top2(jnp.where(in_grp, sel, neg))
        if grp == 0:
            best, e1, e2 = gsum, i1, i2
        else:
            better = gsum > best
            best = jnp.where(better, gsum, best)
            e1 = jnp.where(better, i1, e1)
            e2 = jnp.where(better, i2, e2)

    hot1 = lane == e1
    hot2 = lane == e2
    w1 = jnp.sum(jnp.where(hot1, scores, 0.0), axis=1, keepdims=True)
    w2 = jnp.sum(jnp.where(hot2, scores, 0.0), axis=1, keepdims=True)
    wsum = w1 + w2
    assign = jnp.logical_or(hot1, hot2)
    r = lax.broadcasted_iota(jnp.int32, (tm, tm), 0)
    c = lax.broadcasted_iota(jnp.int32, (tm, tm), 1)
    before = (c < r).astype(BF16)
    excl = jnp.dot(before, assign.astype(BF16), preferred_element_type=F32) + base_s[...]
    rank1 = jnp.sum(jnp.where(hot1, excl, 0.0), axis=1, keepdims=True)
    rank2 = jnp.sum(jnp.where(hot2, excl, 0.0), axis=1, keepdims=True)
    base_s[...] = base_s[...] + jnp.sum(assign.astype(F32), axis=0, keepdims=True)

    l128 = lax.broadcasted_iota(jnp.int32, (tm, 128), 1)
    ri = jnp.where(l128 == 0, e1, jnp.where(l128 == 1, e2, jnp.where(l128 == 2, rank1, jnp.where(l128 == 3, rank2, 0.0))))
    ri_ref[...] = ri.astype(jnp.int32)
    rf_ref[...] = jnp.where(l128 == 0, w1 / wsum, jnp.where(l128 == 1, w2 / wsum, 0.0))
    cnt_ref[...] = jnp.broadcast_to(base_s[...], cnt_ref.shape)


def _router(x_all, g, mods5, layer, router_w, router_b, rows, n_tiles):
    d = x_all.shape[-1]
    tm = rows.tm
    n = n_tiles * tm
    ne = N_EXPERTS
    return pl.pallas_call(
        _router_kernel,
        grid=(n_tiles,),
        in_specs=[pl.BlockSpec((tm, d), lambda i: (i, 0)),
                  pl.BlockSpec((1, d), lambda i: (0, 0)),
                  _mod_spec(rows, layer, 3, d),
                  _mod_spec(rows, layer, 4, d),
                  pl.BlockSpec((d, ne), lambda i: (0, 0)),
                  pl.BlockSpec((1, ne), lambda i: (0, 0))],
        out_specs=[pl.BlockSpec((tm, d), lambda i: (i, 0)),
                   pl.BlockSpec((tm, 128), lambda i: (i, 0)),
                   pl.BlockSpec((tm, 128), lambda i: (i, 0)),
                   pl.BlockSpec((8, ne), lambda i: (0, 0))],
        out_shape=[jax.ShapeDtypeStruct((n, d), F32),
                   jax.ShapeDtypeStruct((n, 128), jnp.int32),
                   jax.ShapeDtypeStruct((n, 128), F32),
                   jax.ShapeDtypeStruct((8, ne), F32)],
        scratch_shapes=[pltpu.VMEM((1, ne), F32)],
        compiler_params=_cp(("arbitrary",)),
        name="moe_router",
    )(x_all, g.reshape(1, d), mods5, mods5, router_w, router_b.reshape(1, ne))


def _expert_kernel(te_ref, nu_ref, src_ref, h_hbm, w1_ref, w3_ref, w2_ref, y_ref, xbuf, sem, w1_s, w3_s, w2_s):
    r = pl.program_id(0)
    n_used = nu_ref[0]
    active = r < n_used
    changed = jnp.logical_or(r == 0, te_ref[r] != te_ref[jnp.maximum(r - 1, 0)])
    slot = lax.rem(r, 2)

    def gather(tile, dst_slot):
        base = tile * TE
        for t in range(TE):
            pltpu.make_async_copy(h_hbm.at[pl.ds(src_ref[base + t], 1)], xbuf.at[dst_slot, pl.ds(t, 1)],
                                  sem.at[dst_slot]).start(priority=t % 2)

    @pl.when(jnp.logical_and(r == 0, active))
    def _():
        gather(0, 0)

    @pl.when(r + 1 < n_used)
    def _():
        gather(r + 1, 1 - slot)

    @pl.when(jnp.logical_and(active, changed))
    def _():
        w1_s[...] = w1_ref[...].astype(BF16)
        w3_s[...] = w3_ref[...].astype(BF16)
        w2_s[...] = w2_ref[...].astype(BF16)

    @pl.when(active)
    def _():
        pltpu.make_async_copy(h_hbm.at[pl.ds(0, TE)], xbuf.at[slot], sem.at[slot]).wait()
        x = xbuf[slot].astype(BF16)
        a = jnp.dot(x, w1_s[...], preferred_element_type=F32)
        b = jnp.dot(x, w3_s[...], preferred_element_type=F32)
        hid = (a * jax.nn.sigmoid(a) * b).astype(BF16)
        y_ref[...] = jnp.dot(hid, w2_s[...], preferred_element_type=F32)

    @pl.when(jnp.logical_not(active))
    def _():
        y_ref[...] = jnp.zeros_like(y_ref)


def _experts(tile_expert, n_used, src, hp, w1, w3, w2, layer):
    p_max = src.shape[0]
    d = hp.shape[-1]
    f = w1.shape[-1]
    grid_spec = pltpu.PrefetchScalarGridSpec(
        num_scalar_prefetch=3,
        grid=(p_max // TE,),
        in_specs=[pl.BlockSpec(memory_space=pl.ANY),
                  pl.BlockSpec((None, None, d, f), lambda r, te, nu, sr: (layer, te[r], 0, 0)),
                  pl.BlockSpec((None, None, d, f), lambda r, te, nu, sr: (layer, te[r], 0, 0)),
                  pl.BlockSpec((None, None, f, d), lambda r, te, nu, sr: (layer, te[r], 0, 0))],
        out_specs=pl.BlockSpec((TE, d), lambda r, te, nu, sr: (r, 0)),
        scratch_shapes=[pltpu.VMEM((2, TE, d), F32), pltpu.SemaphoreType.DMA((2,)),
                        pltpu.VMEM((d, f), BF16), pltpu.VMEM((d, f), BF16), pltpu.VMEM((f, d), BF16)],
    )
    return pl.pallas_call(
        _expert_kernel,
        grid_spec=grid_spec,
        out_shape=jax.ShapeDtypeStruct((p_max, d), F32),
        compiler_params=_cp(("arbitrary",)),
        name="moe_experts",
    )(tile_expert, n_used, src, hp, w1, w3, w2)


def _combine_kernel(dest_ref, x_ref, rf_ref, g_ref, fg_ref, ys_hbm, o_ref, buf, sem, *, tm, n_tok, n_tiles, final):
    i = pl.program_id(0)
    slot = lax.rem(i, 2)

    def gather(tile, dst_slot):
        base = tile * tm
        for t in range(tm):
            pltpu.make_async_copy(ys_hbm.at[pl.ds(dest_ref[base + t], 1)], buf.at[dst_slot, 0, pl.ds(t, 1)],
                                  sem.at[dst_slot]).start(priority=0)
            pltpu.make_async_copy(ys_hbm.at[pl.ds(dest_ref[n_tok + base + t], 1)], buf.at[dst_slot, 1, pl.ds(t, 1)],
                                  sem.at[dst_slot]).start(priority=1)

    @pl.when(i == 0)
    def _():
        gather(0, 0)

    @pl.when(i + 1 < n_tiles)
    def _():
        gather(i + 1, 1 - slot)

    pltpu.make_async_copy(ys_hbm.at[pl.ds(0, tm)], buf.at[slot, 0], sem.at[slot]).wait()
    pltpu.make_async_copy(ys_hbm.at[pl.ds(0, tm)], buf.at[slot, 1], sem.at[slot]).wait()
    w = rf_ref[...]
    y = w[:, 0:1] * buf[slot, 0] + w[:, 1:2] * buf[slot, 1]
    x2 = x_ref[...] + g_ref[...] * y
    if final:
        x2 = _rms(x2, fg_ref[...])
    o_ref[...] = x2


def _combine(dest, x_all, rf, mods5, layer, final_g, ys, rows, n_tiles, n_tok_total, final):
    d = x_all.shape[-1]
    tm = rows.tm
    grid_spec = pltpu.PrefetchScalarGridSpec(
        num_scalar_prefetch=1,
        grid=(n_tiles,),
        in_specs=[pl.BlockSpec((tm, d), lambda i, dr: (i, 0)),
                  pl.BlockSpec((tm, 128), lambda i, dr: (i, 0)),
                  _mod_spec(rows, layer, 5, d),
                  pl.BlockSpec((1, d), lambda i, dr: (0, 0)),
                  pl.BlockSpec(memory_space=pl.ANY)],
        out_specs=pl.BlockSpec((tm, d), lambda i, dr: (i, 0)),
        scratch_shapes=[pltpu.VMEM((2, 2, tm, d), F32), pltpu.SemaphoreType.DMA((2,))],
    )
    return pl.pallas_call(
        functools.partial(_combine_kernel, tm=tm, n_tok=n_tok_total, n_tiles=n_tiles, final=final),
        grid_spec=grid_spec,
        out_shape=jax.ShapeDtypeStruct((n_tiles * tm, d), F32),
        compiler_params=_cp(("arbitrary",)),
        name="moe_combine",
    )(dest, x_all, rf, mods5, final_g.reshape(1, d), ys)


def _moe(x_all, n_tok, norm_g, mods5, layer, router_w, router_b, w1, w3, w2, final_g, final, batch, seq, ctx_len):
    rows_r = _Rows(batch, seq, ctx_len, TROUTE)
    rows_c = _Rows(batch, seq, ctx_len, TCOMB)
    h, ri, rf, cnt = _router(x_all, norm_g, mods5, layer, router_w, router_b, rows_r, n_tok // TROUTE)
    counts = cnt[0].astype(jnp.int32)
    padded = ((counts + TE - 1) // TE) * TE
    ends = jnp.cumsum(padded)
    starts = ends - padded
    e1, e2, r1, r2 = ri[:, 0], ri[:, 1], ri[:, 2], ri[:, 3]
    dest = jnp.concatenate([starts[e1] + r1, starts[e2] + r2]).astype(jnp.int32)
    p_max = 2 * n_tok + N_EXPERTS * TE
    n_tiles = p_max // TE
    n_used = (ends[-1] // TE).astype(jnp.int32)
    tile_start = jnp.arange(n_tiles, dtype=jnp.int32) * TE
    tile_expert = jnp.sum((tile_start[:, None] >= ends[None, :]).astype(jnp.int32), axis=1)
    last_expert = jnp.sum((jnp.maximum(ends[-1] - 1, 0) >= ends).astype(jnp.int32))
    tile_expert = jnp.minimum(jnp.where(tile_start < ends[-1], tile_expert, last_expert), N_EXPERTS - 1).astype(jnp.int32)
    tok = jnp.arange(n_tok, dtype=jnp.int32)
    src = jnp.zeros((p_max,), jnp.int32).at[dest].set(jnp.concatenate([tok, tok]))
    ys = _experts(tile_expert, n_used.reshape(1), src, h, w1, w3, w2, layer)
    return _combine(dest, x_all, rf, mods5, layer, final_g, ys, rows_c, n_tok // TCOMB, n_tok, final)


def _rope_tables(t_len, d_rope):
    rows = t_len // GRID_W
    quarter = d_rope // 4
    freqs = ROPE_THETA ** (-jnp.arange(quarter, dtype=F32) / quarter)
    row = jnp.repeat(jnp.arange(rows, dtype=F32), GRID_W)
    col = jnp.tile(jnp.arange(GRID_W, dtype=F32), rows)
    ang = jnp.concatenate([row[:, None] * freqs, col[:, None] * freqs], axis=-1)
    cos, sin = jnp.cos(ang), jnp.sin(ang)
    return jnp.concatenate([cos, cos], axis=-1), jnp.concatenate([-sin, sin], axis=-1)


def kernel(x, c, ctx, c_ctx, mod_w, mod_b, norm_attn_g, norm_ffn_g, final_norm_g, ab_w_in, ab_w_out, hgrn_lb_logits, hgrn_norm_g, mla_q_norm_g, mla_w_uq, mla_kv_norm_g, mla_w_ukv, cd_w_in, cd_w_out, gqa_q_norm_g, gqa_k_norm_g, gla_w_a2, gla_b_a, gla_norm_g, router_w, router_b, moe_w1, moe_w3, moe_w2):
    batch, seq, d = x.shape
    ctx_len = ctx.shape[1]
    n_lat, n_ctx = batch * seq, batch * ctx_len
    assert ctx_len % TQ == 0 and seq % TQ == 0 and seq % ctx_len == 0 and batch < 8
    tm = min(1024, seq, n_ctx)
    rows = _Rows(batch, seq, ctx_len, tm)

    cvec = jnp.concatenate([c, c_ctx[None, :], jnp.zeros((8 - batch - 1, d), F32)], axis=0)
    mods = _modvec(cvec, mod_w, mod_b)
    mods5 = mods.reshape(mods.shape[0], 8, 6, 1, d)

    cos_b, sin_b = _rope_tables(seq, B_ROPE)
    cos_c, sin_c = _rope_tables(seq, C_DH)
    lb = jnp.cumsum(jax.nn.softmax(hgrn_lb_logits.astype(F32), axis=1), axis=1)

    x_lat = x.reshape(n_lat, d)
    x_ctx = ctx.reshape(n_ctx, d)

    h0 = _norm_mod(x_lat, x_ctx, 0, norm_attn_g[0], mods5, 0, rows)
    ab_main = 5 * A_HEADS * A_DK + B_Q_LORA + B_KV_LORA
    tm_mm = next(t for t in (2304, 2048, 1536, 1024, 512, 256) if (n_lat + n_ctx) % t == 0)
    p0 = _matmul(h0, ab_w_in, 0, ab_main, 256, tm_mm)
    kr0 = _matmul(h0, ab_w_in[:, :, ab_main:], 0, B_ROPE, B_ROPE, tm_mm)
    mix_a = _hgrn(p0, lb[0, 0], lb[1, 0], hgrn_norm_g[0], batch, seq, ctx_len)
    mix_b = _mla(p0, kr0, mla_w_uq[0], mla_w_ukv[0], mla_q_norm_g[0], mla_kv_norm_g[0], cos_b, sin_b,
                 batch, seq, ctx_len)
    half = mix_a.shape[-1]
    w_out0 = ab_w_out.reshape(ab_w_out.shape[0] * 2, half, d)
    x1 = _out_proj(mix_a, mix_b, w_out0, 0, x_lat, x_ctx, 0, mods5, 0, rows, rows.n_all)
    x2 = _moe(x1, n_lat + n_ctx, norm_ffn_g[0], mods5, 0, router_w, router_b, moe_w1, moe_w3, moe_w2,
              final_norm_g, False, batch, seq, ctx_len)

    h1 = _norm_mod(x2, x2, rows.n_lat, norm_attn_g[1], mods5, 1, rows)
    cd_main = (C_HEADS + 2 * C_KV_HEADS) * C_DH + 2 * D_HEADS * D_DK + 2 * D_HEADS * D_DV
    p1 = _matmul(h1, cd_w_in, 0, cd_main, 512, tm_mm)
    ga1 = _matmul(h1, cd_w_in[:, :, cd_main:], 0, 2 * D_GATE_RANK, 2 * D_GATE_RANK, tm_mm)
    mix_c = _gqa(p1, gqa_q_norm_g[0], gqa_k_norm_g[0], cos_c, sin_c, batch, seq, ctx_len)
    mix_d = _gla(p1, ga1, gla_w_a2[0], gla_b_a[0], gla_norm_g[0], batch, seq, ctx_len)
    w_out1 = cd_w_out.reshape(cd_w_out.shape[0] * 2, mix_c.shape[-1], d)
    x3 = _out_proj(mix_c, mix_d, w_out1, 0, x2, x2, rows.n_lat, mods5, 1, rows, rows.n_lat)
    out = _moe(x3, n_lat, norm_ffn_g[1], mods5, 1, router_w, router_b, moe_w1, moe_w3, moe_w2,
               final_norm_g, True, batch, seq, ctx_len)
    return out.reshape(batch, seq, d)
```

```python
import functools

import jax
import jax.numpy as jnp
from jax import lax
from jax.experimental import pallas as pl
from jax.experimental.pallas import tpu as pltpu

F32 = jnp.float32
BF16 = jnp.bfloat16
HI = lax.Precision.HIGHEST

GRID_W = 64
ROPE_THETA = 10000.0
NORM_EPS = 1e-6
A_HEADS, A_DK, A_DV = 8, 128, 128
B_HEADS, B_Q_LORA, B_KV_LORA, B_NOPE, B_ROPE, B_DV = 8, 512, 256, 128, 64, 128
C_HEADS, C_KV_HEADS, C_DH = 8, 2, 128
D_HEADS, D_DK, D_DV, D_GATE_RANK = 4, 128, 256, 16
GLA_TAU = 16.0
N_EXPERTS, N_GROUPS = 16, 4

TQ = 256
SCAN_C = 64
SCAN_BLOCK = 512
TE = 256
TROUTE = 512
TCOMB = 256
DMA_UNROLL = 8
VMEM_MIB = 56


def _cp(sem):
    return pltpu.CompilerParams(dimension_semantics=sem, vmem_limit_bytes=VMEM_MIB * 1024 * 1024)


def _rms(x, g):
    return x * lax.rsqrt(jnp.mean(x * x, axis=-1, keepdims=True) + NORM_EPS) * g


def _rope(x, cos, sin):
    half = x.shape[-1] // 2
    swapped = jnp.concatenate([x[:, half:], x[:, :half]], axis=-1)
    return x * cos + swapped * sin


def _dot_nt(a, b):
    return lax.dot_general(a, b, (((1,), (1,)), ((), ())), preferred_element_type=F32)


def _dot_tn(a, b):
    return lax.dot_general(a, b, (((0,), (0,)), ((), ())), preferred_element_type=F32)


def _modvec_kernel(c_ref, w_ref, b_ref, o_ref):
    c = c_ref[...]
    a = c * jax.nn.sigmoid(c)
    o_ref[...] = jnp.dot(a, w_ref[...], preferred_element_type=F32, precision=HI) + b_ref[...]


def _modvec(cvec, mod_w, mod_b):
    n_layers, d, n6 = mod_w.shape
    tn = min(1024, n6)
    return pl.pallas_call(
        _modvec_kernel,
        grid=(n_layers, n6 // tn),
        in_specs=[pl.BlockSpec((8, d), lambda l, j: (0, 0)),
                  pl.BlockSpec((None, d, tn), lambda l, j: (l, 0, j)),
                  pl.BlockSpec((None, 1, tn), lambda l, j: (l, 0, j))],
        out_specs=pl.BlockSpec((None, 8, tn), lambda l, j: (l, 0, j)),
        out_shape=jax.ShapeDtypeStruct((n_layers, 8, n6), F32),
        compiler_params=_cp(("parallel", "parallel")),
        name="modvec",
    )(cvec, mod_w, mod_b.reshape(n_layers, 1, n6))


class _Rows:
    def __init__(self, batch, seq, ctx_len, tm):
        assert seq % tm == 0 and (batch * ctx_len) % tm == 0
        self.tm = tm
        self.batch = batch
        self.per_batch = seq // tm
        self.n_lat = batch * seq // tm
        self.n_ctx = batch * ctx_len // tm
        self.n_all = self.n_lat + self.n_ctx

    def mod_row(self, i):
        return jnp.where(i < self.n_lat, i // self.per_batch, self.batch)


def _mod_spec(rows, layer, chunk, d):
    return pl.BlockSpec((None, None, None, 1, d), lambda i, *_: (layer, rows.mod_row(i), chunk, 0, 0))


def _norm_mod_kernel(xl_ref, xc_ref, g_ref, sh_ref, sc_ref, o_ref, *, n_lat):
    i = pl.program_id(0)

    def body(x_ref):
        y = _rms(x_ref[...], g_ref[...])
        o_ref[...] = (y * (1.0 + sc_ref[...]) + sh_ref[...]).astype(o_ref.dtype)

    @pl.when(i < n_lat)
    def _():
        body(xl_ref)

    @pl.when(i >= n_lat)
    def _():
        body(xc_ref)


def _norm_mod(x_lat, x_ctx, ctx_block0, g, mods5, layer, rows):
    d = x_lat.shape[-1]
    tm = rows.tm
    nl = rows.n_lat
    return pl.pallas_call(
        functools.partial(_norm_mod_kernel, n_lat=nl),
        grid=(rows.n_all,),
        in_specs=[pl.BlockSpec((tm, d), lambda i: (jnp.minimum(i, nl - 1), 0)),
                  pl.BlockSpec((tm, d), lambda i: (ctx_block0 + jnp.maximum(i - nl, 0), 0)),
                  pl.BlockSpec((1, d), lambda i: (0, 0)),
                  _mod_spec(rows, layer, 0, d),
                  _mod_spec(rows, layer, 1, d)],
        out_specs=pl.BlockSpec((tm, d), lambda i: (i, 0)),
        out_shape=jax.ShapeDtypeStruct((rows.n_all * tm, d), BF16),
        compiler_params=_cp(("parallel",)),
        name="norm_mod",
    )(x_lat, x_ctx, g.reshape(1, d), mods5, mods5)


def _mm_kernel(a_ref, w_ref, o_ref):
    o_ref[...] = jnp.dot(a_ref[...], w_ref[...].astype(BF16), preferred_element_type=F32).astype(o_ref.dtype)


def _matmul(a, w3, layer, n_cols, tn, tm):
    m, k = a.shape
    return pl.pallas_call(
        _mm_kernel,
        grid=(m // tm, n_cols // tn),
        in_specs=[pl.BlockSpec((tm, k), lambda i, j: (i, 0)),
                  pl.BlockSpec((None, k, tn), lambda i, j: (layer, 0, j))],
        out_specs=pl.BlockSpec((tm, tn), lambda i, j: (i, j)),
        out_shape=jax.ShapeDtypeStruct((m, n_cols), BF16),
        compiler_params=_cp(("parallel", "arbitrary")),
        name="in_proj",
    )(a, w3)


def _mm_tail_kernel(a_ref, w_ref, o_ref):
    n = o_ref.shape[-1]
    acc = jnp.dot(a_ref[...], w_ref[...].astype(BF16), preferred_element_type=F32)
    o_ref[...] = acc[:, :n].astype(o_ref.dtype)


def _matmul_tail(a, w3, layer, col0, n_cols, tm):
    m, k = a.shape
    lane = 128
    assert col0 % lane == 0 and n_cols <= lane and col0 + n_cols == w3.shape[-1]
    return pl.pallas_call(
        _mm_tail_kernel,
        grid=(m // tm,),
        in_specs=[pl.BlockSpec((tm, k), lambda i: (i, 0)),
                  pl.BlockSpec((None, k, lane), lambda i: (layer, 0, col0 // lane))],
        out_specs=pl.BlockSpec((tm, n_cols), lambda i: (i, 0)),
        out_shape=jax.ShapeDtypeStruct((m, n_cols), BF16),
        compiler_params=_cp(("parallel",)),
        name="in_proj_tail",
    )(a, w3)


def _out_proj_kernel(ma_ref, mb_ref, wa_ref, wb_ref, xl_ref, xc_ref, g_ref, o_ref, *, n_lat):
    i = pl.program_id(0)
    acc = jnp.dot(ma_ref[...], wa_ref[...].astype(BF16), preferred_element_type=F32)
    acc += jnp.dot(mb_ref[...], wb_ref[...].astype(BF16), preferred_element_type=F32)
    upd = g_ref[...] * acc

    @pl.when(i < n_lat)
    def _():
        o_ref[...] = xl_ref[...] + upd

    @pl.when(i >= n_lat)
    def _():
        o_ref[...] = xc_ref[...] + upd


def _out_proj(mix_a, mix_b, w_out3, widx, x_lat, x_ctx, ctx_block0, mods5, layer, rows, n_tiles):
    d = x_lat.shape[-1]
    ka, kb = mix_a.shape[-1], mix_b.shape[-1]
    assert ka == kb
    tm = rows.tm
    tn = min(512, d)
    nl = rows.n_lat
    return pl.pallas_call(
        functools.partial(_out_proj_kernel, n_lat=nl),
        grid=(n_tiles, d // tn),
        in_specs=[pl.BlockSpec((tm, ka), lambda i, j: (i, 0)),
                  pl.BlockSpec((tm, kb), lambda i, j: (i, 0)),
                  pl.BlockSpec((None, ka, tn), lambda i, j: (2 * widx, 0, j)),
                  pl.BlockSpec((None, kb, tn), lambda i, j: (2 * widx + 1, 0, j)),
                  pl.BlockSpec((tm, tn), lambda i, j: (jnp.minimum(i, nl - 1), j)),
                  pl.BlockSpec((tm, tn), lambda i, j: (ctx_block0 + jnp.maximum(i - nl, 0), j)),
                  pl.BlockSpec((None, None, None, 1, tn), lambda i, j: (layer, rows.mod_row(i), 2, 0, j))],
        out_specs=pl.BlockSpec((tm, tn), lambda i, j: (i, j)),
        out_shape=jax.ShapeDtypeStruct((n_tiles * tm, d), F32),
        compiler_params=_cp(("parallel", "arbitrary")),
        name="out_proj",
    )(mix_a, mix_b, w_out3, w_out3, x_lat, x_ctx, mods5)


def _tri(c, upper):
    r = lax.broadcasted_iota(jnp.int32, (c, c), 0)
    s = lax.broadcasted_iota(jnp.int32, (c, c), 1)
    return (s >= r) if upper else (r >= s)


def _scan_block(q, k, v, g, st, mask, forward):
    c = SCAN_C
    dk, dv = q.shape[-1], v.shape[-1]
    n = q.shape[0] // c
    mid, last = (c // 2 - 1, c - 1) if forward else (c // 2, 0)
    tri = jnp.broadcast_to(mask.astype(BF16)[None], (n, c, c))
    g3 = g.reshape(n, c, dk)
    g_hi = g3.astype(BF16)
    g_lo = (g3 - g_hi.astype(F32)).astype(BF16)
    cum = (jnp.einsum('cts,csd->ctd', tri, g_hi, preferred_element_type=F32)
           + jnp.einsum('cts,csd->ctd', tri, g_lo, preferred_element_type=F32))
    m = cum[:, mid:mid + 1, :]
    tot = cum[:, last:last + 1, :]
    qe = (q.reshape(n, c, dk) * jnp.exp(cum - m)).astype(BF16)
    ke = (k.reshape(n, c, dk) * jnp.exp(m - cum)).astype(BF16)
    a = jnp.einsum('ctd,csd->cts', qe, ke, preferred_element_type=F32)
    a = jnp.where(mask[None], a, 0.0).astype(BF16)
    v3 = v.reshape(n, c, dv)
    o = jnp.einsum('cts,csv->ctv', a, v3, preferred_element_type=F32)
    u = jnp.einsum('csv,csd->cvd', v3, ke, preferred_element_type=F32)
    em = jnp.exp(m)
    et = jnp.exp(tot - m)
    states = [None] * n
    for ci in (range(n) if forward else reversed(range(n))):
        stp = st * em[ci]
        states[ci] = stp.astype(BF16)
        st = (stp + u[ci]) * et[ci]
    o = o + jnp.einsum('ctd,cvd->ctv', qe, jnp.stack(states), preferred_element_type=F32)
    return o.reshape(n * c, dv), st


def _scan_segments(segments, prep_f, prep_b, of_ref, ob_ref, dk, dv):
    low, up = _tri(SCAN_C, False), _tri(SCAN_C, True)
    carry = (jnp.zeros((dv, dk), F32), jnp.zeros((dv, dk), F32))
    for rows, off, seg in segments:
        rb_ = min(SCAN_BLOCK, rows)
        n = rows // rb_

        def body(i, carry, n=n, off=off, seg=seg, rb_=rb_):
            sf, sb = carry
            rf = pl.multiple_of(i * rb_, rb_)
            rb = pl.multiple_of((n - 1 - i) * rb_, rb_)
            q, k, v, g = prep_f(seg, rf, rb_)
            o, sf = _scan_block(q, k, v, g, sf, low, True)
            of_ref[pl.ds(off + rf, rb_), :] = o
            q, k, v, g = prep_b(seg, rb, rb_)
            o, sb = _scan_block(q, k, v, g, sb, up, False)
            ob_ref[pl.ds(off + rb, rb_), :] = o
            return sf, sb

        carry = lax.fori_loop(0, n, body, carry)


def _hgrn_kernel(ql, qc, f1l, f1c, f2l, f2c, vl, vc, gl, gc, lbf, lbb, ng, o_ref, of_s, ob_s, *, ctx_len, seq):
    rt = pl.program_id(2)
    n_ctx_tiles = ctx_len // TQ

    @pl.when(rt == 0)
    def _scan():
        refs = {0: (qc, f1c, f2c, vc), 1: (ql, f1l, f2l, vl)}
        scale = A_DK ** -0.5

        def prep(seg, r, nr, fi, lb_ref):
            x = refs[seg][0][pl.ds(r, nr), :].astype(F32)
            q = x * jax.nn.sigmoid(x) * scale
            v = refs[seg][3][pl.ds(r, nr), :]
            lb = lb_ref[...]
            f = lb + (1.0 - lb) * jax.nn.sigmoid(refs[seg][fi][pl.ds(r, nr), :].astype(F32))
            return q, 1.0 - f, v, jnp.log(f)

        _scan_segments(
            [(ctx_len, 0, 0), (seq, ctx_len, 1)],
            lambda seg, r, nr: prep(seg, r, nr, 1, lbf),
            lambda seg, r, nr: prep(seg, r, nr, 2, lbb),
            of_s, ob_s, A_DK, A_DV)

    r0 = pl.multiple_of(rt * TQ, TQ)
    o = of_s[pl.ds(r0, TQ), :] + ob_s[pl.ds(r0, TQ), :]
    y = _rms(o, ng[...])

    @pl.when(rt < n_ctx_tiles)
    def _():
        gate = gc[pl.ds(r0, TQ), :].astype(F32)
        o_ref[...] = (y * jax.nn.sigmoid(gate)).astype(o_ref.dtype)

    @pl.when(rt >= n_ctx_tiles)
    def _():
        gate = gl[pl.ds(pl.multiple_of(r0 - ctx_len, TQ), TQ), :].astype(F32)
        o_ref[...] = (y * jax.nn.sigmoid(gate)).astype(o_ref.dtype)


def _out_row_block(batch, seq, ctx_len):
    nct = ctx_len // TQ
    nlt = seq // TQ

    def f(b, rt):
        return jnp.where(rt < nct, batch * nlt + b * nct + rt, b * nlt + rt - nct)

    return f


def _hgrn(p, lb_f, lb_b, norm_g, batch, seq, ctx_len):
    h, dk, dv = A_HEADS, A_DK, A_DV
    nct, nlt = ctx_len // TQ, seq // TQ
    ctx_blk0 = batch * seq // ctx_len
    row_block = _out_row_block(batch, seq, ctx_len)
    in_specs = []
    for kcol in range(5):
        in_specs.append(pl.BlockSpec((seq, dk), lambda b, hh, rt, kcol=kcol: (b, kcol * h + hh)))
        in_specs.append(pl.BlockSpec((ctx_len, dk), lambda b, hh, rt, kcol=kcol: (ctx_blk0 + b, kcol * h + hh)))
    vec = pl.BlockSpec((1, dk), lambda b, hh, rt: (0, hh))
    in_specs += [vec, vec, pl.BlockSpec((1, dv), lambda b, hh, rt: (0, 0))]
    return pl.pallas_call(
        functools.partial(_hgrn_kernel, ctx_len=ctx_len, seq=seq),
        grid=(batch, h, nct + nlt),
        in_specs=in_specs,
        out_specs=pl.BlockSpec((TQ, dv), lambda b, hh, rt: (row_block(b, rt), hh)),
        out_shape=jax.ShapeDtypeStruct((batch * (seq + ctx_len), h * dv), BF16),
        scratch_shapes=[pltpu.VMEM((seq + ctx_len, dv), F32), pltpu.VMEM((seq + ctx_len, dv), F32)],
        compiler_params=_cp(("parallel", "parallel", "arbitrary")),
        name="hgrn_scan",
    )(*([p] * 10), lb_f.reshape(1, h * dk), lb_b.reshape(1, h * dk), norm_g.reshape(1, dv))


def _gla_kernel(ql, qc, kl, kc, vl, vc, gl, al, ac, wa, ba, ng, o_ref, of_s, ob_s, *, ctx_len, seq):
    rt = pl.program_id(2)

    @pl.when(rt == 0)
    def _scan():
        refs = {0: (qc, kc, vc, ac), 1: (ql, kl, vl, al)}
        scale = D_DK ** -0.5
        r16 = D_GATE_RANK

        def prep(seg, r, nr, d):
            q = refs[seg][0][pl.ds(r, nr), :].astype(F32) * scale
            k = refs[seg][1][pl.ds(r, nr), :].astype(F32)
            v = refs[seg][2][pl.ds(r, nr), :]
            a = refs[seg][3][pl.ds(r, nr), :].astype(F32)[:, d * r16:(d + 1) * r16]
            z = jnp.dot(a, wa[d], preferred_element_type=F32, precision=HI) + ba[d]
            g = (jnp.minimum(z, 0.0) - jnp.log(1.0 + jnp.exp(-jnp.abs(z)))) * (1.0 / GLA_TAU)
            return q, k, v, g

        _scan_segments(
            [(ctx_len, 0, 0), (seq, ctx_len, 1)],
            lambda seg, r, nr: prep(seg, r, nr, 0),
            lambda seg, r, nr: prep(seg, r, nr, 1),
            of_s, ob_s, D_DK, D_DV)

    r0 = pl.multiple_of(rt * TQ, TQ)
    o = of_s[pl.ds(ctx_len + r0, TQ), :] + ob_s[pl.ds(ctx_len + r0, TQ), :]
    gate = gl[pl.ds(r0, TQ), :].astype(F32)
    o_ref[...] = (_rms(o, ng[...]) * gate * jax.nn.sigmoid(gate)).astype(o_ref.dtype)


def _gla(p, ga, w_a2, b_a, norm_g, batch, seq, ctx_len):
    h, dk, dv = D_HEADS, D_DK, D_DV
    nlt = seq // TQ
    ctx_blk0 = batch * seq // ctx_len
    q0 = (C_HEADS + 2 * C_KV_HEADS) * C_DH // dk
    k0 = q0 + h
    v0 = (k0 + h) * dk // dv
    g0 = v0 + h

    def pair(width, blk0):
        return [pl.BlockSpec((seq, width), lambda b, hh, rt: (b, blk0 + hh)),
                pl.BlockSpec((ctx_len, width), lambda b, hh, rt: (ctx_blk0 + b, blk0 + hh))]

    in_specs = pair(dk, q0) + pair(dk, k0) + pair(dv, v0)
    in_specs += [pl.BlockSpec((seq, dv), lambda b, hh, rt: (b, g0 + hh)),
                 pl.BlockSpec((seq, 2 * D_GATE_RANK), lambda b, hh, rt: (b, 0)),
                 pl.BlockSpec((ctx_len, 2 * D_GATE_RANK), lambda b, hh, rt: (ctx_blk0 + b, 0)),
                 pl.BlockSpec((2, D_GATE_RANK, dk), lambda b, hh, rt: (0, 0, hh)),
                 pl.BlockSpec((2, 1, dk), lambda b, hh, rt: (0, 0, hh)),
                 pl.BlockSpec((1, dv), lambda b, hh, rt: (0, 0))]
    return pl.pallas_call(
        functools.partial(_gla_kernel, ctx_len=ctx_len, seq=seq),
        grid=(batch, h, nlt),
        in_specs=in_specs,
        out_specs=pl.BlockSpec((TQ, dv), lambda b, hh, rt: (b * nlt + rt, hh)),
        out_shape=jax.ShapeDtypeStruct((batch * seq, h * dv), BF16),
        scratch_shapes=[pltpu.VMEM((seq + ctx_len, dv), F32), pltpu.VMEM((seq + ctx_len, dv), F32)],
        compiler_params=_cp(("parallel", "parallel", "arbitrary")),
        name="gla_scan",
    )(p, p, p, p, p, p, p, ga, ga, w_a2, b_a.reshape(2, 1, h * dk), norm_g.reshape(1, dv))


def _softmax_pv(s, v):
    m = jnp.max(s, axis=-1, keepdims=True)
    p = jnp.exp(s - m)
    l = jnp.sum(p, axis=-1, keepdims=True)
    return jnp.dot(p.astype(BF16), v, preferred_element_type=F32) / l


def _mla_kernel(ql_ref, kvl_ref, kvc_ref, krl_ref, krc_ref, wqn_ref, wqr_ref, wkv_ref, gq_ref, gkv_ref,
                cosq_ref, sinq_ref, cosk_ref, sink_ref, o_ref, kn_s, kr_s, v_s, *, ctx_len):
    qt = pl.program_id(1)
    n_ctx_tiles = ctx_len // TQ
    scale = (B_NOPE + B_ROPE) ** -0.5
    dkv = B_NOPE + B_DV

    @pl.when(qt == 0)
    def _prep():
        kvc = _rms(kvc_ref[...].astype(F32), gkv_ref[...]).astype(BF16)
        kvl = _rms(kvl_ref[...].astype(F32), gkv_ref[...]).astype(BF16)
        for h in range(B_HEADS):
            w = wkv_ref[:, h * dkv:(h + 1) * dkv].astype(BF16)
            up_c = jnp.dot(kvc, w, preferred_element_type=F32)
            kn_s[h, 0:ctx_len, :] = up_c[:, :B_NOPE].astype(BF16)
            v_s[h, 0:ctx_len, :] = up_c[:, B_NOPE:].astype(BF16)
            up_l = jnp.dot(kvl, w, preferred_element_type=F32)
            kn_s[h, ctx_len:, :] = up_l[:, :B_NOPE].astype(BF16)
            v_s[h, ctx_len:, :] = up_l[:, B_NOPE:].astype(BF16)
        kr_s[0:ctx_len, :] = krc_ref[...]
        kr_s[ctx_len:, :] = _rope(krl_ref[...].astype(F32), cosk_ref[...], sink_ref[...]).astype(BF16)

    xn = _rms(ql_ref[...].astype(F32), gq_ref[...]).astype(BF16)
    qn_all = jnp.dot(xn, wqn_ref[...].astype(BF16), preferred_element_type=F32) * scale
    qr_all = jnp.dot(xn, wqr_ref[...].astype(BF16), preferred_element_type=F32) * scale

    def heads(n_keys, rotate):
        outs = []
        for h in range(B_HEADS):
            qn = qn_all[:, h * B_NOPE:(h + 1) * B_NOPE].astype(BF16)
            qr = qr_all[:, h * B_ROPE:(h + 1) * B_ROPE]
            if rotate:
                qr = _rope(qr, cosq_ref[...], sinq_ref[...])
            s = _dot_nt(qn, kn_s[h, 0:n_keys, :]) + _dot_nt(qr.astype(BF16), kr_s[0:n_keys, :])
            outs.append(_softmax_pv(s, v_s[h, 0:n_keys, :]).astype(o_ref.dtype))
        o_ref[...] = jnp.concatenate(outs, axis=-1)

    @pl.when(qt < n_ctx_tiles)
    def _():
        heads(ctx_len, False)

    @pl.when(qt >= n_ctx_tiles)
    def _():
        heads(kr_s.shape[0], True)


def _mla(p, kr, w_uq, w_ukv, gq, gkv, cos, sin, batch, seq, ctx_len):
    h = B_HEADS
    nct, nlt = ctx_len // TQ, seq // TQ
    ctx_blk0 = batch * seq // ctx_len
    row_block = _out_row_block(batch, seq, ctx_len)
    ql_blk = 5 * A_HEADS * A_DK // B_Q_LORA
    kv_blk = (5 * A_HEADS * A_DK + B_Q_LORA) // B_KV_LORA
    dq = B_NOPE + B_ROPE
    s_all = seq + ctx_len
    w3 = w_uq.reshape(B_Q_LORA, h, dq)
    wq_n = w3[:, :, :B_NOPE].reshape(B_Q_LORA, h * B_NOPE)
    wq_r = w3[:, :, B_NOPE:].reshape(B_Q_LORA, h * B_ROPE)
    in_specs = [
        pl.BlockSpec((TQ, B_Q_LORA), lambda b, qt: (row_block(b, qt), ql_blk)),
        pl.BlockSpec((seq, B_KV_LORA), lambda b, qt: (b, kv_blk)),
        pl.BlockSpec((ctx_len, B_KV_LORA), lambda b, qt: (ctx_blk0 + b, kv_blk)),
        pl.BlockSpec((seq, B_ROPE), lambda b, qt: (b, 0)),
        pl.BlockSpec((ctx_len, B_ROPE), lambda b, qt: (ctx_blk0 + b, 0)),
        pl.BlockSpec((B_Q_LORA, h * B_NOPE), lambda b, qt: (0, 0)),
        pl.BlockSpec((B_Q_LORA, h * B_ROPE), lambda b, qt: (0, 0)),
        pl.BlockSpec((B_KV_LORA, h * (B_NOPE + B_DV)), lambda b, qt: (0, 0)),
        pl.BlockSpec((1, B_Q_LORA), lambda b, qt: (0, 0)),
        pl.BlockSpec((1, B_KV_LORA), lambda b, qt: (0, 0)),
        pl.BlockSpec((TQ, B_ROPE), lambda b, qt: (jnp.maximum(qt - nct, 0), 0)),
        pl.BlockSpec((TQ, B_ROPE), lambda b, qt: (jnp.maximum(qt - nct, 0), 0)),
        pl.BlockSpec((seq, B_ROPE), lambda b, qt: (0, 0)),
        pl.BlockSpec((seq, B_ROPE), lambda b, qt: (0, 0)),
    ]
    return pl.pallas_call(
        functools.partial(_mla_kernel, ctx_len=ctx_len),
        grid=(batch, nct + nlt),
        in_specs=in_specs,
        out_specs=pl.BlockSpec((TQ, h * B_DV), lambda b, qt: (row_block(b, qt), 0)),
        out_shape=jax.ShapeDtypeStruct((batch * s_all, h * B_DV), BF16),
        scratch_shapes=[pltpu.VMEM((h, s_all, B_NOPE), BF16), pltpu.VMEM((s_all, B_ROPE), BF16),
                        pltpu.VMEM((h, s_all, B_DV), BF16)],
        compiler_params=_cp(("parallel", "arbitrary")),
        name="mla_attn",
    )(p, p, p, kr, kr, wq_n, wq_r, w_ukv, gq.reshape(1, -1), gkv.reshape(1, -1), cos, sin, cos, sin)


def _gqa_kernel(q_ref, kl_ref, kc_ref, vl_ref, vc_ref, gq_ref, gk_ref, cosq_ref, sinq_ref, cosk_ref, sink_ref,
                o_ref, k_s, v_s, *, ctx_len):
    qt = pl.program_id(2)
    scale = C_DH ** -0.5
    dh = C_DH

    @pl.when(qt == 0)
    def _prep():
        k_s[0:ctx_len, :] = _rms(kc_ref[...].astype(F32), gk_ref[...]).astype(BF16)
        kl = _rms(kl_ref[...].astype(F32), gk_ref[...])
        k_s[ctx_len:, :] = _rope(kl, cosk_ref[...], sink_ref[...]).astype(BF16)
        v_s[0:ctx_len, :] = vc_ref[...]
        v_s[ctx_len:, :] = vl_ref[...]

    outs = []
    for g in range(C_HEADS // C_KV_HEADS):
        q = _rms(q_ref[:, g * dh:(g + 1) * dh].astype(F32), gq_ref[...])
        q = _rope(q, cosq_ref[...], sinq_ref[...]) * scale
        s = _dot_nt(q.astype(BF16), k_s[...])
        outs.append(_softmax_pv(s, v_s[...]).astype(o_ref.dtype))
    o_ref[...] = jnp.concatenate(outs, axis=-1)


def _gqa(p, gq, gk, cos, sin, batch, seq, ctx_len):
    kvh, grp, dh = C_KV_HEADS, C_HEADS // C_KV_HEADS, C_DH
    nlt = seq // TQ
    ctx_blk0 = batch * seq // ctx_len
    k0 = C_HEADS
    v0 = C_HEADS + C_KV_HEADS
    s_all = seq + ctx_len
    in_specs = [
        pl.BlockSpec((TQ, grp * dh), lambda b, kh, qt: (b * nlt + qt, kh)),
        pl.BlockSpec((seq, dh), lambda b, kh, qt: (b, k0 + kh)),
        pl.BlockSpec((ctx_len, dh), lambda b, kh, qt: (ctx_blk0 + b, k0 + kh)),
        pl.BlockSpec((seq, dh), lambda b, kh, qt: (b, v0 + kh)),
        pl.BlockSpec((ctx_len, dh), lambda b, kh, qt: (ctx_blk0 + b, v0 + kh)),
        pl.BlockSpec((1, dh), lambda b, kh, qt: (0, 0)),
        pl.BlockSpec((1, dh), lambda b, kh, qt: (0, 0)),
        pl.BlockSpec((TQ, dh), lambda b, kh, qt: (qt, 0)),
        pl.BlockSpec((TQ, dh), lambda b, kh, qt: (qt, 0)),
        pl.BlockSpec((seq, dh), lambda b, kh, qt: (0, 0)),
        pl.BlockSpec((seq, dh), lambda b, kh, qt: (0, 0)),
    ]
    return pl.pallas_call(
        functools.partial(_gqa_kernel, ctx_len=ctx_len),
        grid=(batch, kvh, nlt),
        in_specs=in_specs,
        out_specs=pl.BlockSpec((TQ, grp * dh), lambda b, kh, qt: (b * nlt + qt, kh)),
        out_shape=jax.ShapeDtypeStruct((batch * seq, C_HEADS * dh), BF16),
        scratch_shapes=[pltpu.VMEM((s_all, dh), BF16), pltpu.VMEM((s_all, dh), BF16)],
        compiler_params=_cp(("parallel", "parallel", "arbitrary")),
        name="gqa_attn",
    )(p, p, p, p, p, gq.reshape(1, dh), gk.reshape(1, dh), cos, sin, cos, sin)


def _router_kernel(x_ref, g_ref, sh_ref, sc_ref, rw_ref, rb_ref, h_ref, ri_ref, rf_ref, cnt_ref, base_s):
    i = pl.program_id(0)
    tm = x_ref.shape[0]
    ne = N_EXPERTS
    per = ne // N_GROUPS

    @pl.when(i == 0)
    def _():
        base_s[...] = jnp.zeros_like(base_s)

    h = _rms(x_ref[...], g_ref[...]) * (1.0 + sc_ref[...]) + sh_ref[...]
    h_ref[...] = h
    logits = jnp.dot(h, rw_ref[...], preferred_element_type=F32, precision=HI)
    scores = jax.nn.sigmoid(logits)
    sel = scores + rb_ref[...]
    lane = lax.broadcasted_iota(jnp.int32, (tm, ne), 1).astype(F32)
    neg = -jnp.inf
    big = float(ne)

    def top2(vals):
        m1 = jnp.max(vals, axis=1, keepdims=True)
        i1 = jnp.min(jnp.where(vals == m1, lane, big), axis=1, keepdims=True)
        rest = jnp.where(lane == i1, neg, vals)
        m2 = jnp.max(rest, axis=1, keepdims=True)
        i2 = jnp.min(jnp.where(rest == m2, lane, big), axis=1, keepdims=True)
        return m1 + m2, i1, i2

    best, e1, e2 = None, None, None
    for grp in range(N_GROUPS):
        in_grp = jnp.logical_and(lane >= float(grp * per), lane < float((grp + 1) * per))
        gsum, i1, i2 = top2(jnp.where(in_grp, sel, neg))
        if grp == 0:
            best, e1, e2 = gsum, i1, i2
        else:
            better = gsum > best
            best = jnp.where(better, gsum, best)
            e1 = jnp.where(better, i1, e1)
            e2 = jnp.where(better, i2, e2)

    hot1 = lane == e1
    hot2 = lane == e2
    w1 = jnp.sum(jnp.where(hot1, scores, 0.0), axis=1, keepdims=True)
    w2 = jnp.sum(jnp.where(hot2, scores, 0.0), axis=1, keepdims=True)
    wsum = w1 + w2
    assign = jnp.logical_or(hot1, hot2)
    r = lax.broadcasted_iota(jnp.int32, (tm, tm), 0)
    c = lax.broadcasted_iota(jnp.int32, (tm, tm), 1)
    before = (c < r).astype(BF16)
    excl = jnp.dot(before, assign.astype(BF16), preferred_element_type=F32) + base_s[...]
    rank1 = jnp.sum(jnp.where(hot1, excl, 0.0), axis=1, keepdims=True)
    rank2 = jnp.sum(jnp.where(hot2, excl, 0.0), axis=1, keepdims=True)
    base_s[...] = base_s[...] + jnp.sum(assign.astype(F32), axis=0, keepdims=True)

    l128 = lax.broadcasted_iota(jnp.int32, (tm, 128), 1)
    ri = jnp.where(l128 == 0, e1, jnp.where(l128 == 1, e2, jnp.where(l128 == 2, rank1, jnp.where(l128 == 3, rank2, 0.0))))
    ri_ref[...] = ri.T[0:8, :].astype(jnp.int32)
    rf_ref[...] = jnp.where(l128 == 0, w1 / wsum, jnp.where(l128 == 1, w2 / wsum, 0.0))
    cnt_ref[...] = jnp.broadcast_to(base_s[...], cnt_ref.shape)


def _router(x_all, g, mods5, layer, router_w, router_b, rows, n_tiles):
    d = x_all.shape[-1]
    tm = rows.tm
    n = n_tiles * tm
    ne = N_EXPERTS
    return pl.pallas_call(
        _router_kernel,
        grid=(n_tiles,),
        in_specs=[pl.BlockSpec((tm, d), lambda i: (i, 0)),
                  pl.BlockSpec((1, d), lambda i: (0, 0)),
                  _mod_spec(rows, layer, 3, d),
                  _mod_spec(rows, layer, 4, d),
                  pl.BlockSpec((d, ne), lambda i: (0, 0)),
                  pl.BlockSpec((1, ne), lambda i: (0, 0))],
        out_specs=[pl.BlockSpec((tm, d), lambda i: (i, 0)),
                   pl.BlockSpec((8, tm), lambda i: (0, i)),
                   pl.BlockSpec((tm, 128), lambda i: (i, 0)),
                   pl.BlockSpec((8, ne), lambda i: (0, 0))],
        out_shape=[jax.ShapeDtypeStruct((n, d), F32),
                   jax.ShapeDtypeStruct((8, n), jnp.int32),
                   jax.ShapeDtypeStruct((n, 128), F32),
                   jax.ShapeDtypeStruct((8, ne), F32)],
        scratch_shapes=[pltpu.VMEM((1, ne), F32)],
        compiler_params=_cp(("arbitrary",)),
        name="moe_router",
    )(x_all, g.reshape(1, d), mods5, mods5, router_w, router_b.reshape(1, ne))


def _expert_kernel(te_ref, nu_ref, dest_ref, h_hbm, w1_ref, w3_ref, w2_ref, y_ref, xbuf, sem, w1_s, w3_s, w2_s,
                   src_ref, *, n_tok):
    r = pl.program_id(0)
    n_used = nu_ref[0]
    active = r < n_used
    changed = jnp.logical_or(r == 0, te_ref[r] != te_ref[jnp.maximum(r - 1, 0)])
    slot = lax.rem(r, 2)

    @pl.when(r == 0)
    def _():
        def clear(i, carry):
            src_ref[i] = 0
            return carry

        lax.fori_loop(0, src_ref.shape[0], clear, 0, unroll=DMA_UNROLL)

        def invert(i, carry):
            src_ref[dest_ref[i]] = i
            src_ref[dest_ref[n_tok + i]] = i
            return carry

        lax.fori_loop(0, n_tok, invert, 0, unroll=DMA_UNROLL)

    def gather(tile, dst_slot):
        base = tile * TE
        for t in range(TE):
            pltpu.make_async_copy(h_hbm.at[pl.ds(src_ref[base + t], 1)], xbuf.at[dst_slot, pl.ds(t, 1)],
                                  sem.at[dst_slot]).start(priority=t % 2)

    @pl.when(jnp.logical_and(r == 0, active))
    def _():
        gather(0, 0)

    @pl.when(r + 1 < n_used)
    def _():
        gather(r + 1, 1 - slot)

    @pl.when(jnp.logical_and(active, changed))
    def _():
        w1_s[...] = w1_ref[...].astype(BF16)
        w3_s[...] = w3_ref[...].astype(BF16)
        w2_s[...] = w2_ref[...].astype(BF16)

    @pl.when(active)
    def _():
        pltpu.make_async_copy(h_hbm.at[pl.ds(0, TE)], xbuf.at[slot], sem.at[slot]).wait()
        x = xbuf[slot].astype(BF16)
        a = jnp.dot(x, w1_s[...], preferred_element_type=F32)
        b = jnp.dot(x, w3_s[...], preferred_element_type=F32)
        hid = (a * jax.nn.sigmoid(a) * b).astype(BF16)
        y_ref[...] = jnp.dot(hid, w2_s[...], preferred_element_type=F32)

    @pl.when(jnp.logical_not(active))
    def _():
        y_ref[...] = jnp.zeros_like(y_ref)


def _experts(tile_expert, n_used, dest, hp, w1, w3, w2, layer, p_max):
    n_tok, d = hp.shape
    f = w1.shape[-1]
    grid_spec = pltpu.PrefetchScalarGridSpec(
        num_scalar_prefetch=3,
        grid=(p_max // TE,),
        in_specs=[pl.BlockSpec(memory_space=pl.ANY),
                  pl.BlockSpec((None, None, d, f), lambda r, te, nu, sr: (layer, te[r], 0, 0)),
                  pl.BlockSpec((None, None, d, f), lambda r, te, nu, sr: (layer, te[r], 0, 0)),
                  pl.BlockSpec((None, None, f, d), lambda r, te, nu, sr: (layer, te[r], 0, 0))],
        out_specs=pl.BlockSpec((TE, d), lambda r, te, nu, sr: (r, 0)),
        scratch_shapes=[pltpu.VMEM((2, TE, d), F32), pltpu.SemaphoreType.DMA((2,)),
                        pltpu.VMEM((d, f), BF16), pltpu.VMEM((d, f), BF16), pltpu.VMEM((f, d), BF16),
                        pltpu.SMEM((p_max,), jnp.int32)],
    )
    return pl.pallas_call(
        functools.partial(_expert_kernel, n_tok=n_tok),
        grid_spec=grid_spec,
        out_shape=jax.ShapeDtypeStruct((p_max, d), F32),
        compiler_params=_cp(("arbitrary",)),
        name="moe_experts",
    )(tile_expert, n_used, dest, hp, w1, w3, w2)


def _combine_kernel(dest_ref, x_ref, rf_ref, g_ref, fg_ref, ys_hbm, o_ref, buf, sem, *, tm, n_tok, n_tiles, final):
    i = pl.program_id(0)
    slot = lax.rem(i, 2)

    def gather(tile, dst_slot):
        base = tile * tm
        for t in range(tm):
            pltpu.make_async_copy(ys_hbm.at[pl.ds(dest_ref[base + t], 1)], buf.at[dst_slot, 0, pl.ds(t, 1)],
                                  sem.at[dst_slot]).start(priority=0)
            pltpu.make_async_copy(ys_hbm.at[pl.ds(dest_ref[n_tok + base + t], 1)], buf.at[dst_slot, 1, pl.ds(t, 1)],
                                  sem.at[dst_slot]).start(priority=1)

    @pl.when(i == 0)
    def _():
        gather(0, 0)

    @pl.when(i + 1 < n_tiles)
    def _():
        gather(i + 1, 1 - slot)

    pltpu.make_async_copy(ys_hbm.at[pl.ds(0, tm)], buf.at[slot, 0], sem.at[slot]).wait()
    pltpu.make_async_copy(ys_hbm.at[pl.ds(0, tm)], buf.at[slot, 1], sem.at[slot]).wait()
    w = rf_ref[...]
    y = w[:, 0:1] * buf[slot, 0] + w[:, 1:2] * buf[slot, 1]
    x2 = x_ref[...] + g_ref[...] * y
    if final:
        x2 = _rms(x2, fg_ref[...])
    o_ref[...] = x2


def _combine(dest, x_all, rf, mods5, layer, final_g, ys, rows, n_tiles, n_tok_total, final):
    d = x_all.shape[-1]
    tm = rows.tm
    grid_spec = pltpu.PrefetchScalarGridSpec(
        num_scalar_prefetch=1,
        grid=(n_tiles,),
        in_specs=[pl.BlockSpec((tm, d), lambda i, dr: (i, 0)),
                  pl.BlockSpec((tm, 128), lambda i, dr: (i, 0)),
                  _mod_spec(rows, layer, 5, d),
                  pl.BlockSpec((1, d), lambda i, dr: (0, 0)),
                  pl.BlockSpec(memory_space=pl.ANY)],
        out_specs=pl.BlockSpec((tm, d), lambda i, dr: (i, 0)),
        scratch_shapes=[pltpu.VMEM((2, 2, tm, d), F32), pltpu.SemaphoreType.DMA((2,))],
    )
    return pl.pallas_call(
        functools.partial(_combine_kernel, tm=tm, n_tok=n_tok_total, n_tiles=n_tiles, final=final),
        grid_spec=grid_spec,
        out_shape=jax.ShapeDtypeStruct((n_tiles * tm, d), F32),
        compiler_params=_cp(("arbitrary",)),
        name="moe_combine",
    )(dest, x_all, rf, mods5, final_g.reshape(1, d), ys)


def _moe(x_all, n_tok, norm_g, mods5, layer, router_w, router_b, w1, w3, w2, final_g, final, batch, seq, ctx_len):
    rows_r = _Rows(batch, seq, ctx_len, TROUTE)
    rows_c = _Rows(batch, seq, ctx_len, TCOMB)
    h, ri, rf, cnt = _router(x_all, norm_g, mods5, layer, router_w, router_b, rows_r, n_tok // TROUTE)
    counts = cnt[0].astype(jnp.int32)
    padded = ((counts + TE - 1) // TE) * TE
    ends = jnp.cumsum(padded)
    starts = ends - padded
    e1, e2, r1, r2 = ri[0], ri[1], ri[2], ri[3]
    dest = jnp.concatenate([starts[e1] + r1, starts[e2] + r2]).astype(jnp.int32)
    p_max = 2 * n_tok + N_EXPERTS * TE
    n_tiles = p_max // TE
    n_used = (ends[-1] // TE).astype(jnp.int32)
    tile_start = jnp.arange(n_tiles, dtype=jnp.int32) * TE
    tile_expert = jnp.sum((tile_start[:, None] >= ends[None, :]).astype(jnp.int32), axis=1)
    last_expert = jnp.sum((jnp.maximum(ends[-1] - 1, 0) >= ends).astype(jnp.int32))
    tile_expert = jnp.minimum(jnp.where(tile_start < ends[-1], tile_expert, last_expert), N_EXPERTS - 1).astype(jnp.int32)
    ys = _experts(tile_expert, n_used.reshape(1), dest, h, w1, w3, w2, layer, p_max)
    return _combine(dest, x_all, rf, mods5, layer, final_g, ys, rows_c, n_tok // TCOMB, n_tok, final)


def _rope_tables(t_len, d_rope):
    rows = t_len // GRID_W
    quarter = d_rope // 4
    freqs = ROPE_THETA ** (-jnp.arange(quarter, dtype=F32) / quarter)
    row = jnp.repeat(jnp.arange(rows, dtype=F32), GRID_W)
    col = jnp.tile(jnp.arange(GRID_W, dtype=F32), rows)
    ang = jnp.concatenate([row[:, None] * freqs, col[:, None] * freqs], axis=-1)
    cos, sin = jnp.cos(ang), jnp.sin(ang)
    return jnp.concatenate([cos, cos], axis=-1), jnp.concatenate([-sin, sin], axis=-1)


def kernel(x, c, ctx, c_ctx, mod_w, mod_b, norm_attn_g, norm_ffn_g, final_norm_g, ab_w_in, ab_w_out, hgrn_lb_logits, hgrn_norm_g, mla_q_norm_g, mla_w_uq, mla_kv_norm_g, mla_w_ukv, cd_w_in, cd_w_out, gqa_q_norm_g, gqa_k_norm_g, gla_w_a2, gla_b_a, gla_norm_g, router_w, router_b, moe_w1, moe_w3, moe_w2):
    batch, seq, d = x.shape
    ctx_len = ctx.shape[1]
    n_lat, n_ctx = batch * seq, batch * ctx_len
    assert ctx_len % TQ == 0 and seq % TQ == 0 and seq % ctx_len == 0 and batch < 8
    tm = min(1024, seq, n_ctx)
    rows = _Rows(batch, seq, ctx_len, tm)

    cvec = jnp.concatenate([c, c_ctx[None, :], jnp.zeros((8 - batch - 1, d), F32)], axis=0)
    mods = _modvec(cvec, mod_w, mod_b)
    mods5 = mods.reshape(mods.shape[0], 8, 6, 1, d)

    cos_b, sin_b = _rope_tables(seq, B_ROPE)
    cos_c, sin_c = _rope_tables(seq, C_DH)
    lb = jnp.cumsum(jax.nn.softmax(hgrn_lb_logits.astype(F32), axis=1), axis=1)

    x_lat = x.reshape(n_lat, d)
    x_ctx = ctx.reshape(n_ctx, d)

    h0 = _norm_mod(x_lat, x_ctx, 0, norm_attn_g[0], mods5, 0, rows)
    ab_main = 5 * A_HEADS * A_DK + B_Q_LORA + B_KV_LORA
    tm_mm = next(t for t in (2304, 2048, 1536, 1024, 512, 256) if (n_lat + n_ctx) % t == 0)
    p0 = _matmul(h0, ab_w_in, 0, ab_main, 256, tm_mm)
    kr0 = _matmul_tail(h0, ab_w_in, 0, ab_main, B_ROPE, tm_mm)
    mix_a = _hgrn(p0, lb[0, 0], lb[1, 0], hgrn_norm_g[0], batch, seq, ctx_len)
    mix_b = _mla(p0, kr0, mla_w_uq[0], mla_w_ukv[0], mla_q_norm_g[0], mla_kv_norm_g[0], cos_b, sin_b,
                 batch, seq, ctx_len)
    half = mix_a.shape[-1]
    w_out0 = ab_w_out.reshape(ab_w_out.shape[0] * 2, half, d)
    x1 = _out_proj(mix_a, mix_b, w_out0, 0, x_lat, x_ctx, 0, mods5, 0, rows, rows.n_all)
    x2 = _moe(x1, n_lat + n_ctx, norm_ffn_g[0], mods5, 0, router_w, router_b, moe_w1, moe_w3, moe_w2,
              final_norm_g, False, batch, seq, ctx_len)

    h1 = _norm_mod(x2, x2, rows.n_lat, norm_attn_g[1], mods5, 1, rows)
    cd_main = (C_HEADS + 2 * C_KV_HEADS) * C_DH + 2 * D_HEADS * D_DK + 2 * D_HEADS * D_DV
    p1 = _matmul(h1, cd_w_in, 0, cd_main, 512, tm_mm)
    ga1 = _matmul_tail(h1, cd_w_in, 0, cd_main, 2 * D_GATE_RANK, tm_mm)
    mix_c = _gqa(p1, gqa_q_norm_g[0], gqa_k_norm_g[0], cos_c, sin_c, batch, seq, ctx_len)
    mix_d = _gla(p1, ga1, gla_w_a2[0], gla_b_a[0], gla_norm_g[0], batch, seq, ctx_len)
    w_out1 = cd_w_out.reshape(cd_w_out.shape[0] * 2, mix_c.shape[-1], d)
    x3 = _out_proj(mix_c, mix_d, w_out1, 0, x2, x2, rows.n_lat, mods5, 1, rows, rows.n_lat)
    out = _moe(x3, n_lat, norm_ffn_g[1], mods5, 1, router_w, router_b, moe_w1, moe_w3, moe_w2,
               final_norm_g, True, batch, seq, ctx_len)
    return out.reshape(batch, seq, d)
```

```python
import functools

import jax
import jax.numpy as jnp
from jax import lax
from jax.experimental import pallas as pl
from jax.experimental.pallas import tpu as pltpu

F32 = jnp.float32
BF16 = jnp.bfloat16
HI = lax.Precision.HIGHEST

GRID_W = 64
ROPE_THETA = 10000.0
NORM_EPS = 1e-6
A_HEADS, A_DK, A_DV = 8, 128, 128
B_HEADS, B_Q_LORA, B_KV_LORA, B_NOPE, B_ROPE, B_DV = 8, 512, 256, 128, 64, 128
C_HEADS, C_KV_HEADS, C_DH = 8, 2, 128
D_HEADS, D_DK, D_DV, D_GATE_RANK = 4, 128, 256, 16
GLA_TAU = 16.0
N_EXPERTS, N_GROUPS = 16, 4

TQ = 256
SCAN_C = 64
SCAN_BLOCK = 1024
TE = 256
TROUTE = 512
TCOMB = 256
DMA_UNROLL = 8
VMEM_MIB = 56


def _cp(sem):
    return pltpu.CompilerParams(dimension_semantics=sem, vmem_limit_bytes=VMEM_MIB * 1024 * 1024)


def _rms(x, g):
    return x * lax.rsqrt(jnp.mean(x * x, axis=-1, keepdims=True) + NORM_EPS) * g


def _rope(x, cos, sin):
    half = x.shape[-1] // 2
    swapped = jnp.concatenate([x[:, half:], x[:, :half]], axis=-1)
    return x * cos + swapped * sin


def _dot_nt(a, b):
    return lax.dot_general(a, b, (((1,), (1,)), ((), ())), preferred_element_type=F32)


def _dot_tn(a, b):
    return lax.dot_general(a, b, (((0,), (0,)), ((), ())), preferred_element_type=F32)


def _modvec_kernel(c_ref, w_ref, b_ref, o_ref):
    c = c_ref[...]
    a = c * jax.nn.sigmoid(c)
    o_ref[...] = jnp.dot(a, w_ref[...], preferred_element_type=F32, precision=HI) + b_ref[...]


def _modvec(cvec, mod_w, mod_b):
    n_layers, d, n6 = mod_w.shape
    tn = min(1024, n6)
    return pl.pallas_call(
        _modvec_kernel,
        grid=(n_layers, n6 // tn),
        in_specs=[pl.BlockSpec((8, d), lambda l, j: (0, 0)),
                  pl.BlockSpec((None, d, tn), lambda l, j: (l, 0, j)),
                  pl.BlockSpec((None, 1, tn), lambda l, j: (l, 0, j))],
        out_specs=pl.BlockSpec((None, 8, tn), lambda l, j: (l, 0, j)),
        out_shape=jax.ShapeDtypeStruct((n_layers, 8, n6), F32),
        compiler_params=_cp(("parallel", "parallel")),
        name="modvec",
    )(cvec, mod_w, mod_b.reshape(n_layers, 1, n6))


class _Rows:
    def __init__(self, batch, seq, ctx_len, tm):
        assert seq % tm == 0 and (batch * ctx_len) % tm == 0
        self.tm = tm
        self.batch = batch
        self.per_batch = seq // tm
        self.n_lat = batch * seq // tm
        self.n_ctx = batch * ctx_len // tm
        self.n_all = self.n_lat + self.n_ctx

    def mod_row(self, i):
        return jnp.where(i < self.n_lat, i // self.per_batch, self.batch)


def _mod_spec(rows, layer, chunk, d):
    return pl.BlockSpec((None, None, None, 1, d), lambda i, *_: (layer, rows.mod_row(i), chunk, 0, 0))


def _norm_mod_kernel(xl_ref, xc_ref, g_ref, sh_ref, sc_ref, o_ref, *, n_lat):
    i = pl.program_id(0)

    def body(x_ref):
        y = _rms(x_ref[...], g_ref[...])
        o_ref[...] = (y * (1.0 + sc_ref[...]) + sh_ref[...]).astype(o_ref.dtype)

    @pl.when(i < n_lat)
    def _():
        body(xl_ref)

    @pl.when(i >= n_lat)
    def _():
        body(xc_ref)


def _norm_mod(x_lat, x_ctx, ctx_block0, g, mods5, layer, rows):
    d = x_lat.shape[-1]
    tm = rows.tm
    nl = rows.n_lat
    return pl.pallas_call(
        functools.partial(_norm_mod_kernel, n_lat=nl),
        grid=(rows.n_all,),
        in_specs=[pl.BlockSpec((tm, d), lambda i: (jnp.minimum(i, nl - 1), 0)),
                  pl.BlockSpec((tm, d), lambda i: (ctx_block0 + jnp.maximum(i - nl, 0), 0)),
                  pl.BlockSpec((1, d), lambda i: (0, 0)),
                  _mod_spec(rows, layer, 0, d),
                  _mod_spec(rows, layer, 1, d)],
        out_specs=pl.BlockSpec((tm, d), lambda i: (i, 0)),
        out_shape=jax.ShapeDtypeStruct((rows.n_all * tm, d), BF16),
        compiler_params=_cp(("parallel",)),
        name="norm_mod",
    )(x_lat, x_ctx, g.reshape(1, d), mods5, mods5)


def _mm_kernel(a_ref, w_ref, o_ref):
    o_ref[...] = jnp.dot(a_ref[...], w_ref[...].astype(BF16), preferred_element_type=F32).astype(o_ref.dtype)


def _matmul(a, w3, layer, n_cols, tn, tm):
    m, k = a.shape
    return pl.pallas_call(
        _mm_kernel,
        grid=(m // tm, n_cols // tn),
        in_specs=[pl.BlockSpec((tm, k), lambda i, j: (i, 0)),
                  pl.BlockSpec((None, k, tn), lambda i, j: (layer, 0, j))],
        out_specs=pl.BlockSpec((tm, tn), lambda i, j: (i, j)),
        out_shape=jax.ShapeDtypeStruct((m, n_cols), BF16),
        compiler_params=_cp(("parallel", "arbitrary")),
        name="in_proj",
    )(a, w3)


def _mm_tail_kernel(a_ref, w_ref, o_ref):
    n = o_ref.shape[-1]
    acc = jnp.dot(a_ref[...], w_ref[...].astype(BF16), preferred_element_type=F32)
    o_ref[...] = acc[:, :n].astype(o_ref.dtype)


def _matmul_tail(a, w3, layer, col0, n_cols, tm):
    m, k = a.shape
    lane = 128
    assert col0 % lane == 0 and n_cols <= lane and col0 + n_cols == w3.shape[-1]
    return pl.pallas_call(
        _mm_tail_kernel,
        grid=(m // tm,),
        in_specs=[pl.BlockSpec((tm, k), lambda i: (i, 0)),
                  pl.BlockSpec((None, k, lane), lambda i: (layer, 0, col0 // lane))],
        out_specs=pl.BlockSpec((tm, n_cols), lambda i: (i, 0)),
        out_shape=jax.ShapeDtypeStruct((m, n_cols), BF16),
        compiler_params=_cp(("parallel",)),
        name="in_proj_tail",
    )(a, w3)


def _out_proj_kernel(ma_ref, mb_ref, w_hbm, xl_ref, xc_ref, g_ref, o_ref, w_s, stage, *, n_lat, widx):
    i = pl.program_id(0)
    ka = ma_ref.shape[-1]

    @pl.when(i == 0)
    def _():
        rows = stage.shape[0]
        for c in range(w_s.shape[0] // rows):
            pltpu.sync_copy(w_hbm.at[widx, pl.ds(c * rows, rows)], stage)
            w_s[c * rows:(c + 1) * rows, :] = stage[...].astype(BF16)

    acc = jnp.dot(ma_ref[...], w_s[:ka, :], preferred_element_type=F32)
    acc += jnp.dot(mb_ref[...], w_s[ka:, :], preferred_element_type=F32)
    upd = g_ref[...] * acc

    @pl.when(i < n_lat)
    def _():
        o_ref[...] = xl_ref[...] + upd

    @pl.when(i >= n_lat)
    def _():
        o_ref[...] = xc_ref[...] + upd


def _out_proj(mix_a, mix_b, w_out, widx, x_lat, x_ctx, ctx_block0, mods5, layer, rows, n_tiles):
    d = x_lat.shape[-1]
    ka, kb = mix_a.shape[-1], mix_b.shape[-1]
    tm = rows.tm
    nl = rows.n_lat
    stage_rows = min(512, ka + kb)
    return pl.pallas_call(
        functools.partial(_out_proj_kernel, n_lat=nl, widx=widx),
        grid=(n_tiles,),
        in_specs=[pl.BlockSpec((tm, ka), lambda i: (i, 0)),
                  pl.BlockSpec((tm, kb), lambda i: (i, 0)),
                  pl.BlockSpec(memory_space=pl.ANY),
                  pl.BlockSpec((tm, d), lambda i: (jnp.minimum(i, nl - 1), 0)),
                  pl.BlockSpec((tm, d), lambda i: (ctx_block0 + jnp.maximum(i - nl, 0), 0)),
                  _mod_spec(rows, layer, 2, d)],
        out_specs=pl.BlockSpec((tm, d), lambda i: (i, 0)),
        out_shape=jax.ShapeDtypeStruct((n_tiles * tm, d), F32),
        scratch_shapes=[pltpu.VMEM((ka + kb, d), BF16), pltpu.VMEM((stage_rows, d), F32)],
        compiler_params=_cp(("arbitrary",)),
        name="out_proj",
    )(mix_a, mix_b, w_out, x_lat, x_ctx, mods5)


def _tri(c, upper):
    r = lax.broadcasted_iota(jnp.int32, (c, c), 0)
    s = lax.broadcasted_iota(jnp.int32, (c, c), 1)
    return (s >= r) if upper else (r >= s)


def _scan_block(q, k, v, g, st, mask, forward):
    c = SCAN_C
    dk, dv = q.shape[-1], v.shape[-1]
    n = q.shape[0] // c
    mid, last = (c // 2 - 1, c - 1) if forward else (c // 2, 0)
    tri = jnp.broadcast_to(mask.astype(BF16)[None], (n, c, c))
    g3 = g.reshape(n, c, dk)
    g_hi = g3.astype(BF16)
    g_lo = (g3 - g_hi.astype(F32)).astype(BF16)
    cum = (jnp.einsum('cts,csd->ctd', tri, g_hi, preferred_element_type=F32)
           + jnp.einsum('cts,csd->ctd', tri, g_lo, preferred_element_type=F32))
    m = cum[:, mid:mid + 1, :]
    tot = cum[:, last:last + 1, :]
    qe = (q.reshape(n, c, dk) * jnp.exp(cum - m)).astype(BF16)
    ke = (k.reshape(n, c, dk) * jnp.exp(m - cum)).astype(BF16)
    a = jnp.einsum('ctd,csd->cts', qe, ke, preferred_element_type=F32)
    a = jnp.where(mask[None], a, 0.0).astype(BF16)
    v3 = v.reshape(n, c, dv)
    o = jnp.einsum('cts,csv->ctv', a, v3, preferred_element_type=F32)
    u = jnp.einsum('csv,csd->cvd', v3, ke, preferred_element_type=F32)
    em = jnp.exp(m)
    et = jnp.exp(tot - m)
    states = [None] * n
    for ci in (range(n) if forward else reversed(range(n))):
        stp = st * em[ci]
        states[ci] = stp.astype(BF16)
        st = (stp + u[ci]) * et[ci]
    o = o + jnp.einsum('ctd,cvd->ctv', qe, jnp.stack(states), preferred_element_type=F32)
    return o.reshape(n * c, dv), st


def _scan_segments(segments, prep_f, prep_b, of_ref, ob_ref, dk, dv):
    low, up = _tri(SCAN_C, False), _tri(SCAN_C, True)
    carry = (jnp.zeros((dv, dk), F32), jnp.zeros((dv, dk), F32))
    for rows, off, seg in segments:
        rb_ = min(SCAN_BLOCK, rows)
        n = rows // rb_

        def body(i, carry, n=n, off=off, seg=seg, rb_=rb_):
            sf, sb = carry
            rf = pl.multiple_of(i * rb_, rb_)
            rb = pl.multiple_of((n - 1 - i) * rb_, rb_)
            q, k, v, g = prep_f(seg, rf, rb_)
            o, sf = _scan_block(q, k, v, g, sf, low, True)
            of_ref[pl.ds(off + rf, rb_), :] = o
            q, k, v, g = prep_b(seg, rb, rb_)
            o, sb = _scan_block(q, k, v, g, sb, up, False)
            ob_ref[pl.ds(off + rb, rb_), :] = o
            return sf, sb

        carry = lax.fori_loop(0, n, body, carry)


def _hgrn_kernel(ql, qc, f1l, f1c, f2l, f2c, vl, vc, gl, gc, lbf, lbb, ng, o_ref, of_s, ob_s, *, ctx_len, seq):
    rt = pl.program_id(2)
    n_ctx_tiles = ctx_len // TQ

    @pl.when(rt == 0)
    def _scan():
        refs = {0: (qc, f1c, f2c, vc), 1: (ql, f1l, f2l, vl)}
        scale = A_DK ** -0.5

        def prep(seg, r, nr, fi, lb_ref):
            x = refs[seg][0][pl.ds(r, nr), :].astype(F32)
            q = x * jax.nn.sigmoid(x) * scale
            v = refs[seg][3][pl.ds(r, nr), :]
            lb = lb_ref[...]
            f = lb + (1.0 - lb) * jax.nn.sigmoid(refs[seg][fi][pl.ds(r, nr), :].astype(F32))
            return q, 1.0 - f, v, jnp.log(f)

        _scan_segments(
            [(ctx_len, 0, 0), (seq, ctx_len, 1)],
            lambda seg, r, nr: prep(seg, r, nr, 1, lbf),
            lambda seg, r, nr: prep(seg, r, nr, 2, lbb),
            of_s, ob_s, A_DK, A_DV)

    r0 = pl.multiple_of(rt * TQ, TQ)
    o = of_s[pl.ds(r0, TQ), :] + ob_s[pl.ds(r0, TQ), :]
    y = _rms(o, ng[...])

    @pl.when(rt < n_ctx_tiles)
    def _():
        gate = gc[pl.ds(r0, TQ), :].astype(F32)
        o_ref[...] = (y * jax.nn.sigmoid(gate)).astype(o_ref.dtype)

    @pl.when(rt >= n_ctx_tiles)
    def _():
        gate = gl[pl.ds(pl.multiple_of(r0 - ctx_len, TQ), TQ), :].astype(F32)
        o_ref[...] = (y * jax.nn.sigmoid(gate)).astype(o_ref.dtype)


def _out_row_block(batch, seq, ctx_len):
    nct = ctx_len // TQ
    nlt = seq // TQ

    def f(b, rt):
        return jnp.where(rt < nct, batch * nlt + b * nct + rt, b * nlt + rt - nct)

    return f


def _hgrn(p, lb_f, lb_b, norm_g, batch, seq, ctx_len):
    h, dk, dv = A_HEADS, A_DK, A_DV
    nct, nlt = ctx_len // TQ, seq // TQ
    ctx_blk0 = batch * seq // ctx_len
    row_block = _out_row_block(batch, seq, ctx_len)
    in_specs = []
    for kcol in range(5):
        in_specs.append(pl.BlockSpec((seq, dk), lambda b, hh, rt, kcol=kcol: (b, kcol * h + hh)))
        in_specs.append(pl.BlockSpec((ctx_len, dk), lambda b, hh, rt, kcol=kcol: (ctx_blk0 + b, kcol * h + hh)))
    vec = pl.BlockSpec((1, dk), lambda b, hh, rt: (0, hh))
    in_specs += [vec, vec, pl.BlockSpec((1, dv), lambda b, hh, rt: (0, 0))]
    return pl.pallas_call(
        functools.partial(_hgrn_kernel, ctx_len=ctx_len, seq=seq),
        grid=(batch, h, nct + nlt),
        in_specs=in_specs,
        out_specs=pl.BlockSpec((TQ, dv), lambda b, hh, rt: (row_block(b, rt), hh)),
        out_shape=jax.ShapeDtypeStruct((batch * (seq + ctx_len), h * dv), BF16),
        scratch_shapes=[pltpu.VMEM((seq + ctx_len, dv), F32), pltpu.VMEM((seq + ctx_len, dv), F32)],
        compiler_params=_cp(("parallel", "parallel", "arbitrary")),
        name="hgrn_scan",
    )(*([p] * 10), lb_f.reshape(1, h * dk), lb_b.reshape(1, h * dk), norm_g.reshape(1, dv))


def _gla_kernel(ql, qc, kl, kc, vl, vc, gl, al, ac, wa, ba, ng, o_ref, of_s, ob_s, *, ctx_len, seq):
    rt = pl.program_id(2)

    @pl.when(rt == 0)
    def _scan():
        refs = {0: (qc, kc, vc, ac), 1: (ql, kl, vl, al)}
        scale = D_DK ** -0.5
        r16 = D_GATE_RANK

        def prep(seg, r, nr, d):
            q = refs[seg][0][pl.ds(r, nr), :].astype(F32) * scale
            k = refs[seg][1][pl.ds(r, nr), :].astype(F32)
            v = refs[seg][2][pl.ds(r, nr), :]
            a = refs[seg][3][pl.ds(r, nr), :].astype(F32)[:, d * r16:(d + 1) * r16]
            z = jnp.dot(a, wa[d], preferred_element_type=F32, precision=HI) + ba[d]
            g = (jnp.minimum(z, 0.0) - jnp.log(1.0 + jnp.exp(-jnp.abs(z)))) * (1.0 / GLA_TAU)
            return q, k, v, g

        _scan_segments(
            [(ctx_len, 0, 0), (seq, ctx_len, 1)],
            lambda seg, r, nr: prep(seg, r, nr, 0),
            lambda seg, r, nr: prep(seg, r, nr, 1),
            of_s, ob_s, D_DK, D_DV)

    r0 = pl.multiple_of(rt * TQ, TQ)
    o = of_s[pl.ds(ctx_len + r0, TQ), :] + ob_s[pl.ds(ctx_len + r0, TQ), :]
    gate = gl[pl.ds(r0, TQ), :].astype(F32)
    o_ref[...] = (_rms(o, ng[...]) * gate * jax.nn.sigmoid(gate)).astype(o_ref.dtype)


def _gla(p, ga, w_a2, b_a, norm_g, batch, seq, ctx_len):
    h, dk, dv = D_HEADS, D_DK, D_DV
    nlt = seq // TQ
    ctx_blk0 = batch * seq // ctx_len
    q0 = (C_HEADS + 2 * C_KV_HEADS) * C_DH // dk
    k0 = q0 + h
    v0 = (k0 + h) * dk // dv
    g0 = v0 + h

    def pair(width, blk0):
        return [pl.BlockSpec((seq, width), lambda b, hh, rt: (b, blk0 + hh)),
                pl.BlockSpec((ctx_len, width), lambda b, hh, rt: (ctx_blk0 + b, blk0 + hh))]

    in_specs = pair(dk, q0) + pair(dk, k0) + pair(dv, v0)
    in_specs += [pl.BlockSpec((seq, dv), lambda b, hh, rt: (b, g0 + hh)),
                 pl.BlockSpec((seq, 2 * D_GATE_RANK), lambda b, hh, rt: (b, 0)),
                 pl.BlockSpec((ctx_len, 2 * D_GATE_RANK), lambda b, hh, rt: (ctx_blk0 + b, 0)),
                 pl.BlockSpec((2, D_GATE_RANK, dk), lambda b, hh, rt: (0, 0, hh)),
                 pl.BlockSpec((2, 1, dk), lambda b, hh, rt: (0, 0, hh)),
                 pl.BlockSpec((1, dv), lambda b, hh, rt: (0, 0))]
    return pl.pallas_call(
        functools.partial(_gla_kernel, ctx_len=ctx_len, seq=seq),
        grid=(batch, h, nlt),
        in_specs=in_specs,
        out_specs=pl.BlockSpec((TQ, dv), lambda b, hh, rt: (b * nlt + rt, hh)),
        out_shape=jax.ShapeDtypeStruct((batch * seq, h * dv), BF16),
        scratch_shapes=[pltpu.VMEM((seq + ctx_len, dv), F32), pltpu.VMEM((seq + ctx_len, dv), F32)],
        compiler_params=_cp(("parallel", "parallel", "arbitrary")),
        name="gla_scan",
    )(p, p, p, p, p, p, p, ga, ga, w_a2, b_a.reshape(2, 1, h * dk), norm_g.reshape(1, dv))


def _softmax_pv(s, v):
    m = jnp.max(s, axis=-1, keepdims=True)
    p = jnp.exp(s - m)
    l = jnp.sum(p, axis=-1, keepdims=True)
    return jnp.dot(p.astype(BF16), v, preferred_element_type=F32) / l


def _mla_kernel(ql_ref, kvl_ref, kvc_ref, krl_ref, krc_ref, wqn_ref, wqr_ref, wkv_ref, gq_ref, gkv_ref,
                cosq_ref, sinq_ref, cosk_ref, sink_ref, o_ref, kn_s, kr_s, v_s, *, ctx_len):
    qt = pl.program_id(1)
    n_ctx_tiles = ctx_len // TQ
    scale = (B_NOPE + B_ROPE) ** -0.5
    dkv = B_NOPE + B_DV

    @pl.when(qt == 0)
    def _prep():
        kvc = _rms(kvc_ref[...].astype(F32), gkv_ref[...]).astype(BF16)
        kvl = _rms(kvl_ref[...].astype(F32), gkv_ref[...]).astype(BF16)
        for h in range(B_HEADS):
            w = wkv_ref[:, h * dkv:(h + 1) * dkv].astype(BF16)
            up_c = jnp.dot(kvc, w, preferred_element_type=F32)
            kn_s[h, 0:ctx_len, :] = up_c[:, :B_NOPE].astype(BF16)
            v_s[h, 0:ctx_len, :] = up_c[:, B_NOPE:].astype(BF16)
            up_l = jnp.dot(kvl, w, preferred_element_type=F32)
            kn_s[h, ctx_len:, :] = up_l[:, :B_NOPE].astype(BF16)
            v_s[h, ctx_len:, :] = up_l[:, B_NOPE:].astype(BF16)
        kr_s[0:ctx_len, :] = krc_ref[...]
        kr_s[ctx_len:, :] = _rope(krl_ref[...].astype(F32), cosk_ref[...], sink_ref[...]).astype(BF16)

    xn = _rms(ql_ref[...].astype(F32), gq_ref[...]).astype(BF16)
    qn_all = jnp.dot(xn, wqn_ref[...].astype(BF16), preferred_element_type=F32) * scale
    qr_all = jnp.dot(xn, wqr_ref[...].astype(BF16), preferred_element_type=F32) * scale

    def heads(n_keys, rotate):
        outs = []
        for h in range(B_HEADS):
            qn = qn_all[:, h * B_NOPE:(h + 1) * B_NOPE].astype(BF16)
            qr = qr_all[:, h * B_ROPE:(h + 1) * B_ROPE]
            if rotate:
                qr = _rope(qr, cosq_ref[...], sinq_ref[...])
            s = _dot_nt(qn, kn_s[h, 0:n_keys, :]) + _dot_nt(qr.astype(BF16), kr_s[0:n_keys, :])
            outs.append(_softmax_pv(s, v_s[h, 0:n_keys, :]).astype(o_ref.dtype))
        o_ref[...] = jnp.concatenate(outs, axis=-1)

    @pl.when(qt < n_ctx_tiles)
    def _():
        heads(ctx_len, False)

    @pl.when(qt >= n_ctx_tiles)
    def _():
        heads(kr_s.shape[0], True)


def _mla(p, kr, w_uq, w_ukv, gq, gkv, cos, sin, batch, seq, ctx_len):
    h = B_HEADS
    nct, nlt = ctx_len // TQ, seq // TQ
    ctx_blk0 = batch * seq // ctx_len
    row_block = _out_row_block(batch, seq, ctx_len)
    ql_blk = 5 * A_HEADS * A_DK // B_Q_LORA
    kv_blk = (5 * A_HEADS * A_DK + B_Q_LORA) // B_KV_LORA
    dq = B_NOPE + B_ROPE
    s_all = seq + ctx_len
    w3 = w_uq.reshape(B_Q_LORA, h, dq)
    wq_n = w3[:, :, :B_NOPE].reshape(B_Q_LORA, h * B_NOPE)
    wq_r = w3[:, :, B_NOPE:].reshape(B_Q_LORA, h * B_ROPE)
    in_specs = [
        pl.BlockSpec((TQ, B_Q_LORA), lambda b, qt: (row_block(b, qt), ql_blk)),
        pl.BlockSpec((seq, B_KV_LORA), lambda b, qt: (b, kv_blk)),
        pl.BlockSpec((ctx_len, B_KV_LORA), lambda b, qt: (ctx_blk0 + b, kv_blk)),
        pl.BlockSpec((seq, B_ROPE), lambda b, qt: (b, 0)),
        pl.BlockSpec((ctx_len, B_ROPE), lambda b, qt: (ctx_blk0 + b, 0)),
        pl.BlockSpec((B_Q_LORA, h * B_NOPE), lambda b, qt: (0, 0)),
        pl.BlockSpec((B_Q_LORA, h * B_ROPE), lambda b, qt: (0, 0)),
        pl.BlockSpec((B_KV_LORA, h * (B_NOPE + B_DV)), lambda b, qt: (0, 0)),
        pl.BlockSpec((1, B_Q_LORA), lambda b, qt: (0, 0)),
        pl.BlockSpec((1, B_KV_LORA), lambda b, qt: (0, 0)),
        pl.BlockSpec((TQ, B_ROPE), lambda b, qt: (jnp.maximum(qt - nct, 0), 0)),
        pl.BlockSpec((TQ, B_ROPE), lambda b, qt: (jnp.maximum(qt - nct, 0), 0)),
        pl.BlockSpec((seq, B_ROPE), lambda b, qt: (0, 0)),
        pl.BlockSpec((seq, B_ROPE), lambda b, qt: (0, 0)),
    ]
    return pl.pallas_call(
        functools.partial(_mla_kernel, ctx_len=ctx_len),
        grid=(batch, nct + nlt),
        in_specs=in_specs,
        out_specs=pl.BlockSpec((TQ, h * B_DV), lambda b, qt: (row_block(b, qt), 0)),
        out_shape=jax.ShapeDtypeStruct((batch * s_all, h * B_DV), BF16),
        scratch_shapes=[pltpu.VMEM((h, s_all, B_NOPE), BF16), pltpu.VMEM((s_all, B_ROPE), BF16),
                        pltpu.VMEM((h, s_all, B_DV), BF16)],
        compiler_params=_cp(("parallel", "arbitrary")),
        name="mla_attn",
    )(p, p, p, kr, kr, wq_n, wq_r, w_ukv, gq.reshape(1, -1), gkv.reshape(1, -1), cos, sin, cos, sin)


def _gqa_kernel(q_ref, kl_ref, kc_ref, vl_ref, vc_ref, gq_ref, gk_ref, cosq_ref, sinq_ref, cosk_ref, sink_ref,
                o_ref, k_s, v_s, *, ctx_len):
    qt = pl.program_id(2)
    scale = C_DH ** -0.5
    dh = C_DH

    @pl.when(qt == 0)
    def _prep():
        k_s[0:ctx_len, :] = _rms(kc_ref[...].astype(F32), gk_ref[...]).astype(BF16)
        kl = _rms(kl_ref[...].astype(F32), gk_ref[...])
        k_s[ctx_len:, :] = _rope(kl, cosk_ref[...], sink_ref[...]).astype(BF16)
        v_s[0:ctx_len, :] = vc_ref[...]
        v_s[ctx_len:, :] = vl_ref[...]

    outs = []
    for g in range(C_HEADS // C_KV_HEADS):
        q = _rms(q_ref[:, g * dh:(g + 1) * dh].astype(F32), gq_ref[...])
        q = _rope(q, cosq_ref[...], sinq_ref[...]) * scale
        s = _dot_nt(q.astype(BF16), k_s[...])
        outs.append(_softmax_pv(s, v_s[...]).astype(o_ref.dtype))
    o_ref[...] = jnp.concatenate(outs, axis=-1)


def _gqa(p, gq, gk, cos, sin, batch, seq, ctx_len):
    kvh, grp, dh = C_KV_HEADS, C_HEADS // C_KV_HEADS, C_DH
    nlt = seq // TQ
    ctx_blk0 = batch * seq // ctx_len
    k0 = C_HEADS
    v0 = C_HEADS + C_KV_HEADS
    s_all = seq + ctx_len
    in_specs = [
        pl.BlockSpec((TQ, grp * dh), lambda b, kh, qt: (b * nlt + qt, kh)),
        pl.BlockSpec((seq, dh), lambda b, kh, qt: (b, k0 + kh)),
        pl.BlockSpec((ctx_len, dh), lambda b, kh, qt: (ctx_blk0 + b, k0 + kh)),
        pl.BlockSpec((seq, dh), lambda b, kh, qt: (b, v0 + kh)),
        pl.BlockSpec((ctx_len, dh), lambda b, kh, qt: (ctx_blk0 + b, v0 + kh)),
        pl.BlockSpec((1, dh), lambda b, kh, qt: (0, 0)),
        pl.BlockSpec((1, dh), lambda b, kh, qt: (0, 0)),
        pl.BlockSpec((TQ, dh), lambda b, kh, qt: (qt, 0)),
        pl.BlockSpec((TQ, dh), lambda b, kh, qt: (qt, 0)),
        pl.BlockSpec((seq, dh), lambda b, kh, qt: (0, 0)),
        pl.BlockSpec((seq, dh), lambda b, kh, qt: (0, 0)),
    ]
    return pl.pallas_call(
        functools.partial(_gqa_kernel, ctx_len=ctx_len),
        grid=(batch, kvh, nlt),
        in_specs=in_specs,
        out_specs=pl.BlockSpec((TQ, grp * dh), lambda b, kh, qt: (b * nlt + qt, kh)),
        out_shape=jax.ShapeDtypeStruct((batch * seq, C_HEADS * dh), BF16),
        scratch_shapes=[pltpu.VMEM((s_all, dh), BF16), pltpu.VMEM((s_all, dh), BF16)],
        compiler_params=_cp(("parallel", "parallel", "arbitrary")),
        name="gqa_attn",
    )(p, p, p, p, p, gq.reshape(1, dh), gk.reshape(1, dh), cos, sin, cos, sin)


def _router_kernel(x_ref, g_ref, sh_ref, sc_ref, rw_ref, rb_ref, h_ref, ri_ref, rf_ref, cnt_ref, base_s):
    i = pl.program_id(0)
    tm = x_ref.shape[0]
    ne = N_EXPERTS
    per = ne // N_GROUPS

    @pl.when(i == 0)
    def _():
        base_s[...] = jnp.zeros_like(base_s)

    h = _rms(x_ref[...], g_ref[...]) * (1.0 + sc_ref[...]) + sh_ref[...]
    h_ref[...] = h
    h_hi = h.astype(BF16)
    h_lo = (h - h_hi.astype(F32)).astype(BF16)
    rw = rw_ref[...]
    w_hi = rw.astype(BF16)
    w_lo = (rw - w_hi.astype(F32)).astype(BF16)
    hw = jnp.dot(h_hi, jnp.concatenate([w_hi, w_lo], axis=1), preferred_element_type=F32)
    logits = hw[:, :ne] + hw[:, ne:] + jnp.dot(h_lo, w_hi, preferred_element_type=F32)
    scores = jax.nn.sigmoid(logits)
    sel = scores + rb_ref[...]
    lane = lax.broadcasted_iota(jnp.int32, (tm, ne), 1).astype(F32)
    neg = -jnp.inf
    big = float(ne)

    def top2(vals):
        m1 = jnp.max(vals, axis=1, keepdims=True)
        i1 = jnp.min(jnp.where(vals == m1, lane, big), axis=1, keepdims=True)
        rest = jnp.where(lane == i1, neg, vals)
        m2 = jnp.max(rest, axis=1, keepdims=True)
        i2 = jnp.min(jnp.where(rest == m2, lane, big), axis=1, keepdims=True)
        return m1 + m2, i1, i2

    best, e1, e2 = None, None, None
    for grp in range(N_GROUPS):
        in_grp = jnp.logical_and(lane >= float(grp * per), lane < float((grp + 1) * per))
        gsum, i1, i2 = top2(jnp.where(in_grp, sel, neg))
        if grp == 0:
            best, e1, e2 = gsum, i1, i2
        else:
            better = gsum > best
            best = jnp.where(better, gsum, best)
            e1 = jnp.where(better, i1, e1)
            e2 = jnp.where(better, i2, e2)

    hot1 = lane == e1
    hot2 = lane == e2
    w1 = jnp.sum(jnp.where(hot1, scores, 0.0), axis=1, keepdims=True)
    w2 = jnp.sum(jnp.where(hot2, scores, 0.0), axis=1, keepdims=True)
    wsum = w1 + w2
    assign = jnp.logical_or(hot1, hot2)
    r = lax.broadcasted_iota(jnp.int32, (tm, tm), 0)
    c = lax.broadcasted_iota(jnp.int32, (tm, tm), 1)
    before = (c < r).astype(BF16)
    excl = jnp.dot(before, assign.astype(BF16), preferred_element_type=F32) + base_s[...]
    rank1 = jnp.sum(jnp.where(hot1, excl, 0.0), axis=1, keepdims=True)
    rank2 = jnp.sum(jnp.where(hot2, excl, 0.0), axis=1, keepdims=True)
    base_s[...] = base_s[...] + jnp.sum(assign.astype(F32), axis=0, keepdims=True)

    l128 = lax.broadcasted_iota(jnp.int32, (tm, 128), 1)
    ri = jnp.where(l128 == 0, e1, jnp.where(l128 == 1, e2, jnp.where(l128 == 2, rank1, jnp.where(l128 == 3, rank2, 0.0))))
    ri_ref[...] = ri.T[0:8, :].astype(jnp.int32)
    rf_ref[...] = jnp.where(l128 == 0, w1 / wsum, jnp.where(l128 == 1, w2 / wsum, 0.0))
    cnt_ref[...] = jnp.broadcast_to(base_s[...], cnt_ref.shape)


def _router(x_all, g, mods5, layer, router_w, router_b, rows, n_tiles):
    d = x_all.shape[-1]
    tm = rows.tm
    n = n_tiles * tm
    ne = N_EXPERTS
    return pl.pallas_call(
        _router_kernel,
        grid=(n_tiles,),
        in_specs=[pl.BlockSpec((tm, d), lambda i: (i, 0)),
                  pl.BlockSpec((1, d), lambda i: (0, 0)),
                  _mod_spec(rows, layer, 3, d),
                  _mod_spec(rows, layer, 4, d),
                  pl.BlockSpec((d, ne), lambda i: (0, 0)),
                  pl.BlockSpec((1, ne), lambda i: (0, 0))],
        out_specs=[pl.BlockSpec((tm, d), lambda i: (i, 0)),
                   pl.BlockSpec((8, tm), lambda i: (0, i)),
                   pl.BlockSpec((tm, 128), lambda i: (i, 0)),
                   pl.BlockSpec((8, ne), lambda i: (0, 0))],
        out_shape=[jax.ShapeDtypeStruct((n, d), F32),
                   jax.ShapeDtypeStruct((8, n), jnp.int32),
                   jax.ShapeDtypeStruct((n, 128), F32),
                   jax.ShapeDtypeStruct((8, ne), F32)],
        scratch_shapes=[pltpu.VMEM((1, ne), F32)],
        compiler_params=_cp(("arbitrary",)),
        name="moe_router",
    )(x_all, g.reshape(1, d), mods5, mods5, router_w, router_b.reshape(1, ne))


def _expert_kernel(te_ref, nx_ref, nu_ref, dest_ref, h_hbm, w1_hbm, w3_hbm, w2_hbm, y_ref, xbuf, sem,
                   w1_s, w3_s, w2_s, wf1, wf3, wf2, wsem, wslot_ref, src_ref, *, n_tok, layer):
    r = pl.program_id(0)
    n_used = nu_ref[0]
    active = r < n_used
    changed = jnp.logical_or(r == 0, te_ref[r] != te_ref[jnp.maximum(r - 1, 0)])
    slot = lax.rem(r, 2)

    def weight_copies(e, ws):
        return (pltpu.make_async_copy(w1_hbm.at[layer, e], wf1.at[ws], wsem.at[ws]),
                pltpu.make_async_copy(w3_hbm.at[layer, e], wf3.at[ws], wsem.at[ws]),
                pltpu.make_async_copy(w2_hbm.at[layer, e], wf2.at[ws], wsem.at[ws]))

    @pl.when(r == 0)
    def _():
        wslot_ref[0] = 0
        for cp in weight_copies(te_ref[0], 0):
            cp.start()

        def clear(i, carry):
            src_ref[i] = 0
            return carry

        lax.fori_loop(0, src_ref.shape[0], clear, 0, unroll=DMA_UNROLL)

        def invert(i, carry):
            src_ref[dest_ref[i]] = i
            src_ref[dest_ref[n_tok + i]] = i
            return carry

        lax.fori_loop(0, n_tok, invert, 0, unroll=DMA_UNROLL)

    def gather(tile, dst_slot):
        base = tile * TE
        for t in range(TE):
            pltpu.make_async_copy(h_hbm.at[pl.ds(src_ref[base + t], 1)], xbuf.at[dst_slot, pl.ds(t, 1)],
                                  sem.at[dst_slot]).start(priority=t % 2)

    @pl.when(jnp.logical_and(r == 0, active))
    def _():
        gather(0, 0)

    @pl.when(r + 1 < n_used)
    def _():
        gather(r + 1, 1 - slot)

    @pl.when(jnp.logical_and(active, changed))
    def _():
        ws = wslot_ref[0]
        for cp in weight_copies(te_ref[r], ws):
            cp.wait()

        @pl.when(nx_ref[r] != te_ref[r])
        def _():
            for cp in weight_copies(nx_ref[r], 1 - ws):
                cp.start()

        w1_s[...] = wf1[ws].astype(BF16)
        w3_s[...] = wf3[ws].astype(BF16)
        w2_s[...] = wf2[ws].astype(BF16)
        wslot_ref[0] = 1 - ws

    @pl.when(active)
    def _():
        pltpu.make_async_copy(h_hbm.at[pl.ds(0, TE)], xbuf.at[slot], sem.at[slot]).wait()
        x = xbuf[slot].astype(BF16)
        a = jnp.dot(x, w1_s[...], preferred_element_type=F32)
        b = jnp.dot(x, w3_s[...], preferred_element_type=F32)
        hid = (a * jax.nn.sigmoid(a) * b).astype(BF16)
        y_ref[...] = jnp.dot(hid, w2_s[...], preferred_element_type=F32)

    @pl.when(jnp.logical_not(active))
    def _():
        y_ref[...] = jnp.zeros_like(y_ref)


def _experts(tile_expert, next_expert, n_used, dest, hp, w1, w3, w2, layer, p_max):
    n_tok, d = hp.shape
    f = w1.shape[-1]
    any_spec = pl.BlockSpec(memory_space=pl.ANY)
    grid_spec = pltpu.PrefetchScalarGridSpec(
        num_scalar_prefetch=4,
        grid=(p_max // TE,),
        in_specs=[any_spec, any_spec, any_spec, any_spec],
        out_specs=pl.BlockSpec((TE, d), lambda r, te, nx, nu, sr: (r, 0)),
        scratch_shapes=[pltpu.VMEM((2, TE, d), F32), pltpu.SemaphoreType.DMA((2,)),
                        pltpu.VMEM((d, f), BF16), pltpu.VMEM((d, f), BF16), pltpu.VMEM((f, d), BF16),
                        pltpu.VMEM((2, d, f), F32), pltpu.VMEM((2, d, f), F32), pltpu.VMEM((2, f, d), F32),
                        pltpu.SemaphoreType.DMA((2,)), pltpu.SMEM((1,), jnp.int32),
                        pltpu.SMEM((p_max,), jnp.int32)],
    )
    return pl.pallas_call(
        functools.partial(_expert_kernel, n_tok=n_tok, layer=layer),
        grid_spec=grid_spec,
        out_shape=jax.ShapeDtypeStruct((p_max, d), F32),
        compiler_params=_cp(("arbitrary",)),
        name="moe_experts",
    )(tile_expert, next_expert, n_used, dest, hp, w1, w3, w2)


def _combine_kernel(dest_ref, x_ref, rf_ref, g_ref, fg_ref, ys_hbm, o_ref, buf, sem, *, tm, n_tok, n_tiles, final):
    i = pl.program_id(0)
    slot = lax.rem(i, 2)

    def gather(tile, dst_slot):
        base = tile * tm
        for t in range(tm):
            pltpu.make_async_copy(ys_hbm.at[pl.ds(dest_ref[base + t], 1)], buf.at[dst_slot, 0, pl.ds(t, 1)],
                                  sem.at[dst_slot]).start(priority=0)
            pltpu.make_async_copy(ys_hbm.at[pl.ds(dest_ref[n_tok + base + t], 1)], buf.at[dst_slot, 1, pl.ds(t, 1)],
                                  sem.at[dst_slot]).start(priority=1)

    @pl.when(i == 0)
    def _():
        gather(0, 0)

    @pl.when(i + 1 < n_tiles)
    def _():
        gather(i + 1, 1 - slot)

    pltpu.make_async_copy(ys_hbm.at[pl.ds(0, tm)], buf.at[slot, 0], sem.at[slot]).wait()
    pltpu.make_async_copy(ys_hbm.at[pl.ds(0, tm)], buf.at[slot, 1], sem.at[slot]).wait()
    w = rf_ref[...]
    y = w[:, 0:1] * buf[slot, 0] + w[:, 1:2] * buf[slot, 1]
    x2 = x_ref[...] + g_ref[...] * y
    if final:
        x2 = _rms(x2, fg_ref[...])
    o_ref[...] = x2


def _combine(dest, x_all, rf, mods5, layer, final_g, ys, rows, n_tiles, n_tok_total, final):
    d = x_all.shape[-1]
    tm = rows.tm
    grid_spec = pltpu.PrefetchScalarGridSpec(
        num_scalar_prefetch=1,
        grid=(n_tiles,),
        in_specs=[pl.BlockSpec((tm, d), lambda i, dr: (i, 0)),
                  pl.BlockSpec((tm, 128), lambda i, dr: (i, 0)),
                  _mod_spec(rows, layer, 5, d),
                  pl.BlockSpec((1, d), lambda i, dr: (0, 0)),
                  pl.BlockSpec(memory_space=pl.ANY)],
        out_specs=pl.BlockSpec((tm, d), lambda i, dr: (i, 0)),
        scratch_shapes=[pltpu.VMEM((2, 2, tm, d), F32), pltpu.SemaphoreType.DMA((2,))],
    )
    return pl.pallas_call(
        functools.partial(_combine_kernel, tm=tm, n_tok=n_tok_total, n_tiles=n_tiles, final=final),
        grid_spec=grid_spec,
        out_shape=jax.ShapeDtypeStruct((n_tiles * tm, d), F32),
        compiler_params=_cp(("arbitrary",)),
        name="moe_combine",
    )(dest, x_all, rf, mods5, final_g.reshape(1, d), ys)


def _moe(x_all, n_tok, norm_g, mods5, layer, router_w, router_b, w1, w3, w2, final_g, final, batch, seq, ctx_len):
    rows_r = _Rows(batch, seq, ctx_len, TROUTE)
    rows_c = _Rows(batch, seq, ctx_len, TCOMB)
    h, ri, rf, cnt = _router(x_all, norm_g, mods5, layer, router_w, router_b, rows_r, n_tok // TROUTE)
    counts = cnt[0].astype(jnp.int32)
    padded = ((counts + TE - 1) // TE) * TE
    ends = jnp.cumsum(padded)
    starts = ends - padded
    e1, e2, r1, r2 = ri[0], ri[1], ri[2], ri[3]
    dest = jnp.concatenate([starts[e1] + r1, starts[e2] + r2]).astype(jnp.int32)
    p_max = 2 * n_tok + N_EXPERTS * TE
    n_tiles = p_max // TE
    n_used = (ends[-1] // TE).astype(jnp.int32)
    tile_start = jnp.arange(n_tiles, dtype=jnp.int32) * TE
    tile_expert = jnp.sum((tile_start[:, None] >= ends[None, :]).astype(jnp.int32), axis=1)
    last_expert = jnp.sum((jnp.maximum(ends[-1] - 1, 0) >= ends).astype(jnp.int32))
    tile_expert = jnp.minimum(jnp.where(tile_start < ends[-1], tile_expert, last_expert), N_EXPERTS - 1).astype(jnp.int32)
    eid = jnp.arange(N_EXPERTS, dtype=jnp.int32)
    later_used = jnp.logical_and(eid[None, :] > eid[:, None], (padded > 0)[None, :])
    next_used = jnp.min(jnp.where(later_used, eid[None, :], N_EXPERTS), axis=1)
    next_used = jnp.where(next_used == N_EXPERTS, eid, next_used)
    next_expert = jnp.sum(jnp.where(tile_expert[:, None] == eid[None, :], next_used[None, :], 0), axis=1).astype(jnp.int32)
    ys = _experts(tile_expert, next_expert, n_used.reshape(1), dest, h, w1, w3, w2, layer, p_max)
    return _combine(dest, x_all, rf, mods5, layer, final_g, ys, rows_c, n_tok // TCOMB, n_tok, final)


def _rope_tables(t_len, d_rope):
    rows = t_len // GRID_W
    quarter = d_rope // 4
    freqs = ROPE_THETA ** (-jnp.arange(quarter, dtype=F32) / quarter)
    row = jnp.repeat(jnp.arange(rows, dtype=F32), GRID_W)
    col = jnp.tile(jnp.arange(GRID_W, dtype=F32), rows)
    ang = jnp.concatenate([row[:, None] * freqs, col[:, None] * freqs], axis=-1)
    cos, sin = jnp.cos(ang), jnp.sin(ang)
    return jnp.concatenate([cos, cos], axis=-1), jnp.concatenate([-sin, sin], axis=-1)


def kernel(x, c, ctx, c_ctx, mod_w, mod_b, norm_attn_g, norm_ffn_g, final_norm_g, ab_w_in, ab_w_out, hgrn_lb_logits, hgrn_norm_g, mla_q_norm_g, mla_w_uq, mla_kv_norm_g, mla_w_ukv, cd_w_in, cd_w_out, gqa_q_norm_g, gqa_k_norm_g, gla_w_a2, gla_b_a, gla_norm_g, router_w, router_b, moe_w1, moe_w3, moe_w2):
    batch, seq, d = x.shape
    ctx_len = ctx.shape[1]
    n_lat, n_ctx = batch * seq, batch * ctx_len
    assert ctx_len % TQ == 0 and seq % TQ == 0 and seq % ctx_len == 0 and batch < 8
    tm = min(1024, seq, n_ctx)
    rows = _Rows(batch, seq, ctx_len, tm)

    cvec = jnp.concatenate([c, c_ctx[None, :], jnp.zeros((8 - batch - 1, d), F32)], axis=0)
    mods = _modvec(cvec, mod_w, mod_b)
    mods5 = mods.reshape(mods.shape[0], 8, 6, 1, d)

    cos_b, sin_b = _rope_tables(seq, B_ROPE)
    cos_c, sin_c = _rope_tables(seq, C_DH)
    lb = jnp.cumsum(jax.nn.softmax(hgrn_lb_logits.astype(F32), axis=1), axis=1)

    x_lat = x.reshape(n_lat, d)
    x_ctx = ctx.reshape(n_ctx, d)

    h0 = _norm_mod(x_lat, x_ctx, 0, norm_attn_g[0], mods5, 0, rows)
    ab_main = 5 * A_HEADS * A_DK + B_Q_LORA + B_KV_LORA
    tm_mm = next(t for t in (2304, 2048, 1536, 1024, 512, 256) if (n_lat + n_ctx) % t == 0)
    p0 = _matmul(h0, ab_w_in, 0, ab_main, 256, tm_mm)
    kr0 = _matmul_tail(h0, ab_w_in, 0, ab_main, B_ROPE, tm_mm)
    mix_a = _hgrn(p0, lb[0, 0], lb[1, 0], hgrn_norm_g[0], batch, seq, ctx_len)
    mix_b = _mla(p0, kr0, mla_w_uq[0], mla_w_ukv[0], mla_q_norm_g[0], mla_kv_norm_g[0], cos_b, sin_b,
                 batch, seq, ctx_len)
    rows_o = _Rows(batch, seq, ctx_len, min(512, tm))
    x1 = _out_proj(mix_a, mix_b, ab_w_out, 0, x_lat, x_ctx, 0, mods5, 0, rows_o, rows_o.n_all)
    x2 = _moe(x1, n_lat + n_ctx, norm_ffn_g[0], mods5, 0, router_w, router_b, moe_w1, moe_w3, moe_w2,
              final_norm_g, False, batch, seq, ctx_len)

    h1 = _norm_mod(x2, x2, rows.n_lat, norm_attn_g[1], mods5, 1, rows)
    cd_main = (C_HEADS + 2 * C_KV_HEADS) * C_DH + 2 * D_HEADS * D_DK + 2 * D_HEADS * D_DV
    p1 = _matmul(h1, cd_w_in, 0, cd_main, 512, tm_mm)
    ga1 = _matmul_tail(h1, cd_w_in, 0, cd_main, 2 * D_GATE_RANK, tm_mm)
    mix_c = _gqa(p1, gqa_q_norm_g[0], gqa_k_norm_g[0], cos_c, sin_c, batch, seq, ctx_len)
    mix_d = _gla(p1, ga1, gla_w_a2[0], gla_b_a[0], gla_norm_g[0], batch, seq, ctx_len)
    x3 = _out_proj(mix_c, mix_d, cd_w_out, 0, x2, x2, rows_o.n_lat, mods5, 1, rows_o, rows_o.n_lat)
    out = _moe(x3, n_lat, norm_ffn_g[1], mods5, 1, router_w, router_b, moe_w1, moe_w3, moe_w2,
               final_norm_g, True, batch, seq, ctx_len)
    return out.reshape(batch, seq, d)
```

```python
import functools

import jax
import jax.numpy as jnp
from jax import lax
from jax.experimental import pallas as pl
from jax.experimental.pallas import tpu as pltpu

F32 = jnp.float32
BF16 = jnp.bfloat16
HI = lax.Precision.HIGHEST

GRID_W = 64
ROPE_THETA = 10000.0
NORM_EPS = 1e-6
A_HEADS, A_DK, A_DV = 8, 128, 128
B_HEADS, B_Q_LORA, B_KV_LORA, B_NOPE, B_ROPE, B_DV = 8, 512, 256, 128, 64, 128
C_HEADS, C_KV_HEADS, C_DH = 8, 2, 128
D_HEADS, D_DK, D_DV, D_GATE_RANK = 4, 128, 256, 16
GLA_TAU = 16.0
N_EXPERTS, N_GROUPS = 16, 4

TQ = 256
SCAN_C = 64
SCAN_BLOCK = 1024
TE = 256
TROUTE = 512
TCOMB = 256
DMA_UNROLL = 8
VMEM_MIB = 56


def _cp(sem):
    return pltpu.CompilerParams(dimension_semantics=sem, vmem_limit_bytes=VMEM_MIB * 1024 * 1024)


def _rms(x, g):
    return x * lax.rsqrt(jnp.mean(x * x, axis=-1, keepdims=True) + NORM_EPS) * g


def _rope(x, cos, sin):
    half = x.shape[-1] // 2
    swapped = jnp.concatenate([x[:, half:], x[:, :half]], axis=-1)
    return x * cos + swapped * sin


def _dot_nt(a, b):
    return lax.dot_general(a, b, (((1,), (1,)), ((), ())), preferred_element_type=F32)


def _dot_tn(a, b):
    return lax.dot_general(a, b, (((0,), (0,)), ((), ())), preferred_element_type=F32)


def _modvec_kernel(c_ref, w_ref, b_ref, o_ref):
    c = c_ref[...]
    a = c * jax.nn.sigmoid(c)
    a_hi = a.astype(BF16)
    a_lo = (a - a_hi.astype(F32)).astype(BF16)
    w = w_ref[...]
    w_hi = w.astype(BF16)
    w_lo = (w - w_hi.astype(F32)).astype(BF16)
    acc = jnp.dot(a_hi, w_hi, preferred_element_type=F32) + jnp.dot(a_lo, w_hi, preferred_element_type=F32)
    o_ref[...] = acc + jnp.dot(a_hi, w_lo, preferred_element_type=F32) + b_ref[...]


def _modvec(cvec, mod_w, mod_b):
    n_layers, d, n6 = mod_w.shape
    tn = min(1024, n6)
    return pl.pallas_call(
        _modvec_kernel,
        grid=(n_layers, n6 // tn),
        in_specs=[pl.BlockSpec((8, d), lambda l, j: (0, 0)),
                  pl.BlockSpec((None, d, tn), lambda l, j: (l, 0, j)),
                  pl.BlockSpec((None, 1, tn), lambda l, j: (l, 0, j))],
        out_specs=pl.BlockSpec((None, 8, tn), lambda l, j: (l, 0, j)),
        out_shape=jax.ShapeDtypeStruct((n_layers, 8, n6), F32),
        compiler_params=_cp(("parallel", "parallel")),
        name="modvec",
    )(cvec, mod_w, mod_b.reshape(n_layers, 1, n6))


class _Rows:
    def __init__(self, batch, seq, ctx_len, tm):
        assert seq % tm == 0 and (batch * ctx_len) % tm == 0
        self.tm = tm
        self.batch = batch
        self.per_batch = seq // tm
        self.n_lat = batch * seq // tm
        self.n_ctx = batch * ctx_len // tm
        self.n_all = self.n_lat + self.n_ctx

    def mod_row(self, i):
        return jnp.where(i < self.n_lat, i // self.per_batch, self.batch)


def _mod_spec(rows, layer, chunk, d):
    return pl.BlockSpec((None, None, None, 1, d), lambda i, *_: (layer, rows.mod_row(i), chunk, 0, 0))


def _norm_mod_kernel(xl_ref, xc_ref, g_ref, sh_ref, sc_ref, o_ref, *, n_lat):
    i = pl.program_id(0)

    def body(x_ref):
        y = _rms(x_ref[...], g_ref[...])
        o_ref[...] = (y * (1.0 + sc_ref[...]) + sh_ref[...]).astype(o_ref.dtype)

    @pl.when(i < n_lat)
    def _():
        body(xl_ref)

    @pl.when(i >= n_lat)
    def _():
        body(xc_ref)


def _norm_mod(x_lat, x_ctx, ctx_block0, g, mods5, layer, rows):
    d = x_lat.shape[-1]
    tm = rows.tm
    nl = rows.n_lat
    return pl.pallas_call(
        functools.partial(_norm_mod_kernel, n_lat=nl),
        grid=(rows.n_all,),
        in_specs=[pl.BlockSpec((tm, d), lambda i: (jnp.minimum(i, nl - 1), 0)),
                  pl.BlockSpec((tm, d), lambda i: (ctx_block0 + jnp.maximum(i - nl, 0), 0)),
                  pl.BlockSpec((1, d), lambda i: (0, 0)),
                  _mod_spec(rows, layer, 0, d),
                  _mod_spec(rows, layer, 1, d)],
        out_specs=pl.BlockSpec((tm, d), lambda i: (i, 0)),
        out_shape=jax.ShapeDtypeStruct((rows.n_all * tm, d), BF16),
        compiler_params=_cp(("parallel",)),
        name="norm_mod",
    )(x_lat, x_ctx, g.reshape(1, d), mods5, mods5)


def _mm_kernel(a_ref, wt_ref, o_ref):
    o_ref[...] = _dot_nt(a_ref[...], wt_ref[...].astype(BF16)).astype(o_ref.dtype)


def _matmul(a, wt3, layer, n_cols, tn, tm):
    m, k = a.shape
    return pl.pallas_call(
        _mm_kernel,
        grid=(m // tm, n_cols // tn),
        in_specs=[pl.BlockSpec((tm, k), lambda i, j: (i, 0)),
                  pl.BlockSpec((None, tn, k), lambda i, j: (layer, j, 0))],
        out_specs=pl.BlockSpec((tm, tn), lambda i, j: (i, j)),
        out_shape=jax.ShapeDtypeStruct((m, n_cols), BF16),
        compiler_params=_cp(("parallel", "arbitrary")),
        name="in_proj",
    )(a, wt3)


def _mm_tail_kernel(a_ref, wt_ref, o_ref):
    n = o_ref.shape[-1]
    acc = _dot_nt(a_ref[...], wt_ref[...].astype(BF16))
    o_ref[...] = acc[:, :n].astype(o_ref.dtype)


def _matmul_tail(a, wt3, layer, col0, n_cols, tm):
    m, k = a.shape
    lane = 128
    assert col0 % lane == 0 and n_cols <= lane and col0 + n_cols == wt3.shape[1]
    return pl.pallas_call(
        _mm_tail_kernel,
        grid=(m // tm,),
        in_specs=[pl.BlockSpec((tm, k), lambda i: (i, 0)),
                  pl.BlockSpec((None, lane, k), lambda i: (layer, col0 // lane, 0))],
        out_specs=pl.BlockSpec((tm, n_cols), lambda i: (i, 0)),
        out_shape=jax.ShapeDtypeStruct((m, n_cols), BF16),
        compiler_params=_cp(("parallel",)),
        name="in_proj_tail",
    )(a, wt3)


def _out_proj_kernel(ma_ref, mb_ref, w_hbm, xl_ref, xc_ref, g_ref, o_ref, w_s, stage, *, n_lat, widx):
    i = pl.program_id(0)
    ka = ma_ref.shape[-1]

    @pl.when(i == 0)
    def _():
        rows = stage.shape[0]
        for c in range(w_s.shape[0] // rows):
            pltpu.sync_copy(w_hbm.at[widx, pl.ds(c * rows, rows)], stage)
            w_s[c * rows:(c + 1) * rows, :] = stage[...].astype(BF16)

    acc = jnp.dot(ma_ref[...], w_s[:ka, :], preferred_element_type=F32)
    acc += jnp.dot(mb_ref[...], w_s[ka:, :], preferred_element_type=F32)
    upd = g_ref[...] * acc

    @pl.when(i < n_lat)
    def _():
        o_ref[...] = xl_ref[...] + upd

    @pl.when(i >= n_lat)
    def _():
        o_ref[...] = xc_ref[...] + upd


def _out_proj(mix_a, mix_b, w_out, widx, x_lat, x_ctx, ctx_block0, mods5, layer, rows, n_tiles):
    d = x_lat.shape[-1]
    ka, kb = mix_a.shape[-1], mix_b.shape[-1]
    tm = rows.tm
    nl = rows.n_lat
    stage_rows = min(512, ka + kb)
    return pl.pallas_call(
        functools.partial(_out_proj_kernel, n_lat=nl, widx=widx),
        grid=(n_tiles,),
        in_specs=[pl.BlockSpec((tm, ka), lambda i: (i, 0)),
                  pl.BlockSpec((tm, kb), lambda i: (i, 0)),
                  pl.BlockSpec(memory_space=pl.ANY),
                  pl.BlockSpec((tm, d), lambda i: (jnp.minimum(i, nl - 1), 0)),
                  pl.BlockSpec((tm, d), lambda i: (ctx_block0 + jnp.maximum(i - nl, 0), 0)),
                  _mod_spec(rows, layer, 2, d)],
        out_specs=pl.BlockSpec((tm, d), lambda i: (i, 0)),
        out_shape=jax.ShapeDtypeStruct((n_tiles * tm, d), F32),
        scratch_shapes=[pltpu.VMEM((ka + kb, d), BF16), pltpu.VMEM((stage_rows, d), F32)],
        compiler_params=_cp(("arbitrary",)),
        name="out_proj",
    )(mix_a, mix_b, w_out, x_lat, x_ctx, mods5)


def _tri(c, upper):
    r = lax.broadcasted_iota(jnp.int32, (c, c), 0)
    s = lax.broadcasted_iota(jnp.int32, (c, c), 1)
    return (s >= r) if upper else (r >= s)


def _scan_block(q, k, v, g, st, mask, forward):
    c = SCAN_C
    dk, dv = q.shape[-1], v.shape[-1]
    n = q.shape[0] // c
    mid, last = (c // 2 - 1, c - 1) if forward else (c // 2, 0)
    tri = jnp.broadcast_to(mask.astype(BF16)[None], (n, c, c))
    g3 = g.reshape(n, c, dk)
    g_hi = g3.astype(BF16)
    g_lo = (g3 - g_hi.astype(F32)).astype(BF16)
    cum = (jnp.einsum('cts,csd->ctd', tri, g_hi, preferred_element_type=F32)
           + jnp.einsum('cts,csd->ctd', tri, g_lo, preferred_element_type=F32))
    m = cum[:, mid:mid + 1, :]
    tot = cum[:, last:last + 1, :]
    qe = (q.reshape(n, c, dk) * jnp.exp(cum - m)).astype(BF16)
    ke = (k.reshape(n, c, dk) * jnp.exp(m - cum)).astype(BF16)
    a = jnp.einsum('ctd,csd->cts', qe, ke, preferred_element_type=F32)
    a = jnp.where(mask[None], a, 0.0).astype(BF16)
    v3 = v.reshape(n, c, dv)
    o = jnp.einsum('cts,csv->ctv', a, v3, preferred_element_type=F32)
    u = jnp.einsum('csv,csd->cvd', v3, ke, preferred_element_type=F32)
    em = jnp.exp(m)
    et = jnp.exp(tot - m)
    states = [None] * n
    for ci in (range(n) if forward else reversed(range(n))):
        stp = st * em[ci]
        states[ci] = stp.astype(BF16)
        st = (stp + u[ci]) * et[ci]
    o = o + jnp.einsum('ctd,cvd->ctv', qe, jnp.stack(states), preferred_element_type=F32)
    return o.reshape(n * c, dv), st


def _scan_segments(segments, prep_f, prep_b, of_ref, ob_ref, dk, dv):
    low, up = _tri(SCAN_C, False), _tri(SCAN_C, True)
    carry = (jnp.zeros((dv, dk), F32), jnp.zeros((dv, dk), F32))
    for rows, off, seg in segments:
        rb_ = min(SCAN_BLOCK, rows)
        n = rows // rb_

        def body(i, carry, n=n, off=off, seg=seg, rb_=rb_):
            sf, sb = carry
            rf = pl.multiple_of(i * rb_, rb_)
            rb = pl.multiple_of((n - 1 - i) * rb_, rb_)
            q, k, v, g = prep_f(seg, rf, rb_)
            o, sf = _scan_block(q, k, v, g, sf, low, True)
            of_ref[pl.ds(off + rf, rb_), :] = o
            q, k, v, g = prep_b(seg, rb, rb_)
            o, sb = _scan_block(q, k, v, g, sb, up, False)
            ob_ref[pl.ds(off + rb, rb_), :] = o
            return sf, sb

        carry = lax.fori_loop(0, n, body, carry)


def _hgrn_kernel(ql, qc, f1l, f1c, f2l, f2c, vl, vc, gl, gc, lbf, lbb, ng, o_ref, of_s, ob_s, *, ctx_len, seq):
    rt = pl.program_id(2)
    n_ctx_tiles = ctx_len // TQ

    @pl.when(rt == 0)
    def _scan():
        refs = {0: (qc, f1c, f2c, vc), 1: (ql, f1l, f2l, vl)}
        scale = A_DK ** -0.5

        def prep(seg, r, nr, fi, lb_ref):
            x = refs[seg][0][pl.ds(r, nr), :].astype(F32)
            q = x * jax.nn.sigmoid(x) * scale
            v = refs[seg][3][pl.ds(r, nr), :]
            lb = lb_ref[...]
            f = lb + (1.0 - lb) * jax.nn.sigmoid(refs[seg][fi][pl.ds(r, nr), :].astype(F32))
            return q, 1.0 - f, v, jnp.log(f)

        _scan_segments(
            [(ctx_len, 0, 0), (seq, ctx_len, 1)],
            lambda seg, r, nr: prep(seg, r, nr, 1, lbf),
            lambda seg, r, nr: prep(seg, r, nr, 2, lbb),
            of_s, ob_s, A_DK, A_DV)

    r0 = pl.multiple_of(rt * TQ, TQ)
    o = of_s[pl.ds(r0, TQ), :] + ob_s[pl.ds(r0, TQ), :]
    y = _rms(o, ng[...])

    @pl.when(rt < n_ctx_tiles)
    def _():
        gate = gc[pl.ds(r0, TQ), :].astype(F32)
        o_ref[...] = (y * jax.nn.sigmoid(gate)).astype(o_ref.dtype)

    @pl.when(rt >= n_ctx_tiles)
    def _():
        gate = gl[pl.ds(pl.multiple_of(r0 - ctx_len, TQ), TQ), :].astype(F32)
        o_ref[...] = (y * jax.nn.sigmoid(gate)).astype(o_ref.dtype)


def _out_row_block(batch, seq, ctx_len):
    nct = ctx_len // TQ
    nlt = seq // TQ

    def f(b, rt):
        return jnp.where(rt < nct, batch * nlt + b * nct + rt, b * nlt + rt - nct)

    return f


def _hgrn(p, lb_f, lb_b, norm_g, batch, seq, ctx_len):
    h, dk, dv = A_HEADS, A_DK, A_DV
    nct, nlt = ctx_len // TQ, seq // TQ
    ctx_blk0 = batch * seq // ctx_len
    row_block = _out_row_block(batch, seq, ctx_len)
    in_specs = []
    for kcol in range(5):
        in_specs.append(pl.BlockSpec((seq, dk), lambda b, hh, rt, kcol=kcol: (b, kcol * h + hh)))
        in_specs.append(pl.BlockSpec((ctx_len, dk), lambda b, hh, rt, kcol=kcol: (ctx_blk0 + b, kcol * h + hh)))
    vec = pl.BlockSpec((1, dk), lambda b, hh, rt: (0, hh))
    in_specs += [vec, vec, pl.BlockSpec((1, dv), lambda b, hh, rt: (0, 0))]
    return pl.pallas_call(
        functools.partial(_hgrn_kernel, ctx_len=ctx_len, seq=seq),
        grid=(batch, h, nct + nlt),
        in_specs=in_specs,
        out_specs=pl.BlockSpec((TQ, dv), lambda b, hh, rt: (row_block(b, rt), hh)),
        out_shape=jax.ShapeDtypeStruct((batch * (seq + ctx_len), h * dv), BF16),
        scratch_shapes=[pltpu.VMEM((seq + ctx_len, dv), F32), pltpu.VMEM((seq + ctx_len, dv), F32)],
        compiler_params=_cp(("parallel", "parallel", "arbitrary")),
        name="hgrn_scan",
    )(*([p] * 10), lb_f.reshape(1, h * dk), lb_b.reshape(1, h * dk), norm_g.reshape(1, dv))


def _gla_kernel(ql, qc, kl, kc, vl, vc, gl, al, ac, wa, ba, ng, o_ref, of_s, ob_s, *, ctx_len, seq):
    rt = pl.program_id(2)

    @pl.when(rt == 0)
    def _scan():
        refs = {0: (qc, kc, vc, ac), 1: (ql, kl, vl, al)}
        scale = D_DK ** -0.5
        r16 = D_GATE_RANK

        def prep(seg, r, nr, d):
            q = refs[seg][0][pl.ds(r, nr), :].astype(F32) * scale
            k = refs[seg][1][pl.ds(r, nr), :].astype(F32)
            v = refs[seg][2][pl.ds(r, nr), :]
            a = refs[seg][3][pl.ds(r, nr), :].astype(F32)[:, d * r16:(d + 1) * r16]
            z = jnp.dot(a, wa[d], preferred_element_type=F32, precision=HI) + ba[d]
            g = (jnp.minimum(z, 0.0) - jnp.log(1.0 + jnp.exp(-jnp.abs(z)))) * (1.0 / GLA_TAU)
            return q, k, v, g

        _scan_segments(
            [(ctx_len, 0, 0), (seq, ctx_len, 1)],
            lambda seg, r, nr: prep(seg, r, nr, 0),
            lambda seg, r, nr: prep(seg, r, nr, 1),
            of_s, ob_s, D_DK, D_DV)

    r0 = pl.multiple_of(rt * TQ, TQ)
    o = of_s[pl.ds(ctx_len + r0, TQ), :] + ob_s[pl.ds(ctx_len + r0, TQ), :]
    gate = gl[pl.ds(r0, TQ), :].astype(F32)
    o_ref[...] = (_rms(o, ng[...]) * gate * jax.nn.sigmoid(gate)).astype(o_ref.dtype)


def _gla(p, ga, w_a2, b_a, norm_g, batch, seq, ctx_len):
    h, dk, dv = D_HEADS, D_DK, D_DV
    nlt = seq // TQ
    ctx_blk0 = batch * seq // ctx_len
    q0 = (C_HEADS + 2 * C_KV_HEADS) * C_DH // dk
    k0 = q0 + h
    v0 = (k0 + h) * dk // dv
    g0 = v0 + h

    def pair(width, blk0):
        return [pl.BlockSpec((seq, width), lambda b, hh, rt: (b, blk0 + hh)),
                pl.BlockSpec((ctx_len, width), lambda b, hh, rt: (ctx_blk0 + b, blk0 + hh))]

    in_specs = pair(dk, q0) + pair(dk, k0) + pair(dv, v0)
    in_specs += [pl.BlockSpec((seq, dv), lambda b, hh, rt: (b, g0 + hh)),
                 pl.BlockSpec((seq, 2 * D_GATE_RANK), lambda b, hh, rt: (b, 0)),
                 pl.BlockSpec((ctx_len, 2 * D_GATE_RANK), lambda b, hh, rt: (ctx_blk0 + b, 0)),
                 pl.BlockSpec((2, D_GATE_RANK, dk), lambda b, hh, rt: (0, 0, hh)),
                 pl.BlockSpec((2, 1, dk), lambda b, hh, rt: (0, 0, hh)),
                 pl.BlockSpec((1, dv), lambda b, hh, rt: (0, 0))]
    return pl.pallas_call(
        functools.partial(_gla_kernel, ctx_len=ctx_len, seq=seq),
        grid=(batch, h, nlt),
        in_specs=in_specs,
        out_specs=pl.BlockSpec((TQ, dv), lambda b, hh, rt: (b * nlt + rt, hh)),
        out_shape=jax.ShapeDtypeStruct((batch * seq, h * dv), BF16),
        scratch_shapes=[pltpu.VMEM((seq + ctx_len, dv), F32), pltpu.VMEM((seq + ctx_len, dv), F32)],
        compiler_params=_cp(("parallel", "parallel", "arbitrary")),
        name="gla_scan",
    )(p, p, p, p, p, p, p, ga, ga, w_a2, b_a.reshape(2, 1, h * dk), norm_g.reshape(1, dv))


def _softmax_pv(s, v):
    m = jnp.max(s, axis=-1, keepdims=True)
    p = jnp.exp(s - m)
    l = jnp.sum(p, axis=-1, keepdims=True)
    return jnp.dot(p.astype(BF16), v, preferred_element_type=F32) / l


def _mla_kernel(ql_ref, kvl_ref, kvc_ref, krl_ref, krc_ref, wqn_ref, wqr_ref, wkv_ref, gq_ref, gkv_ref,
                cosq_ref, sinq_ref, cosk_ref, sink_ref, o_ref, kn_s, kr_s, v_s, *, ctx_len):
    qt = pl.program_id(1)
    n_ctx_tiles = ctx_len // TQ
    scale = (B_NOPE + B_ROPE) ** -0.5
    dkv = B_NOPE + B_DV

    @pl.when(qt == 0)
    def _prep():
        kvc = _rms(kvc_ref[...].astype(F32), gkv_ref[...]).astype(BF16)
        kvl = _rms(kvl_ref[...].astype(F32), gkv_ref[...]).astype(BF16)
        for h in range(B_HEADS):
            w = wkv_ref[:, h * dkv:(h + 1) * dkv].astype(BF16)
            up_c = jnp.dot(kvc, w, preferred_element_type=F32)
            kn_s[h, 0:ctx_len, :] = up_c[:, :B_NOPE].astype(BF16)
            v_s[h, 0:ctx_len, :] = up_c[:, B_NOPE:].astype(BF16)
            up_l = jnp.dot(kvl, w, preferred_element_type=F32)
            kn_s[h, ctx_len:, :] = up_l[:, :B_NOPE].astype(BF16)
            v_s[h, ctx_len:, :] = up_l[:, B_NOPE:].astype(BF16)
        kr_s[0:ctx_len, :] = krc_ref[...]
        kr_s[ctx_len:, :] = _rope(krl_ref[...].astype(F32), cosk_ref[...], sink_ref[...]).astype(BF16)

    xn = _rms(ql_ref[...].astype(F32), gq_ref[...]).astype(BF16)
    qn_all = jnp.dot(xn, wqn_ref[...].astype(BF16), preferred_element_type=F32) * scale
    qr_all = jnp.dot(xn, wqr_ref[...].astype(BF16), preferred_element_type=F32) * scale

    def heads(n_keys, rotate):
        outs = []
        for h in range(B_HEADS):
            qn = qn_all[:, h * B_NOPE:(h + 1) * B_NOPE].astype(BF16)
            qr = qr_all[:, h * B_ROPE:(h + 1) * B_ROPE]
            if rotate:
                qr = _rope(qr, cosq_ref[...], sinq_ref[...])
            s = _dot_nt(qn, kn_s[h, 0:n_keys, :]) + _dot_nt(qr.astype(BF16), kr_s[0:n_keys, :])
            outs.append(_softmax_pv(s, v_s[h, 0:n_keys, :]).astype(o_ref.dtype))
        o_ref[...] = jnp.concatenate(outs, axis=-1)

    @pl.when(qt < n_ctx_tiles)
    def _():
        heads(ctx_len, False)

    @pl.when(qt >= n_ctx_tiles)
    def _():
        heads(kr_s.shape[0], True)


def _mla(p, kr, w_uq, w_ukv, gq, gkv, cos, sin, batch, seq, ctx_len):
    h = B_HEADS
    nct, nlt = ctx_len // TQ, seq // TQ
    ctx_blk0 = batch * seq // ctx_len
    row_block = _out_row_block(batch, seq, ctx_len)
    ql_blk = 5 * A_HEADS * A_DK // B_Q_LORA
    kv_blk = (5 * A_HEADS * A_DK + B_Q_LORA) // B_KV_LORA
    dq = B_NOPE + B_ROPE
    s_all = seq + ctx_len
    w3 = w_uq.reshape(B_Q_LORA, h, dq)
    wq_n = w3[:, :, :B_NOPE].reshape(B_Q_LORA, h * B_NOPE)
    wq_r = w3[:, :, B_NOPE:].reshape(B_Q_LORA, h * B_ROPE)
    in_specs = [
        pl.BlockSpec((TQ, B_Q_LORA), lambda b, qt: (row_block(b, qt), ql_blk)),
        pl.BlockSpec((seq, B_KV_LORA), lambda b, qt: (b, kv_blk)),
        pl.BlockSpec((ctx_len, B_KV_LORA), lambda b, qt: (ctx_blk0 + b, kv_blk)),
        pl.BlockSpec((seq, B_ROPE), lambda b, qt: (b, 0)),
        pl.BlockSpec((ctx_len, B_ROPE), lambda b, qt: (ctx_blk0 + b, 0)),
        pl.BlockSpec((B_Q_LORA, h * B_NOPE), lambda b, qt: (0, 0)),
        pl.BlockSpec((B_Q_LORA, h * B_ROPE), lambda b, qt: (0, 0)),
        pl.BlockSpec((B_KV_LORA, h * (B_NOPE + B_DV)), lambda b, qt: (0, 0)),
        pl.BlockSpec((1, B_Q_LORA), lambda b, qt: (0, 0)),
        pl.BlockSpec((1, B_KV_LORA), lambda b, qt: (0, 0)),
        pl.BlockSpec((TQ, B_ROPE), lambda b, qt: (jnp.maximum(qt - nct, 0), 0)),
        pl.BlockSpec((TQ, B_ROPE), lambda b, qt: (jnp.maximum(qt - nct, 0), 0)),
        pl.BlockSpec((seq, B_ROPE), lambda b, qt: (0, 0)),
        pl.BlockSpec((seq, B_ROPE), lambda b, qt: (0, 0)),
    ]
    return pl.pallas_call(
        functools.partial(_mla_kernel, ctx_len=ctx_len),
        grid=(batch, nct + nlt),
        in_specs=in_specs,
        out_specs=pl.BlockSpec((TQ, h * B_DV), lambda b, qt: (row_block(b, qt), 0)),
        out_shape=jax.ShapeDtypeStruct((batch * s_all, h * B_DV), BF16),
        scratch_shapes=[pltpu.VMEM((h, s_all, B_NOPE), BF16), pltpu.VMEM((s_all, B_ROPE), BF16),
                        pltpu.VMEM((h, s_all, B_DV), BF16)],
        compiler_params=_cp(("parallel", "arbitrary")),
        name="mla_attn",
    )(p, p, p, kr, kr, wq_n, wq_r, w_ukv, gq.reshape(1, -1), gkv.reshape(1, -1), cos, sin, cos, sin)


def _gqa_kernel(q_ref, kl_ref, kc_ref, vl_ref, vc_ref, gq_ref, gk_ref, cosq_ref, sinq_ref, cosk_ref, sink_ref,
                o_ref, k_s, v_s, *, ctx_len):
    qt = pl.program_id(2)
    scale = C_DH ** -0.5
    dh = C_DH

    @pl.when(qt == 0)
    def _prep():
        k_s[0:ctx_len, :] = _rms(kc_ref[...].astype(F32), gk_ref[...]).astype(BF16)
        kl = _rms(kl_ref[...].astype(F32), gk_ref[...])
        k_s[ctx_len:, :] = _rope(kl, cosk_ref[...], sink_ref[...]).astype(BF16)
        v_s[0:ctx_len, :] = vc_ref[...]
        v_s[ctx_len:, :] = vl_ref[...]

    outs = []
    for g in range(C_HEADS // C_KV_HEADS):
        q = _rms(q_ref[:, g * dh:(g + 1) * dh].astype(F32), gq_ref[...])
        q = _rope(q, cosq_ref[...], sinq_ref[...]) * scale
        s = _dot_nt(q.astype(BF16), k_s[...])
        outs.append(_softmax_pv(s, v_s[...]).astype(o_ref.dtype))
    o_ref[...] = jnp.concatenate(outs, axis=-1)


def _gqa(p, gq, gk, cos, sin, batch, seq, ctx_len):
    kvh, grp, dh = C_KV_HEADS, C_HEADS // C_KV_HEADS, C_DH
    nlt = seq // TQ
    ctx_blk0 = batch * seq // ctx_len
    k0 = C_HEADS
    v0 = C_HEADS + C_KV_HEADS
    s_all = seq + ctx_len
    in_specs = [
        pl.BlockSpec((TQ, grp * dh), lambda b, kh, qt: (b * nlt + qt, kh)),
        pl.BlockSpec((seq, dh), lambda b, kh, qt: (b, k0 + kh)),
        pl.BlockSpec((ctx_len, dh), lambda b, kh, qt: (ctx_blk0 + b, k0 + kh)),
        pl.BlockSpec((seq, dh), lambda b, kh, qt: (b, v0 + kh)),
        pl.BlockSpec((ctx_len, dh), lambda b, kh, qt: (ctx_blk0 + b, v0 + kh)),
        pl.BlockSpec((1, dh), lambda b, kh, qt: (0, 0)),
        pl.BlockSpec((1, dh), lambda b, kh, qt: (0, 0)),
        pl.BlockSpec((TQ, dh), lambda b, kh, qt: (qt, 0)),
        pl.BlockSpec((TQ, dh), lambda b, kh, qt: (qt, 0)),
        pl.BlockSpec((seq, dh), lambda b, kh, qt: (0, 0)),
        pl.BlockSpec((seq, dh), lambda b, kh, qt: (0, 0)),
    ]
    return pl.pallas_call(
        functools.partial(_gqa_kernel, ctx_len=ctx_len),
        grid=(batch, kvh, nlt),
        in_specs=in_specs,
        out_specs=pl.BlockSpec((TQ, grp * dh), lambda b, kh, qt: (b * nlt + qt, kh)),
        out_shape=jax.ShapeDtypeStruct((batch * seq, C_HEADS * dh), BF16),
        scratch_shapes=[pltpu.VMEM((s_all, dh), BF16), pltpu.VMEM((s_all, dh), BF16)],
        compiler_params=_cp(("parallel", "parallel", "arbitrary")),
        name="gqa_attn",
    )(p, p, p, p, p, gq.reshape(1, dh), gk.reshape(1, dh), cos, sin, cos, sin)


def _router_kernel(x_ref, g_ref, sh_ref, sc_ref, rw_ref, rb_ref, h_ref, ri_ref, rf_ref, cnt_ref, base_s):
    i = pl.program_id(0)
    tm = x_ref.shape[0]
    ne = N_EXPERTS
    per = ne // N_GROUPS

    @pl.when(i == 0)
    def _():
        base_s[...] = jnp.zeros_like(base_s)

    h = _rms(x_ref[...], g_ref[...]) * (1.0 + sc_ref[...]) + sh_ref[...]
    h_ref[...] = h
    h_hi = h.astype(BF16)
    h_lo = (h - h_hi.astype(F32)).astype(BF16)
    rw = rw_ref[...]
    w_hi = rw.astype(BF16)
    w_lo = (rw - w_hi.astype(F32)).astype(BF16)
    hw = jnp.dot(h_hi, jnp.concatenate([w_hi, w_lo], axis=1), preferred_element_type=F32)
    logits = hw[:, :ne] + hw[:, ne:] + jnp.dot(h_lo, w_hi, preferred_element_type=F32)
    scores = jax.nn.sigmoid(logits)
    sel = scores + rb_ref[...]
    lane = lax.broadcasted_iota(jnp.int32, (tm, ne), 1).astype(F32)
    neg = -jnp.inf
    big = float(ne)

    def top2(vals):
        m1 = jnp.max(vals, axis=1, keepdims=True)
        i1 = jnp.min(jnp.where(vals == m1, lane, big), axis=1, keepdims=True)
        rest = jnp.where(lane == i1, neg, vals)
        m2 = jnp.max(rest, axis=1, keepdims=True)
        i2 = jnp.min(jnp.where(rest == m2, lane, big), axis=1, keepdims=True)
        return m1 + m2, i1, i2

    best, e1, e2 = None, None, None
    for grp in range(N_GROUPS):
        in_grp = jnp.logical_and(lane >= float(grp * per), lane < float((grp + 1) * per))
        gsum, i1, i2 = top2(jnp.where(in_grp, sel, neg))
        if grp == 0:
            best, e1, e2 = gsum, i1, i2
        else:
            better = gsum > best
            best = jnp.where(better, gsum, best)
            e1 = jnp.where(better, i1, e1)
            e2 = jnp.where(better, i2, e2)

    hot1 = lane == e1
    hot2 = lane == e2
    w1 = jnp.sum(jnp.where(hot1, scores, 0.0), axis=1, keepdims=True)
    w2 = jnp.sum(jnp.where(hot2, scores, 0.0), axis=1, keepdims=True)
    wsum = w1 + w2
    assign = jnp.logical_or(hot1, hot2)
    r = lax.broadcasted_iota(jnp.int32, (tm, tm), 0)
    c = lax.broadcasted_iota(jnp.int32, (tm, tm), 1)
    before = (c < r).astype(BF16)
    excl = jnp.dot(before, assign.astype(BF16), preferred_element_type=F32) + base_s[...]
    rank1 = jnp.sum(jnp.where(hot1, excl, 0.0), axis=1, keepdims=True)
    rank2 = jnp.sum(jnp.where(hot2, excl, 0.0), axis=1, keepdims=True)
    base_s[...] = base_s[...] + jnp.sum(assign.astype(F32), axis=0, keepdims=True)

    l128 = lax.broadcasted_iota(jnp.int32, (tm, 128), 1)
    ri = jnp.where(l128 == 0, e1, jnp.where(l128 == 1, e2, jnp.where(l128 == 2, rank1, jnp.where(l128 == 3, rank2, 0.0))))
    ri_ref[...] = ri.T[0:8, :].astype(jnp.int32)
    rf_ref[...] = jnp.where(l128 == 0, w1 / wsum, jnp.where(l128 == 1, w2 / wsum, 0.0))
    cnt_ref[...] = jnp.broadcast_to(base_s[...], cnt_ref.shape)


def _router(x_all, g, mods5, layer, router_w, router_b, rows, n_tiles):
    d = x_all.shape[-1]
    tm = rows.tm
    n = n_tiles * tm
    ne = N_EXPERTS
    return pl.pallas_call(
        _router_kernel,
        grid=(n_tiles,),
        in_specs=[pl.BlockSpec((tm, d), lambda i: (i, 0)),
                  pl.BlockSpec((1, d), lambda i: (0, 0)),
                  _mod_spec(rows, layer, 3, d),
                  _mod_spec(rows, layer, 4, d),
                  pl.BlockSpec((d, ne), lambda i: (0, 0)),
                  pl.BlockSpec((1, ne), lambda i: (0, 0))],
        out_specs=[pl.BlockSpec((tm, d), lambda i: (i, 0)),
                   pl.BlockSpec((8, tm), lambda i: (0, i)),
                   pl.BlockSpec((tm, 128), lambda i: (i, 0)),
                   pl.BlockSpec((8, ne), lambda i: (0, 0))],
        out_shape=[jax.ShapeDtypeStruct((n, d), F32),
                   jax.ShapeDtypeStruct((8, n), jnp.int32),
                   jax.ShapeDtypeStruct((n, 128), F32),
                   jax.ShapeDtypeStruct((8, ne), F32)],
        scratch_shapes=[pltpu.VMEM((1, ne), F32)],
        compiler_params=_cp(("arbitrary",)),
        name="moe_router",
    )(x_all, g.reshape(1, d), mods5, mods5, router_w, router_b.reshape(1, ne))


def _expert_kernel(te_ref, nx_ref, nu_ref, dest_ref, h_hbm, w1_hbm, w3_hbm, w2_hbm, y_ref, xbuf, sem,
                   w1_s, w3_s, w2_s, wf1, wf3, wf2, wsem, wslot_ref, src_ref, *, n_tok, layer):
    r = pl.program_id(0)
    n_used = nu_ref[0]
    active = r < n_used
    changed = jnp.logical_or(r == 0, te_ref[r] != te_ref[jnp.maximum(r - 1, 0)])
    slot = lax.rem(r, 2)

    def weight_copies(e, ws):
        return (pltpu.make_async_copy(w1_hbm.at[layer, e], wf1.at[ws], wsem.at[ws]),
                pltpu.make_async_copy(w3_hbm.at[layer, e], wf3.at[ws], wsem.at[ws]),
                pltpu.make_async_copy(w2_hbm.at[layer, e], wf2.at[ws], wsem.at[ws]))

    @pl.when(r == 0)
    def _():
        wslot_ref[0] = 0
        for cp in weight_copies(te_ref[0], 0):
            cp.start()

        def clear(i, carry):
            src_ref[i] = 0
            return carry

        lax.fori_loop(0, src_ref.shape[0], clear, 0, unroll=DMA_UNROLL)

        def invert(i, carry):
            src_ref[dest_ref[i]] = i
            src_ref[dest_ref[n_tok + i]] = i
            return carry

        lax.fori_loop(0, n_tok, invert, 0, unroll=DMA_UNROLL)

    def gather(tile, dst_slot):
        base = tile * TE
        for t in range(TE):
            pltpu.make_async_copy(h_hbm.at[pl.ds(src_ref[base + t], 1)], xbuf.at[dst_slot, pl.ds(t, 1)],
                                  sem.at[dst_slot]).start(priority=t % 2)

    @pl.when(jnp.logical_and(r == 0, active))
    def _():
        gather(0, 0)

    @pl.when(r + 1 < n_used)
    def _():
        gather(r + 1, 1 - slot)

    @pl.when(jnp.logical_and(active, changed))
    def _():
        ws = wslot_ref[0]
        for cp in weight_copies(te_ref[r], ws):
            cp.wait()

        @pl.when(nx_ref[r] != te_ref[r])
        def _():
            for cp in weight_copies(nx_ref[r], 1 - ws):
                cp.start()

        w1_s[...] = wf1[ws].astype(BF16)
        w3_s[...] = wf3[ws].astype(BF16)
        w2_s[...] = wf2[ws].astype(BF16)
        wslot_ref[0] = 1 - ws

    @pl.when(active)
    def _():
        pltpu.make_async_copy(h_hbm.at[pl.ds(0, TE)], xbuf.at[slot], sem.at[slot]).wait()
        x = xbuf[slot].astype(BF16)
        a = jnp.dot(x, w1_s[...], preferred_element_type=F32)
        b = jnp.dot(x, w3_s[...], preferred_element_type=F32)
        hid = (a * jax.nn.sigmoid(a) * b).astype(BF16)
        y_ref[...] = jnp.dot(hid, w2_s[...], preferred_element_type=F32)

    @pl.when(jnp.logical_not(active))
    def _():
        y_ref[...] = jnp.zeros_like(y_ref)


def _experts(tile_expert, next_expert, n_used, dest, hp, w1, w3, w2, layer, p_max):
    n_tok, d = hp.shape
    f = w1.shape[-1]
    any_spec = pl.BlockSpec(memory_space=pl.ANY)
    grid_spec = pltpu.PrefetchScalarGridSpec(
        num_scalar_prefetch=4,
        grid=(p_max // TE,),
        in_specs=[any_spec, any_spec, any_spec, any_spec],
        out_specs=pl.BlockSpec((TE, d), lambda r, te, nx, nu, sr: (r, 0)),
        scratch_shapes=[pltpu.VMEM((2, TE, d), F32), pltpu.SemaphoreType.DMA((2,)),
                        pltpu.VMEM((d, f), BF16), pltpu.VMEM((d, f), BF16), pltpu.VMEM((f, d), BF16),
                        pltpu.VMEM((2, d, f), F32), pltpu.VMEM((2, d, f), F32), pltpu.VMEM((2, f, d), F32),
                        pltpu.SemaphoreType.DMA((2,)), pltpu.SMEM((1,), jnp.int32),
                        pltpu.SMEM((p_max,), jnp.int32)],
    )
    return pl.pallas_call(
        functools.partial(_expert_kernel, n_tok=n_tok, layer=layer),
        grid_spec=grid_spec,
        out_shape=jax.ShapeDtypeStruct((p_max, d), F32),
        compiler_params=_cp(("arbitrary",)),
        name="moe_experts",
    )(tile_expert, next_expert, n_used, dest, hp, w1, w3, w2)


def _combine_kernel(dest_ref, x_ref, rf_ref, g_ref, fg_ref, ys_hbm, o_ref, buf, sem, *, tm, n_tok, n_tiles, final):
    i = pl.program_id(0)
    slot = lax.rem(i, 2)

    def gather(tile, dst_slot):
        base = tile * tm
        for t in range(tm):
            pltpu.make_async_copy(ys_hbm.at[pl.ds(dest_ref[base + t], 1)], buf.at[dst_slot, 0, pl.ds(t, 1)],
                                  sem.at[dst_slot]).start(priority=0)
            pltpu.make_async_copy(ys_hbm.at[pl.ds(dest_ref[n_tok + base + t], 1)], buf.at[dst_slot, 1, pl.ds(t, 1)],
                                  sem.at[dst_slot]).start(priority=1)

    @pl.when(i == 0)
    def _():
        gather(0, 0)

    @pl.when(i + 1 < n_tiles)
    def _():
        gather(i + 1, 1 - slot)

    pltpu.make_async_copy(ys_hbm.at[pl.ds(0, tm)], buf.at[slot, 0], sem.at[slot]).wait()
    pltpu.make_async_copy(ys_hbm.at[pl.ds(0, tm)], buf.at[slot, 1], sem.at[slot]).wait()
    w = rf_ref[...]
    y = w[:, 0:1] * buf[slot, 0] + w[:, 1:2] * buf[slot, 1]
    x2 = x_ref[...] + g_ref[...] * y
    if final:
        x2 = _rms(x2, fg_ref[...])
    o_ref[...] = x2


def _combine(dest, x_all, rf, mods5, layer, final_g, ys, rows, n_tiles, n_tok_total, final):
    d = x_all.shape[-1]
    tm = rows.tm
    grid_spec = pltpu.PrefetchScalarGridSpec(
        num_scalar_prefetch=1,
        grid=(n_tiles,),
        in_specs=[pl.BlockSpec((tm, d), lambda i, dr: (i, 0)),
                  pl.BlockSpec((tm, 128), lambda i, dr: (i, 0)),
                  _mod_spec(rows, layer, 5, d),
                  pl.BlockSpec((1, d), lambda i, dr: (0, 0)),
                  pl.BlockSpec(memory_space=pl.ANY)],
        out_specs=pl.BlockSpec((tm, d), lambda i, dr: (i, 0)),
        scratch_shapes=[pltpu.VMEM((2, 2, tm, d), F32), pltpu.SemaphoreType.DMA((2,))],
    )
    return pl.pallas_call(
        functools.partial(_combine_kernel, tm=tm, n_tok=n_tok_total, n_tiles=n_tiles, final=final),
        grid_spec=grid_spec,
        out_shape=jax.ShapeDtypeStruct((n_tiles * tm, d), F32),
        compiler_params=_cp(("arbitrary",)),
        name="moe_combine",
    )(dest, x_all, rf, mods5, final_g.reshape(1, d), ys)


def _moe(x_all, n_tok, norm_g, mods5, layer, router_w, router_b, w1, w3, w2, final_g, final, batch, seq, ctx_len):
    rows_r = _Rows(batch, seq, ctx_len, TROUTE)
    rows_c = _Rows(batch, seq, ctx_len, TCOMB)
    h, ri, rf, cnt = _router(x_all, norm_g, mods5, layer, router_w, router_b, rows_r, n_tok // TROUTE)
    counts = cnt[0].astype(jnp.int32)
    padded = ((counts + TE - 1) // TE) * TE
    ends = jnp.cumsum(padded)
    starts = ends - padded
    e1, e2, r1, r2 = ri[0], ri[1], ri[2], ri[3]
    dest = jnp.concatenate([starts[e1] + r1, starts[e2] + r2]).astype(jnp.int32)
    p_max = 2 * n_tok + N_EXPERTS * TE
    n_tiles = p_max // TE
    n_used = (ends[-1] // TE).astype(jnp.int32)
    tile_start = jnp.arange(n_tiles, dtype=jnp.int32) * TE
    tile_expert = jnp.sum((tile_start[:, None] >= ends[None, :]).astype(jnp.int32), axis=1)
    last_expert = jnp.sum((jnp.maximum(ends[-1] - 1, 0) >= ends).astype(jnp.int32))
    tile_expert = jnp.minimum(jnp.where(tile_start < ends[-1], tile_expert, last_expert), N_EXPERTS - 1).astype(jnp.int32)
    eid = jnp.arange(N_EXPERTS, dtype=jnp.int32)
    later_used = jnp.logical_and(eid[None, :] > eid[:, None], (padded > 0)[None, :])
    next_used = jnp.min(jnp.where(later_used, eid[None, :], N_EXPERTS), axis=1)
    next_used = jnp.where(next_used == N_EXPERTS, eid, next_used)
    next_expert = jnp.sum(jnp.where(tile_expert[:, None] == eid[None, :], next_used[None, :], 0), axis=1).astype(jnp.int32)
    ys = _experts(tile_expert, next_expert, n_used.reshape(1), dest, h, w1, w3, w2, layer, p_max)
    return _combine(dest, x_all, rf, mods5, layer, final_g, ys, rows_c, n_tok // TCOMB, n_tok, final)


def _rope_tables(t_len, d_rope):
    rows = t_len // GRID_W
    quarter = d_rope // 4
    freqs = ROPE_THETA ** (-jnp.arange(quarter, dtype=F32) / quarter)
    row = jnp.repeat(jnp.arange(rows, dtype=F32), GRID_W)
    col = jnp.tile(jnp.arange(GRID_W, dtype=F32), rows)
    ang = jnp.concatenate([row[:, None] * freqs, col[:, None] * freqs], axis=-1)
    cos, sin = jnp.cos(ang), jnp.sin(ang)
    return jnp.concatenate([cos, cos], axis=-1), jnp.concatenate([-sin, sin], axis=-1)


def kernel(x, c, ctx, c_ctx, mod_w, mod_b, norm_attn_g, norm_ffn_g, final_norm_g, ab_w_in, ab_w_out, hgrn_lb_logits, hgrn_norm_g, mla_q_norm_g, mla_w_uq, mla_kv_norm_g, mla_w_ukv, cd_w_in, cd_w_out, gqa_q_norm_g, gqa_k_norm_g, gla_w_a2, gla_b_a, gla_norm_g, router_w, router_b, moe_w1, moe_w3, moe_w2):
    batch, seq, d = x.shape
    ctx_len = ctx.shape[1]
    n_lat, n_ctx = batch * seq, batch * ctx_len
    assert ctx_len % TQ == 0 and seq % TQ == 0 and seq % ctx_len == 0 and batch < 8
    tm = min(1024, seq, n_ctx)
    rows = _Rows(batch, seq, ctx_len, tm)

    cvec = jnp.concatenate([c, c_ctx[None, :], jnp.zeros((8 - batch - 1, d), F32)], axis=0)
    mods = _modvec(cvec, mod_w, mod_b)
    mods5 = mods.reshape(mods.shape[0], 8, 6, 1, d)

    cos_b, sin_b = _rope_tables(seq, B_ROPE)
    cos_c, sin_c = _rope_tables(seq, C_DH)
    lb = jnp.cumsum(jax.nn.softmax(hgrn_lb_logits.astype(F32), axis=1), axis=1)

    x_lat = x.reshape(n_lat, d)
    x_ctx = ctx.reshape(n_ctx, d)

    h0 = _norm_mod(x_lat, x_ctx, 0, norm_attn_g[0], mods5, 0, rows)
    ab_main = 5 * A_HEADS * A_DK + B_Q_LORA + B_KV_LORA
    tm_mm = next(t for t in (2304, 2048, 1536, 1024, 512, 256) if (n_lat + n_ctx) % t == 0)
    ab_wt = jnp.swapaxes(ab_w_in, 1, 2)
    p0 = _matmul(h0, ab_wt, 0, ab_main, 256, tm_mm)
    kr0 = _matmul_tail(h0, ab_wt, 0, ab_main, B_ROPE, tm_mm)
    mix_a = _hgrn(p0, lb[0, 0], lb[1, 0], hgrn_norm_g[0], batch, seq, ctx_len)
    mix_b = _mla(p0, kr0, mla_w_uq[0], mla_w_ukv[0], mla_q_norm_g[0], mla_kv_norm_g[0], cos_b, sin_b,
                 batch, seq, ctx_len)
    rows_o = _Rows(batch, seq, ctx_len, min(512, tm))
    x1 = _out_proj(mix_a, mix_b, ab_w_out, 0, x_lat, x_ctx, 0, mods5, 0, rows_o, rows_o.n_all)
    x2 = _moe(x1, n_lat + n_ctx, norm_ffn_g[0], mods5, 0, router_w, router_b, moe_w1, moe_w3, moe_w2,
              final_norm_g, False, batch, seq, ctx_len)

    h1 = _norm_mod(x2, x2, rows.n_lat, norm_attn_g[1], mods5, 1, rows)
    cd_main = (C_HEADS + 2 * C_KV_HEADS) * C_DH + 2 * D_HEADS * D_DK + 2 * D_HEADS * D_DV
    cd_wt = jnp.swapaxes(cd_w_in, 1, 2)
    p1 = _matmul(h1, cd_wt, 0, cd_main, 512, tm_mm)
    ga1 = _matmul_tail(h1, cd_wt, 0, cd_main, 2 * D_GATE_RANK, tm_mm)
    mix_c = _gqa(p1, gqa_q_norm_g[0], gqa_k_norm_g[0], cos_c, sin_c, batch, seq, ctx_len)
    mix_d = _gla(p1, ga1, gla_w_a2[0], gla_b_a[0], gla_norm_g[0], batch, seq, ctx_len)
    x3 = _out_proj(mix_c, mix_d, cd_w_out, 0, x2, x2, rows_o.n_lat, mods5, 1, rows_o, rows_o.n_lat)
    out = _moe(x3, n_lat, norm_ffn_g[1], mods5, 1, router_w, router_b, moe_w1, moe_w3, moe_w2,
               final_norm_g, True, batch, seq, ctx_len)
    return out.reshape(batch, seq, d)
```

```python
import functools

import jax
import jax.numpy as jnp
from jax import lax
from jax.experimental import pallas as pl
from jax.experimental.pallas import tpu as pltpu

F32 = jnp.float32
BF16 = jnp.bfloat16
HI = lax.Precision.HIGHEST

GRID_W = 64
ROPE_THETA = 10000.0
NORM_EPS = 1e-6
A_HEADS, A_DK, A_DV = 8, 128, 128
B_HEADS, B_Q_LORA, B_KV_LORA, B_NOPE, B_ROPE, B_DV = 8, 512, 256, 128, 64, 128
C_HEADS, C_KV_HEADS, C_DH = 8, 2, 128
D_HEADS, D_DK, D_DV, D_GATE_RANK = 4, 128, 256, 16
GLA_TAU = 16.0
N_EXPERTS, N_GROUPS = 16, 4

TQ = 256
SCAN_C = 64
SCAN_BLOCK = 1024
TE = 256
TROUTE = 512
TCOMB = 256
DMA_UNROLL = 8
LANES = 128
VMEM_MIB = 56


def _cp(sem):
    return pltpu.CompilerParams(dimension_semantics=sem, vmem_limit_bytes=VMEM_MIB * 1024 * 1024)


def _rms(x, g):
    return x * lax.rsqrt(jnp.mean(x * x, axis=-1, keepdims=True) + NORM_EPS) * g


def _rope(x, cos, sin):
    half = x.shape[-1] // 2
    swapped = jnp.concatenate([x[:, half:], x[:, :half]], axis=-1)
    return x * cos + swapped * sin


def _dot_nt(a, b):
    return lax.dot_general(a, b, (((1,), (1,)), ((), ())), preferred_element_type=F32)


def _dot_tn(a, b):
    return lax.dot_general(a, b, (((0,), (0,)), ((), ())), preferred_element_type=F32)


def _modvec_kernel(c_ref, w_ref, b_ref, o_ref):
    c = c_ref[...]
    a = c * jax.nn.sigmoid(c)
    a_hi = a.astype(BF16)
    a_lo = (a - a_hi.astype(F32)).astype(BF16)
    w = w_ref[...]
    w_hi = w.astype(BF16)
    w_lo = (w - w_hi.astype(F32)).astype(BF16)
    acc = jnp.dot(a_hi, w_hi, preferred_element_type=F32) + jnp.dot(a_lo, w_hi, preferred_element_type=F32)
    o_ref[...] = acc + jnp.dot(a_hi, w_lo, preferred_element_type=F32) + b_ref[...]


def _modvec(cvec, mod_w, mod_b):
    n_layers, d, n6 = mod_w.shape
    tn = min(1024, n6)
    return pl.pallas_call(
        _modvec_kernel,
        grid=(n_layers, n6 // tn),
        in_specs=[pl.BlockSpec((8, d), lambda l, j: (0, 0)),
                  pl.BlockSpec((None, d, tn), lambda l, j: (l, 0, j)),
                  pl.BlockSpec((None, 1, tn), lambda l, j: (l, 0, j))],
        out_specs=pl.BlockSpec((None, 8, tn), lambda l, j: (l, 0, j)),
        out_shape=jax.ShapeDtypeStruct((n_layers, 8, n6), F32),
        compiler_params=_cp(("parallel", "parallel")),
        name="modvec",
    )(cvec, mod_w, mod_b.reshape(n_layers, 1, n6))


class _Rows:
    def __init__(self, batch, seq, ctx_len, tm):
        assert seq % tm == 0 and (batch * ctx_len) % tm == 0
        self.tm = tm
        self.batch = batch
        self.per_batch = seq // tm
        self.n_lat = batch * seq // tm
        self.n_ctx = batch * ctx_len // tm
        self.n_all = self.n_lat + self.n_ctx

    def mod_row(self, i):
        return jnp.where(i < self.n_lat, i // self.per_batch, self.batch)


def _mod_spec(rows, layer, chunk, d):
    return pl.BlockSpec((None, None, None, 1, d), lambda i, *_: (layer, rows.mod_row(i), chunk, 0, 0))


def _norm_mod_kernel(xl_ref, xc_ref, g_ref, sh_ref, sc_ref, o_ref, *, n_lat):
    i = pl.program_id(0)

    def body(x_ref):
        y = _rms(x_ref[...], g_ref[...])
        o_ref[...] = (y * (1.0 + sc_ref[...]) + sh_ref[...]).astype(o_ref.dtype)

    @pl.when(i < n_lat)
    def _():
        body(xl_ref)

    @pl.when(i >= n_lat)
    def _():
        body(xc_ref)


def _norm_mod(x_lat, x_ctx, ctx_block0, g, mods5, layer, rows):
    d = x_lat.shape[-1]
    tm = rows.tm
    nl = rows.n_lat
    return pl.pallas_call(
        functools.partial(_norm_mod_kernel, n_lat=nl),
        grid=(rows.n_all,),
        in_specs=[pl.BlockSpec((tm, d), lambda i: (jnp.minimum(i, nl - 1), 0)),
                  pl.BlockSpec((tm, d), lambda i: (ctx_block0 + jnp.maximum(i - nl, 0), 0)),
                  pl.BlockSpec((1, d), lambda i: (0, 0)),
                  _mod_spec(rows, layer, 0, d),
                  _mod_spec(rows, layer, 1, d)],
        out_specs=pl.BlockSpec((tm, d), lambda i: (i, 0)),
        out_shape=jax.ShapeDtypeStruct((rows.n_all * tm, d), BF16),
        compiler_params=_cp(("parallel",)),
        name="norm_mod",
    )(x_lat, x_ctx, g.reshape(1, d), mods5, mods5)


def _mm_kernel(a_ref, wt_ref, o_ref):
    o_ref[...] = _dot_nt(a_ref[...], wt_ref[...].astype(BF16)).astype(o_ref.dtype)


def _matmul(a, wt3, layer, n_cols, tn, tm):
    m, k = a.shape
    return pl.pallas_call(
        _mm_kernel,
        grid=(m // tm, n_cols // tn),
        in_specs=[pl.BlockSpec((tm, k), lambda i, j: (i, 0)),
                  pl.BlockSpec((None, tn, k), lambda i, j: (layer, j, 0))],
        out_specs=pl.BlockSpec((tm, tn), lambda i, j: (i, j)),
        out_shape=jax.ShapeDtypeStruct((m, n_cols), BF16),
        compiler_params=_cp(("parallel", "arbitrary")),
        name="in_proj",
    )(a, wt3)


def _mm_tail_kernel(a_ref, wt_ref, o_ref):
    n = o_ref.shape[-1]
    acc = _dot_nt(a_ref[...], wt_ref[...].astype(BF16))
    o_ref[...] = acc[:, :n].astype(o_ref.dtype)


def _matmul_tail(a, wt3, layer, col0, n_cols, tm):
    m, k = a.shape
    lane = 128
    assert col0 % lane == 0 and n_cols <= lane and col0 + n_cols == wt3.shape[1]
    return pl.pallas_call(
        _mm_tail_kernel,
        grid=(m // tm,),
        in_specs=[pl.BlockSpec((tm, k), lambda i: (i, 0)),
                  pl.BlockSpec((None, lane, k), lambda i: (layer, col0 // lane, 0))],
        out_specs=pl.BlockSpec((tm, n_cols), lambda i: (i, 0)),
        out_shape=jax.ShapeDtypeStruct((m, n_cols), BF16),
        compiler_params=_cp(("parallel",)),
        name="in_proj_tail",
    )(a, wt3)


def _out_proj_kernel(ma_ref, mb_ref, w_hbm, xl_ref, xc_ref, g_ref, o_ref, w_s, stage, *, n_lat, widx):
    i = pl.program_id(0)
    ka = ma_ref.shape[-1]

    @pl.when(i == 0)
    def _():
        rows = stage.shape[0]
        for c in range(w_s.shape[0] // rows):
            pltpu.sync_copy(w_hbm.at[widx, pl.ds(c * rows, rows)], stage)
            w_s[c * rows:(c + 1) * rows, :] = stage[...].astype(BF16)

    acc = jnp.dot(ma_ref[...], w_s[:ka, :], preferred_element_type=F32)
    acc += jnp.dot(mb_ref[...], w_s[ka:, :], preferred_element_type=F32)
    upd = g_ref[...] * acc

    @pl.when(i < n_lat)
    def _():
        o_ref[...] = xl_ref[...] + upd

    @pl.when(i >= n_lat)
    def _():
        o_ref[...] = xc_ref[...] + upd


def _out_proj(mix_a, mix_b, w_out, widx, x_lat, x_ctx, ctx_block0, mods5, layer, rows, n_tiles):
    d = x_lat.shape[-1]
    ka, kb = mix_a.shape[-1], mix_b.shape[-1]
    tm = rows.tm
    nl = rows.n_lat
    stage_rows = min(512, ka + kb)
    return pl.pallas_call(
        functools.partial(_out_proj_kernel, n_lat=nl, widx=widx),
        grid=(n_tiles,),
        in_specs=[pl.BlockSpec((tm, ka), lambda i: (i, 0)),
                  pl.BlockSpec((tm, kb), lambda i: (i, 0)),
                  pl.BlockSpec(memory_space=pl.ANY),
                  pl.BlockSpec((tm, d), lambda i: (jnp.minimum(i, nl - 1), 0)),
                  pl.BlockSpec((tm, d), lambda i: (ctx_block0 + jnp.maximum(i - nl, 0), 0)),
                  _mod_spec(rows, layer, 2, d)],
        out_specs=pl.BlockSpec((tm, d), lambda i: (i, 0)),
        out_shape=jax.ShapeDtypeStruct((n_tiles * tm, d), F32),
        scratch_shapes=[pltpu.VMEM((ka + kb, d), BF16), pltpu.VMEM((stage_rows, d), F32)],
        compiler_params=_cp(("arbitrary",)),
        name="out_proj",
    )(mix_a, mix_b, w_out, x_lat, x_ctx, mods5)


def _tri(c, upper):
    r = lax.broadcasted_iota(jnp.int32, (c, c), 0)
    s = lax.broadcasted_iota(jnp.int32, (c, c), 1)
    return (s >= r) if upper else (r >= s)


def _scan_block(q, k, v, g, st, mask, forward):
    c = SCAN_C
    dk, dv = q.shape[-1], v.shape[-1]
    n = q.shape[0] // c
    mid, last = (c // 2 - 1, c - 1) if forward else (c // 2, 0)
    tri = jnp.broadcast_to(mask.astype(BF16)[None], (n, c, c))
    g3 = g.reshape(n, c, dk)
    g_hi = g3.astype(BF16)
    g_lo = (g3 - g_hi.astype(F32)).astype(BF16)
    cum = (jnp.einsum('cts,csd->ctd', tri, g_hi, preferred_element_type=F32)
           + jnp.einsum('cts,csd->ctd', tri, g_lo, preferred_element_type=F32))
    m = cum[:, mid:mid + 1, :]
    tot = cum[:, last:last + 1, :]
    qe = (q.reshape(n, c, dk) * jnp.exp(cum - m)).astype(BF16)
    ke = (k.reshape(n, c, dk) * jnp.exp(m - cum)).astype(BF16)
    a = jnp.einsum('ctd,csd->cts', qe, ke, preferred_element_type=F32)
    a = jnp.where(mask[None], a, 0.0).astype(BF16)
    v3 = v.reshape(n, c, dv)
    o = jnp.einsum('cts,csv->ctv', a, v3, preferred_element_type=F32)
    u = jnp.einsum('csv,csd->cvd', v3, ke, preferred_element_type=F32)
    em = jnp.exp(m)
    et = jnp.exp(tot - m)
    states = [None] * n
    for ci in (range(n) if forward else reversed(range(n))):
        stp = st * em[ci]
        states[ci] = stp.astype(BF16)
        st = (stp + u[ci]) * et[ci]
    o = o + jnp.einsum('ctd,cvd->ctv', qe, jnp.stack(states), preferred_element_type=F32)
    return o.reshape(n * c, dv), st


def _scan_segments(segments, prep_f, prep_b, of_ref, ob_ref, dk, dv):
    low, up = _tri(SCAN_C, False), _tri(SCAN_C, True)
    carry = (jnp.zeros((dv, dk), F32), jnp.zeros((dv, dk), F32))
    for rows, off, seg in segments:
        rb_ = min(SCAN_BLOCK, rows)
        n = rows // rb_

        def body(i, carry, n=n, off=off, seg=seg, rb_=rb_):
            sf, sb = carry
            rf = pl.multiple_of(i * rb_, rb_)
            rb = pl.multiple_of((n - 1 - i) * rb_, rb_)
            q, k, v, g = prep_f(seg, rf, rb_)
            o, sf = _scan_block(q, k, v, g, sf, low, True)
            of_ref[pl.ds(off + rf, rb_), :] = o
            q, k, v, g = prep_b(seg, rb, rb_)
            o, sb = _scan_block(q, k, v, g, sb, up, False)
            ob_ref[pl.ds(off + rb, rb_), :] = o
            return sf, sb

        carry = lax.fori_loop(0, n, body, carry)


def _hgrn_kernel(ql, qc, f1l, f1c, f2l, f2c, vl, vc, gl, gc, lbf, lbb, ng, o_ref, of_s, ob_s, *, ctx_len, seq):
    rt = pl.program_id(2)
    n_ctx_tiles = ctx_len // TQ

    @pl.when(rt == 0)
    def _scan():
        refs = {0: (qc, f1c, f2c, vc), 1: (ql, f1l, f2l, vl)}
        scale = A_DK ** -0.5

        def prep(seg, r, nr, fi, lb_ref):
            x = refs[seg][0][pl.ds(r, nr), :].astype(F32)
            q = x * jax.nn.sigmoid(x) * scale
            v = refs[seg][3][pl.ds(r, nr), :]
            lb = lb_ref[...]
            f = lb + (1.0 - lb) * jax.nn.sigmoid(refs[seg][fi][pl.ds(r, nr), :].astype(F32))
            return q, 1.0 - f, v, jnp.log(f)

        _scan_segments(
            [(ctx_len, 0, 0), (seq, ctx_len, 1)],
            lambda seg, r, nr: prep(seg, r, nr, 1, lbf),
            lambda seg, r, nr: prep(seg, r, nr, 2, lbb),
            of_s, ob_s, A_DK, A_DV)

    r0 = pl.multiple_of(rt * TQ, TQ)
    o = of_s[pl.ds(r0, TQ), :] + ob_s[pl.ds(r0, TQ), :]
    y = _rms(o, ng[...])

    @pl.when(rt < n_ctx_tiles)
    def _():
        gate = gc[pl.ds(r0, TQ), :].astype(F32)
        o_ref[...] = (y * jax.nn.sigmoid(gate)).astype(o_ref.dtype)

    @pl.when(rt >= n_ctx_tiles)
    def _():
        gate = gl[pl.ds(pl.multiple_of(r0 - ctx_len, TQ), TQ), :].astype(F32)
        o_ref[...] = (y * jax.nn.sigmoid(gate)).astype(o_ref.dtype)


def _out_row_block(batch, seq, ctx_len):
    nct = ctx_len // TQ
    nlt = seq // TQ

    def f(b, rt):
        return jnp.where(rt < nct, batch * nlt + b * nct + rt, b * nlt + rt - nct)

    return f


def _hgrn(p, lb_f, lb_b, norm_g, batch, seq, ctx_len):
    h, dk, dv = A_HEADS, A_DK, A_DV
    nct, nlt = ctx_len // TQ, seq // TQ
    ctx_blk0 = batch * seq // ctx_len
    row_block = _out_row_block(batch, seq, ctx_len)
    in_specs = []
    for kcol in range(5):
        in_specs.append(pl.BlockSpec((seq, dk), lambda b, hh, rt, kcol=kcol: (b, kcol * h + hh)))
        in_specs.append(pl.BlockSpec((ctx_len, dk), lambda b, hh, rt, kcol=kcol: (ctx_blk0 + b, kcol * h + hh)))
    vec = pl.BlockSpec((1, dk), lambda b, hh, rt: (0, hh))
    in_specs += [vec, vec, pl.BlockSpec((1, dv), lambda b, hh, rt: (0, 0))]
    return pl.pallas_call(
        functools.partial(_hgrn_kernel, ctx_len=ctx_len, seq=seq),
        grid=(batch, h, nct + nlt),
        in_specs=in_specs,
        out_specs=pl.BlockSpec((TQ, dv), lambda b, hh, rt: (row_block(b, rt), hh)),
        out_shape=jax.ShapeDtypeStruct((batch * (seq + ctx_len), h * dv), BF16),
        scratch_shapes=[pltpu.VMEM((seq + ctx_len, dv), F32), pltpu.VMEM((seq + ctx_len, dv), F32)],
        compiler_params=_cp(("parallel", "parallel", "arbitrary")),
        name="hgrn_scan",
    )(*([p] * 10), lb_f.reshape(1, h * dk), lb_b.reshape(1, h * dk), norm_g.reshape(1, dv))


def _gla_kernel(ql, qc, kl, kc, vl, vc, gl, al, ac, wa, ba, ng, o_ref, of_s, ob_s, *, ctx_len, seq):
    rt = pl.program_id(2)

    @pl.when(rt == 0)
    def _scan():
        refs = {0: (qc, kc, vc, ac), 1: (ql, kl, vl, al)}
        scale = D_DK ** -0.5
        r16 = D_GATE_RANK

        def prep(seg, r, nr, d):
            q = refs[seg][0][pl.ds(r, nr), :].astype(F32) * scale
            k = refs[seg][1][pl.ds(r, nr), :].astype(F32)
            v = refs[seg][2][pl.ds(r, nr), :]
            a = refs[seg][3][pl.ds(r, nr), :].astype(F32)[:, d * r16:(d + 1) * r16]
            z = jnp.dot(a, wa[d], preferred_element_type=F32, precision=HI) + ba[d]
            g = (jnp.minimum(z, 0.0) - jnp.log(1.0 + jnp.exp(-jnp.abs(z)))) * (1.0 / GLA_TAU)
            return q, k, v, g

        _scan_segments(
            [(ctx_len, 0, 0), (seq, ctx_len, 1)],
            lambda seg, r, nr: prep(seg, r, nr, 0),
            lambda seg, r, nr: prep(seg, r, nr, 1),
            of_s, ob_s, D_DK, D_DV)

    r0 = pl.multiple_of(rt * TQ, TQ)
    o = of_s[pl.ds(ctx_len + r0, TQ), :] + ob_s[pl.ds(ctx_len + r0, TQ), :]
    gate = gl[pl.ds(r0, TQ), :].astype(F32)
    o_ref[...] = (_rms(o, ng[...]) * gate * jax.nn.sigmoid(gate)).astype(o_ref.dtype)


def _gla(p, ga, w_a2, b_a, norm_g, batch, seq, ctx_len):
    h, dk, dv = D_HEADS, D_DK, D_DV
    nlt = seq // TQ
    ctx_blk0 = batch * seq // ctx_len
    q0 = (C_HEADS + 2 * C_KV_HEADS) * C_DH // dk
    k0 = q0 + h
    v0 = (k0 + h) * dk // dv
    g0 = v0 + h

    def pair(width, blk0):
        return [pl.BlockSpec((seq, width), lambda b, hh, rt: (b, blk0 + hh)),
                pl.BlockSpec((ctx_len, width), lambda b, hh, rt: (ctx_blk0 + b, blk0 + hh))]

    in_specs = pair(dk, q0) + pair(dk, k0) + pair(dv, v0)
    in_specs += [pl.BlockSpec((seq, dv), lambda b, hh, rt: (b, g0 + hh)),
                 pl.BlockSpec((seq, 2 * D_GATE_RANK), lambda b, hh, rt: (b, 0)),
                 pl.BlockSpec((ctx_len, 2 * D_GATE_RANK), lambda b, hh, rt: (ctx_blk0 + b, 0)),
                 pl.BlockSpec((2, D_GATE_RANK, dk), lambda b, hh, rt: (0, 0, hh)),
                 pl.BlockSpec((2, 1, dk), lambda b, hh, rt: (0, 0, hh)),
                 pl.BlockSpec((1, dv), lambda b, hh, rt: (0, 0))]
    return pl.pallas_call(
        functools.partial(_gla_kernel, ctx_len=ctx_len, seq=seq),
        grid=(batch, h, nlt),
        in_specs=in_specs,
        out_specs=pl.BlockSpec((TQ, dv), lambda b, hh, rt: (b * nlt + rt, hh)),
        out_shape=jax.ShapeDtypeStruct((batch * seq, h * dv), BF16),
        scratch_shapes=[pltpu.VMEM((seq + ctx_len, dv), F32), pltpu.VMEM((seq + ctx_len, dv), F32)],
        compiler_params=_cp(("parallel", "parallel", "arbitrary")),
        name="gla_scan",
    )(p, p, p, p, p, p, p, ga, ga, w_a2, b_a.reshape(2, 1, h * dk), norm_g.reshape(1, dv))


def _softmax_pv(s, v):
    m = jnp.max(s, axis=-1, keepdims=True)
    p = jnp.exp(s - m)
    l = jnp.sum(p, axis=-1, keepdims=True)
    return jnp.dot(p.astype(BF16), v, preferred_element_type=F32) / l


def _mla_kernel(ql_ref, kvl_ref, kvc_ref, krl_ref, krc_ref, wqn_ref, wqr_ref, wkv_ref, gq_ref, gkv_ref,
                cosq_ref, sinq_ref, cosk_ref, sink_ref, o_ref, kn_s, kr_s, v_s, *, ctx_len):
    qt = pl.program_id(1)
    n_ctx_tiles = ctx_len // TQ
    scale = (B_NOPE + B_ROPE) ** -0.5
    dkv = B_NOPE + B_DV

    @pl.when(qt == 0)
    def _prep():
        kvc = _rms(kvc_ref[...].astype(F32), gkv_ref[...]).astype(BF16)
        kvl = _rms(kvl_ref[...].astype(F32), gkv_ref[...]).astype(BF16)
        for h in range(B_HEADS):
            w = wkv_ref[:, h * dkv:(h + 1) * dkv].astype(BF16)
            up_c = jnp.dot(kvc, w, preferred_element_type=F32)
            kn_s[h, 0:ctx_len, :] = up_c[:, :B_NOPE].astype(BF16)
            v_s[h, 0:ctx_len, :] = up_c[:, B_NOPE:].astype(BF16)
            up_l = jnp.dot(kvl, w, preferred_element_type=F32)
            kn_s[h, ctx_len:, :] = up_l[:, :B_NOPE].astype(BF16)
            v_s[h, ctx_len:, :] = up_l[:, B_NOPE:].astype(BF16)
        kr_s[0:ctx_len, :] = krc_ref[...]
        kr_s[ctx_len:, :] = _rope(krl_ref[...].astype(F32), cosk_ref[...], sink_ref[...]).astype(BF16)

    xn = _rms(ql_ref[...].astype(F32), gq_ref[...]).astype(BF16)
    qn_all = jnp.dot(xn, wqn_ref[...].astype(BF16), preferred_element_type=F32) * scale
    qr_all = jnp.dot(xn, wqr_ref[...].astype(BF16), preferred_element_type=F32) * scale

    def heads(n_keys, rotate):
        outs = []
        for h in range(B_HEADS):
            qn = qn_all[:, h * B_NOPE:(h + 1) * B_NOPE].astype(BF16)
            qr = qr_all[:, h * B_ROPE:(h + 1) * B_ROPE]
            if rotate:
                qr = _rope(qr, cosq_ref[...], sinq_ref[...])
            s = _dot_nt(qn, kn_s[h, 0:n_keys, :]) + _dot_nt(qr.astype(BF16), kr_s[0:n_keys, :])
            outs.append(_softmax_pv(s, v_s[h, 0:n_keys, :]).astype(o_ref.dtype))
        o_ref[...] = jnp.concatenate(outs, axis=-1)

    @pl.when(qt < n_ctx_tiles)
    def _():
        heads(ctx_len, False)

    @pl.when(qt >= n_ctx_tiles)
    def _():
        heads(kr_s.shape[0], True)


def _mla(p, kr, w_uq, w_ukv, gq, gkv, cos, sin, batch, seq, ctx_len):
    h = B_HEADS
    nct, nlt = ctx_len // TQ, seq // TQ
    ctx_blk0 = batch * seq // ctx_len
    row_block = _out_row_block(batch, seq, ctx_len)
    ql_blk = 5 * A_HEADS * A_DK // B_Q_LORA
    kv_blk = (5 * A_HEADS * A_DK + B_Q_LORA) // B_KV_LORA
    dq = B_NOPE + B_ROPE
    s_all = seq + ctx_len
    w3 = w_uq.reshape(B_Q_LORA, h, dq)
    wq_n = w3[:, :, :B_NOPE].reshape(B_Q_LORA, h * B_NOPE)
    wq_r = w3[:, :, B_NOPE:].reshape(B_Q_LORA, h * B_ROPE)
    in_specs = [
        pl.BlockSpec((TQ, B_Q_LORA), lambda b, qt: (row_block(b, qt), ql_blk)),
        pl.BlockSpec((seq, B_KV_LORA), lambda b, qt: (b, kv_blk)),
        pl.BlockSpec((ctx_len, B_KV_LORA), lambda b, qt: (ctx_blk0 + b, kv_blk)),
        pl.BlockSpec((seq, B_ROPE), lambda b, qt: (b, 0)),
        pl.BlockSpec((ctx_len, B_ROPE), lambda b, qt: (ctx_blk0 + b, 0)),
        pl.BlockSpec((B_Q_LORA, h * B_NOPE), lambda b, qt: (0, 0)),
        pl.BlockSpec((B_Q_LORA, h * B_ROPE), lambda b, qt: (0, 0)),
        pl.BlockSpec((B_KV_LORA, h * (B_NOPE + B_DV)), lambda b, qt: (0, 0)),
        pl.BlockSpec((1, B_Q_LORA), lambda b, qt: (0, 0)),
        pl.BlockSpec((1, B_KV_LORA), lambda b, qt: (0, 0)),
        pl.BlockSpec((TQ, B_ROPE), lambda b, qt: (jnp.maximum(qt - nct, 0), 0)),
        pl.BlockSpec((TQ, B_ROPE), lambda b, qt: (jnp.maximum(qt - nct, 0), 0)),
        pl.BlockSpec((seq, B_ROPE), lambda b, qt: (0, 0)),
        pl.BlockSpec((seq, B_ROPE), lambda b, qt: (0, 0)),
    ]
    return pl.pallas_call(
        functools.partial(_mla_kernel, ctx_len=ctx_len),
        grid=(batch, nct + nlt),
        in_specs=in_specs,
        out_specs=pl.BlockSpec((TQ, h * B_DV), lambda b, qt: (row_block(b, qt), 0)),
        out_shape=jax.ShapeDtypeStruct((batch * s_all, h * B_DV), BF16),
        scratch_shapes=[pltpu.VMEM((h, s_all, B_NOPE), BF16), pltpu.VMEM((s_all, B_ROPE), BF16),
                        pltpu.VMEM((h, s_all, B_DV), BF16)],
        compiler_params=_cp(("parallel", "arbitrary")),
        name="mla_attn",
    )(p, p, p, kr, kr, wq_n, wq_r, w_ukv, gq.reshape(1, -1), gkv.reshape(1, -1), cos, sin, cos, sin)


def _gqa_kernel(q_ref, kl_ref, kc_ref, vl_ref, vc_ref, gq_ref, gk_ref, cosq_ref, sinq_ref, cosk_ref, sink_ref,
                o_ref, k_s, v_s, *, ctx_len):
    qt = pl.program_id(2)
    scale = C_DH ** -0.5
    dh = C_DH

    @pl.when(qt == 0)
    def _prep():
        k_s[0:ctx_len, :] = _rms(kc_ref[...].astype(F32), gk_ref[...]).astype(BF16)
        kl = _rms(kl_ref[...].astype(F32), gk_ref[...])
        k_s[ctx_len:, :] = _rope(kl, cosk_ref[...], sink_ref[...]).astype(BF16)
        v_s[0:ctx_len, :] = vc_ref[...]
        v_s[ctx_len:, :] = vl_ref[...]

    outs = []
    for g in range(C_HEADS // C_KV_HEADS):
        q = _rms(q_ref[:, g * dh:(g + 1) * dh].astype(F32), gq_ref[...])
        q = _rope(q, cosq_ref[...], sinq_ref[...]) * scale
        s = _dot_nt(q.astype(BF16), k_s[...])
        outs.append(_softmax_pv(s, v_s[...]).astype(o_ref.dtype))
    o_ref[...] = jnp.concatenate(outs, axis=-1)


def _gqa(p, gq, gk, cos, sin, batch, seq, ctx_len):
    kvh, grp, dh = C_KV_HEADS, C_HEADS // C_KV_HEADS, C_DH
    nlt = seq // TQ
    ctx_blk0 = batch * seq // ctx_len
    k0 = C_HEADS
    v0 = C_HEADS + C_KV_HEADS
    s_all = seq + ctx_len
    in_specs = [
        pl.BlockSpec((TQ, grp * dh), lambda b, kh, qt: (b * nlt + qt, kh)),
        pl.BlockSpec((seq, dh), lambda b, kh, qt: (b, k0 + kh)),
        pl.BlockSpec((ctx_len, dh), lambda b, kh, qt: (ctx_blk0 + b, k0 + kh)),
        pl.BlockSpec((seq, dh), lambda b, kh, qt: (b, v0 + kh)),
        pl.BlockSpec((ctx_len, dh), lambda b, kh, qt: (ctx_blk0 + b, v0 + kh)),
        pl.BlockSpec((1, dh), lambda b, kh, qt: (0, 0)),
        pl.BlockSpec((1, dh), lambda b, kh, qt: (0, 0)),
        pl.BlockSpec((TQ, dh), lambda b, kh, qt: (qt, 0)),
        pl.BlockSpec((TQ, dh), lambda b, kh, qt: (qt, 0)),
        pl.BlockSpec((seq, dh), lambda b, kh, qt: (0, 0)),
        pl.BlockSpec((seq, dh), lambda b, kh, qt: (0, 0)),
    ]
    return pl.pallas_call(
        functools.partial(_gqa_kernel, ctx_len=ctx_len),
        grid=(batch, kvh, nlt),
        in_specs=in_specs,
        out_specs=pl.BlockSpec((TQ, grp * dh), lambda b, kh, qt: (b * nlt + qt, kh)),
        out_shape=jax.ShapeDtypeStruct((batch * seq, C_HEADS * dh), BF16),
        scratch_shapes=[pltpu.VMEM((s_all, dh), BF16), pltpu.VMEM((s_all, dh), BF16)],
        compiler_params=_cp(("parallel", "parallel", "arbitrary")),
        name="gqa_attn",
    )(p, p, p, p, p, gq.reshape(1, dh), gk.reshape(1, dh), cos, sin, cos, sin)


def _router_kernel(x_ref, g_ref, sh_ref, sc_ref, rw_ref, rb_ref, h_ref, ri_ref, rf_ref, cnt_ref, base_s):
    i = pl.program_id(0)
    tm = x_ref.shape[0]
    ne = N_EXPERTS
    per = ne // N_GROUPS

    @pl.when(i == 0)
    def _():
        base_s[...] = jnp.zeros_like(base_s)

    h = _rms(x_ref[...], g_ref[...]) * (1.0 + sc_ref[...]) + sh_ref[...]
    nch = h.shape[1] // LANES
    for j in range(nch):
        h_ref[pl.ds(j, tm, stride=nch), :] = h[:, j * LANES:(j + 1) * LANES]
    h_hi = h.astype(BF16)
    h_lo = (h - h_hi.astype(F32)).astype(BF16)
    rw = rw_ref[...]
    w_hi = rw.astype(BF16)
    w_lo = (rw - w_hi.astype(F32)).astype(BF16)
    hw = jnp.dot(h_hi, jnp.concatenate([w_hi, w_lo], axis=1), preferred_element_type=F32)
    logits = hw[:, :ne] + hw[:, ne:] + jnp.dot(h_lo, w_hi, preferred_element_type=F32)
    scores = jax.nn.sigmoid(logits)
    sel = scores + rb_ref[...]
    lane = lax.broadcasted_iota(jnp.int32, (tm, ne), 1).astype(F32)
    neg = -jnp.inf
    big = float(ne)

    def top2(vals):
        m1 = jnp.max(vals, axis=1, keepdims=True)
        i1 = jnp.min(jnp.where(vals == m1, lane, big), axis=1, keepdims=True)
        rest = jnp.where(lane == i1, neg, vals)
        m2 = jnp.max(rest, axis=1, keepdims=True)
        i2 = jnp.min(jnp.where(rest == m2, lane, big), axis=1, keepdims=True)
        return m1 + m2, i1, i2

    best, e1, e2 = None, None, None
    for grp in range(N_GROUPS):
        in_grp = jnp.logical_and(lane >= float(grp * per), lane < float((grp + 1) * per))
        gsum, i1, i2 = top2(jnp.where(in_grp, sel, neg))
        if grp == 0:
            best, e1, e2 = gsum, i1, i2
        else:
            better = gsum > best
            best = jnp.where(better, gsum, best)
            e1 = jnp.where(better, i1, e1)
            e2 = jnp.where(better, i2, e2)

    hot1 = lane == e1
    hot2 = lane == e2
    w1 = jnp.sum(jnp.where(hot1, scores, 0.0), axis=1, keepdims=True)
    w2 = jnp.sum(jnp.where(hot2, scores, 0.0), axis=1, keepdims=True)
    wsum = w1 + w2
    assign = jnp.logical_or(hot1, hot2)
    r = lax.broadcasted_iota(jnp.int32, (tm, tm), 0)
    c = lax.broadcasted_iota(jnp.int32, (tm, tm), 1)
    before = (c < r).astype(BF16)
    excl = jnp.dot(before, assign.astype(BF16), preferred_element_type=F32) + base_s[...]
    rank1 = jnp.sum(jnp.where(hot1, excl, 0.0), axis=1, keepdims=True)
    rank2 = jnp.sum(jnp.where(hot2, excl, 0.0), axis=1, keepdims=True)
    base_s[...] = base_s[...] + jnp.sum(assign.astype(F32), axis=0, keepdims=True)

    l128 = lax.broadcasted_iota(jnp.int32, (tm, 128), 1)
    ri = jnp.where(l128 == 0, e1, jnp.where(l128 == 1, e2, jnp.where(l128 == 2, rank1, jnp.where(l128 == 3, rank2, 0.0))))
    ri_ref[...] = ri.T[0:8, :].astype(jnp.int32)
    rf_ref[...] = jnp.where(l128 == 0, w1 / wsum, jnp.where(l128 == 1, w2 / wsum, 0.0))
    cnt_ref[...] = jnp.broadcast_to(base_s[...], cnt_ref.shape)


def _router(x_all, g, mods5, layer, router_w, router_b, rows, n_tiles):
    d = x_all.shape[-1]
    tm = rows.tm
    n = n_tiles * tm
    ne = N_EXPERTS
    return pl.pallas_call(
        _router_kernel,
        grid=(n_tiles,),
        in_specs=[pl.BlockSpec((tm, d), lambda i: (i, 0)),
                  pl.BlockSpec((1, d), lambda i: (0, 0)),
                  _mod_spec(rows, layer, 3, d),
                  _mod_spec(rows, layer, 4, d),
                  pl.BlockSpec((d, ne), lambda i: (0, 0)),
                  pl.BlockSpec((1, ne), lambda i: (0, 0))],
        out_specs=[pl.BlockSpec((tm * (d // LANES), LANES), lambda i: (i, 0)),
                   pl.BlockSpec((8, tm), lambda i: (0, i)),
                   pl.BlockSpec((tm, 128), lambda i: (i, 0)),
                   pl.BlockSpec((8, ne), lambda i: (0, 0))],
        out_shape=[jax.ShapeDtypeStruct((n * (d // LANES), LANES), F32),
                   jax.ShapeDtypeStruct((8, n), jnp.int32),
                   jax.ShapeDtypeStruct((n, 128), F32),
                   jax.ShapeDtypeStruct((8, ne), F32)],
        scratch_shapes=[pltpu.VMEM((1, ne), F32)],
        compiler_params=_cp(("arbitrary",)),
        name="moe_router",
    )(x_all, g.reshape(1, d), mods5, mods5, router_w, router_b.reshape(1, ne))


def _expert_kernel(te_ref, nx_ref, nu_ref, pad_ref, dest_ref, h_hbm, w1_hbm, w3_hbm, w2_hbm, y_ref, xbuf, sem,
                   w1_s, w3_s, w2_s, wf1, wf3, wf2, wsem, wslot_ref, src_ref, *, n_tok, layer):
    r = pl.program_id(0)
    n_used = nu_ref[0]
    active = r < n_used
    changed = jnp.logical_or(r == 0, te_ref[r] != te_ref[jnp.maximum(r - 1, 0)])
    slot = lax.rem(r, 2)
    nch = xbuf.shape[1] // TE

    def weight_copies(e, ws):
        return (pltpu.make_async_copy(w1_hbm.at[layer, e], wf1.at[ws], wsem.at[ws]),
                pltpu.make_async_copy(w3_hbm.at[layer, e], wf3.at[ws], wsem.at[ws]),
                pltpu.make_async_copy(w2_hbm.at[layer, e], wf2.at[ws], wsem.at[ws]))

    @pl.when(r == 0)
    def _():
        wslot_ref[0] = 0
        for cp in weight_copies(te_ref[0], 0):
            cp.start()

        def clear(i, carry):
            src_ref[i] = 0
            return carry

        for e in range(N_EXPERTS):
            lax.fori_loop(pad_ref[e], pad_ref[N_EXPERTS + e], clear, 0)

        def invert(i, carry):
            src_ref[dest_ref[i]] = i
            src_ref[dest_ref[n_tok + i]] = i
            return carry

        lax.fori_loop(0, n_tok, invert, 0, unroll=DMA_UNROLL)

    def gather(tile, dst_slot):
        base = tile * TE
        for t in range(TE):
            row0 = pl.multiple_of(src_ref[base + t] * nch, nch)
            pltpu.make_async_copy(h_hbm.at[pl.ds(row0, nch)], xbuf.at[dst_slot, pl.ds(t * nch, nch)],
                                  sem.at[dst_slot]).start(priority=t % 2)

    @pl.when(jnp.logical_and(r == 0, active))
    def _():
        gather(0, 0)

    @pl.when(r + 1 < n_used)
    def _():
        gather(r + 1, 1 - slot)

    @pl.when(jnp.logical_and(active, changed))
    def _():
        ws = wslot_ref[0]
        for cp in weight_copies(te_ref[r], ws):
            cp.wait()

        @pl.when(nx_ref[r] != te_ref[r])
        def _():
            for cp in weight_copies(nx_ref[r], 1 - ws):
                cp.start()

        w1_s[...] = wf1[ws].astype(BF16)
        w3_s[...] = wf3[ws].astype(BF16)
        w2_s[...] = wf2[ws].astype(BF16)
        wslot_ref[0] = 1 - ws

    @pl.when(active)
    def _():
        pltpu.make_async_copy(h_hbm.at[pl.ds(0, TE * nch)], xbuf.at[slot], sem.at[slot]).wait()
        x = jnp.concatenate([xbuf[slot, pl.ds(j, TE, stride=nch), :] for j in range(nch)], axis=1).astype(BF16)
        a = jnp.dot(x, w1_s[...], preferred_element_type=F32)
        b = jnp.dot(x, w3_s[...], preferred_element_type=F32)
        hid = (a * jax.nn.sigmoid(a) * b).astype(BF16)
        y_ref[...] = jnp.dot(hid, w2_s[...], preferred_element_type=F32)

    @pl.when(jnp.logical_not(active))
    def _():
        y_ref[...] = jnp.zeros_like(y_ref)


def _experts(tile_expert, next_expert, n_used, pad, dest, hp, w1, w3, w2, layer, p_max):
    d = w1.shape[-2]
    nch = d // LANES
    n_tok = hp.shape[0] // nch
    f = w1.shape[-1]
    any_spec = pl.BlockSpec(memory_space=pl.ANY)
    grid_spec = pltpu.PrefetchScalarGridSpec(
        num_scalar_prefetch=5,
        grid=(p_max // TE,),
        in_specs=[any_spec, any_spec, any_spec, any_spec],
        out_specs=pl.BlockSpec((TE, d), lambda r, te, nx, nu, pd, sr: (r, 0)),
        scratch_shapes=[pltpu.VMEM((2, TE * nch, LANES), F32), pltpu.SemaphoreType.DMA((2,)),
                        pltpu.VMEM((d, f), BF16), pltpu.VMEM((d, f), BF16), pltpu.VMEM((f, d), BF16),
                        pltpu.VMEM((2, d, f), F32), pltpu.VMEM((2, d, f), F32), pltpu.VMEM((2, f, d), F32),
                        pltpu.SemaphoreType.DMA((2,)), pltpu.SMEM((1,), jnp.int32),
                        pltpu.SMEM((p_max,), jnp.int32)],
    )
    return pl.pallas_call(
        functools.partial(_expert_kernel, n_tok=n_tok, layer=layer),
        grid_spec=grid_spec,
        out_shape=jax.ShapeDtypeStruct((p_max, d), F32),
        compiler_params=_cp(("arbitrary",)),
        name="moe_experts",
    )(tile_expert, next_expert, n_used, pad, dest, hp, w1, w3, w2)


def _combine_kernel(dest_ref, x_ref, rf_ref, g_ref, pg_ref, psh_ref, psc_ref, ys_hbm, *rest, tm, n_tok, n_tiles, final):
    o_ref = rest[0]
    buf, sem = rest[-2], rest[-1]
    i = pl.program_id(0)
    slot = lax.rem(i, 2)

    def gather(tile, dst_slot):
        base = tile * tm
        for t in range(tm):
            pltpu.make_async_copy(ys_hbm.at[pl.ds(dest_ref[base + t], 1)], buf.at[dst_slot, 0, pl.ds(t, 1)],
                                  sem.at[dst_slot]).start(priority=0)
            pltpu.make_async_copy(ys_hbm.at[pl.ds(dest_ref[n_tok + base + t], 1)], buf.at[dst_slot, 1, pl.ds(t, 1)],
                                  sem.at[dst_slot]).start(priority=1)

    @pl.when(i == 0)
    def _():
        gather(0, 0)

    @pl.when(i + 1 < n_tiles)
    def _():
        gather(i + 1, 1 - slot)

    pltpu.make_async_copy(ys_hbm.at[pl.ds(0, tm)], buf.at[slot, 0], sem.at[slot]).wait()
    pltpu.make_async_copy(ys_hbm.at[pl.ds(0, tm)], buf.at[slot, 1], sem.at[slot]).wait()
    w = rf_ref[...]
    y = w[:, 0:1] * buf[slot, 0] + w[:, 1:2] * buf[slot, 1]
    x2 = x_ref[...] + g_ref[...] * y
    z = _rms(x2, pg_ref[...])
    if final:
        o_ref[...] = z
    else:
        o_ref[...] = x2
        rest[1][...] = (z * (1.0 + psc_ref[...]) + psh_ref[...]).astype(rest[1].dtype)


def _combine(dest, x_all, rf, mods5, layer, post_g, ys, rows, n_tiles, n_tok_total, final):
    d = x_all.shape[-1]
    tm = rows.tm
    nxt = min(layer + 1, mods5.shape[0] - 1)
    row_spec = pl.BlockSpec((tm, d), lambda i, dr: (i, 0))
    grid_spec = pltpu.PrefetchScalarGridSpec(
        num_scalar_prefetch=1,
        grid=(n_tiles,),
        in_specs=[row_spec,
                  pl.BlockSpec((tm, 128), lambda i, dr: (i, 0)),
                  _mod_spec(rows, layer, 5, d),
                  pl.BlockSpec((1, d), lambda i, dr: (0, 0)),
                  _mod_spec(rows, nxt, 0, d),
                  _mod_spec(rows, nxt, 1, d),
                  pl.BlockSpec(memory_space=pl.ANY)],
        out_specs=row_spec if final else [row_spec, row_spec],
        scratch_shapes=[pltpu.VMEM((2, 2, tm, d), F32), pltpu.SemaphoreType.DMA((2,))],
    )
    stream = jax.ShapeDtypeStruct((n_tiles * tm, d), F32)
    return pl.pallas_call(
        functools.partial(_combine_kernel, tm=tm, n_tok=n_tok_total, n_tiles=n_tiles, final=final),
        grid_spec=grid_spec,
        out_shape=stream if final else [stream, jax.ShapeDtypeStruct((n_tiles * tm, d), BF16)],
        compiler_params=_cp(("arbitrary",)),
        name="moe_combine",
    )(dest, x_all, rf, mods5, post_g.reshape(1, d), mods5, mods5, ys)


def _moe(x_all, n_tok, norm_g, mods5, layer, router_w, router_b, w1, w3, w2, final_g, final, batch, seq, ctx_len):
    rows_r = _Rows(batch, seq, ctx_len, TROUTE)
    rows_c = _Rows(batch, seq, ctx_len, TCOMB)
    h, ri, rf, cnt = _router(x_all, norm_g, mods5, layer, router_w, router_b, rows_r, n_tok // TROUTE)
    counts = cnt[0].astype(jnp.int32)
    padded = ((counts + TE - 1) // TE) * TE
    ends = jnp.cumsum(padded)
    starts = ends - padded
    e1, e2, r1, r2 = ri[0], ri[1], ri[2], ri[3]
    dest = jnp.concatenate([starts[e1] + r1, starts[e2] + r2]).astype(jnp.int32)
    p_max = 2 * n_tok + N_EXPERTS * TE
    n_tiles = p_max // TE
    n_used = (ends[-1] // TE).astype(jnp.int32)
    tile_start = jnp.arange(n_tiles, dtype=jnp.int32) * TE
    tile_expert = jnp.sum((tile_start[:, None] >= ends[None, :]).astype(jnp.int32), axis=1)
    last_expert = jnp.sum((jnp.maximum(ends[-1] - 1, 0) >= ends).astype(jnp.int32))
    tile_expert = jnp.minimum(jnp.where(tile_start < ends[-1], tile_expert, last_expert), N_EXPERTS - 1).astype(jnp.int32)
    eid = jnp.arange(N_EXPERTS, dtype=jnp.int32)
    later_used = jnp.logical_and(eid[None, :] > eid[:, None], (padded > 0)[None, :])
    next_used = jnp.min(jnp.where(later_used, eid[None, :], N_EXPERTS), axis=1)
    next_used = jnp.where(next_used == N_EXPERTS, eid, next_used)
    next_expert = jnp.sum(jnp.where(tile_expert[:, None] == eid[None, :], next_used[None, :], 0), axis=1).astype(jnp.int32)
    pad = jnp.concatenate([starts + counts, ends]).astype(jnp.int32)
    ys = _experts(tile_expert, next_expert, n_used.reshape(1), pad, dest, h, w1, w3, w2, layer, p_max)
    return _combine(dest, x_all, rf, mods5, layer, final_g, ys, rows_c, n_tok // TCOMB, n_tok, final)


def _rope_tables(t_len, d_rope):
    rows = t_len // GRID_W
    quarter = d_rope // 4
    freqs = ROPE_THETA ** (-jnp.arange(quarter, dtype=F32) / quarter)
    row = jnp.repeat(jnp.arange(rows, dtype=F32), GRID_W)
    col = jnp.tile(jnp.arange(GRID_W, dtype=F32), rows)
    ang = jnp.concatenate([row[:, None] * freqs, col[:, None] * freqs], axis=-1)
    cos, sin = jnp.cos(ang), jnp.sin(ang)
    return jnp.concatenate([cos, cos], axis=-1), jnp.concatenate([-sin, sin], axis=-1)


def kernel(x, c, ctx, c_ctx, mod_w, mod_b, norm_attn_g, norm_ffn_g, final_norm_g, ab_w_in, ab_w_out, hgrn_lb_logits, hgrn_norm_g, mla_q_norm_g, mla_w_uq, mla_kv_norm_g, mla_w_ukv, cd_w_in, cd_w_out, gqa_q_norm_g, gqa_k_norm_g, gla_w_a2, gla_b_a, gla_norm_g, router_w, router_b, moe_w1, moe_w3, moe_w2):
    batch, seq, d = x.shape
    ctx_len = ctx.shape[1]
    n_lat, n_ctx = batch * seq, batch * ctx_len
    assert ctx_len % TQ == 0 and seq % TQ == 0 and seq % ctx_len == 0 and batch < 8
    tm = min(1024, seq, n_ctx)
    rows = _Rows(batch, seq, ctx_len, tm)

    cvec = jnp.concatenate([c, c_ctx[None, :], jnp.zeros((8 - batch - 1, d), F32)], axis=0)
    mods = _modvec(cvec, mod_w, mod_b)
    mods5 = mods.reshape(mods.shape[0], 8, 6, 1, d)

    cos_b, sin_b = _rope_tables(seq, B_ROPE)
    cos_c, sin_c = _rope_tables(seq, C_DH)
    lb = jnp.cumsum(jax.nn.softmax(hgrn_lb_logits.astype(F32), axis=1), axis=1)

    x_lat = x.reshape(n_lat, d)
    x_ctx = ctx.reshape(n_ctx, d)

    h0 = _norm_mod(x_lat, x_ctx, 0, norm_attn_g[0], mods5, 0, rows)
    ab_main = 5 * A_HEADS * A_DK + B_Q_LORA + B_KV_LORA
    tm_mm = next(t for t in (2304, 2048, 1536, 1024, 512, 256) if (n_lat + n_ctx) % t == 0)
    ab_wt = jnp.swapaxes(ab_w_in, 1, 2)
    p0 = _matmul(h0, ab_wt, 0, ab_main, 256, tm_mm)
    kr0 = _matmul_tail(h0, ab_wt, 0, ab_main, B_ROPE, tm_mm)
    mix_a = _hgrn(p0, lb[0, 0], lb[1, 0], hgrn_norm_g[0], batch, seq, ctx_len)
    mix_b = _mla(p0, kr0, mla_w_uq[0], mla_w_ukv[0], mla_q_norm_g[0], mla_kv_norm_g[0], cos_b, sin_b,
                 batch, seq, ctx_len)
    rows_o = _Rows(batch, seq, ctx_len, min(512, tm))
    x1 = _out_proj(mix_a, mix_b, ab_w_out, 0, x_lat, x_ctx, 0, mods5, 0, rows_o, rows_o.n_all)
    x2, h1 = _moe(x1, n_lat + n_ctx, norm_ffn_g[0], mods5, 0, router_w, router_b, moe_w1, moe_w3, moe_w2,
                  norm_attn_g[1], False, batch, seq, ctx_len)

    cd_main = (C_HEADS + 2 * C_KV_HEADS) * C_DH + 2 * D_HEADS * D_DK + 2 * D_HEADS * D_DV
    cd_wt = jnp.swapaxes(cd_w_in, 1, 2)
    p1 = _matmul(h1, cd_wt, 0, cd_main, 512, tm_mm)
    ga1 = _matmul_tail(h1, cd_wt, 0, cd_main, 2 * D_GATE_RANK, tm_mm)
    mix_c = _gqa(p1, gqa_q_norm_g[0], gqa_k_norm_g[0], cos_c, sin_c, batch, seq, ctx_len)
    mix_d = _gla(p1, ga1, gla_w_a2[0], gla_b_a[0], gla_norm_g[0], batch, seq, ctx_len)
    x3 = _out_proj(mix_c, mix_d, cd_w_out, 0, x2, x2, rows_o.n_lat, mods5, 1, rows_o, rows_o.n_lat)
    out = _moe(x3, n_lat, norm_ffn_g[1], mods5, 1, router_w, router_b, moe_w1, moe_w3, moe_w2,
               final_norm_g, True, batch, seq, ctx_len)
    return out.reshape(batch, seq, d)
```

```python
import functools

import jax
import jax.numpy as jnp
from jax import lax
from jax.experimental import pallas as pl
from jax.experimental.pallas import tpu as pltpu

F32 = jnp.float32
BF16 = jnp.bfloat16
HI = lax.Precision.HIGHEST

GRID_W = 64
ROPE_THETA = 10000.0
NORM_EPS = 1e-6
A_HEADS, A_DK, A_DV = 8, 128, 128
B_HEADS, B_Q_LORA, B_KV_LORA, B_NOPE, B_ROPE, B_DV = 8, 512, 256, 128, 64, 128
C_HEADS, C_KV_HEADS, C_DH = 8, 2, 128
D_HEADS, D_DK, D_DV, D_GATE_RANK = 4, 128, 256, 16
GLA_TAU = 16.0
N_EXPERTS, N_GROUPS = 16, 4

TQ = 256
SCAN_C = 64
SCAN_HEADS = 2
SCAN_BLOCK = 2048
TE = 256
TROUTE = 512
TCOMB = 256
DMA_UNROLL = 8
LANES = 128
VMEM_MIB = 56


def _cp(sem):
    return pltpu.CompilerParams(dimension_semantics=sem, vmem_limit_bytes=VMEM_MIB * 1024 * 1024)


def _rms(x, g):
    return x * lax.rsqrt(jnp.mean(x * x, axis=-1, keepdims=True) + NORM_EPS) * g


def _rope(x, cos, sin):
    half = x.shape[-1] // 2
    swapped = jnp.concatenate([x[:, half:], x[:, :half]], axis=-1)
    return x * cos + swapped * sin


def _dot_nt(a, b):
    return lax.dot_general(a, b, (((1,), (1,)), ((), ())), preferred_element_type=F32)


def _dot_tn(a, b):
    return lax.dot_general(a, b, (((0,), (0,)), ((), ())), preferred_element_type=F32)


def _modvec_kernel(c_ref, w_ref, b_ref, o_ref):
    c = c_ref[...]
    a = c * jax.nn.sigmoid(c)
    a_hi = a.astype(BF16)
    a_lo = (a - a_hi.astype(F32)).astype(BF16)
    w = w_ref[...]
    w_hi = w.astype(BF16)
    w_lo = (w - w_hi.astype(F32)).astype(BF16)
    acc = jnp.dot(a_hi, w_hi, preferred_element_type=F32) + jnp.dot(a_lo, w_hi, preferred_element_type=F32)
    o_ref[...] = acc + jnp.dot(a_hi, w_lo, preferred_element_type=F32) + b_ref[...]


def _modvec(cvec, mod_w, mod_b):
    n_layers, d, n6 = mod_w.shape
    tn = min(1024, n6)
    return pl.pallas_call(
        _modvec_kernel,
        grid=(n_layers, n6 // tn),
        in_specs=[pl.BlockSpec((8, d), lambda l, j: (0, 0)),
                  pl.BlockSpec((None, d, tn), lambda l, j: (l, 0, j)),
                  pl.BlockSpec((None, 1, tn), lambda l, j: (l, 0, j))],
        out_specs=pl.BlockSpec((None, 8, tn), lambda l, j: (l, 0, j)),
        out_shape=jax.ShapeDtypeStruct((n_layers, 8, n6), F32),
        compiler_params=_cp(("parallel", "parallel")),
        name="modvec",
    )(cvec, mod_w, mod_b.reshape(n_layers, 1, n6))


class _Rows:
    def __init__(self, batch, seq, ctx_len, tm):
        assert seq % tm == 0 and (batch * ctx_len) % tm == 0
        self.tm = tm
        self.batch = batch
        self.per_batch = seq // tm
        self.n_lat = batch * seq // tm
        self.n_ctx = batch * ctx_len // tm
        self.n_all = self.n_lat + self.n_ctx

    def mod_row(self, i):
        return jnp.where(i < self.n_lat, i // self.per_batch, self.batch)


def _mod_spec(rows, layer, chunk, d):
    return pl.BlockSpec((None, None, None, 1, d), lambda i, *_: (layer, rows.mod_row(i), chunk, 0, 0))


def _norm_mod_kernel(xl_ref, xc_ref, g_ref, sh_ref, sc_ref, o_ref, *, n_lat):
    i = pl.program_id(0)

    def body(x_ref):
        y = _rms(x_ref[...], g_ref[...])
        o_ref[...] = (y * (1.0 + sc_ref[...]) + sh_ref[...]).astype(o_ref.dtype)

    @pl.when(i < n_lat)
    def _():
        body(xl_ref)

    @pl.when(i >= n_lat)
    def _():
        body(xc_ref)


def _norm_mod(x_lat, x_ctx, ctx_block0, g, mods5, layer, rows):
    d = x_lat.shape[-1]
    tm = rows.tm
    nl = rows.n_lat
    return pl.pallas_call(
        functools.partial(_norm_mod_kernel, n_lat=nl),
        grid=(rows.n_all,),
        in_specs=[pl.BlockSpec((tm, d), lambda i: (jnp.minimum(i, nl - 1), 0)),
                  pl.BlockSpec((tm, d), lambda i: (ctx_block0 + jnp.maximum(i - nl, 0), 0)),
                  pl.BlockSpec((1, d), lambda i: (0, 0)),
                  _mod_spec(rows, layer, 0, d),
                  _mod_spec(rows, layer, 1, d)],
        out_specs=pl.BlockSpec((tm, d), lambda i: (i, 0)),
        out_shape=jax.ShapeDtypeStruct((rows.n_all * tm, d), BF16),
        compiler_params=_cp(("parallel",)),
        name="norm_mod",
    )(x_lat, x_ctx, g.reshape(1, d), mods5, mods5)


def _mm_kernel(a_ref, wt_ref, o_ref):
    o_ref[...] = _dot_nt(a_ref[...], wt_ref[...].astype(BF16)).astype(o_ref.dtype)


def _matmul(a, wt3, layer, n_cols, tn, tm):
    m, k = a.shape
    return pl.pallas_call(
        _mm_kernel,
        grid=(m // tm, n_cols // tn),
        in_specs=[pl.BlockSpec((tm, k), lambda i, j: (i, 0)),
                  pl.BlockSpec((None, tn, k), lambda i, j: (layer, j, 0))],
        out_specs=pl.BlockSpec((tm, tn), lambda i, j: (i, j)),
        out_shape=jax.ShapeDtypeStruct((m, n_cols), BF16),
        compiler_params=_cp(("parallel", "arbitrary")),
        name="in_proj",
    )(a, wt3)


def _mm_tail_kernel(a_ref, wt_ref, o_ref):
    n = o_ref.shape[-1]
    acc = _dot_nt(a_ref[...], wt_ref[...].astype(BF16))
    o_ref[...] = acc[:, :n].astype(o_ref.dtype)


def _matmul_tail(a, wt3, layer, col0, n_cols, tm):
    m, k = a.shape
    lane = 128
    assert col0 % lane == 0 and n_cols <= lane and col0 + n_cols == wt3.shape[1]
    return pl.pallas_call(
        _mm_tail_kernel,
        grid=(m // tm,),
        in_specs=[pl.BlockSpec((tm, k), lambda i: (i, 0)),
                  pl.BlockSpec((None, lane, k), lambda i: (layer, col0 // lane, 0))],
        out_specs=pl.BlockSpec((tm, n_cols), lambda i: (i, 0)),
        out_shape=jax.ShapeDtypeStruct((m, n_cols), BF16),
        compiler_params=_cp(("parallel",)),
        name="in_proj_tail",
    )(a, wt3)


def _out_proj_kernel(ma_ref, mb_ref, w_hbm, xl_ref, xc_ref, g_ref, o_ref, w_s, stage, *, n_lat, widx):
    i = pl.program_id(0)
    ka = ma_ref.shape[-1]

    @pl.when(i == 0)
    def _():
        rows = stage.shape[0]
        for c in range(w_s.shape[0] // rows):
            pltpu.sync_copy(w_hbm.at[widx, pl.ds(c * rows, rows)], stage)
            w_s[c * rows:(c + 1) * rows, :] = stage[...].astype(BF16)

    acc = jnp.dot(ma_ref[...], w_s[:ka, :], preferred_element_type=F32)
    acc += jnp.dot(mb_ref[...], w_s[ka:, :], preferred_element_type=F32)
    upd = g_ref[...] * acc

    @pl.when(i < n_lat)
    def _():
        o_ref[...] = xl_ref[...] + upd

    @pl.when(i >= n_lat)
    def _():
        o_ref[...] = xc_ref[...] + upd


def _out_proj(mix_a, mix_b, w_out, widx, x_lat, x_ctx, ctx_block0, mods5, layer, rows, n_tiles):
    d = x_lat.shape[-1]
    ka, kb = mix_a.shape[-1], mix_b.shape[-1]
    tm = rows.tm
    nl = rows.n_lat
    stage_rows = min(512, ka + kb)
    return pl.pallas_call(
        functools.partial(_out_proj_kernel, n_lat=nl, widx=widx),
        grid=(n_tiles,),
        in_specs=[pl.BlockSpec((tm, ka), lambda i: (i, 0)),
                  pl.BlockSpec((tm, kb), lambda i: (i, 0)),
                  pl.BlockSpec(memory_space=pl.ANY),
                  pl.BlockSpec((tm, d), lambda i: (jnp.minimum(i, nl - 1), 0)),
                  pl.BlockSpec((tm, d), lambda i: (ctx_block0 + jnp.maximum(i - nl, 0), 0)),
                  _mod_spec(rows, layer, 2, d)],
        out_specs=pl.BlockSpec((tm, d), lambda i: (i, 0)),
        out_shape=jax.ShapeDtypeStruct((n_tiles * tm, d), F32),
        scratch_shapes=[pltpu.VMEM((ka + kb, d), BF16), pltpu.VMEM((stage_rows, d), F32)],
        compiler_params=_cp(("arbitrary",)),
        name="out_proj",
    )(mix_a, mix_b, w_out, x_lat, x_ctx, mods5)


def _tri(c, upper):
    r = lax.broadcasted_iota(jnp.int32, (c, c), 0)
    s = lax.broadcasted_iota(jnp.int32, (c, c), 1)
    return (s >= r) if upper else (r >= s)


def _scan_block(q, k, v, g, st, mask, forward):
    c = SCAN_C
    dk, dv = q.shape[-1], v.shape[-1]
    n = q.shape[0] // c
    mid, last = (c // 2 - 1, c - 1) if forward else (c // 2, 0)
    tri = jnp.broadcast_to(mask.astype(BF16)[None], (n, c, c))
    g3 = g.reshape(n, c, dk)
    g_hi = g3.astype(BF16)
    g_lo = (g3 - g_hi.astype(F32)).astype(BF16)
    cum = (jnp.einsum('cts,csd->ctd', tri, g_hi, preferred_element_type=F32)
           + jnp.einsum('cts,csd->ctd', tri, g_lo, preferred_element_type=F32))
    m = cum[:, mid:mid + 1, :]
    tot = cum[:, last:last + 1, :]
    qe = (q.reshape(n, c, dk) * jnp.exp(cum - m)).astype(BF16)
    ke = (k.reshape(n, c, dk) * jnp.exp(m - cum)).astype(BF16)
    a = jnp.einsum('ctd,csd->cts', qe, ke, preferred_element_type=F32)
    a = jnp.where(mask[None], a, 0.0).astype(BF16)
    v3 = v.reshape(n, c, dv)
    o = jnp.einsum('cts,csv->ctv', a, v3, preferred_element_type=F32)
    u = jnp.einsum('csv,csd->cvd', v3, ke, preferred_element_type=F32)
    em = jnp.exp(m)
    et = jnp.exp(tot - m)
    states = [None] * n
    for ci in (range(n) if forward else reversed(range(n))):
        stp = st * em[ci]
        states[ci] = stp.astype(BF16)
        st = (stp + u[ci]) * et[ci]
    o = o + jnp.einsum('ctd,cvd->ctv', qe, jnp.stack(states), preferred_element_type=F32)
    return o.reshape(n * c, dv), st


def _scan_segments(segments, prep_f, prep_b, of_ref, ob_ref, dk, dv):
    low, up = _tri(SCAN_C, False), _tri(SCAN_C, True)
    carry = (jnp.zeros((dv, dk), F32), jnp.zeros((dv, dk), F32))
    for rows, off, seg in segments:
        rb_ = min(SCAN_BLOCK, rows)
        n = rows // rb_

        def body(i, carry, n=n, off=off, seg=seg, rb_=rb_):
            sf, sb = carry
            rf = pl.multiple_of(i * rb_, rb_)
            rb = pl.multiple_of((n - 1 - i) * rb_, rb_)
            q, k, v, g = prep_f(seg, rf, rb_)
            o, sf = _scan_block(q, k, v, g, sf, low, True)
            of_ref[pl.ds(off + rf, rb_), :] = o
            q, k, v, g = prep_b(seg, rb, rb_)
            o, sb = _scan_block(q, k, v, g, sb, up, False)
            ob_ref[pl.ds(off + rb, rb_), :] = o
            return sf, sb

        carry = lax.fori_loop(0, n, body, carry)


def _hgrn_kernel(ql, qc, f1l, f1c, f2l, f2c, vl, vc, gl, gc, lbf, lbb, ng, o_ref, of_s, ob_s, *, ctx_len, seq):
    rt = pl.program_id(2)
    n_ctx_tiles = ctx_len // TQ

    @pl.when(rt == 0)
    def _scan():
        refs = {0: (qc, f1c, f2c, vc), 1: (ql, f1l, f2l, vl)}
        scale = A_DK ** -0.5

        for hh in range(SCAN_HEADS):
            ks = slice(hh * A_DK, (hh + 1) * A_DK)
            vs = slice(hh * A_DV, (hh + 1) * A_DV)

            def prep(seg, r, nr, fi, lb_ref, ks=ks, vs=vs):
                x = refs[seg][0][pl.ds(r, nr), ks].astype(F32)
                q = x * jax.nn.sigmoid(x) * scale
                v = refs[seg][3][pl.ds(r, nr), vs]
                lb = lb_ref[:, ks]
                f = lb + (1.0 - lb) * jax.nn.sigmoid(refs[seg][fi][pl.ds(r, nr), ks].astype(F32))
                return q, 1.0 - f, v, jnp.log(f)

            _scan_segments(
                [(ctx_len, 0, 0), (seq, ctx_len, 1)],
                lambda seg, r, nr, prep=prep: prep(seg, r, nr, 1, lbf),
                lambda seg, r, nr, prep=prep: prep(seg, r, nr, 2, lbb),
                of_s.at[hh], ob_s.at[hh], A_DK, A_DV)

    r0 = pl.multiple_of(rt * TQ, TQ)
    ys = [_rms(of_s[hh, pl.ds(r0, TQ), :] + ob_s[hh, pl.ds(r0, TQ), :], ng[...]) for hh in range(SCAN_HEADS)]
    y = jnp.concatenate(ys, axis=-1)

    @pl.when(rt < n_ctx_tiles)
    def _():
        gate = gc[pl.ds(r0, TQ), :].astype(F32)
        o_ref[...] = (y * jax.nn.sigmoid(gate)).astype(o_ref.dtype)

    @pl.when(rt >= n_ctx_tiles)
    def _():
        gate = gl[pl.ds(pl.multiple_of(r0 - ctx_len, TQ), TQ), :].astype(F32)
        o_ref[...] = (y * jax.nn.sigmoid(gate)).astype(o_ref.dtype)


def _out_row_block(batch, seq, ctx_len):
    nct = ctx_len // TQ
    nlt = seq // TQ

    def f(b, rt):
        return jnp.where(rt < nct, batch * nlt + b * nct + rt, b * nlt + rt - nct)

    return f


def _hgrn(p, lb_f, lb_b, norm_g, batch, seq, ctx_len):
    h, dk, dv = A_HEADS, A_DK, A_DV
    hp = SCAN_HEADS
    hg = h // hp
    nct, nlt = ctx_len // TQ, seq // TQ
    ctx_blk0 = batch * seq // ctx_len
    row_block = _out_row_block(batch, seq, ctx_len)
    in_specs = []
    for kcol in range(5):
        in_specs.append(pl.BlockSpec((seq, hp * dk), lambda b, hh, rt, kcol=kcol: (b, kcol * hg + hh)))
        in_specs.append(pl.BlockSpec((ctx_len, hp * dk), lambda b, hh, rt, kcol=kcol: (ctx_blk0 + b, kcol * hg + hh)))
    vec = pl.BlockSpec((1, hp * dk), lambda b, hh, rt: (0, hh))
    in_specs += [vec, vec, pl.BlockSpec((1, dv), lambda b, hh, rt: (0, 0))]
    return pl.pallas_call(
        functools.partial(_hgrn_kernel, ctx_len=ctx_len, seq=seq),
        grid=(batch, hg, nct + nlt),
        in_specs=in_specs,
        out_specs=pl.BlockSpec((TQ, hp * dv), lambda b, hh, rt: (row_block(b, rt), hh)),
        out_shape=jax.ShapeDtypeStruct((batch * (seq + ctx_len), h * dv), BF16),
        scratch_shapes=[pltpu.VMEM((hp, seq + ctx_len, dv), F32), pltpu.VMEM((hp, seq + ctx_len, dv), F32)],
        compiler_params=_cp(("parallel", "parallel", "arbitrary")),
        name="hgrn_scan",
    )(*([p] * 10), lb_f.reshape(1, h * dk), lb_b.reshape(1, h * dk), norm_g.reshape(1, dv))


def _gla_kernel(ql, qc, kl, kc, vl, vc, gl, al, ac, wa, ba, ng, o_ref, of_s, ob_s, *, ctx_len, seq):
    rt = pl.program_id(2)

    @pl.when(rt == 0)
    def _scan():
        refs = {0: (qc, kc, vc, ac), 1: (ql, kl, vl, al)}
        scale = D_DK ** -0.5
        r16 = D_GATE_RANK

        def prep(seg, r, nr, d):
            q = refs[seg][0][pl.ds(r, nr), :].astype(F32) * scale
            k = refs[seg][1][pl.ds(r, nr), :].astype(F32)
            v = refs[seg][2][pl.ds(r, nr), :]
            a = refs[seg][3][pl.ds(r, nr), :].astype(F32)[:, d * r16:(d + 1) * r16]
            z = jnp.dot(a, wa[d], preferred_element_type=F32, precision=HI) + ba[d]
            g = (jnp.minimum(z, 0.0) - jnp.log(1.0 + jnp.exp(-jnp.abs(z)))) * (1.0 / GLA_TAU)
            return q, k, v, g

        _scan_segments(
            [(ctx_len, 0, 0), (seq, ctx_len, 1)],
            lambda seg, r, nr: prep(seg, r, nr, 0),
            lambda seg, r, nr: prep(seg, r, nr, 1),
            of_s, ob_s, D_DK, D_DV)

    r0 = pl.multiple_of(rt * TQ, TQ)
    o = of_s[pl.ds(ctx_len + r0, TQ), :] + ob_s[pl.ds(ctx_len + r0, TQ), :]
    gate = gl[pl.ds(r0, TQ), :].astype(F32)
    o_ref[...] = (_rms(o, ng[...]) * gate * jax.nn.sigmoid(gate)).astype(o_ref.dtype)


def _gla(p, ga, w_a2, b_a, norm_g, batch, seq, ctx_len):
    h, dk, dv = D_HEADS, D_DK, D_DV
    nlt = seq // TQ
    ctx_blk0 = batch * seq // ctx_len
    q0 = (C_HEADS + 2 * C_KV_HEADS) * C_DH // dk
    k0 = q0 + h
    v0 = (k0 + h) * dk // dv
    g0 = v0 + h

    def pair(width, blk0):
        return [pl.BlockSpec((seq, width), lambda b, hh, rt: (b, blk0 + hh)),
                pl.BlockSpec((ctx_len, width), lambda b, hh, rt: (ctx_blk0 + b, blk0 + hh))]

    in_specs = pair(dk, q0) + pair(dk, k0) + pair(dv, v0)
    in_specs += [pl.BlockSpec((seq, dv), lambda b, hh, rt: (b, g0 + hh)),
                 pl.BlockSpec((seq, 2 * D_GATE_RANK), lambda b, hh, rt: (b, 0)),
                 pl.BlockSpec((ctx_len, 2 * D_GATE_RANK), lambda b, hh, rt: (ctx_blk0 + b, 0)),
                 pl.BlockSpec((2, D_GATE_RANK, dk), lambda b, hh, rt: (0, 0, hh)),
                 pl.BlockSpec((2, 1, dk), lambda b, hh, rt: (0, 0, hh)),
                 pl.BlockSpec((1, dv), lambda b, hh, rt: (0, 0))]
    return pl.pallas_call(
        functools.partial(_gla_kernel, ctx_len=ctx_len, seq=seq),
        grid=(batch, h, nlt),
        in_specs=in_specs,
        out_specs=pl.BlockSpec((TQ, dv), lambda b, hh, rt: (b * nlt + rt, hh)),
        out_shape=jax.ShapeDtypeStruct((batch * seq, h * dv), BF16),
        scratch_shapes=[pltpu.VMEM((seq + ctx_len, dv), F32), pltpu.VMEM((seq + ctx_len, dv), F32)],
        compiler_params=_cp(("parallel", "parallel", "arbitrary")),
        name="gla_scan",
    )(p, p, p, p, p, p, p, ga, ga, w_a2, b_a.reshape(2, 1, h * dk), norm_g.reshape(1, dv))


def _softmax_pv(s, v):
    m = jnp.max(s, axis=-1, keepdims=True)
    p = jnp.exp(s - m)
    l = jnp.sum(p, axis=-1, keepdims=True)
    return jnp.dot(p.astype(BF16), v, preferred_element_type=F32) / l


def _mla_kernel(ql_ref, kvl_ref, kvc_ref, krl_ref, krc_ref, wqn_ref, wqr_ref, wkv_ref, gq_ref, gkv_ref,
                cosq_ref, sinq_ref, cosk_ref, sink_ref, o_ref, kn_s, kr_s, v_s, *, ctx_len):
    qt = pl.program_id(1)
    n_ctx_tiles = ctx_len // TQ
    scale = (B_NOPE + B_ROPE) ** -0.5
    dkv = B_NOPE + B_DV

    @pl.when(qt == 0)
    def _prep():
        kvc = _rms(kvc_ref[...].astype(F32), gkv_ref[...]).astype(BF16)
        kvl = _rms(kvl_ref[...].astype(F32), gkv_ref[...]).astype(BF16)
        for h in range(B_HEADS):
            w = wkv_ref[:, h * dkv:(h + 1) * dkv].astype(BF16)
            up_c = jnp.dot(kvc, w, preferred_element_type=F32)
            kn_s[h, 0:ctx_len, :] = up_c[:, :B_NOPE].astype(BF16)
            v_s[h, 0:ctx_len, :] = up_c[:, B_NOPE:].astype(BF16)
            up_l = jnp.dot(kvl, w, preferred_element_type=F32)
            kn_s[h, ctx_len:, :] = up_l[:, :B_NOPE].astype(BF16)
            v_s[h, ctx_len:, :] = up_l[:, B_NOPE:].astype(BF16)
        kr_s[0:ctx_len, :] = krc_ref[...]
        kr_s[ctx_len:, :] = _rope(krl_ref[...].astype(F32), cosk_ref[...], sink_ref[...]).astype(BF16)

    xn = _rms(ql_ref[...].astype(F32), gq_ref[...]).astype(BF16)
    qn_all = jnp.dot(xn, wqn_ref[...].astype(BF16), preferred_element_type=F32) * scale
    qr_all = jnp.dot(xn, wqr_ref[...].astype(BF16), preferred_element_type=F32) * scale

    def heads(n_keys, rotate):
        outs = []
        for h in range(B_HEADS):
            qn = qn_all[:, h * B_NOPE:(h + 1) * B_NOPE].astype(BF16)
            qr = qr_all[:, h * B_ROPE:(h + 1) * B_ROPE]
            if rotate:
                qr = _rope(qr, cosq_ref[...], sinq_ref[...])
            s = _dot_nt(qn, kn_s[h, 0:n_keys, :]) + _dot_nt(qr.astype(BF16), kr_s[0:n_keys, :])
            outs.append(_softmax_pv(s, v_s[h, 0:n_keys, :]).astype(o_ref.dtype))
        o_ref[...] = jnp.concatenate(outs, axis=-1)

    @pl.when(qt < n_ctx_tiles)
    def _():
        heads(ctx_len, False)

    @pl.when(qt >= n_ctx_tiles)
    def _():
        heads(kr_s.shape[0], True)


def _mla(p, kr, w_uq, w_ukv, gq, gkv, cos, sin, batch, seq, ctx_len):
    h = B_HEADS
    nct, nlt = ctx_len // TQ, seq // TQ
    ctx_blk0 = batch * seq // ctx_len
    row_block = _out_row_block(batch, seq, ctx_len)
    ql_blk = 5 * A_HEADS * A_DK // B_Q_LORA
    kv_blk = (5 * A_HEADS * A_DK + B_Q_LORA) // B_KV_LORA
    dq = B_NOPE + B_ROPE
    s_all = seq + ctx_len
    w3 = w_uq.reshape(B_Q_LORA, h, dq)
    wq_n = w3[:, :, :B_NOPE].reshape(B_Q_LORA, h * B_NOPE)
    wq_r = w3[:, :, B_NOPE:].reshape(B_Q_LORA, h * B_ROPE)
    in_specs = [
        pl.BlockSpec((TQ, B_Q_LORA), lambda b, qt: (row_block(b, qt), ql_blk)),
        pl.BlockSpec((seq, B_KV_LORA), lambda b, qt: (b, kv_blk)),
        pl.BlockSpec((ctx_len, B_KV_LORA), lambda b, qt: (ctx_blk0 + b, kv_blk)),
        pl.BlockSpec((seq, B_ROPE), lambda b, qt: (b, 0)),
        pl.BlockSpec((ctx_len, B_ROPE), lambda b, qt: (ctx_blk0 + b, 0)),
        pl.BlockSpec((B_Q_LORA, h * B_NOPE), lambda b, qt: (0, 0)),
        pl.BlockSpec((B_Q_LORA, h * B_ROPE), lambda b, qt: (0, 0)),
        pl.BlockSpec((B_KV_LORA, h * (B_NOPE + B_DV)), lambda b, qt: (0, 0)),
        pl.BlockSpec((1, B_Q_LORA), lambda b, qt: (0, 0)),
        pl.BlockSpec((1, B_KV_LORA), lambda b, qt: (0, 0)),
        pl.BlockSpec((TQ, B_ROPE), lambda b, qt: (jnp.maximum(qt - nct, 0), 0)),
        pl.BlockSpec((TQ, B_ROPE), lambda b, qt: (jnp.maximum(qt - nct, 0), 0)),
        pl.BlockSpec((seq, B_ROPE), lambda b, qt: (0, 0)),
        pl.BlockSpec((seq, B_ROPE), lambda b, qt: (0, 0)),
    ]
    return pl.pallas_call(
        functools.partial(_mla_kernel, ctx_len=ctx_len),
        grid=(batch, nct + nlt),
        in_specs=in_specs,
        out_specs=pl.BlockSpec((TQ, h * B_DV), lambda b, qt: (row_block(b, qt), 0)),
        out_shape=jax.ShapeDtypeStruct((batch * s_all, h * B_DV), BF16),
        scratch_shapes=[pltpu.VMEM((h, s_all, B_NOPE), BF16), pltpu.VMEM((s_all, B_ROPE), BF16),
                        pltpu.VMEM((h, s_all, B_DV), BF16)],
        compiler_params=_cp(("parallel", "arbitrary")),
        name="mla_attn",
    )(p, p, p, kr, kr, wq_n, wq_r, w_ukv, gq.reshape(1, -1), gkv.reshape(1, -1), cos, sin, cos, sin)


def _gqa_kernel(q_ref, kl_ref, kc_ref, vl_ref, vc_ref, gq_ref, gk_ref, cosq_ref, sinq_ref, cosk_ref, sink_ref,
                o_ref, k_s, v_s, *, ctx_len):
    qt = pl.program_id(2)
    scale = C_DH ** -0.5
    dh = C_DH

    @pl.when(qt == 0)
    def _prep():
        k_s[0:ctx_len, :] = _rms(kc_ref[...].astype(F32), gk_ref[...]).astype(BF16)
        kl = _rms(kl_ref[...].astype(F32), gk_ref[...])
        k_s[ctx_len:, :] = _rope(kl, cosk_ref[...], sink_ref[...]).astype(BF16)
        v_s[0:ctx_len, :] = vc_ref[...]
        v_s[ctx_len:, :] = vl_ref[...]

    outs = []
    for g in range(C_HEADS // C_KV_HEADS):
        q = _rms(q_ref[:, g * dh:(g + 1) * dh].astype(F32), gq_ref[...])
        q = _rope(q, cosq_ref[...], sinq_ref[...]) * scale
        s = _dot_nt(q.astype(BF16), k_s[...])
        outs.append(_softmax_pv(s, v_s[...]).astype(o_ref.dtype))
    o_ref[...] = jnp.concatenate(outs, axis=-1)


def _gqa(p, gq, gk, cos, sin, batch, seq, ctx_len):
    kvh, grp, dh = C_KV_HEADS, C_HEADS // C_KV_HEADS, C_DH
    nlt = seq // TQ
    ctx_blk0 = batch * seq // ctx_len
    k0 = C_HEADS
    v0 = C_HEADS + C_KV_HEADS
    s_all = seq + ctx_len
    in_specs = [
        pl.BlockSpec((TQ, grp * dh), lambda b, kh, qt: (b * nlt + qt, kh)),
        pl.BlockSpec((seq, dh), lambda b, kh, qt: (b, k0 + kh)),
        pl.BlockSpec((ctx_len, dh), lambda b, kh, qt: (ctx_blk0 + b, k0 + kh)),
        pl.BlockSpec((seq, dh), lambda b, kh, qt: (b, v0 + kh)),
        pl.BlockSpec((ctx_len, dh), lambda b, kh, qt: (ctx_blk0 + b, v0 + kh)),
        pl.BlockSpec((1, dh), lambda b, kh, qt: (0, 0)),
        pl.BlockSpec((1, dh), lambda b, kh, qt: (0, 0)),
        pl.BlockSpec((TQ, dh), lambda b, kh, qt: (qt, 0)),
        pl.BlockSpec((TQ, dh), lambda b, kh, qt: (qt, 0)),
        pl.BlockSpec((seq, dh), lambda b, kh, qt: (0, 0)),
        pl.BlockSpec((seq, dh), lambda b, kh, qt: (0, 0)),
    ]
    return pl.pallas_call(
        functools.partial(_gqa_kernel, ctx_len=ctx_len),
        grid=(batch, kvh, nlt),
        in_specs=in_specs,
        out_specs=pl.BlockSpec((TQ, grp * dh), lambda b, kh, qt: (b * nlt + qt, kh)),
        out_shape=jax.ShapeDtypeStruct((batch * seq, C_HEADS * dh), BF16),
        scratch_shapes=[pltpu.VMEM((s_all, dh), BF16), pltpu.VMEM((s_all, dh), BF16)],
        compiler_params=_cp(("parallel", "parallel", "arbitrary")),
        name="gqa_attn",
    )(p, p, p, p, p, gq.reshape(1, dh), gk.reshape(1, dh), cos, sin, cos, sin)


def _router_kernel(x_ref, g_ref, sh_ref, sc_ref, rw_ref, rb_ref, h_ref, ri_ref, rf_ref, cnt_ref, base_s):
    i = pl.program_id(0)
    tm = x_ref.shape[0]
    ne = N_EXPERTS
    per = ne // N_GROUPS

    @pl.when(i == 0)
    def _():
        base_s[...] = jnp.zeros_like(base_s)

    h = _rms(x_ref[...], g_ref[...]) * (1.0 + sc_ref[...]) + sh_ref[...]
    nch = h.shape[1] // LANES
    for j in range(nch):
        h_ref[pl.ds(j, tm, stride=nch), :] = h[:, j * LANES:(j + 1) * LANES]
    h_hi = h.astype(BF16)
    h_lo = (h - h_hi.astype(F32)).astype(BF16)
    rw = rw_ref[...]
    w_hi = rw.astype(BF16)
    w_lo = (rw - w_hi.astype(F32)).astype(BF16)
    hw = jnp.dot(h_hi, jnp.concatenate([w_hi, w_lo], axis=1), preferred_element_type=F32)
    logits = hw[:, :ne] + hw[:, ne:] + jnp.dot(h_lo, w_hi, preferred_element_type=F32)
    scores = jax.nn.sigmoid(logits)
    sel = scores + rb_ref[...]
    lane = lax.broadcasted_iota(jnp.int32, (tm, ne), 1).astype(F32)
    neg = -jnp.inf
    big = float(ne)

    def top2(vals):
        m1 = jnp.max(vals, axis=1, keepdims=True)
        i1 = jnp.min(jnp.where(vals == m1, lane, big), axis=1, keepdims=True)
        rest = jnp.where(lane == i1, neg, vals)
        m2 = jnp.max(rest, axis=1, keepdims=True)
        i2 = jnp.min(jnp.where(rest == m2, lane, big), axis=1, keepdims=True)
        return m1 + m2, i1, i2

    best, e1, e2 = None, None, None
    for grp in range(N_GROUPS):
        in_grp = jnp.logical_and(lane >= float(grp * per), lane < float((grp + 1) * per))
        gsum, i1, i2 = top2(jnp.where(in_grp, sel, neg))
        if grp == 0:
            best, e1, e2 = gsum, i1, i2
        else:
            better = gsum > best
            best = jnp.where(better, gsum, best)
            e1 = jnp.where(better, i1, e1)
            e2 = jnp.where(better, i2, e2)

    hot1 = lane == e1
    hot2 = lane == e2
    w1 = jnp.sum(jnp.where(hot1, scores, 0.0), axis=1, keepdims=True)
    w2 = jnp.sum(jnp.where(hot2, scores, 0.0), axis=1, keepdims=True)
    wsum = w1 + w2
    assign = jnp.logical_or(hot1, hot2)
    r = lax.broadcasted_iota(jnp.int32, (tm, tm), 0)
    c = lax.broadcasted_iota(jnp.int32, (tm, tm), 1)
    before = (c < r).astype(BF16)
    excl = jnp.dot(before, assign.astype(BF16), preferred_element_type=F32) + base_s[...]
    rank1 = jnp.sum(jnp.where(hot1, excl, 0.0), axis=1, keepdims=True)
    rank2 = jnp.sum(jnp.where(hot2, excl, 0.0), axis=1, keepdims=True)
    base_s[...] = base_s[...] + jnp.sum(assign.astype(F32), axis=0, keepdims=True)

    l128 = lax.broadcasted_iota(jnp.int32, (tm, 128), 1)
    ri = jnp.where(l128 == 0, e1, jnp.where(l128 == 1, e2, jnp.where(l128 == 2, rank1, jnp.where(l128 == 3, rank2, 0.0))))
    ri_ref[...] = ri.T[0:8, :].astype(jnp.int32)
    rf_ref[...] = jnp.where(l128 == 0, w1 / wsum, jnp.where(l128 == 1, w2 / wsum, 0.0))
    cnt_ref[...] = jnp.broadcast_to(base_s[...], cnt_ref.shape)


def _router(x_all, g, mods5, layer, router_w, router_b, rows, n_tiles):
    d = x_all.shape[-1]
    tm = rows.tm
    n = n_tiles * tm
    ne = N_EXPERTS
    return pl.pallas_call(
        _router_kernel,
        grid=(n_tiles,),
        in_specs=[pl.BlockSpec((tm, d), lambda i: (i, 0)),
                  pl.BlockSpec((1, d), lambda i: (0, 0)),
                  _mod_spec(rows, layer, 3, d),
                  _mod_spec(rows, layer, 4, d),
                  pl.BlockSpec((d, ne), lambda i: (0, 0)),
                  pl.BlockSpec((1, ne), lambda i: (0, 0))],
        out_specs=[pl.BlockSpec((tm * (d // LANES), LANES), lambda i: (i, 0)),
                   pl.BlockSpec((8, tm), lambda i: (0, i)),
                   pl.BlockSpec((tm, 128), lambda i: (i, 0)),
                   pl.BlockSpec((8, ne), lambda i: (0, 0))],
        out_shape=[jax.ShapeDtypeStruct((n * (d // LANES), LANES), F32),
                   jax.ShapeDtypeStruct((8, n), jnp.int32),
                   jax.ShapeDtypeStruct((n, 128), F32),
                   jax.ShapeDtypeStruct((8, ne), F32)],
        scratch_shapes=[pltpu.VMEM((1, ne), F32)],
        compiler_params=_cp(("arbitrary",)),
        name="moe_router",
    )(x_all, g.reshape(1, d), mods5, mods5, router_w, router_b.reshape(1, ne))


def _expert_kernel(te_ref, nx_ref, nu_ref, pad_ref, dest_ref, h_hbm, w1_hbm, w3_hbm, w2_hbm, y_ref, xbuf, sem,
                   w1_s, w3_s, w2_s, wf1, wf3, wf2, wsem, wslot_ref, src_ref, *, n_tok, layer):
    r = pl.program_id(0)
    n_used = nu_ref[0]
    active = r < n_used
    changed = jnp.logical_or(r == 0, te_ref[r] != te_ref[jnp.maximum(r - 1, 0)])
    slot = lax.rem(r, 2)
    nch = xbuf.shape[1] // TE

    def weight_copies(e, ws):
        return (pltpu.make_async_copy(w1_hbm.at[layer, e], wf1.at[ws], wsem.at[ws]),
                pltpu.make_async_copy(w3_hbm.at[layer, e], wf3.at[ws], wsem.at[ws]),
                pltpu.make_async_copy(w2_hbm.at[layer, e], wf2.at[ws], wsem.at[ws]))

    @pl.when(r == 0)
    def _():
        wslot_ref[0] = 0
        for cp in weight_copies(te_ref[0], 0):
            cp.start()

        def clear(i, carry):
            src_ref[i] = 0
            return carry

        for e in range(N_EXPERTS):
            lax.fori_loop(pad_ref[e], pad_ref[N_EXPERTS + e], clear, 0)

        def invert(i, carry):
            src_ref[dest_ref[i]] = i
            src_ref[dest_ref[n_tok + i]] = i
            return carry

        lax.fori_loop(0, n_tok, invert, 0, unroll=DMA_UNROLL)

    def gather(tile, dst_slot):
        base = tile * TE
        for t in range(TE):
            row0 = pl.multiple_of(src_ref[base + t] * nch, nch)
            pltpu.make_async_copy(h_hbm.at[pl.ds(row0, nch)], xbuf.at[dst_slot, pl.ds(t * nch, nch)],
                                  sem.at[dst_slot]).start(priority=0)

    @pl.when(jnp.logical_and(r == 0, active))
    def _():
        gather(0, 0)

    @pl.when(r + 1 < n_used)
    def _():
        gather(r + 1, 1 - slot)

    @pl.when(jnp.logical_and(active, changed))
    def _():
        ws = wslot_ref[0]
        for cp in weight_copies(te_ref[r], ws):
            cp.wait()

        @pl.when(nx_ref[r] != te_ref[r])
        def _():
            for cp in weight_copies(nx_ref[r], 1 - ws):
                cp.start(priority=1)

        w1_s[...] = wf1[ws].astype(BF16)
        w3_s[...] = wf3[ws].astype(BF16)
        w2_s[...] = wf2[ws].astype(BF16)
        wslot_ref[0] = 1 - ws

    @pl.when(active)
    def _():
        pltpu.make_async_copy(h_hbm.at[pl.ds(0, TE * nch)], xbuf.at[slot], sem.at[slot]).wait()
        x = jnp.concatenate([xbuf[slot, pl.ds(j, TE, stride=nch), :] for j in range(nch)], axis=1).astype(BF16)
        a = jnp.dot(x, w1_s[...], preferred_element_type=F32)
        b = jnp.dot(x, w3_s[...], preferred_element_type=F32)
        hid = (a * jax.nn.sigmoid(a) * b).astype(BF16)
        y_ref[...] = jnp.dot(hid, w2_s[...], preferred_element_type=F32)

    @pl.when(jnp.logical_not(active))
    def _():
        y_ref[...] = jnp.zeros_like(y_ref)


def _experts(tile_expert, next_expert, n_used, pad, dest, hp, w1, w3, w2, layer, p_max):
    d = w1.shape[-2]
    nch = d // LANES
    n_tok = hp.shape[0] // nch
    f = w1.shape[-1]
    any_spec = pl.BlockSpec(memory_space=pl.ANY)
    grid_spec = pltpu.PrefetchScalarGridSpec(
        num_scalar_prefetch=5,
        grid=(p_max // TE,),
        in_specs=[any_spec, any_spec, any_spec, any_spec],
        out_specs=pl.BlockSpec((TE, d), lambda r, te, nx, nu, pd, sr: (r, 0)),
        scratch_shapes=[pltpu.VMEM((2, TE * nch, LANES), F32), pltpu.SemaphoreType.DMA((2,)),
                        pltpu.VMEM((d, f), BF16), pltpu.VMEM((d, f), BF16), pltpu.VMEM((f, d), BF16),
                        pltpu.VMEM((2, d, f), F32), pltpu.VMEM((2, d, f), F32), pltpu.VMEM((2, f, d), F32),
                        pltpu.SemaphoreType.DMA((2,)), pltpu.SMEM((1,), jnp.int32),
                        pltpu.SMEM((p_max,), jnp.int32)],
    )
    return pl.pallas_call(
        functools.partial(_expert_kernel, n_tok=n_tok, layer=layer),
        grid_spec=grid_spec,
        out_shape=jax.ShapeDtypeStruct((p_max, d), F32),
        compiler_params=_cp(("arbitrary",)),
        name="moe_experts",
    )(tile_expert, next_expert, n_used, pad, dest, hp, w1, w3, w2)


def _combine_kernel(dest_ref, x_ref, rf_ref, g_ref, pg_ref, psh_ref, psc_ref, ys_hbm, *rest, tm, n_tok, n_tiles, final):
    o_ref = rest[0]
    buf, sem = rest[-2], rest[-1]
    i = pl.program_id(0)
    slot = lax.rem(i, 2)

    def gather(tile, dst_slot):
        base = tile * tm
        for t in range(tm):
            pltpu.make_async_copy(ys_hbm.at[pl.ds(dest_ref[base + t], 1)], buf.at[dst_slot, 0, pl.ds(t, 1)],
                                  sem.at[dst_slot]).start(priority=0)
            pltpu.make_async_copy(ys_hbm.at[pl.ds(dest_ref[n_tok + base + t], 1)], buf.at[dst_slot, 1, pl.ds(t, 1)],
                                  sem.at[dst_slot]).start(priority=1)

    @pl.when(i == 0)
    def _():
        gather(0, 0)

    @pl.when(i + 1 < n_tiles)
    def _():
        gather(i + 1, 1 - slot)

    pltpu.make_async_copy(ys_hbm.at[pl.ds(0, tm)], buf.at[slot, 0], sem.at[slot]).wait()
    pltpu.make_async_copy(ys_hbm.at[pl.ds(0, tm)], buf.at[slot, 1], sem.at[slot]).wait()
    w = rf_ref[...]
    y = w[:, 0:1] * buf[slot, 0] + w[:, 1:2] * buf[slot, 1]
    x2 = x_ref[...] + g_ref[...] * y
    z = _rms(x2, pg_ref[...])
    if final:
        o_ref[...] = z
    else:
        o_ref[...] = x2
        rest[1][...] = (z * (1.0 + psc_ref[...]) + psh_ref[...]).astype(rest[1].dtype)


def _combine(dest, x_all, rf, mods5, layer, post_g, ys, rows, n_tiles, n_tok_total, final):
    d = x_all.shape[-1]
    tm = rows.tm
    nxt = min(layer + 1, mods5.shape[0] - 1)
    row_spec = pl.BlockSpec((tm, d), lambda i, dr: (i, 0))
    grid_spec = pltpu.PrefetchScalarGridSpec(
        num_scalar_prefetch=1,
        grid=(n_tiles,),
        in_specs=[row_spec,
                  pl.BlockSpec((tm, 128), lambda i, dr: (i, 0)),
                  _mod_spec(rows, layer, 5, d),
                  pl.BlockSpec((1, d), lambda i, dr: (0, 0)),
                  _mod_spec(rows, nxt, 0, d),
                  _mod_spec(rows, nxt, 1, d),
                  pl.BlockSpec(memory_space=pl.ANY)],
        out_specs=row_spec if final else [row_spec, row_spec],
        scratch_shapes=[pltpu.VMEM((2, 2, tm, d), F32), pltpu.SemaphoreType.DMA((2,))],
    )
    stream = jax.ShapeDtypeStruct((n_tiles * tm, d), F32)
    return pl.pallas_call(
        functools.partial(_combine_kernel, tm=tm, n_tok=n_tok_total, n_tiles=n_tiles, final=final),
        grid_spec=grid_spec,
        out_shape=stream if final else [stream, jax.ShapeDtypeStruct((n_tiles * tm, d), BF16)],
        compiler_params=_cp(("arbitrary",)),
        name="moe_combine",
    )(dest, x_all, rf, mods5, post_g.reshape(1, d), mods5, mods5, ys)


def _moe(x_all, n_tok, norm_g, mods5, layer, router_w, router_b, w1, w3, w2, final_g, final, batch, seq, ctx_len):
    rows_r = _Rows(batch, seq, ctx_len, TROUTE)
    rows_c = _Rows(batch, seq, ctx_len, TCOMB)
    h, ri, rf, cnt = _router(x_all, norm_g, mods5, layer, router_w, router_b, rows_r, n_tok // TROUTE)
    counts = cnt[0].astype(jnp.int32)
    padded = ((counts + TE - 1) // TE) * TE
    ends = jnp.cumsum(padded)
    starts = ends - padded
    e1, e2, r1, r2 = ri[0], ri[1], ri[2], ri[3]
    dest = jnp.concatenate([starts[e1] + r1, starts[e2] + r2]).astype(jnp.int32)
    p_max = 2 * n_tok + N_EXPERTS * TE
    n_tiles = p_max // TE
    n_used = (ends[-1] // TE).astype(jnp.int32)
    tile_start = jnp.arange(n_tiles, dtype=jnp.int32) * TE
    tile_expert = jnp.sum((tile_start[:, None] >= ends[None, :]).astype(jnp.int32), axis=1)
    last_expert = jnp.sum((jnp.maximum(ends[-1] - 1, 0) >= ends).astype(jnp.int32))
    tile_expert = jnp.minimum(jnp.where(tile_start < ends[-1], tile_expert, last_expert), N_EXPERTS - 1).astype(jnp.int32)
    eid = jnp.arange(N_EXPERTS, dtype=jnp.int32)
    later_used = jnp.logical_and(eid[None, :] > eid[:, None], (padded > 0)[None, :])
    next_used = jnp.min(jnp.where(later_used, eid[None, :], N_EXPERTS), axis=1)
    next_used = jnp.where(next_used == N_EXPERTS, eid, next_used)
    next_expert = jnp.sum(jnp.where(tile_expert[:, None] == eid[None, :], next_used[None, :], 0), axis=1).astype(jnp.int32)
    pad = jnp.concatenate([starts + counts, ends]).astype(jnp.int32)
    ys = _experts(tile_expert, next_expert, n_used.reshape(1), pad, dest, h, w1, w3, w2, layer, p_max)
    return _combine(dest, x_all, rf, mods5, layer, final_g, ys, rows_c, n_tok // TCOMB, n_tok, final)


def _rope_tables(t_len, d_rope):
    rows = t_len // GRID_W
    quarter = d_rope // 4
    freqs = ROPE_THETA ** (-jnp.arange(quarter, dtype=F32) / quarter)
    row = jnp.repeat(jnp.arange(rows, dtype=F32), GRID_W)
    col = jnp.tile(jnp.arange(GRID_W, dtype=F32), rows)
    ang = jnp.concatenate([row[:, None] * freqs, col[:, None] * freqs], axis=-1)
    cos, sin = jnp.cos(ang), jnp.sin(ang)
    return jnp.concatenate([cos, cos], axis=-1), jnp.concatenate([-sin, sin], axis=-1)


def kernel(x, c, ctx, c_ctx, mod_w, mod_b, norm_attn_g, norm_ffn_g, final_norm_g, ab_w_in, ab_w_out, hgrn_lb_logits, hgrn_norm_g, mla_q_norm_g, mla_w_uq, mla_kv_norm_g, mla_w_ukv, cd_w_in, cd_w_out, gqa_q_norm_g, gqa_k_norm_g, gla_w_a2, gla_b_a, gla_norm_g, router_w, router_b, moe_w1, moe_w3, moe_w2):
    batch, seq, d = x.shape
    ctx_len = ctx.shape[1]
    n_lat, n_ctx = batch * seq, batch * ctx_len
    assert ctx_len % TQ == 0 and seq % TQ == 0 and seq % ctx_len == 0 and batch < 8
    tm = min(1024, seq, n_ctx)
    rows = _Rows(batch, seq, ctx_len, tm)

    cvec = jnp.concatenate([c, c_ctx[None, :], jnp.zeros((8 - batch - 1, d), F32)], axis=0)
    mods = _modvec(cvec, mod_w, mod_b)
    mods5 = mods.reshape(mods.shape[0], 8, 6, 1, d)

    cos_b, sin_b = _rope_tables(seq, B_ROPE)
    cos_c, sin_c = _rope_tables(seq, C_DH)
    lb = jnp.cumsum(jax.nn.softmax(hgrn_lb_logits.astype(F32), axis=1), axis=1)

    x_lat = x.reshape(n_lat, d)
    x_ctx = ctx.reshape(n_ctx, d)

    h0 = _norm_mod(x_lat, x_ctx, 0, norm_attn_g[0], mods5, 0, rows)
    ab_main = 5 * A_HEADS * A_DK + B_Q_LORA + B_KV_LORA
    tm_mm = next(t for t in (2304, 2048, 1536, 1024, 512, 256) if (n_lat + n_ctx) % t == 0)
    ab_wt = jnp.swapaxes(ab_w_in, 1, 2)
    p0 = _matmul(h0, ab_wt, 0, ab_main, 256, tm_mm)
    kr0 = _matmul_tail(h0, ab_wt, 0, ab_main, B_ROPE, tm_mm)
    mix_a = _hgrn(p0, lb[0, 0], lb[1, 0], hgrn_norm_g[0], batch, seq, ctx_len)
    mix_b = _mla(p0, kr0, mla_w_uq[0], mla_w_ukv[0], mla_q_norm_g[0], mla_kv_norm_g[0], cos_b, sin_b,
                 batch, seq, ctx_len)
    rows_o = _Rows(batch, seq, ctx_len, min(512, tm))
    x1 = _out_proj(mix_a, mix_b, ab_w_out, 0, x_lat, x_ctx, 0, mods5, 0, rows_o, rows_o.n_all)
    x2, h1 = _moe(x1, n_lat + n_ctx, norm_ffn_g[0], mods5, 0, router_w, router_b, moe_w1, moe_w3, moe_w2,
                  norm_attn_g[1], False, batch, seq, ctx_len)

    cd_main = (C_HEADS + 2 * C_KV_HEADS) * C_DH + 2 * D_HEADS * D_DK + 2 * D_HEADS * D_DV
    cd_wt = jnp.swapaxes(cd_w_in, 1, 2)
    p1 = _matmul(h1, cd_wt, 0, cd_main, 512, tm_mm)
    ga1 = _matmul_tail(h1, cd_wt, 0, cd_main, 2 * D_GATE_RANK, tm_mm)
    mix_c = _gqa(p1, gqa_q_norm_g[0], gqa_k_norm_g[0], cos_c, sin_c, batch, seq, ctx_len)
    mix_d = _gla(p1, ga1, gla_w_a2[0], gla_b_a[0], gla_norm_g[0], batch, seq, ctx_len)
    x3 = _out_proj(mix_c, mix_d, cd_w_out, 0, x2, x2, rows_o.n_lat, mods5, 1, rows_o, rows_o.n_lat)
    out = _moe(x3, n_lat, norm_ffn_g[1], mods5, 1, router_w, router_b, moe_w1, moe_w3, moe_w2,
               final_norm_g, True, batch, seq, ctx_len)
    return out.reshape(batch, seq, d)
```

```python
import functools

import jax
import jax.numpy as jnp
from jax import lax
from jax.experimental import pallas as pl
from jax.experimental.pallas import tpu as pltpu

F32 = jnp.float32
BF16 = jnp.bfloat16
HI = lax.Precision.HIGHEST

GRID_W = 64
ROPE_THETA = 10000.0
NORM_EPS = 1e-6
A_HEADS, A_DK, A_DV = 8, 128, 128
B_HEADS, B_Q_LORA, B_KV_LORA, B_NOPE, B_ROPE, B_DV = 8, 512, 256, 128, 64, 128
C_HEADS, C_KV_HEADS, C_DH = 8, 2, 128
D_HEADS, D_DK, D_DV, D_GATE_RANK = 4, 128, 256, 16
GLA_TAU = 16.0
N_EXPERTS, N_GROUPS = 16, 4

TQ = 256
SCAN_C = 64
SCAN_HEADS = 4
GLA_HEADS = 2
SCAN_BLOCK = 2048
TE = 256
TROUTE = 512
TCOMB = 256
DMA_UNROLL = 8
LANES = 128
VMEM_MIB = 56


def _cp(sem):
    return pltpu.CompilerParams(dimension_semantics=sem, vmem_limit_bytes=VMEM_MIB * 1024 * 1024)


def _rms(x, g):
    return x * lax.rsqrt(jnp.mean(x * x, axis=-1, keepdims=True) + NORM_EPS) * g


def _rope(x, cos, sin):
    half = x.shape[-1] // 2
    swapped = jnp.concatenate([x[:, half:], x[:, :half]], axis=-1)
    return x * cos + swapped * sin


def _dot_nt(a, b):
    return lax.dot_general(a, b, (((1,), (1,)), ((), ())), preferred_element_type=F32)


def _dot_tn(a, b):
    return lax.dot_general(a, b, (((0,), (0,)), ((), ())), preferred_element_type=F32)


def _modvec_kernel(c_ref, w_ref, b_ref, o_ref):
    c = c_ref[...]
    a = c * jax.nn.sigmoid(c)
    a_hi = a.astype(BF16)
    a_lo = (a - a_hi.astype(F32)).astype(BF16)
    w = w_ref[...]
    w_hi = w.astype(BF16)
    w_lo = (w - w_hi.astype(F32)).astype(BF16)
    acc = jnp.dot(a_hi, w_hi, preferred_element_type=F32) + jnp.dot(a_lo, w_hi, preferred_element_type=F32)
    o_ref[...] = acc + jnp.dot(a_hi, w_lo, preferred_element_type=F32) + b_ref[...]


def _modvec(cvec, mod_w, mod_b):
    n_layers, d, n6 = mod_w.shape
    tn = min(1024, n6)
    return pl.pallas_call(
        _modvec_kernel,
        grid=(n_layers, n6 // tn),
        in_specs=[pl.BlockSpec((8, d), lambda l, j: (0, 0)),
                  pl.BlockSpec((None, d, tn), lambda l, j: (l, 0, j)),
                  pl.BlockSpec((None, 1, tn), lambda l, j: (l, 0, j))],
        out_specs=pl.BlockSpec((None, 8, tn), lambda l, j: (l, 0, j)),
        out_shape=jax.ShapeDtypeStruct((n_layers, 8, n6), F32),
        compiler_params=_cp(("parallel", "parallel")),
        name="modvec",
    )(cvec, mod_w, mod_b.reshape(n_layers, 1, n6))


class _Rows:
    def __init__(self, batch, seq, ctx_len, tm):
        assert seq % tm == 0 and (batch * ctx_len) % tm == 0
        self.tm = tm
        self.batch = batch
        self.per_batch = seq // tm
        self.n_lat = batch * seq // tm
        self.n_ctx = batch * ctx_len // tm
        self.n_all = self.n_lat + self.n_ctx

    def mod_row(self, i):
        return jnp.where(i < self.n_lat, i // self.per_batch, self.batch)


def _mod_spec(rows, layer, chunk, d):
    return pl.BlockSpec((None, None, None, 1, d), lambda i, *_: (layer, rows.mod_row(i), chunk, 0, 0))


def _norm_mod_kernel(xl_ref, xc_ref, g_ref, sh_ref, sc_ref, o_ref, *, n_lat):
    i = pl.program_id(0)

    def body(x_ref):
        y = _rms(x_ref[...], g_ref[...])
        o_ref[...] = (y * (1.0 + sc_ref[...]) + sh_ref[...]).astype(o_ref.dtype)

    @pl.when(i < n_lat)
    def _():
        body(xl_ref)

    @pl.when(i >= n_lat)
    def _():
        body(xc_ref)


def _norm_mod(x_lat, x_ctx, ctx_block0, g, mods5, layer, rows):
    d = x_lat.shape[-1]
    tm = rows.tm
    nl = rows.n_lat
    return pl.pallas_call(
        functools.partial(_norm_mod_kernel, n_lat=nl),
        grid=(rows.n_all,),
        in_specs=[pl.BlockSpec((tm, d), lambda i: (jnp.minimum(i, nl - 1), 0)),
                  pl.BlockSpec((tm, d), lambda i: (ctx_block0 + jnp.maximum(i - nl, 0), 0)),
                  pl.BlockSpec((1, d), lambda i: (0, 0)),
                  _mod_spec(rows, layer, 0, d),
                  _mod_spec(rows, layer, 1, d)],
        out_specs=pl.BlockSpec((tm, d), lambda i: (i, 0)),
        out_shape=jax.ShapeDtypeStruct((rows.n_all * tm, d), BF16),
        compiler_params=_cp(("parallel",)),
        name="norm_mod",
    )(x_lat, x_ctx, g.reshape(1, d), mods5, mods5)


def _mm_kernel(a_ref, wt_ref, o_ref):
    o_ref[...] = _dot_nt(a_ref[...], wt_ref[...].astype(BF16)).astype(o_ref.dtype)


def _matmul(a, wt3, layer, n_cols, tn, tm):
    m, k = a.shape
    return pl.pallas_call(
        _mm_kernel,
        grid=(m // tm, n_cols // tn),
        in_specs=[pl.BlockSpec((tm, k), lambda i, j: (i, 0)),
                  pl.BlockSpec((None, tn, k), lambda i, j: (layer, j, 0))],
        out_specs=pl.BlockSpec((tm, tn), lambda i, j: (i, j)),
        out_shape=jax.ShapeDtypeStruct((m, n_cols), BF16),
        compiler_params=_cp(("parallel", "arbitrary")),
        name="in_proj",
    )(a, wt3)


def _mm_tail_kernel(a_ref, wt_ref, o_ref):
    n = o_ref.shape[-1]
    acc = _dot_nt(a_ref[...], wt_ref[...].astype(BF16))
    o_ref[...] = acc[:, :n].astype(o_ref.dtype)


def _matmul_tail(a, wt3, layer, col0, n_cols, tm):
    m, k = a.shape
    lane = 128
    assert col0 % lane == 0 and n_cols <= lane and col0 + n_cols == wt3.shape[1]
    return pl.pallas_call(
        _mm_tail_kernel,
        grid=(m // tm,),
        in_specs=[pl.BlockSpec((tm, k), lambda i: (i, 0)),
                  pl.BlockSpec((None, lane, k), lambda i: (layer, col0 // lane, 0))],
        out_specs=pl.BlockSpec((tm, n_cols), lambda i: (i, 0)),
        out_shape=jax.ShapeDtypeStruct((m, n_cols), BF16),
        compiler_params=_cp(("parallel",)),
        name="in_proj_tail",
    )(a, wt3)


def _out_proj_kernel(ma_ref, mb_ref, w_hbm, xl_ref, xc_ref, g_ref, o_ref, w_s, stage, *, n_lat, widx):
    i = pl.program_id(0)
    ka = ma_ref.shape[-1]

    @pl.when(i == 0)
    def _():
        rows = stage.shape[0]
        for c in range(w_s.shape[0] // rows):
            pltpu.sync_copy(w_hbm.at[widx, pl.ds(c * rows, rows)], stage)
            w_s[c * rows:(c + 1) * rows, :] = stage[...].astype(BF16)

    acc = jnp.dot(ma_ref[...], w_s[:ka, :], preferred_element_type=F32)
    acc += jnp.dot(mb_ref[...], w_s[ka:, :], preferred_element_type=F32)
    upd = g_ref[...] * acc

    @pl.when(i < n_lat)
    def _():
        o_ref[...] = xl_ref[...] + upd

    @pl.when(i >= n_lat)
    def _():
        o_ref[...] = xc_ref[...] + upd


def _out_proj(mix_a, mix_b, w_out, widx, x_lat, x_ctx, ctx_block0, mods5, layer, rows, n_tiles):
    d = x_lat.shape[-1]
    ka, kb = mix_a.shape[-1], mix_b.shape[-1]
    tm = rows.tm
    nl = rows.n_lat
    stage_rows = min(512, ka + kb)
    return pl.pallas_call(
        functools.partial(_out_proj_kernel, n_lat=nl, widx=widx),
        grid=(n_tiles,),
        in_specs=[pl.BlockSpec((tm, ka), lambda i: (i, 0)),
                  pl.BlockSpec((tm, kb), lambda i: (i, 0)),
                  pl.BlockSpec(memory_space=pl.ANY),
                  pl.BlockSpec((tm, d), lambda i: (jnp.minimum(i, nl - 1), 0)),
                  pl.BlockSpec((tm, d), lambda i: (ctx_block0 + jnp.maximum(i - nl, 0), 0)),
                  _mod_spec(rows, layer, 2, d)],
        out_specs=pl.BlockSpec((tm, d), lambda i: (i, 0)),
        out_shape=jax.ShapeDtypeStruct((n_tiles * tm, d), F32),
        scratch_shapes=[pltpu.VMEM((ka + kb, d), BF16), pltpu.VMEM((stage_rows, d), F32)],
        compiler_params=_cp(("arbitrary",)),
        name="out_proj",
    )(mix_a, mix_b, w_out, x_lat, x_ctx, mods5)


def _tri(c, upper):
    r = lax.broadcasted_iota(jnp.int32, (c, c), 0)
    s = lax.broadcasted_iota(jnp.int32, (c, c), 1)
    return (s >= r) if upper else (r >= s)


def _scan_block(q, k, v, g, st, mask, forward):
    c = SCAN_C
    dk, dv = q.shape[-1], v.shape[-1]
    n = q.shape[0] // c
    mid, last = (c // 2 - 1, c - 1) if forward else (c // 2, 0)
    tri = jnp.broadcast_to(mask.astype(BF16)[None], (n, c, c))
    g3 = g.reshape(n, c, dk)
    g_hi = g3.astype(BF16)
    g_lo = (g3 - g_hi.astype(F32)).astype(BF16)
    cum = (jnp.einsum('cts,csd->ctd', tri, g_hi, preferred_element_type=F32)
           + jnp.einsum('cts,csd->ctd', tri, g_lo, preferred_element_type=F32))
    m = cum[:, mid:mid + 1, :]
    tot = cum[:, last:last + 1, :]
    qe = (q.reshape(n, c, dk) * jnp.exp(cum - m)).astype(BF16)
    ke = (k.reshape(n, c, dk) * jnp.exp(m - cum)).astype(BF16)
    a = jnp.einsum('ctd,csd->cts', qe, ke, preferred_element_type=F32)
    a = jnp.where(mask[None], a, 0.0).astype(BF16)
    v3 = v.reshape(n, c, dv)
    o = jnp.einsum('cts,csv->ctv', a, v3, preferred_element_type=F32)
    u = jnp.einsum('csv,csd->cvd', v3, ke, preferred_element_type=F32)
    em = jnp.exp(m)
    et = jnp.exp(tot - m)
    states = [None] * n
    for ci in (range(n) if forward else reversed(range(n))):
        stp = st * em[ci]
        states[ci] = stp.astype(BF16)
        st = (stp + u[ci]) * et[ci]
    o = o + jnp.einsum('ctd,cvd->ctv', qe, jnp.stack(states), preferred_element_type=F32)
    return o.reshape(n * c, dv), st


def _scan_segments(segments, prep_f, prep_b, of_ref, ob_ref, dk, dv):
    low, up = _tri(SCAN_C, False), _tri(SCAN_C, True)
    carry = (jnp.zeros((dv, dk), F32), jnp.zeros((dv, dk), F32))
    for rows, off, seg in segments:
        rb_ = min(SCAN_BLOCK, rows)
        n = rows // rb_

        def body(i, carry, n=n, off=off, seg=seg, rb_=rb_):
            sf, sb = carry
            rf = pl.multiple_of(i * rb_, rb_)
            rb = pl.multiple_of((n - 1 - i) * rb_, rb_)
            q, k, v, g = prep_f(seg, rf, rb_)
            o, sf = _scan_block(q, k, v, g, sf, low, True)
            of_ref[pl.ds(off + rf, rb_), :] = o
            q, k, v, g = prep_b(seg, rb, rb_)
            o, sb = _scan_block(q, k, v, g, sb, up, False)
            ob_ref[pl.ds(off + rb, rb_), :] = o
            return sf, sb

        carry = lax.fori_loop(0, n, body, carry)


def _hgrn_kernel(ql, qc, f1l, f1c, f2l, f2c, vl, vc, gl, gc, lbf, lbb, ng, o_ref, of_s, ob_s, *, ctx_len, seq):
    rt = pl.program_id(2)
    n_ctx_tiles = ctx_len // TQ

    @pl.when(rt == 0)
    def _scan():
        refs = {0: (qc, f1c, f2c, vc), 1: (ql, f1l, f2l, vl)}
        scale = A_DK ** -0.5

        for hh in range(SCAN_HEADS):
            ks = slice(hh * A_DK, (hh + 1) * A_DK)
            vs = slice(hh * A_DV, (hh + 1) * A_DV)

            def prep(seg, r, nr, fi, lb_ref, ks=ks, vs=vs):
                x = refs[seg][0][pl.ds(r, nr), ks].astype(F32)
                q = x * jax.nn.sigmoid(x) * scale
                v = refs[seg][3][pl.ds(r, nr), vs]
                lb = lb_ref[:, ks]
                f = lb + (1.0 - lb) * jax.nn.sigmoid(refs[seg][fi][pl.ds(r, nr), ks].astype(F32))
                return q, 1.0 - f, v, jnp.log(f)

            _scan_segments(
                [(ctx_len, 0, 0), (seq, ctx_len, 1)],
                lambda seg, r, nr, prep=prep: prep(seg, r, nr, 1, lbf),
                lambda seg, r, nr, prep=prep: prep(seg, r, nr, 2, lbb),
                of_s.at[hh], ob_s.at[hh], A_DK, A_DV)

    r0 = pl.multiple_of(rt * TQ, TQ)
    ys = [_rms(of_s[hh, pl.ds(r0, TQ), :] + ob_s[hh, pl.ds(r0, TQ), :], ng[...]) for hh in range(SCAN_HEADS)]
    y = jnp.concatenate(ys, axis=-1)

    @pl.when(rt < n_ctx_tiles)
    def _():
        gate = gc[pl.ds(r0, TQ), :].astype(F32)
        o_ref[...] = (y * jax.nn.sigmoid(gate)).astype(o_ref.dtype)

    @pl.when(rt >= n_ctx_tiles)
    def _():
        gate = gl[pl.ds(pl.multiple_of(r0 - ctx_len, TQ), TQ), :].astype(F32)
        o_ref[...] = (y * jax.nn.sigmoid(gate)).astype(o_ref.dtype)


def _out_row_block(batch, seq, ctx_len):
    nct = ctx_len // TQ
    nlt = seq // TQ

    def f(b, rt):
        return jnp.where(rt < nct, batch * nlt + b * nct + rt, b * nlt + rt - nct)

    return f


def _hgrn(p, lb_f, lb_b, norm_g, batch, seq, ctx_len):
    h, dk, dv = A_HEADS, A_DK, A_DV
    hp = SCAN_HEADS
    hg = h // hp
    nct, nlt = ctx_len // TQ, seq // TQ
    ctx_blk0 = batch * seq // ctx_len
    row_block = _out_row_block(batch, seq, ctx_len)
    in_specs = []
    for kcol in range(5):
        in_specs.append(pl.BlockSpec((seq, hp * dk), lambda b, hh, rt, kcol=kcol: (b, kcol * hg + hh)))
        in_specs.append(pl.BlockSpec((ctx_len, hp * dk), lambda b, hh, rt, kcol=kcol: (ctx_blk0 + b, kcol * hg + hh)))
    vec = pl.BlockSpec((1, hp * dk), lambda b, hh, rt: (0, hh))
    in_specs += [vec, vec, pl.BlockSpec((1, dv), lambda b, hh, rt: (0, 0))]
    return pl.pallas_call(
        functools.partial(_hgrn_kernel, ctx_len=ctx_len, seq=seq),
        grid=(batch, hg, nct + nlt),
        in_specs=in_specs,
        out_specs=pl.BlockSpec((TQ, hp * dv), lambda b, hh, rt: (row_block(b, rt), hh)),
        out_shape=jax.ShapeDtypeStruct((batch * (seq + ctx_len), h * dv), BF16),
        scratch_shapes=[pltpu.VMEM((hp, seq + ctx_len, dv), F32), pltpu.VMEM((hp, seq + ctx_len, dv), F32)],
        compiler_params=_cp(("parallel", "parallel", "arbitrary")),
        name="hgrn_scan",
    )(*([p] * 10), lb_f.reshape(1, h * dk), lb_b.reshape(1, h * dk), norm_g.reshape(1, dv))


def _gla_kernel(ql, qc, kl, kc, vl, vc, gl, al, ac, wa, ba, ng, o_ref, of_s, ob_s, *, ctx_len, seq):
    rt = pl.program_id(2)

    @pl.when(rt == 0)
    def _scan():
        refs = {0: (qc, kc, vc, ac), 1: (ql, kl, vl, al)}
        scale = D_DK ** -0.5
        r16 = D_GATE_RANK

        for hh in range(GLA_HEADS):
            ks = slice(hh * D_DK, (hh + 1) * D_DK)
            vs = slice(hh * D_DV, (hh + 1) * D_DV)

            def prep(seg, r, nr, d, ks=ks, vs=vs):
                q = refs[seg][0][pl.ds(r, nr), ks].astype(F32) * scale
                k = refs[seg][1][pl.ds(r, nr), ks].astype(F32)
                v = refs[seg][2][pl.ds(r, nr), vs]
                a = refs[seg][3][pl.ds(r, nr), :].astype(F32)[:, d * r16:(d + 1) * r16]
                z = jnp.dot(a, wa[d, :, ks], preferred_element_type=F32, precision=HI) + ba[d, :, ks]
                g = (jnp.minimum(z, 0.0) - jnp.log(1.0 + jnp.exp(-jnp.abs(z)))) * (1.0 / GLA_TAU)
                return q, k, v, g

            _scan_segments(
                [(ctx_len, 0, 0), (seq, ctx_len, 1)],
                lambda seg, r, nr, prep=prep: prep(seg, r, nr, 0),
                lambda seg, r, nr, prep=prep: prep(seg, r, nr, 1),
                of_s.at[hh], ob_s.at[hh], D_DK, D_DV)

    r0 = pl.multiple_of(rt * TQ, TQ)
    ys = [_rms(of_s[hh, pl.ds(ctx_len + r0, TQ), :] + ob_s[hh, pl.ds(ctx_len + r0, TQ), :], ng[...])
          for hh in range(GLA_HEADS)]
    gate = gl[pl.ds(r0, TQ), :].astype(F32)
    o_ref[...] = (jnp.concatenate(ys, axis=-1) * gate * jax.nn.sigmoid(gate)).astype(o_ref.dtype)


def _gla(p, ga, w_a2, b_a, norm_g, batch, seq, ctx_len):
    h, dk, dv = D_HEADS, D_DK, D_DV
    hp = GLA_HEADS
    hg = h // hp
    nlt = seq // TQ
    ctx_blk0 = batch * seq // ctx_len
    wk, wv = hp * dk, hp * dv
    q0 = (C_HEADS + 2 * C_KV_HEADS) * C_DH // wk
    k0 = q0 + hg
    v0 = (k0 + hg) * wk // wv
    g0 = v0 + hg

    def pair(width, blk0):
        return [pl.BlockSpec((seq, width), lambda b, hh, rt: (b, blk0 + hh)),
                pl.BlockSpec((ctx_len, width), lambda b, hh, rt: (ctx_blk0 + b, blk0 + hh))]

    in_specs = pair(wk, q0) + pair(wk, k0) + pair(wv, v0)
    in_specs += [pl.BlockSpec((seq, wv), lambda b, hh, rt: (b, g0 + hh)),
                 pl.BlockSpec((seq, 2 * D_GATE_RANK), lambda b, hh, rt: (b, 0)),
                 pl.BlockSpec((ctx_len, 2 * D_GATE_RANK), lambda b, hh, rt: (ctx_blk0 + b, 0)),
                 pl.BlockSpec((2, D_GATE_RANK, wk), lambda b, hh, rt: (0, 0, hh)),
                 pl.BlockSpec((2, 1, wk), lambda b, hh, rt: (0, 0, hh)),
                 pl.BlockSpec((1, dv), lambda b, hh, rt: (0, 0))]
    return pl.pallas_call(
        functools.partial(_gla_kernel, ctx_len=ctx_len, seq=seq),
        grid=(batch, hg, nlt),
        in_specs=in_specs,
        out_specs=pl.BlockSpec((TQ, wv), lambda b, hh, rt: (b * nlt + rt, hh)),
        out_shape=jax.ShapeDtypeStruct((batch * seq, h * dv), BF16),
        scratch_shapes=[pltpu.VMEM((hp, seq + ctx_len, dv), F32), pltpu.VMEM((hp, seq + ctx_len, dv), F32)],
        compiler_params=_cp(("parallel", "parallel", "arbitrary")),
        name="gla_scan",
    )(p, p, p, p, p, p, p, ga, ga, w_a2, b_a.reshape(2, 1, h * dk), norm_g.reshape(1, dv))


def _softmax_pv(s, v):
    m = jnp.max(s, axis=-1, keepdims=True)
    p = jnp.exp(s - m)
    l = jnp.sum(p, axis=-1, keepdims=True)
    return jnp.dot(p.astype(BF16), v, preferred_element_type=F32) / l


def _mla_kernel(ql_ref, kvl_ref, kvc_ref, krl_ref, krc_ref, wqn_ref, wqr_ref, wkv_ref, gq_ref, gkv_ref,
                cosq_ref, sinq_ref, cosk_ref, sink_ref, o_ref, k_s, v_s, *, ctx_len):
    qt = pl.program_id(1)
    n_ctx_tiles = ctx_len // TQ
    scale = (B_NOPE + B_ROPE) ** -0.5
    dkv = B_NOPE + B_DV

    @pl.when(qt == 0)
    def _prep():
        kvc = _rms(kvc_ref[...].astype(F32), gkv_ref[...]).astype(BF16)
        kvl = _rms(kvl_ref[...].astype(F32), gkv_ref[...]).astype(BF16)
        kr_c = krc_ref[...]
        kr_l = _rope(krl_ref[...].astype(F32), cosk_ref[...], sink_ref[...]).astype(BF16)
        for h in range(B_HEADS):
            w = wkv_ref[:, h * dkv:(h + 1) * dkv].astype(BF16)
            up_c = jnp.dot(kvc, w, preferred_element_type=F32)
            k_s[h, 0:ctx_len, :] = jnp.concatenate([up_c[:, :B_NOPE].astype(BF16), kr_c], axis=-1)
            v_s[h, 0:ctx_len, :] = up_c[:, B_NOPE:].astype(BF16)
            up_l = jnp.dot(kvl, w, preferred_element_type=F32)
            k_s[h, ctx_len:, :] = jnp.concatenate([up_l[:, :B_NOPE].astype(BF16), kr_l], axis=-1)
            v_s[h, ctx_len:, :] = up_l[:, B_NOPE:].astype(BF16)

    xn = _rms(ql_ref[...].astype(F32), gq_ref[...]).astype(BF16)
    qn_all = jnp.dot(xn, wqn_ref[...].astype(BF16), preferred_element_type=F32) * scale
    qr_all = jnp.dot(xn, wqr_ref[...].astype(BF16), preferred_element_type=F32) * scale

    def heads(n_keys, rotate):
        outs = []
        for h in range(B_HEADS):
            qn = qn_all[:, h * B_NOPE:(h + 1) * B_NOPE]
            qr = qr_all[:, h * B_ROPE:(h + 1) * B_ROPE]
            if rotate:
                qr = _rope(qr, cosq_ref[...], sinq_ref[...])
            q = jnp.concatenate([qn, qr], axis=-1).astype(BF16)
            s = _dot_nt(q, k_s[h, 0:n_keys, :])
            outs.append(_softmax_pv(s, v_s[h, 0:n_keys, :]).astype(o_ref.dtype))
        o_ref[...] = jnp.concatenate(outs, axis=-1)

    @pl.when(qt < n_ctx_tiles)
    def _():
        heads(ctx_len, False)

    @pl.when(qt >= n_ctx_tiles)
    def _():
        heads(k_s.shape[1], True)


def _mla(p, kr, w_uq, w_ukv, gq, gkv, cos, sin, batch, seq, ctx_len):
    h = B_HEADS
    nct, nlt = ctx_len // TQ, seq // TQ
    ctx_blk0 = batch * seq // ctx_len
    row_block = _out_row_block(batch, seq, ctx_len)
    ql_blk = 5 * A_HEADS * A_DK // B_Q_LORA
    kv_blk = (5 * A_HEADS * A_DK + B_Q_LORA) // B_KV_LORA
    dq = B_NOPE + B_ROPE
    s_all = seq + ctx_len
    w3 = w_uq.reshape(B_Q_LORA, h, dq)
    wq_n = w3[:, :, :B_NOPE].reshape(B_Q_LORA, h * B_NOPE)
    wq_r = w3[:, :, B_NOPE:].reshape(B_Q_LORA, h * B_ROPE)
    in_specs = [
        pl.BlockSpec((TQ, B_Q_LORA), lambda b, qt: (row_block(b, qt), ql_blk)),
        pl.BlockSpec((seq, B_KV_LORA), lambda b, qt: (b, kv_blk)),
        pl.BlockSpec((ctx_len, B_KV_LORA), lambda b, qt: (ctx_blk0 + b, kv_blk)),
        pl.BlockSpec((seq, B_ROPE), lambda b, qt: (b, 0)),
        pl.BlockSpec((ctx_len, B_ROPE), lambda b, qt: (ctx_blk0 + b, 0)),
        pl.BlockSpec((B_Q_LORA, h * B_NOPE), lambda b, qt: (0, 0)),
        pl.BlockSpec((B_Q_LORA, h * B_ROPE), lambda b, qt: (0, 0)),
        pl.BlockSpec((B_KV_LORA, h * (B_NOPE + B_DV)), lambda b, qt: (0, 0)),
        pl.BlockSpec((1, B_Q_LORA), lambda b, qt: (0, 0)),
        pl.BlockSpec((1, B_KV_LORA), lambda b, qt: (0, 0)),
        pl.BlockSpec((TQ, B_ROPE), lambda b, qt: (jnp.maximum(qt - nct, 0), 0)),
        pl.BlockSpec((TQ, B_ROPE), lambda b, qt: (jnp.maximum(qt - nct, 0), 0)),
        pl.BlockSpec((seq, B_ROPE), lambda b, qt: (0, 0)),
        pl.BlockSpec((seq, B_ROPE), lambda b, qt: (0, 0)),
    ]
    return pl.pallas_call(
        functools.partial(_mla_kernel, ctx_len=ctx_len),
        grid=(batch, nct + nlt),
        in_specs=in_specs,
        out_specs=pl.BlockSpec((TQ, h * B_DV), lambda b, qt: (row_block(b, qt), 0)),
        out_shape=jax.ShapeDtypeStruct((batch * s_all, h * B_DV), BF16),
        scratch_shapes=[pltpu.VMEM((h, s_all, dq), BF16), pltpu.VMEM((h, s_all, B_DV), BF16)],
        compiler_params=_cp(("parallel", "arbitrary")),
        name="mla_attn",
    )(p, p, p, kr, kr, wq_n, wq_r, w_ukv, gq.reshape(1, -1), gkv.reshape(1, -1), cos, sin, cos, sin)


def _gqa_kernel(q_ref, kl_ref, kc_ref, vl_ref, vc_ref, gq_ref, gk_ref, cosq_ref, sinq_ref, cosk_ref, sink_ref,
                o_ref, k_s, v_s, *, ctx_len):
    qt = pl.program_id(2)
    scale = C_DH ** -0.5
    dh = C_DH

    @pl.when(qt == 0)
    def _prep():
        k_s[0:ctx_len, :] = _rms(kc_ref[...].astype(F32), gk_ref[...]).astype(BF16)
        kl = _rms(kl_ref[...].astype(F32), gk_ref[...])
        k_s[ctx_len:, :] = _rope(kl, cosk_ref[...], sink_ref[...]).astype(BF16)
        v_s[0:ctx_len, :] = vc_ref[...]
        v_s[ctx_len:, :] = vl_ref[...]

    outs = []
    for g in range(C_HEADS // C_KV_HEADS):
        q = _rms(q_ref[:, g * dh:(g + 1) * dh].astype(F32), gq_ref[...])
        q = _rope(q, cosq_ref[...], sinq_ref[...]) * scale
        s = _dot_nt(q.astype(BF16), k_s[...])
        outs.append(_softmax_pv(s, v_s[...]).astype(o_ref.dtype))
    o_ref[...] = jnp.concatenate(outs, axis=-1)


def _gqa(p, gq, gk, cos, sin, batch, seq, ctx_len):
    kvh, grp, dh = C_KV_HEADS, C_HEADS // C_KV_HEADS, C_DH
    nlt = seq // TQ
    ctx_blk0 = batch * seq // ctx_len
    k0 = C_HEADS
    v0 = C_HEADS + C_KV_HEADS
    s_all = seq + ctx_len
    in_specs = [
        pl.BlockSpec((TQ, grp * dh), lambda b, kh, qt: (b * nlt + qt, kh)),
        pl.BlockSpec((seq, dh), lambda b, kh, qt: (b, k0 + kh)),
        pl.BlockSpec((ctx_len, dh), lambda b, kh, qt: (ctx_blk0 + b, k0 + kh)),
        pl.BlockSpec((seq, dh), lambda b, kh, qt: (b, v0 + kh)),
        pl.BlockSpec((ctx_len, dh), lambda b, kh, qt: (ctx_blk0 + b, v0 + kh)),
        pl.BlockSpec((1, dh), lambda b, kh, qt: (0, 0)),
        pl.BlockSpec((1, dh), lambda b, kh, qt: (0, 0)),
        pl.BlockSpec((TQ, dh), lambda b, kh, qt: (qt, 0)),
        pl.BlockSpec((TQ, dh), lambda b, kh, qt: (qt, 0)),
        pl.BlockSpec((seq, dh), lambda b, kh, qt: (0, 0)),
        pl.BlockSpec((seq, dh), lambda b, kh, qt: (0, 0)),
    ]
    return pl.pallas_call(
        functools.partial(_gqa_kernel, ctx_len=ctx_len),
        grid=(batch, kvh, nlt),
        in_specs=in_specs,
        out_specs=pl.BlockSpec((TQ, grp * dh), lambda b, kh, qt: (b * nlt + qt, kh)),
        out_shape=jax.ShapeDtypeStruct((batch * seq, C_HEADS * dh), BF16),
        scratch_shapes=[pltpu.VMEM((s_all, dh), BF16), pltpu.VMEM((s_all, dh), BF16)],
        compiler_params=_cp(("parallel", "parallel", "arbitrary")),
        name="gqa_attn",
    )(p, p, p, p, p, gq.reshape(1, dh), gk.reshape(1, dh), cos, sin, cos, sin)


def _router_kernel(x_ref, g_ref, sh_ref, sc_ref, rw_ref, rb_ref, h_ref, ri_ref, rf_ref, cnt_ref, base_s):
    i = pl.program_id(0)
    tm = x_ref.shape[0]
    ne = N_EXPERTS
    per = ne // N_GROUPS

    @pl.when(i == 0)
    def _():
        base_s[...] = jnp.zeros_like(base_s)

    h = _rms(x_ref[...], g_ref[...]) * (1.0 + sc_ref[...]) + sh_ref[...]
    nch = h.shape[1] // LANES
    for j in range(nch):
        h_ref[pl.ds(j, tm, stride=nch), :] = h[:, j * LANES:(j + 1) * LANES]
    h_hi = h.astype(BF16)
    h_lo = (h - h_hi.astype(F32)).astype(BF16)
    rw = rw_ref[...]
    w_hi = rw.astype(BF16)
    w_lo = (rw - w_hi.astype(F32)).astype(BF16)
    hw = jnp.dot(h_hi, jnp.concatenate([w_hi, w_lo], axis=1), preferred_element_type=F32)
    logits = hw[:, :ne] + hw[:, ne:] + jnp.dot(h_lo, w_hi, preferred_element_type=F32)
    scores = jax.nn.sigmoid(logits)
    sel = scores + rb_ref[...]
    lane = lax.broadcasted_iota(jnp.int32, (tm, ne), 1).astype(F32)
    neg = -jnp.inf
    big = float(ne)

    def top2(vals):
        m1 = jnp.max(vals, axis=1, keepdims=True)
        i1 = jnp.min(jnp.where(vals == m1, lane, big), axis=1, keepdims=True)
        rest = jnp.where(lane == i1, neg, vals)
        m2 = jnp.max(rest, axis=1, keepdims=True)
        i2 = jnp.min(jnp.where(rest == m2, lane, big), axis=1, keepdims=True)
        return m1 + m2, i1, i2

    best, e1, e2 = None, None, None
    for grp in range(N_GROUPS):
        in_grp = jnp.logical_and(lane >= float(grp * per), lane < float((grp + 1) * per))
        gsum, i1, i2 = top2(jnp.where(in_grp, sel, neg))
        if grp == 0:
            best, e1, e2 = gsum, i1, i2
        else:
            better = gsum > best
            best = jnp.where(better, gsum, best)
            e1 = jnp.where(better, i1, e1)
            e2 = jnp.where(better, i2, e2)

    hot1 = lane == e1
    hot2 = lane == e2
    w1 = jnp.sum(jnp.where(hot1, scores, 0.0), axis=1, keepdims=True)
    w2 = jnp.sum(jnp.where(hot2, scores, 0.0), axis=1, keepdims=True)
    wsum = w1 + w2
    assign = jnp.logical_or(hot1, hot2)
    r = lax.broadcasted_iota(jnp.int32, (tm, tm), 0)
    c = lax.broadcasted_iota(jnp.int32, (tm, tm), 1)
    before = (c < r).astype(BF16)
    excl = jnp.dot(before, assign.astype(BF16), preferred_element_type=F32) + base_s[...]
    rank1 = jnp.sum(jnp.where(hot1, excl, 0.0), axis=1, keepdims=True)
    rank2 = jnp.sum(jnp.where(hot2, excl, 0.0), axis=1, keepdims=True)
    base_s[...] = base_s[...] + jnp.sum(assign.astype(F32), axis=0, keepdims=True)

    l128 = lax.broadcasted_iota(jnp.int32, (tm, 128), 1)
    ri = jnp.where(l128 == 0, e1, jnp.where(l128 == 1, e2, jnp.where(l128 == 2, rank1, jnp.where(l128 == 3, rank2, 0.0))))
    ri_ref[...] = ri.T[0:8, :].astype(jnp.int32)
    rf_ref[...] = jnp.where(l128 == 0, w1 / wsum, jnp.where(l128 == 1, w2 / wsum, 0.0))
    cnt_ref[...] = jnp.broadcast_to(base_s[...], cnt_ref.shape)


def _router(x_all, g, mods5, layer, router_w, router_b, rows, n_tiles):
    d = x_all.shape[-1]
    tm = rows.tm
    n = n_tiles * tm
    ne = N_EXPERTS
    return pl.pallas_call(
        _router_kernel,
        grid=(n_tiles,),
        in_specs=[pl.BlockSpec((tm, d), lambda i: (i, 0)),
                  pl.BlockSpec((1, d), lambda i: (0, 0)),
                  _mod_spec(rows, layer, 3, d),
                  _mod_spec(rows, layer, 4, d),
                  pl.BlockSpec((d, ne), lambda i: (0, 0)),
                  pl.BlockSpec((1, ne), lambda i: (0, 0))],
        out_specs=[pl.BlockSpec((tm * (d // LANES), LANES), lambda i: (i, 0)),
                   pl.BlockSpec((8, tm), lambda i: (0, i)),
                   pl.BlockSpec((tm, 128), lambda i: (i, 0)),
                   pl.BlockSpec((8, ne), lambda i: (0, 0))],
        out_shape=[jax.ShapeDtypeStruct((n * (d // LANES), LANES), F32),
                   jax.ShapeDtypeStruct((8, n), jnp.int32),
                   jax.ShapeDtypeStruct((n, 128), F32),
                   jax.ShapeDtypeStruct((8, ne), F32)],
        scratch_shapes=[pltpu.VMEM((1, ne), F32)],
        compiler_params=_cp(("arbitrary",)),
        name="moe_router",
    )(x_all, g.reshape(1, d), mods5, mods5, router_w, router_b.reshape(1, ne))


def _expert_kernel(te_ref, nx_ref, nu_ref, pad_ref, dest_ref, h_hbm, w1_hbm, w3_hbm, w2_hbm, y_ref, xbuf, sem,
                   w1_s, w3_s, w2_s, wf1, wf3, wf2, wsem, wslot_ref, src_ref, *, n_tok, layer):
    r = pl.program_id(0)
    n_used = nu_ref[0]
    active = r < n_used
    changed = jnp.logical_or(r == 0, te_ref[r] != te_ref[jnp.maximum(r - 1, 0)])
    slot = lax.rem(r, 2)
    nch = xbuf.shape[1] // TE

    def weight_copies(e, ws):
        return (pltpu.make_async_copy(w1_hbm.at[layer, e], wf1.at[ws], wsem.at[ws]),
                pltpu.make_async_copy(w3_hbm.at[layer, e], wf3.at[ws], wsem.at[ws]),
                pltpu.make_async_copy(w2_hbm.at[layer, e], wf2.at[ws], wsem.at[ws]))

    @pl.when(r == 0)
    def _():
        wslot_ref[0] = 0
        for cp in weight_copies(te_ref[0], 0):
            cp.start()

        def clear(i, carry):
            src_ref[i] = 0
            return carry

        for e in range(N_EXPERTS):
            lax.fori_loop(pad_ref[e], pad_ref[N_EXPERTS + e], clear, 0)

        def invert(i, carry):
            src_ref[dest_ref[i]] = i
            src_ref[dest_ref[n_tok + i]] = i
            return carry

        lax.fori_loop(0, n_tok, invert, 0, unroll=DMA_UNROLL)

    def gather(tile, dst_slot):
        base = tile * TE
        for t in range(TE):
            row0 = pl.multiple_of(src_ref[base + t] * nch, nch)
            pltpu.make_async_copy(h_hbm.at[pl.ds(row0, nch)], xbuf.at[dst_slot, pl.ds(t * nch, nch)],
                                  sem.at[dst_slot]).start(priority=0)

    @pl.when(jnp.logical_and(r == 0, active))
    def _():
        gather(0, 0)

    @pl.when(r + 1 < n_used)
    def _():
        gather(r + 1, 1 - slot)

    @pl.when(jnp.logical_and(active, changed))
    def _():
        ws = wslot_ref[0]
        for cp in weight_copies(te_ref[r], ws):
            cp.wait()

        @pl.when(nx_ref[r] != te_ref[r])
        def _():
            for cp in weight_copies(nx_ref[r], 1 - ws):
                cp.start(priority=1)

        w1_s[...] = wf1[ws].astype(BF16)
        w3_s[...] = wf3[ws].astype(BF16)
        w2_s[...] = wf2[ws].astype(BF16)
        wslot_ref[0] = 1 - ws

    @pl.when(active)
    def _():
        pltpu.make_async_copy(h_hbm.at[pl.ds(0, TE * nch)], xbuf.at[slot], sem.at[slot]).wait()
        x = jnp.concatenate([xbuf[slot, pl.ds(j, TE, stride=nch), :] for j in range(nch)], axis=1).astype(BF16)
        a = jnp.dot(x, w1_s[...], preferred_element_type=F32)
        b = jnp.dot(x, w3_s[...], preferred_element_type=F32)
        hid = (a * jax.nn.sigmoid(a) * b).astype(BF16)
        y_ref[...] = jnp.dot(hid, w2_s[...], preferred_element_type=F32)

    @pl.when(jnp.logical_not(active))
    def _():
        y_ref[...] = jnp.zeros_like(y_ref)


def _experts(tile_expert, next_expert, n_used, pad, dest, hp, w1, w3, w2, layer, p_max):
    d = w1.shape[-2]
    nch = d // LANES
    n_tok = hp.shape[0] // nch
    f = w1.shape[-1]
    any_spec = pl.BlockSpec(memory_space=pl.ANY)
    grid_spec = pltpu.PrefetchScalarGridSpec(
        num_scalar_prefetch=5,
        grid=(p_max // TE,),
        in_specs=[any_spec, any_spec, any_spec, any_spec],
        out_specs=pl.BlockSpec((TE, d), lambda r, te, nx, nu, pd, sr: (r, 0)),
        scratch_shapes=[pltpu.VMEM((2, TE * nch, LANES), F32), pltpu.SemaphoreType.DMA((2,)),
                        pltpu.VMEM((d, f), BF16), pltpu.VMEM((d, f), BF16), pltpu.VMEM((f, d), BF16),
                        pltpu.VMEM((2, d, f), F32), pltpu.VMEM((2, d, f), F32), pltpu.VMEM((2, f, d), F32),
                        pltpu.SemaphoreType.DMA((2,)), pltpu.SMEM((1,), jnp.int32),
                        pltpu.SMEM((p_max,), jnp.int32)],
    )
    return pl.pallas_call(
        functools.partial(_expert_kernel, n_tok=n_tok, layer=layer),
        grid_spec=grid_spec,
        out_shape=jax.ShapeDtypeStruct((p_max, d), F32),
        compiler_params=_cp(("arbitrary",)),
        name="moe_experts",
    )(tile_expert, next_expert, n_used, pad, dest, hp, w1, w3, w2)


def _combine_kernel(dest_ref, x_ref, rf_ref, g_ref, pg_ref, psh_ref, psc_ref, ys_hbm, *rest, tm, n_tok, n_tiles, final):
    o_ref = rest[0]
    buf, sem = rest[-2], rest[-1]
    i = pl.program_id(0)
    slot = lax.rem(i, 2)

    def gather(tile, dst_slot):
        base = tile * tm
        for t in range(tm):
            pltpu.make_async_copy(ys_hbm.at[pl.ds(dest_ref[base + t], 1)], buf.at[dst_slot, 0, pl.ds(t, 1)],
                                  sem.at[dst_slot]).start(priority=0)
            pltpu.make_async_copy(ys_hbm.at[pl.ds(dest_ref[n_tok + base + t], 1)], buf.at[dst_slot, 1, pl.ds(t, 1)],
                                  sem.at[dst_slot]).start(priority=1)

    @pl.when(i == 0)
    def _():
        gather(0, 0)

    @pl.when(i + 1 < n_tiles)
    def _():
        gather(i + 1, 1 - slot)

    pltpu.make_async_copy(ys_hbm.at[pl.ds(0, tm)], buf.at[slot, 0], sem.at[slot]).wait()
    pltpu.make_async_copy(ys_hbm.at[pl.ds(0, tm)], buf.at[slot, 1], sem.at[slot]).wait()
    w = rf_ref[...]
    y = w[:, 0:1] * buf[slot, 0] + w[:, 1:2] * buf[slot, 1]
    x2 = x_ref[...] + g_ref[...] * y
    z = _rms(x2, pg_ref[...])
    if final:
        o_ref[...] = z
    else:
        o_ref[...] = x2
        rest[1][...] = (z * (1.0 + psc_ref[...]) + psh_ref[...]).astype(rest[1].dtype)


def _combine(dest, x_all, rf, mods5, layer, post_g, ys, rows, n_tiles, n_tok_total, final):
    d = x_all.shape[-1]
    tm = rows.tm
    nxt = min(layer + 1, mods5.shape[0] - 1)
    row_spec = pl.BlockSpec((tm, d), lambda i, dr: (i, 0))
    grid_spec = pltpu.PrefetchScalarGridSpec(
        num_scalar_prefetch=1,
        grid=(n_tiles,),
        in_specs=[row_spec,
                  pl.BlockSpec((tm, 128), lambda i, dr: (i, 0)),
                  _mod_spec(rows, layer, 5, d),
                  pl.BlockSpec((1, d), lambda i, dr: (0, 0)),
                  _mod_spec(rows, nxt, 0, d),
                  _mod_spec(rows, nxt, 1, d),
                  pl.BlockSpec(memory_space=pl.ANY)],
        out_specs=row_spec if final else [row_spec, row_spec],
        scratch_shapes=[pltpu.VMEM((2, 2, tm, d), F32), pltpu.SemaphoreType.DMA((2,))],
    )
    stream = jax.ShapeDtypeStruct((n_tiles * tm, d), F32)
    return pl.pallas_call(
        functools.partial(_combine_kernel, tm=tm, n_tok=n_tok_total, n_tiles=n_tiles, final=final),
        grid_spec=grid_spec,
        out_shape=stream if final else [stream, jax.ShapeDtypeStruct((n_tiles * tm, d), BF16)],
        compiler_params=_cp(("arbitrary",)),
        name="moe_combine",
    )(dest, x_all, rf, mods5, post_g.reshape(1, d), mods5, mods5, ys)


def _moe(x_all, n_tok, norm_g, mods5, layer, router_w, router_b, w1, w3, w2, final_g, final, batch, seq, ctx_len):
    rows_r = _Rows(batch, seq, ctx_len, TROUTE)
    rows_c = _Rows(batch, seq, ctx_len, TCOMB)
    h, ri, rf, cnt = _router(x_all, norm_g, mods5, layer, router_w, router_b, rows_r, n_tok // TROUTE)
    counts = cnt[0].astype(jnp.int32)
    padded = ((counts + TE - 1) // TE) * TE
    ends = jnp.cumsum(padded)
    starts = ends - padded
    e1, e2, r1, r2 = ri[0], ri[1], ri[2], ri[3]
    dest = jnp.concatenate([starts[e1] + r1, starts[e2] + r2]).astype(jnp.int32)
    p_max = 2 * n_tok + N_EXPERTS * TE
    n_tiles = p_max // TE
    n_used = (ends[-1] // TE).astype(jnp.int32)
    tile_start = jnp.arange(n_tiles, dtype=jnp.int32) * TE
    tile_expert = jnp.sum((tile_start[:, None] >= ends[None, :]).astype(jnp.int32), axis=1)
    last_expert = jnp.sum((jnp.maximum(ends[-1] - 1, 0) >= ends).astype(jnp.int32))
    tile_expert = jnp.minimum(jnp.where(tile_start < ends[-1], tile_expert, last_expert), N_EXPERTS - 1).astype(jnp.int32)
    eid = jnp.arange(N_EXPERTS, dtype=jnp.int32)
    later_used = jnp.logical_and(eid[None, :] > eid[:, None], (padded > 0)[None, :])
    next_used = jnp.min(jnp.where(later_used, eid[None, :], N_EXPERTS), axis=1)
    next_used = jnp.where(next_used == N_EXPERTS, eid, next_used)
    next_expert = jnp.sum(jnp.where(tile_expert[:, None] == eid[None, :], next_used[None, :], 0), axis=1).astype(jnp.int32)
    pad = jnp.concatenate([starts + counts, ends]).astype(jnp.int32)
    ys = _experts(tile_expert, next_expert, n_used.reshape(1), pad, dest, h, w1, w3, w2, layer, p_max)
    return _combine(dest, x_all, rf, mods5, layer, final_g, ys, rows_c, n_tok // TCOMB, n_tok, final)


def _rope_tables(t_len, d_rope):
    rows = t_len // GRID_W
    quarter = d_rope // 4
    freqs = ROPE_THETA ** (-jnp.arange(quarter, dtype=F32) / quarter)
    row = jnp.repeat(jnp.arange(rows, dtype=F32), GRID_W)
    col = jnp.tile(jnp.arange(GRID_W, dtype=F32), rows)
    ang = jnp.concatenate([row[:, None] * freqs, col[:, None] * freqs], axis=-1)
    cos, sin = jnp.cos(ang), jnp.sin(ang)
    return jnp.concatenate([cos, cos], axis=-1), jnp.concatenate([-sin, sin], axis=-1)


def kernel(x, c, ctx, c_ctx, mod_w, mod_b, norm_attn_g, norm_ffn_g, final_norm_g, ab_w_in, ab_w_out, hgrn_lb_logits, hgrn_norm_g, mla_q_norm_g, mla_w_uq, mla_kv_norm_g, mla_w_ukv, cd_w_in, cd_w_out, gqa_q_norm_g, gqa_k_norm_g, gla_w_a2, gla_b_a, gla_norm_g, router_w, router_b, moe_w1, moe_w3, moe_w2):
    batch, seq, d = x.shape
    ctx_len = ctx.shape[1]
    n_lat, n_ctx = batch * seq, batch * ctx_len
    assert ctx_len % TQ == 0 and seq % TQ == 0 and seq % ctx_len == 0 and batch < 8
    tm = min(1024, seq, n_ctx)
    rows = _Rows(batch, seq, ctx_len, tm)

    cvec = jnp.concatenate([c, c_ctx[None, :], jnp.zeros((8 - batch - 1, d), F32)], axis=0)
    mods = _modvec(cvec, mod_w, mod_b)
    mods5 = mods.reshape(mods.shape[0], 8, 6, 1, d)

    cos_b, sin_b = _rope_tables(seq, B_ROPE)
    cos_c, sin_c = _rope_tables(seq, C_DH)
    lb = jnp.cumsum(jax.nn.softmax(hgrn_lb_logits.astype(F32), axis=1), axis=1)

    x_lat = x.reshape(n_lat, d)
    x_ctx = ctx.reshape(n_ctx, d)

    h0 = _norm_mod(x_lat, x_ctx, 0, norm_attn_g[0], mods5, 0, rows)
    ab_main = 5 * A_HEADS * A_DK + B_Q_LORA + B_KV_LORA
    tm_mm = next(t for t in (2304, 2048, 1536, 1024, 512, 256) if (n_lat + n_ctx) % t == 0)
    ab_wt = jnp.swapaxes(ab_w_in, 1, 2)
    p0 = _matmul(h0, ab_wt, 0, ab_main, 256, tm_mm)
    kr0 = _matmul_tail(h0, ab_wt, 0, ab_main, B_ROPE, tm_mm)
    mix_a = _hgrn(p0, lb[0, 0], lb[1, 0], hgrn_norm_g[0], batch, seq, ctx_len)
    mix_b = _mla(p0, kr0, mla_w_uq[0], mla_w_ukv[0], mla_q_norm_g[0], mla_kv_norm_g[0], cos_b, sin_b,
                 batch, seq, ctx_len)
    rows_o = _Rows(batch, seq, ctx_len, min(512, tm))
    x1 = _out_proj(mix_a, mix_b, ab_w_out, 0, x_lat, x_ctx, 0, mods5, 0, rows_o, rows_o.n_all)
    x2, h1 = _moe(x1, n_lat + n_ctx, norm_ffn_g[0], mods5, 0, router_w, router_b, moe_w1, moe_w3, moe_w2,
                  norm_attn_g[1], False, batch, seq, ctx_len)

    cd_main = (C_HEADS + 2 * C_KV_HEADS) * C_DH + 2 * D_HEADS * D_DK + 2 * D_HEADS * D_DV
    cd_wt = jnp.swapaxes(cd_w_in, 1, 2)
    p1 = _matmul(h1, cd_wt, 0, cd_main, 512, tm_mm)
    ga1 = _matmul_tail(h1, cd_wt, 0, cd_main, 2 * D_GATE_RANK, tm_mm)
    mix_c = _gqa(p1, gqa_q_norm_g[0], gqa_k_norm_g[0], cos_c, sin_c, batch, seq, ctx_len)
    mix_d = _gla(p1, ga1, gla_w_a2[0], gla_b_a[0], gla_norm_g[0], batch, seq, ctx_len)
    x3 = _out_proj(mix_c, mix_d, cd_w_out, 0, x2, x2, rows_o.n_lat, mods5, 1, rows_o, rows_o.n_lat)
    out = _moe(x3, n_lat, norm_ffn_g[1], mods5, 1, router_w, router_b, moe_w1, moe_w3, moe_w2,
               final_norm_g, True, batch, seq, ctx_len)
    return out.reshape(batch, seq, d)
```

```python
import functools

import jax
import jax.numpy as jnp
from jax import lax
from jax.experimental import pallas as pl
from jax.experimental.pallas import tpu as pltpu

F32 = jnp.float32
BF16 = jnp.bfloat16
HI = lax.Precision.HIGHEST

GRID_W = 64
ROPE_THETA = 10000.0
NORM_EPS = 1e-6
A_HEADS, A_DK, A_DV = 8, 128, 128
B_HEADS, B_Q_LORA, B_KV_LORA, B_NOPE, B_ROPE, B_DV = 8, 512, 256, 128, 64, 128
C_HEADS, C_KV_HEADS, C_DH = 8, 2, 128
D_HEADS, D_DK, D_DV, D_GATE_RANK = 4, 128, 256, 16
GLA_TAU = 16.0
N_EXPERTS, N_GROUPS = 16, 4

TQ = 256
SCAN_C = 64
SCAN_HEADS = 4
GLA_HEADS = 2
SCAN_BLOCK = 2048
TE = 256
TROUTE = 512
TCOMB = 256
DMA_UNROLL = 8
LANES = 128
LOG2E = 1.4426950408889634
VMEM_MIB = 56


def _cp(sem):
    return pltpu.CompilerParams(dimension_semantics=sem, vmem_limit_bytes=VMEM_MIB * 1024 * 1024)


def _rms(x, g):
    return x * lax.rsqrt(jnp.mean(x * x, axis=-1, keepdims=True) + NORM_EPS) * g


def _rope(x, cos, sin):
    half = x.shape[-1] // 2
    swapped = jnp.concatenate([x[:, half:], x[:, :half]], axis=-1)
    return x * cos + swapped * sin


def _dot_nt(a, b):
    return lax.dot_general(a, b, (((1,), (1,)), ((), ())), preferred_element_type=F32)


def _dot_tn(a, b):
    return lax.dot_general(a, b, (((0,), (0,)), ((), ())), preferred_element_type=F32)


def _modvec_kernel(c_ref, w_ref, b_ref, o_ref):
    c = c_ref[...]
    a = c * jax.nn.sigmoid(c)
    a_hi = a.astype(BF16)
    a_lo = (a - a_hi.astype(F32)).astype(BF16)
    w = w_ref[...]
    w_hi = w.astype(BF16)
    w_lo = (w - w_hi.astype(F32)).astype(BF16)
    acc = jnp.dot(a_hi, w_hi, preferred_element_type=F32) + jnp.dot(a_lo, w_hi, preferred_element_type=F32)
    o_ref[...] = acc + jnp.dot(a_hi, w_lo, preferred_element_type=F32) + b_ref[...]


def _modvec(cvec, mod_w, mod_b):
    n_layers, d, n6 = mod_w.shape
    tn = min(1024, n6)
    return pl.pallas_call(
        _modvec_kernel,
        grid=(n_layers, n6 // tn),
        in_specs=[pl.BlockSpec((8, d), lambda l, j: (0, 0)),
                  pl.BlockSpec((None, d, tn), lambda l, j: (l, 0, j)),
                  pl.BlockSpec((None, 1, tn), lambda l, j: (l, 0, j))],
        out_specs=pl.BlockSpec((None, 8, tn), lambda l, j: (l, 0, j)),
        out_shape=jax.ShapeDtypeStruct((n_layers, 8, n6), F32),
        compiler_params=_cp(("parallel", "parallel")),
        name="modvec",
    )(cvec, mod_w, mod_b.reshape(n_layers, 1, n6))


class _Rows:
    def __init__(self, batch, seq, ctx_len, tm):
        assert seq % tm == 0 and (batch * ctx_len) % tm == 0
        self.tm = tm
        self.batch = batch
        self.per_batch = seq // tm
        self.n_lat = batch * seq // tm
        self.n_ctx = batch * ctx_len // tm
        self.n_all = self.n_lat + self.n_ctx

    def mod_row(self, i):
        return jnp.where(i < self.n_lat, i // self.per_batch, self.batch)


def _mod_spec(rows, layer, chunk, d):
    return pl.BlockSpec((None, None, None, 1, d), lambda i, *_: (layer, rows.mod_row(i), chunk, 0, 0))


def _norm_mod_kernel(xl_ref, xc_ref, g_ref, sh_ref, sc_ref, o_ref, *, n_lat):
    i = pl.program_id(0)

    def body(x_ref):
        y = _rms(x_ref[...], g_ref[...])
        o_ref[...] = (y * (1.0 + sc_ref[...]) + sh_ref[...]).astype(o_ref.dtype)

    @pl.when(i < n_lat)
    def _():
        body(xl_ref)

    @pl.when(i >= n_lat)
    def _():
        body(xc_ref)


def _norm_mod(x_lat, x_ctx, ctx_block0, g, mods5, layer, rows):
    d = x_lat.shape[-1]
    tm = rows.tm
    nl = rows.n_lat
    return pl.pallas_call(
        functools.partial(_norm_mod_kernel, n_lat=nl),
        grid=(rows.n_all,),
        in_specs=[pl.BlockSpec((tm, d), lambda i: (jnp.minimum(i, nl - 1), 0)),
                  pl.BlockSpec((tm, d), lambda i: (ctx_block0 + jnp.maximum(i - nl, 0), 0)),
                  pl.BlockSpec((1, d), lambda i: (0, 0)),
                  _mod_spec(rows, layer, 0, d),
                  _mod_spec(rows, layer, 1, d)],
        out_specs=pl.BlockSpec((tm, d), lambda i: (i, 0)),
        out_shape=jax.ShapeDtypeStruct((rows.n_all * tm, d), BF16),
        compiler_params=_cp(("parallel",)),
        name="norm_mod",
    )(x_lat, x_ctx, g.reshape(1, d), mods5, mods5)


def _mm_kernel(a_ref, wt_ref, o_ref):
    o_ref[...] = _dot_nt(a_ref[...], wt_ref[...].astype(BF16)).astype(o_ref.dtype)


def _matmul(a, wt3, layer, n_cols, tn, tm):
    m, k = a.shape
    return pl.pallas_call(
        _mm_kernel,
        grid=(m // tm, n_cols // tn),
        in_specs=[pl.BlockSpec((tm, k), lambda i, j: (i, 0)),
                  pl.BlockSpec((None, tn, k), lambda i, j: (layer, j, 0))],
        out_specs=pl.BlockSpec((tm, tn), lambda i, j: (i, j)),
        out_shape=jax.ShapeDtypeStruct((m, n_cols), BF16),
        compiler_params=_cp(("parallel", "arbitrary")),
        name="in_proj",
    )(a, wt3)


def _mm_tail_kernel(a_ref, wt_ref, o_ref):
    n = o_ref.shape[-1]
    acc = _dot_nt(a_ref[...], wt_ref[...].astype(BF16))
    o_ref[...] = acc[:, :n].astype(o_ref.dtype)


def _matmul_tail(a, wt3, layer, col0, n_cols, tm):
    m, k = a.shape
    lane = 128
    assert col0 % lane == 0 and n_cols <= lane and col0 + n_cols == wt3.shape[1]
    return pl.pallas_call(
        _mm_tail_kernel,
        grid=(m // tm,),
        in_specs=[pl.BlockSpec((tm, k), lambda i: (i, 0)),
                  pl.BlockSpec((None, lane, k), lambda i: (layer, col0 // lane, 0))],
        out_specs=pl.BlockSpec((tm, n_cols), lambda i: (i, 0)),
        out_shape=jax.ShapeDtypeStruct((m, n_cols), BF16),
        compiler_params=_cp(("parallel",)),
        name="in_proj_tail",
    )(a, wt3)


def _out_proj_kernel(ma_ref, mb_ref, w_hbm, xl_ref, xc_ref, g_ref, o_ref, w_s, stage, *, n_lat, widx):
    i = pl.program_id(0)
    ka = ma_ref.shape[-1]

    @pl.when(i == 0)
    def _():
        rows = stage.shape[0]
        for c in range(w_s.shape[0] // rows):
            pltpu.sync_copy(w_hbm.at[widx, pl.ds(c * rows, rows)], stage)
            w_s[c * rows:(c + 1) * rows, :] = stage[...].astype(BF16)

    acc = jnp.dot(ma_ref[...], w_s[:ka, :], preferred_element_type=F32)
    acc += jnp.dot(mb_ref[...], w_s[ka:, :], preferred_element_type=F32)
    upd = g_ref[...] * acc

    @pl.when(i < n_lat)
    def _():
        o_ref[...] = xl_ref[...] + upd

    @pl.when(i >= n_lat)
    def _():
        o_ref[...] = xc_ref[...] + upd


def _out_proj(mix_a, mix_b, w_out, widx, x_lat, x_ctx, ctx_block0, mods5, layer, rows, n_tiles):
    d = x_lat.shape[-1]
    ka, kb = mix_a.shape[-1], mix_b.shape[-1]
    tm = rows.tm
    nl = rows.n_lat
    stage_rows = min(512, ka + kb)
    return pl.pallas_call(
        functools.partial(_out_proj_kernel, n_lat=nl, widx=widx),
        grid=(n_tiles,),
        in_specs=[pl.BlockSpec((tm, ka), lambda i: (i, 0)),
                  pl.BlockSpec((tm, kb), lambda i: (i, 0)),
                  pl.BlockSpec(memory_space=pl.ANY),
                  pl.BlockSpec((tm, d), lambda i: (jnp.minimum(i, nl - 1), 0)),
                  pl.BlockSpec((tm, d), lambda i: (ctx_block0 + jnp.maximum(i - nl, 0), 0)),
                  _mod_spec(rows, layer, 2, d)],
        out_specs=pl.BlockSpec((tm, d), lambda i: (i, 0)),
        out_shape=jax.ShapeDtypeStruct((n_tiles * tm, d), F32),
        scratch_shapes=[pltpu.VMEM((ka + kb, d), BF16), pltpu.VMEM((stage_rows, d), F32)],
        compiler_params=_cp(("arbitrary",)),
        name="out_proj",
    )(mix_a, mix_b, w_out, x_lat, x_ctx, mods5)


def _tri(c, upper):
    r = lax.broadcasted_iota(jnp.int32, (c, c), 0)
    s = lax.broadcasted_iota(jnp.int32, (c, c), 1)
    return (s >= r) if upper else (r >= s)


def _scan_block(q, k, v, g, st, mask, forward):
    c = SCAN_C
    dk, dv = q.shape[-1], v.shape[-1]
    n = q.shape[0] // c
    mid, last = (c // 2 - 1, c - 1) if forward else (c // 2, 0)
    tri = jnp.broadcast_to(mask.astype(BF16)[None], (n, c, c))
    g3 = (g * LOG2E).reshape(n, c, dk)
    g_hi = g3.astype(BF16)
    g_lo = (g3 - g_hi.astype(F32)).astype(BF16)
    cum = (jnp.einsum('cts,csd->ctd', tri, g_hi, preferred_element_type=F32)
           + jnp.einsum('cts,csd->ctd', tri, g_lo, preferred_element_type=F32))
    m = cum[:, mid:mid + 1, :]
    tot = cum[:, last:last + 1, :]
    qe = (q.reshape(n, c, dk) * jnp.exp2(cum - m)).astype(BF16)
    ke = (k.reshape(n, c, dk) * jnp.exp2(m - cum)).astype(BF16)
    a = jnp.einsum('ctd,csd->cts', qe, ke, preferred_element_type=F32)
    a = jnp.where(mask[None], a, 0.0).astype(BF16)
    v3 = v.reshape(n, c, dv)
    o = jnp.einsum('cts,csv->ctv', a, v3, preferred_element_type=F32)
    u = jnp.einsum('csv,csd->cvd', v3, ke, preferred_element_type=F32)
    em = jnp.exp2(m)
    et = jnp.exp2(tot - m)
    states = [None] * n
    for ci in (range(n) if forward else reversed(range(n))):
        stp = st * em[ci]
        states[ci] = stp.astype(BF16)
        st = (stp + u[ci]) * et[ci]
    o = o + jnp.einsum('ctd,cvd->ctv', qe, jnp.stack(states), preferred_element_type=F32)
    return o.reshape(n * c, dv), st


def _scan_segments(segments, prep_f, prep_b, of_ref, ob_ref, dk, dv):
    low, up = _tri(SCAN_C, False), _tri(SCAN_C, True)
    carry = (jnp.zeros((dv, dk), F32), jnp.zeros((dv, dk), F32))
    for rows, off, seg in segments:
        rb_ = min(SCAN_BLOCK, rows)
        n = rows // rb_

        def body(i, carry, n=n, off=off, seg=seg, rb_=rb_):
            sf, sb = carry
            rf = pl.multiple_of(i * rb_, rb_)
            rb = pl.multiple_of((n - 1 - i) * rb_, rb_)
            q, k, v, g = prep_f(seg, rf, rb_)
            o, sf = _scan_block(q, k, v, g, sf, low, True)
            of_ref[pl.ds(off + rf, rb_), :] = o
            q, k, v, g = prep_b(seg, rb, rb_)
            o, sb = _scan_block(q, k, v, g, sb, up, False)
            ob_ref[pl.ds(off + rb, rb_), :] = o
            return sf, sb

        carry = lax.fori_loop(0, n, body, carry)


def _hgrn_kernel(ql, qc, f1l, f1c, f2l, f2c, vl, vc, gl, gc, lbf, lbb, ng, o_ref, of_s, ob_s, *, ctx_len, seq):
    rt = pl.program_id(2)
    n_ctx_tiles = ctx_len // TQ

    @pl.when(rt == 0)
    def _scan():
        refs = {0: (qc, f1c, f2c, vc), 1: (ql, f1l, f2l, vl)}
        scale = A_DK ** -0.5

        for hh in range(SCAN_HEADS):
            ks = slice(hh * A_DK, (hh + 1) * A_DK)
            vs = slice(hh * A_DV, (hh + 1) * A_DV)

            def prep(seg, r, nr, fi, lb_ref, ks=ks, vs=vs):
                x = refs[seg][0][pl.ds(r, nr), ks].astype(F32)
                q = x * jax.nn.sigmoid(x) * scale
                v = refs[seg][3][pl.ds(r, nr), vs]
                lb = lb_ref[:, ks]
                f = lb + (1.0 - lb) * jax.nn.sigmoid(refs[seg][fi][pl.ds(r, nr), ks].astype(F32))
                return q, 1.0 - f, v, jnp.log(f)

            _scan_segments(
                [(ctx_len, 0, 0), (seq, ctx_len, 1)],
                lambda seg, r, nr, prep=prep: prep(seg, r, nr, 1, lbf),
                lambda seg, r, nr, prep=prep: prep(seg, r, nr, 2, lbb),
                of_s.at[hh], ob_s.at[hh], A_DK, A_DV)

    r0 = pl.multiple_of(rt * TQ, TQ)
    ys = [_rms(of_s[hh, pl.ds(r0, TQ), :] + ob_s[hh, pl.ds(r0, TQ), :], ng[...]) for hh in range(SCAN_HEADS)]
    y = jnp.concatenate(ys, axis=-1)

    @pl.when(rt < n_ctx_tiles)
    def _():
        gate = gc[pl.ds(r0, TQ), :].astype(F32)
        o_ref[...] = (y * jax.nn.sigmoid(gate)).astype(o_ref.dtype)

    @pl.when(rt >= n_ctx_tiles)
    def _():
        gate = gl[pl.ds(pl.multiple_of(r0 - ctx_len, TQ), TQ), :].astype(F32)
        o_ref[...] = (y * jax.nn.sigmoid(gate)).astype(o_ref.dtype)


def _out_row_block(batch, seq, ctx_len):
    nct = ctx_len // TQ
    nlt = seq // TQ

    def f(b, rt):
        return jnp.where(rt < nct, batch * nlt + b * nct + rt, b * nlt + rt - nct)

    return f


def _hgrn(p, lb_f, lb_b, norm_g, batch, seq, ctx_len):
    h, dk, dv = A_HEADS, A_DK, A_DV
    hp = SCAN_HEADS
    hg = h // hp
    nct, nlt = ctx_len // TQ, seq // TQ
    ctx_blk0 = batch * seq // ctx_len
    row_block = _out_row_block(batch, seq, ctx_len)
    in_specs = []
    for kcol in range(5):
        in_specs.append(pl.BlockSpec((seq, hp * dk), lambda b, hh, rt, kcol=kcol: (b, kcol * hg + hh)))
        in_specs.append(pl.BlockSpec((ctx_len, hp * dk), lambda b, hh, rt, kcol=kcol: (ctx_blk0 + b, kcol * hg + hh)))
    vec = pl.BlockSpec((1, hp * dk), lambda b, hh, rt: (0, hh))
    in_specs += [vec, vec, pl.BlockSpec((1, dv), lambda b, hh, rt: (0, 0))]
    return pl.pallas_call(
        functools.partial(_hgrn_kernel, ctx_len=ctx_len, seq=seq),
        grid=(batch, hg, nct + nlt),
        in_specs=in_specs,
        out_specs=pl.BlockSpec((TQ, hp * dv), lambda b, hh, rt: (row_block(b, rt), hh)),
        out_shape=jax.ShapeDtypeStruct((batch * (seq + ctx_len), h * dv), BF16),
        scratch_shapes=[pltpu.VMEM((hp, seq + ctx_len, dv), F32), pltpu.VMEM((hp, seq + ctx_len, dv), F32)],
        compiler_params=_cp(("parallel", "parallel", "arbitrary")),
        name="hgrn_scan",
    )(*([p] * 10), lb_f.reshape(1, h * dk), lb_b.reshape(1, h * dk), norm_g.reshape(1, dv))


def _gla_kernel(ql, qc, kl, kc, vl, vc, gl, al, ac, wa, ba, ng, o_ref, of_s, ob_s, *, ctx_len, seq):
    rt = pl.program_id(2)

    @pl.when(rt == 0)
    def _scan():
        refs = {0: (qc, kc, vc, ac), 1: (ql, kl, vl, al)}
        scale = D_DK ** -0.5
        r16 = D_GATE_RANK

        for hh in range(GLA_HEADS):
            ks = slice(hh * D_DK, (hh + 1) * D_DK)
            vs = slice(hh * D_DV, (hh + 1) * D_DV)

            def prep(seg, r, nr, d, ks=ks, vs=vs):
                q = refs[seg][0][pl.ds(r, nr), ks].astype(F32) * scale
                k = refs[seg][1][pl.ds(r, nr), ks].astype(F32)
                v = refs[seg][2][pl.ds(r, nr), vs]
                a = refs[seg][3][pl.ds(r, nr), :].astype(F32)[:, d * r16:(d + 1) * r16]
                z = jnp.dot(a, wa[d, :, ks], preferred_element_type=F32, precision=HI) + ba[d, :, ks]
                g = (jnp.minimum(z, 0.0) - jnp.log(1.0 + jnp.exp(-jnp.abs(z)))) * (1.0 / GLA_TAU)
                return q, k, v, g

            _scan_segments(
                [(ctx_len, 0, 0), (seq, ctx_len, 1)],
                lambda seg, r, nr, prep=prep: prep(seg, r, nr, 0),
                lambda seg, r, nr, prep=prep: prep(seg, r, nr, 1),
                of_s.at[hh], ob_s.at[hh], D_DK, D_DV)

    r0 = pl.multiple_of(rt * TQ, TQ)
    ys = [_rms(of_s[hh, pl.ds(ctx_len + r0, TQ), :] + ob_s[hh, pl.ds(ctx_len + r0, TQ), :], ng[...])
          for hh in range(GLA_HEADS)]
    gate = gl[pl.ds(r0, TQ), :].astype(F32)
    o_ref[...] = (jnp.concatenate(ys, axis=-1) * gate * jax.nn.sigmoid(gate)).astype(o_ref.dtype)


def _gla(p, ga, w_a2, b_a, norm_g, batch, seq, ctx_len):
    h, dk, dv = D_HEADS, D_DK, D_DV
    hp = GLA_HEADS
    hg = h // hp
    nlt = seq // TQ
    ctx_blk0 = batch * seq // ctx_len
    wk, wv = hp * dk, hp * dv
    q0 = (C_HEADS + 2 * C_KV_HEADS) * C_DH // wk
    k0 = q0 + hg
    v0 = (k0 + hg) * wk // wv
    g0 = v0 + hg

    def pair(width, blk0):
        return [pl.BlockSpec((seq, width), lambda b, hh, rt: (b, blk0 + hh)),
                pl.BlockSpec((ctx_len, width), lambda b, hh, rt: (ctx_blk0 + b, blk0 + hh))]

    in_specs = pair(wk, q0) + pair(wk, k0) + pair(wv, v0)
    in_specs += [pl.BlockSpec((seq, wv), lambda b, hh, rt: (b, g0 + hh)),
                 pl.BlockSpec((seq, 2 * D_GATE_RANK), lambda b, hh, rt: (b, 0)),
                 pl.BlockSpec((ctx_len, 2 * D_GATE_RANK), lambda b, hh, rt: (ctx_blk0 + b, 0)),
                 pl.BlockSpec((2, D_GATE_RANK, wk), lambda b, hh, rt: (0, 0, hh)),
                 pl.BlockSpec((2, 1, wk), lambda b, hh, rt: (0, 0, hh)),
                 pl.BlockSpec((1, dv), lambda b, hh, rt: (0, 0))]
    return pl.pallas_call(
        functools.partial(_gla_kernel, ctx_len=ctx_len, seq=seq),
        grid=(batch, hg, nlt),
        in_specs=in_specs,
        out_specs=pl.BlockSpec((TQ, wv), lambda b, hh, rt: (b * nlt + rt, hh)),
        out_shape=jax.ShapeDtypeStruct((batch * seq, h * dv), BF16),
        scratch_shapes=[pltpu.VMEM((hp, seq + ctx_len, dv), F32), pltpu.VMEM((hp, seq + ctx_len, dv), F32)],
        compiler_params=_cp(("parallel", "parallel", "arbitrary")),
        name="gla_scan",
    )(p, p, p, p, p, p, p, ga, ga, w_a2, b_a.reshape(2, 1, h * dk), norm_g.reshape(1, dv))


def _softmax_pv(s, v):
    m = jnp.max(s, axis=-1, keepdims=True)
    p = jnp.exp2(s - m)
    l = jnp.sum(p, axis=-1, keepdims=True)
    return jnp.dot(p.astype(BF16), v, preferred_element_type=F32) / l


def _mla_kernel(ql_ref, kvl_ref, kvc_ref, krl_ref, krc_ref, wqn_ref, wqr_ref, wkv_ref, gq_ref, gkv_ref,
                cosq_ref, sinq_ref, cosk_ref, sink_ref, o_ref, k_s, v_s, *, ctx_len):
    qt = pl.program_id(1)
    n_ctx_tiles = ctx_len // TQ
    scale = (B_NOPE + B_ROPE) ** -0.5 * LOG2E
    dkv = B_NOPE + B_DV

    @pl.when(qt == 0)
    def _prep():
        kvc = _rms(kvc_ref[...].astype(F32), gkv_ref[...]).astype(BF16)
        kvl = _rms(kvl_ref[...].astype(F32), gkv_ref[...]).astype(BF16)
        kr_c = krc_ref[...]
        kr_l = _rope(krl_ref[...].astype(F32), cosk_ref[...], sink_ref[...]).astype(BF16)
        for h in range(B_HEADS):
            w = wkv_ref[:, h * dkv:(h + 1) * dkv].astype(BF16)
            up_c = jnp.dot(kvc, w, preferred_element_type=F32)
            k_s[h, 0:ctx_len, :] = jnp.concatenate([up_c[:, :B_NOPE].astype(BF16), kr_c], axis=-1)
            v_s[h, 0:ctx_len, :] = up_c[:, B_NOPE:].astype(BF16)
            up_l = jnp.dot(kvl, w, preferred_element_type=F32)
            k_s[h, ctx_len:, :] = jnp.concatenate([up_l[:, :B_NOPE].astype(BF16), kr_l], axis=-1)
            v_s[h, ctx_len:, :] = up_l[:, B_NOPE:].astype(BF16)

    xn = _rms(ql_ref[...].astype(F32), gq_ref[...]).astype(BF16)
    qn_all = jnp.dot(xn, wqn_ref[...].astype(BF16), preferred_element_type=F32) * scale
    qr_all = jnp.dot(xn, wqr_ref[...].astype(BF16), preferred_element_type=F32) * scale

    def heads(n_keys, rotate):
        outs = []
        for h in range(B_HEADS):
            qn = qn_all[:, h * B_NOPE:(h + 1) * B_NOPE]
            qr = qr_all[:, h * B_ROPE:(h + 1) * B_ROPE]
            if rotate:
                qr = _rope(qr, cosq_ref[...], sinq_ref[...])
            q = jnp.concatenate([qn, qr], axis=-1).astype(BF16)
            s = _dot_nt(q, k_s[h, 0:n_keys, :])
            outs.append(_softmax_pv(s, v_s[h, 0:n_keys, :]).astype(o_ref.dtype))
        o_ref[...] = jnp.concatenate(outs, axis=-1)

    @pl.when(qt < n_ctx_tiles)
    def _():
        heads(ctx_len, False)

    @pl.when(qt >= n_ctx_tiles)
    def _():
        heads(k_s.shape[1], True)


def _mla(p, kr, w_uq, w_ukv, gq, gkv, cos, sin, batch, seq, ctx_len):
    h = B_HEADS
    nct, nlt = ctx_len // TQ, seq // TQ
    ctx_blk0 = batch * seq // ctx_len
    row_block = _out_row_block(batch, seq, ctx_len)
    ql_blk = 5 * A_HEADS * A_DK // B_Q_LORA
    kv_blk = (5 * A_HEADS * A_DK + B_Q_LORA) // B_KV_LORA
    dq = B_NOPE + B_ROPE
    s_all = seq + ctx_len
    w3 = w_uq.reshape(B_Q_LORA, h, dq)
    wq_n = w3[:, :, :B_NOPE].reshape(B_Q_LORA, h * B_NOPE)
    wq_r = w3[:, :, B_NOPE:].reshape(B_Q_LORA, h * B_ROPE)
    in_specs = [
        pl.BlockSpec((TQ, B_Q_LORA), lambda b, qt: (row_block(b, qt), ql_blk)),
        pl.BlockSpec((seq, B_KV_LORA), lambda b, qt: (b, kv_blk)),
        pl.BlockSpec((ctx_len, B_KV_LORA), lambda b, qt: (ctx_blk0 + b, kv_blk)),
        pl.BlockSpec((seq, B_ROPE), lambda b, qt: (b, 0)),
        pl.BlockSpec((ctx_len, B_ROPE), lambda b, qt: (ctx_blk0 + b, 0)),
        pl.BlockSpec((B_Q_LORA, h * B_NOPE), lambda b, qt: (0, 0)),
        pl.BlockSpec((B_Q_LORA, h * B_ROPE), lambda b, qt: (0, 0)),
        pl.BlockSpec((B_KV_LORA, h * (B_NOPE + B_DV)), lambda b, qt: (0, 0)),
        pl.BlockSpec((1, B_Q_LORA), lambda b, qt: (0, 0)),
        pl.BlockSpec((1, B_KV_LORA), lambda b, qt: (0, 0)),
        pl.BlockSpec((TQ, B_ROPE), lambda b, qt: (jnp.maximum(qt - nct, 0), 0)),
        pl.BlockSpec((TQ, B_ROPE), lambda b, qt: (jnp.maximum(qt - nct, 0), 0)),
        pl.BlockSpec((seq, B_ROPE), lambda b, qt: (0, 0)),
        pl.BlockSpec((seq, B_ROPE), lambda b, qt: (0, 0)),
    ]
    return pl.pallas_call(
        functools.partial(_mla_kernel, ctx_len=ctx_len),
        grid=(batch, nct + nlt),
        in_specs=in_specs,
        out_specs=pl.BlockSpec((TQ, h * B_DV), lambda b, qt: (row_block(b, qt), 0)),
        out_shape=jax.ShapeDtypeStruct((batch * s_all, h * B_DV), BF16),
        scratch_shapes=[pltpu.VMEM((h, s_all, dq), BF16), pltpu.VMEM((h, s_all, B_DV), BF16)],
        compiler_params=_cp(("parallel", "arbitrary")),
        name="mla_attn",
    )(p, p, p, kr, kr, wq_n, wq_r, w_ukv, gq.reshape(1, -1), gkv.reshape(1, -1), cos, sin, cos, sin)


def _gqa_kernel(q_ref, kl_ref, kc_ref, vl_ref, vc_ref, gq_ref, gk_ref, cosq_ref, sinq_ref, cosk_ref, sink_ref,
                o_ref, k_s, v_s, *, ctx_len):
    qt = pl.program_id(1)
    scale = C_DH ** -0.5 * LOG2E
    dh = C_DH
    grp = C_HEADS // C_KV_HEADS

    @pl.when(qt == 0)
    def _prep():
        for kh in range(C_KV_HEADS):
            cols = slice(kh * dh, (kh + 1) * dh)
            k_s[kh, 0:ctx_len, :] = _rms(kc_ref[:, cols].astype(F32), gk_ref[...]).astype(BF16)
            kl = _rms(kl_ref[:, cols].astype(F32), gk_ref[...])
            k_s[kh, ctx_len:, :] = _rope(kl, cosk_ref[...], sink_ref[...]).astype(BF16)
            v_s[kh, 0:ctx_len, :] = vc_ref[:, cols]
            v_s[kh, ctx_len:, :] = vl_ref[:, cols]

    outs = []
    for hq in range(C_HEADS):
        kh = hq // grp
        q = _rms(q_ref[:, hq * dh:(hq + 1) * dh].astype(F32), gq_ref[...])
        q = _rope(q, cosq_ref[...], sinq_ref[...]) * scale
        s = _dot_nt(q.astype(BF16), k_s[kh])
        outs.append(_softmax_pv(s, v_s[kh]).astype(o_ref.dtype))
    o_ref[...] = jnp.concatenate(outs, axis=-1)


def _gqa(p, gq, gk, cos, sin, batch, seq, ctx_len):
    kvh, dh = C_KV_HEADS, C_DH
    nlt = seq // TQ
    ctx_blk0 = batch * seq // ctx_len
    wq, wkv = C_HEADS * dh, kvh * dh
    k0 = wq // wkv
    v0 = k0 + 1
    s_all = seq + ctx_len
    in_specs = [
        pl.BlockSpec((TQ, wq), lambda b, qt: (b * nlt + qt, 0)),
        pl.BlockSpec((seq, wkv), lambda b, qt: (b, k0)),
        pl.BlockSpec((ctx_len, wkv), lambda b, qt: (ctx_blk0 + b, k0)),
        pl.BlockSpec((seq, wkv), lambda b, qt: (b, v0)),
        pl.BlockSpec((ctx_len, wkv), lambda b, qt: (ctx_blk0 + b, v0)),
        pl.BlockSpec((1, dh), lambda b, qt: (0, 0)),
        pl.BlockSpec((1, dh), lambda b, qt: (0, 0)),
        pl.BlockSpec((TQ, dh), lambda b, qt: (qt, 0)),
        pl.BlockSpec((TQ, dh), lambda b, qt: (qt, 0)),
        pl.BlockSpec((seq, dh), lambda b, qt: (0, 0)),
        pl.BlockSpec((seq, dh), lambda b, qt: (0, 0)),
    ]
    return pl.pallas_call(
        functools.partial(_gqa_kernel, ctx_len=ctx_len),
        grid=(batch, nlt),
        in_specs=in_specs,
        out_specs=pl.BlockSpec((TQ, wq), lambda b, qt: (b * nlt + qt, 0)),
        out_shape=jax.ShapeDtypeStruct((batch * seq, wq), BF16),
        scratch_shapes=[pltpu.VMEM((kvh, s_all, dh), BF16), pltpu.VMEM((kvh, s_all, dh), BF16)],
        compiler_params=_cp(("parallel", "arbitrary")),
        name="gqa_attn",
    )(p, p, p, p, p, gq.reshape(1, dh), gk.reshape(1, dh), cos, sin, cos, sin)


def _router_kernel(x_ref, g_ref, sh_ref, sc_ref, rw_ref, rb_ref, h_ref, ri_ref, rf_ref, cnt_ref, base_s):
    i = pl.program_id(0)
    tm = x_ref.shape[0]
    ne = N_EXPERTS
    per = ne // N_GROUPS

    @pl.when(i == 0)
    def _():
        base_s[...] = jnp.zeros_like(base_s)

    h = _rms(x_ref[...], g_ref[...]) * (1.0 + sc_ref[...]) + sh_ref[...]
    nch = h.shape[1] // LANES
    for j in range(nch):
        h_ref[pl.ds(j, tm, stride=nch), :] = h[:, j * LANES:(j + 1) * LANES]
    h_hi = h.astype(BF16)
    h_lo = (h - h_hi.astype(F32)).astype(BF16)
    rw = rw_ref[...]
    w_hi = rw.astype(BF16)
    w_lo = (rw - w_hi.astype(F32)).astype(BF16)
    hw = jnp.dot(h_hi, jnp.concatenate([w_hi, w_lo], axis=1), preferred_element_type=F32)
    logits = hw[:, :ne] + hw[:, ne:] + jnp.dot(h_lo, w_hi, preferred_element_type=F32)
    scores = jax.nn.sigmoid(logits)
    sel = scores + rb_ref[...]
    lane = lax.broadcasted_iota(jnp.int32, (tm, ne), 1).astype(F32)
    neg = -jnp.inf
    big = float(ne)

    def top2(vals):
        m1 = jnp.max(vals, axis=1, keepdims=True)
        i1 = jnp.min(jnp.where(vals == m1, lane, big), axis=1, keepdims=True)
        rest = jnp.where(lane == i1, neg, vals)
        m2 = jnp.max(rest, axis=1, keepdims=True)
        i2 = jnp.min(jnp.where(rest == m2, lane, big), axis=1, keepdims=True)
        return m1 + m2, i1, i2

    best, e1, e2 = None, None, None
    for grp in range(N_GROUPS):
        in_grp = jnp.logical_and(lane >= float(grp * per), lane < float((grp + 1) * per))
        gsum, i1, i2 = top2(jnp.where(in_grp, sel, neg))
        if grp == 0:
            best, e1, e2 = gsum, i1, i2
        else:
            better = gsum > best
            best = jnp.where(better, gsum, best)
            e1 = jnp.where(better, i1, e1)
            e2 = jnp.where(better, i2, e2)

    hot1 = lane == e1
    hot2 = lane == e2
    w1 = jnp.sum(jnp.where(hot1, scores, 0.0), axis=1, keepdims=True)
    w2 = jnp.sum(jnp.where(hot2, scores, 0.0), axis=1, keepdims=True)
    wsum = w1 + w2
    assign = jnp.logical_or(hot1, hot2)
    r = lax.broadcasted_iota(jnp.int32, (tm, tm), 0)
    c = lax.broadcasted_iota(jnp.int32, (tm, tm), 1)
    before = (c < r).astype(BF16)
    excl = jnp.dot(before, assign.astype(BF16), preferred_element_type=F32) + base_s[...]
    rank1 = jnp.sum(jnp.where(hot1, excl, 0.0), axis=1, keepdims=True)
    rank2 = jnp.sum(jnp.where(hot2, excl, 0.0), axis=1, keepdims=True)
    base_s[...] = base_s[...] + jnp.sum(assign.astype(F32), axis=0, keepdims=True)

    l128 = lax.broadcasted_iota(jnp.int32, (tm, 128), 1)
    ri = jnp.where(l128 == 0, e1, jnp.where(l128 == 1, e2, jnp.where(l128 == 2, rank1, jnp.where(l128 == 3, rank2, 0.0))))
    ri_ref[...] = ri.T[0:8, :].astype(jnp.int32)
    rf_ref[...] = jnp.where(l128 == 0, w1 / wsum, jnp.where(l128 == 1, w2 / wsum, 0.0))
    cnt_ref[...] = jnp.broadcast_to(base_s[...], cnt_ref.shape)


def _router(x_all, g, mods5, layer, router_w, router_b, rows, n_tiles):
    d = x_all.shape[-1]
    tm = rows.tm
    n = n_tiles * tm
    ne = N_EXPERTS
    return pl.pallas_call(
        _router_kernel,
        grid=(n_tiles,),
        in_specs=[pl.BlockSpec((tm, d), lambda i: (i, 0)),
                  pl.BlockSpec((1, d), lambda i: (0, 0)),
                  _mod_spec(rows, layer, 3, d),
                  _mod_spec(rows, layer, 4, d),
                  pl.BlockSpec((d, ne), lambda i: (0, 0)),
                  pl.BlockSpec((1, ne), lambda i: (0, 0))],
        out_specs=[pl.BlockSpec((tm * (d // LANES), LANES), lambda i: (i, 0)),
                   pl.BlockSpec((8, tm), lambda i: (0, i)),
                   pl.BlockSpec((tm, 128), lambda i: (i, 0)),
                   pl.BlockSpec((8, ne), lambda i: (0, 0))],
        out_shape=[jax.ShapeDtypeStruct((n * (d // LANES), LANES), F32),
                   jax.ShapeDtypeStruct((8, n), jnp.int32),
                   jax.ShapeDtypeStruct((n, 128), F32),
                   jax.ShapeDtypeStruct((8, ne), F32)],
        scratch_shapes=[pltpu.VMEM((1, ne), F32)],
        compiler_params=_cp(("arbitrary",)),
        name="moe_router",
    )(x_all, g.reshape(1, d), mods5, mods5, router_w, router_b.reshape(1, ne))


def _expert_kernel(te_ref, nx_ref, nu_ref, pad_ref, dest_ref, h_hbm, w1_hbm, w3_hbm, w2_hbm, y_ref, xbuf, sem,
                   w1_s, w3_s, w2_s, wf1, wf3, wf2, wsem, wslot_ref, src_ref, *, n_tok, layer):
    r = pl.program_id(0)
    n_used = nu_ref[0]
    active = r < n_used
    changed = jnp.logical_or(r == 0, te_ref[r] != te_ref[jnp.maximum(r - 1, 0)])
    slot = lax.rem(r, 2)
    nch = xbuf.shape[1] // TE

    def weight_copies(e, ws):
        return (pltpu.make_async_copy(w1_hbm.at[layer, e], wf1.at[ws], wsem.at[ws]),
                pltpu.make_async_copy(w3_hbm.at[layer, e], wf3.at[ws], wsem.at[ws]),
                pltpu.make_async_copy(w2_hbm.at[layer, e], wf2.at[ws], wsem.at[ws]))

    @pl.when(r == 0)
    def _():
        wslot_ref[0] = 0
        for cp in weight_copies(te_ref[0], 0):
            cp.start()

        def clear(i, carry):
            src_ref[i] = 0
            return carry

        for e in range(N_EXPERTS):
            lax.fori_loop(pad_ref[e], pad_ref[N_EXPERTS + e], clear, 0)

        def invert(i, carry):
            src_ref[dest_ref[i]] = i
            src_ref[dest_ref[n_tok + i]] = i
            return carry

        lax.fori_loop(0, n_tok, invert, 0, unroll=DMA_UNROLL)

    def gather(tile, dst_slot):
        base = tile * TE
        for t in range(TE):
            row0 = pl.multiple_of(src_ref[base + t] * nch, nch)
            pltpu.make_async_copy(h_hbm.at[pl.ds(row0, nch)], xbuf.at[dst_slot, pl.ds(t * nch, nch)],
                                  sem.at[dst_slot]).start(priority=0)

    @pl.when(jnp.logical_and(r == 0, active))
    def _():
        gather(0, 0)

    @pl.when(r + 1 < n_used)
    def _():
        gather(r + 1, 1 - slot)

    @pl.when(jnp.logical_and(active, changed))
    def _():
        ws = wslot_ref[0]
        for cp in weight_copies(te_ref[r], ws):
            cp.wait()

        @pl.when(nx_ref[r] != te_ref[r])
        def _():
            for cp in weight_copies(nx_ref[r], 1 - ws):
                cp.start(priority=1)

        w1_s[...] = wf1[ws].astype(BF16)
        w3_s[...] = wf3[ws].astype(BF16)
        w2_s[...] = wf2[ws].astype(BF16)
        wslot_ref[0] = 1 - ws

    @pl.when(active)
    def _():
        pltpu.make_async_copy(h_hbm.at[pl.ds(0, TE * nch)], xbuf.at[slot], sem.at[slot]).wait()
        x = jnp.concatenate([xbuf[slot, pl.ds(j, TE, stride=nch), :] for j in range(nch)], axis=1).astype(BF16)
        a = jnp.dot(x, w1_s[...], preferred_element_type=F32)
        b = jnp.dot(x, w3_s[...], preferred_element_type=F32)
        hid = (a * jax.nn.sigmoid(a) * b).astype(BF16)
        y_ref[...] = jnp.dot(hid, w2_s[...], preferred_element_type=F32)

    @pl.when(jnp.logical_not(active))
    def _():
        y_ref[...] = jnp.zeros_like(y_ref)


def _experts(tile_expert, next_expert, n_used, pad, dest, hp, w1, w3, w2, layer, p_max):
    d = w1.shape[-2]
    nch = d // LANES
    n_tok = hp.shape[0] // nch
    f = w1.shape[-1]
    any_spec = pl.BlockSpec(memory_space=pl.ANY)
    grid_spec = pltpu.PrefetchScalarGridSpec(
        num_scalar_prefetch=5,
        grid=(p_max // TE,),
        in_specs=[any_spec, any_spec, any_spec, any_spec],
        out_specs=pl.BlockSpec((TE, d), lambda r, te, nx, nu, pd, sr: (r, 0)),
        scratch_shapes=[pltpu.VMEM((2, TE * nch, LANES), F32), pltpu.SemaphoreType.DMA((2,)),
                        pltpu.VMEM((d, f), BF16), pltpu.VMEM((d, f), BF16), pltpu.VMEM((f, d), BF16),
                        pltpu.VMEM((2, d, f), F32), pltpu.VMEM((2, d, f), F32), pltpu.VMEM((2, f, d), F32),
                        pltpu.SemaphoreType.DMA((2,)), pltpu.SMEM((1,), jnp.int32),
                        pltpu.SMEM((p_max,), jnp.int32)],
    )
    return pl.pallas_call(
        functools.partial(_expert_kernel, n_tok=n_tok, layer=layer),
        grid_spec=grid_spec,
        out_shape=jax.ShapeDtypeStruct((p_max, d), F32),
        compiler_params=_cp(("arbitrary",)),
        name="moe_experts",
    )(tile_expert, next_expert, n_used, pad, dest, hp, w1, w3, w2)


def _combine_kernel(dest_ref, x_ref, rf_ref, g_ref, pg_ref, psh_ref, psc_ref, ys_hbm, *rest, tm, n_tok, n_tiles, final):
    o_ref = rest[0]
    buf, sem = rest[-2], rest[-1]
    i = pl.program_id(0)
    slot = lax.rem(i, 2)

    def gather(tile, dst_slot):
        base = tile * tm
        for t in range(tm):
            pltpu.make_async_copy(ys_hbm.at[pl.ds(dest_ref[base + t], 1)], buf.at[dst_slot, 0, pl.ds(t, 1)],
                                  sem.at[dst_slot]).start(priority=0)
            pltpu.make_async_copy(ys_hbm.at[pl.ds(dest_ref[n_tok + base + t], 1)], buf.at[dst_slot, 1, pl.ds(t, 1)],
                                  sem.at[dst_slot]).start(priority=1)

    @pl.when(i == 0)
    def _():
        gather(0, 0)

    @pl.when(i + 1 < n_tiles)
    def _():
        gather(i + 1, 1 - slot)

    pltpu.make_async_copy(ys_hbm.at[pl.ds(0, tm)], buf.at[slot, 0], sem.at[slot]).wait()
    pltpu.make_async_copy(ys_hbm.at[pl.ds(0, tm)], buf.at[slot, 1], sem.at[slot]).wait()
    w = rf_ref[...]
    y = w[:, 0:1] * buf[slot, 0] + w[:, 1:2] * buf[slot, 1]
    x2 = x_ref[...] + g_ref[...] * y
    z = _rms(x2, pg_ref[...])
    if final:
        o_ref[...] = z
    else:
        o_ref[...] = x2
        rest[1][...] = (z * (1.0 + psc_ref[...]) + psh_ref[...]).astype(rest[1].dtype)


def _combine(dest, x_all, rf, mods5, layer, post_g, ys, rows, n_tiles, n_tok_total, final):
    d = x_all.shape[-1]
    tm = rows.tm
    nxt = min(layer + 1, mods5.shape[0] - 1)
    row_spec = pl.BlockSpec((tm, d), lambda i, dr: (i, 0))
    grid_spec = pltpu.PrefetchScalarGridSpec(
        num_scalar_prefetch=1,
        grid=(n_tiles,),
        in_specs=[row_spec,
                  pl.BlockSpec((tm, 128), lambda i, dr: (i, 0)),
                  _mod_spec(rows, layer, 5, d),
                  pl.BlockSpec((1, d), lambda i, dr: (0, 0)),
                  _mod_spec(rows, nxt, 0, d),
                  _mod_spec(rows, nxt, 1, d),
                  pl.BlockSpec(memory_space=pl.ANY)],
        out_specs=row_spec if final else [row_spec, row_spec],
        scratch_shapes=[pltpu.VMEM((2, 2, tm, d), F32), pltpu.SemaphoreType.DMA((2,))],
    )
    stream = jax.ShapeDtypeStruct((n_tiles * tm, d), F32)
    return pl.pallas_call(
        functools.partial(_combine_kernel, tm=tm, n_tok=n_tok_total, n_tiles=n_tiles, final=final),
        grid_spec=grid_spec,
        out_shape=stream if final else [stream, jax.ShapeDtypeStruct((n_tiles * tm, d), BF16)],
        compiler_params=_cp(("arbitrary",)),
        name="moe_combine",
    )(dest, x_all, rf, mods5, post_g.reshape(1, d), mods5, mods5, ys)


def _moe(x_all, n_tok, norm_g, mods5, layer, router_w, router_b, w1, w3, w2, final_g, final, batch, seq, ctx_len):
    rows_r = _Rows(batch, seq, ctx_len, TROUTE)
    rows_c = _Rows(batch, seq, ctx_len, TCOMB)
    h, ri, rf, cnt = _router(x_all, norm_g, mods5, layer, router_w, router_b, rows_r, n_tok // TROUTE)
    counts = cnt[0].astype(jnp.int32)
    padded = ((counts + TE - 1) // TE) * TE
    ends = jnp.cumsum(padded)
    starts = ends - padded
    e1, e2, r1, r2 = ri[0], ri[1], ri[2], ri[3]
    dest = jnp.concatenate([starts[e1] + r1, starts[e2] + r2]).astype(jnp.int32)
    p_max = 2 * n_tok + N_EXPERTS * TE
    n_tiles = p_max // TE
    n_used = (ends[-1] // TE).astype(jnp.int32)
    tile_start = jnp.arange(n_tiles, dtype=jnp.int32) * TE
    tile_expert = jnp.sum((tile_start[:, None] >= ends[None, :]).astype(jnp.int32), axis=1)
    last_expert = jnp.sum((jnp.maximum(ends[-1] - 1, 0) >= ends).astype(jnp.int32))
    tile_expert = jnp.minimum(jnp.where(tile_start < ends[-1], tile_expert, last_expert), N_EXPERTS - 1).astype(jnp.int32)
    eid = jnp.arange(N_EXPERTS, dtype=jnp.int32)
    later_used = jnp.logical_and(eid[None, :] > eid[:, None], (padded > 0)[None, :])
    next_used = jnp.min(jnp.where(later_used, eid[None, :], N_EXPERTS), axis=1)
    next_used = jnp.where(next_used == N_EXPERTS, eid, next_used)
    next_expert = jnp.sum(jnp.where(tile_expert[:, None] == eid[None, :], next_used[None, :], 0), axis=1).astype(jnp.int32)
    pad = jnp.concatenate([starts + counts, ends]).astype(jnp.int32)
    ys = _experts(tile_expert, next_expert, n_used.reshape(1), pad, dest, h, w1, w3, w2, layer, p_max)
    return _combine(dest, x_all, rf, mods5, layer, final_g, ys, rows_c, n_tok // TCOMB, n_tok, final)


def _rope_tables(t_len, d_rope):
    rows = t_len // GRID_W
    quarter = d_rope // 4
    freqs = ROPE_THETA ** (-jnp.arange(quarter, dtype=F32) / quarter)
    row = jnp.repeat(jnp.arange(rows, dtype=F32), GRID_W)
    col = jnp.tile(jnp.arange(GRID_W, dtype=F32), rows)
    ang = jnp.concatenate([row[:, None] * freqs, col[:, None] * freqs], axis=-1)
    cos, sin = jnp.cos(ang), jnp.sin(ang)
    return jnp.concatenate([cos, cos], axis=-1), jnp.concatenate([-sin, sin], axis=-1)


def kernel(x, c, ctx, c_ctx, mod_w, mod_b, norm_attn_g, norm_ffn_g, final_norm_g, ab_w_in, ab_w_out, hgrn_lb_logits, hgrn_norm_g, mla_q_norm_g, mla_w_uq, mla_kv_norm_g, mla_w_ukv, cd_w_in, cd_w_out, gqa_q_norm_g, gqa_k_norm_g, gla_w_a2, gla_b_a, gla_norm_g, router_w, router_b, moe_w1, moe_w3, moe_w2):
    batch, seq, d = x.shape
    ctx_len = ctx.shape[1]
    n_lat, n_ctx = batch * seq, batch * ctx_len
    assert ctx_len % TQ == 0 and seq % TQ == 0 and seq % ctx_len == 0 and batch < 8
    tm = min(1024, seq, n_ctx)
    rows = _Rows(batch, seq, ctx_len, tm)

    cvec = jnp.concatenate([c, c_ctx[None, :], jnp.zeros((8 - batch - 1, d), F32)], axis=0)
    mods = _modvec(cvec, mod_w, mod_b)
    mods5 = mods.reshape(mods.shape[0], 8, 6, 1, d)

    cos_b, sin_b = _rope_tables(seq, B_ROPE)
    cos_c, sin_c = _rope_tables(seq, C_DH)
    lb = jnp.cumsum(jax.nn.softmax(hgrn_lb_logits.astype(F32), axis=1), axis=1)

    x_lat = x.reshape(n_lat, d)
    x_ctx = ctx.reshape(n_ctx, d)

    h0 = _norm_mod(x_lat, x_ctx, 0, norm_attn_g[0], mods5, 0, rows)
    ab_main = 5 * A_HEADS * A_DK + B_Q_LORA + B_KV_LORA
    tm_mm = next(t for t in (2304, 2048, 1536, 1024, 512, 256) if (n_lat + n_ctx) % t == 0)
    ab_wt = jnp.swapaxes(ab_w_in, 1, 2)
    p0 = _matmul(h0, ab_wt, 0, ab_main, 256, tm_mm)
    kr0 = _matmul_tail(h0, ab_wt, 0, ab_main, B_ROPE, tm_mm)
    mix_a = _hgrn(p0, lb[0, 0], lb[1, 0], hgrn_norm_g[0], batch, seq, ctx_len)
    mix_b = _mla(p0, kr0, mla_w_uq[0], mla_w_ukv[0], mla_q_norm_g[0], mla_kv_norm_g[0], cos_b, sin_b,
                 batch, seq, ctx_len)
    rows_o = _Rows(batch, seq, ctx_len, min(512, tm))
    x1 = _out_proj(mix_a, mix_b, ab_w_out, 0, x_lat, x_ctx, 0, mods5, 0, rows_o, rows_o.n_all)
    x2, h1 = _moe(x1, n_lat + n_ctx, norm_ffn_g[0], mods5, 0, router_w, router_b, moe_w1, moe_w3, moe_w2,
                  norm_attn_g[1], False, batch, seq, ctx_len)

    cd_main = (C_HEADS + 2 * C_KV_HEADS) * C_DH + 2 * D_HEADS * D_DK + 2 * D_HEADS * D_DV
    cd_wt = jnp.swapaxes(cd_w_in, 1, 2)
    p1 = _matmul(h1, cd_wt, 0, cd_main, 512, tm_mm)
    ga1 = _matmul_tail(h1, cd_wt, 0, cd_main, 2 * D_GATE_RANK, tm_mm)
    mix_c = _gqa(p1, gqa_q_norm_g[0], gqa_k_norm_g[0], cos_c, sin_c, batch, seq, ctx_len)
    mix_d = _gla(p1, ga1, gla_w_a2[0], gla_b_a[0], gla_norm_g[0], batch, seq, ctx_len)
    x3 = _out_proj(mix_c, mix_d, cd_w_out, 0, x2, x2, rows_o.n_lat, mods5, 1, rows_o, rows_o.n_lat)
    out = _moe(x3, n_lat, norm_ffn_g[1], mods5, 1, router_w, router_b, moe_w1, moe_w3, moe_w2,
               final_norm_g, True, batch, seq, ctx_len)
    return out.reshape(batch, seq, d)
```

```python
import functools

import jax
import jax.numpy as jnp
from jax import lax
from jax.experimental import pallas as pl
from jax.experimental.pallas import tpu as pltpu

F32 = jnp.float32
BF16 = jnp.bfloat16
HI = lax.Precision.HIGHEST

GRID_W = 64
ROPE_THETA = 10000.0
NORM_EPS = 1e-6
A_HEADS, A_DK, A_DV = 8, 128, 128
B_HEADS, B_Q_LORA, B_KV_LORA, B_NOPE, B_ROPE, B_DV = 8, 512, 256, 128, 64, 128
C_HEADS, C_KV_HEADS, C_DH = 8, 2, 128
D_HEADS, D_DK, D_DV, D_GATE_RANK = 4, 128, 256, 16
GLA_TAU = 16.0
N_EXPERTS, N_GROUPS = 16, 4

TQ = 256
SCAN_C = 64
SCAN_HEADS = 4
GLA_HEADS = 2
SCAN_BLOCK = 2048
TE = 256
TROUTE = 512
TCOMB = 256
DMA_UNROLL = 8
LANES = 128
LOG2E = 1.4426950408889634
VMEM_MIB = 56


def _cp(sem):
    return pltpu.CompilerParams(dimension_semantics=sem, vmem_limit_bytes=VMEM_MIB * 1024 * 1024)


def _rms(x, g):
    return x * lax.rsqrt(jnp.mean(x * x, axis=-1, keepdims=True) + NORM_EPS) * g


def _rope(x, cos, sin):
    half = x.shape[-1] // 2
    swapped = jnp.concatenate([x[:, half:], x[:, :half]], axis=-1)
    return x * cos + swapped * sin


def _dot_nt(a, b):
    return lax.dot_general(a, b, (((1,), (1,)), ((), ())), preferred_element_type=F32)


def _dot_tn(a, b):
    return lax.dot_general(a, b, (((0,), (0,)), ((), ())), preferred_element_type=F32)


def _modvec_kernel(c_ref, w_ref, b_ref, o_ref):
    c = c_ref[...]
    a = c * jax.nn.sigmoid(c)
    a_hi = a.astype(BF16)
    a_lo = (a - a_hi.astype(F32)).astype(BF16)
    w = w_ref[...]
    w_hi = w.astype(BF16)
    w_lo = (w - w_hi.astype(F32)).astype(BF16)
    acc = jnp.dot(a_hi, w_hi, preferred_element_type=F32) + jnp.dot(a_lo, w_hi, preferred_element_type=F32)
    o_ref[...] = acc + jnp.dot(a_hi, w_lo, preferred_element_type=F32) + b_ref[...]


def _modvec(cvec, mod_w, mod_b):
    n_layers, d, n6 = mod_w.shape
    tn = min(1024, n6)
    return pl.pallas_call(
        _modvec_kernel,
        grid=(n_layers, n6 // tn),
        in_specs=[pl.BlockSpec((8, d), lambda l, j: (0, 0)),
                  pl.BlockSpec((None, d, tn), lambda l, j: (l, 0, j)),
                  pl.BlockSpec((None, 1, tn), lambda l, j: (l, 0, j))],
        out_specs=pl.BlockSpec((None, 8, tn), lambda l, j: (l, 0, j)),
        out_shape=jax.ShapeDtypeStruct((n_layers, 8, n6), F32),
        compiler_params=_cp(("parallel", "parallel")),
        name="modvec",
    )(cvec, mod_w, mod_b.reshape(n_layers, 1, n6))


class _Rows:
    def __init__(self, batch, seq, ctx_len, tm):
        assert seq % tm == 0 and (batch * ctx_len) % tm == 0
        self.tm = tm
        self.batch = batch
        self.per_batch = seq // tm
        self.n_lat = batch * seq // tm
        self.n_ctx = batch * ctx_len // tm
        self.n_all = self.n_lat + self.n_ctx

    def mod_row(self, i):
        return jnp.where(i < self.n_lat, i // self.per_batch, self.batch)


def _mod_spec(rows, layer, chunk, d):
    return pl.BlockSpec((None, None, None, 1, d), lambda i, *_: (layer, rows.mod_row(i), chunk, 0, 0))


def _norm_mod_kernel(xl_ref, xc_ref, g_ref, sh_ref, sc_ref, o_ref, *, n_lat):
    i = pl.program_id(0)

    def body(x_ref):
        y = _rms(x_ref[...], g_ref[...])
        o_ref[...] = (y * (1.0 + sc_ref[...]) + sh_ref[...]).astype(o_ref.dtype)

    @pl.when(i < n_lat)
    def _():
        body(xl_ref)

    @pl.when(i >= n_lat)
    def _():
        body(xc_ref)


def _norm_mod(x_lat, x_ctx, ctx_block0, g, mods5, layer, rows):
    d = x_lat.shape[-1]
    tm = rows.tm
    nl = rows.n_lat
    return pl.pallas_call(
        functools.partial(_norm_mod_kernel, n_lat=nl),
        grid=(rows.n_all,),
        in_specs=[pl.BlockSpec((tm, d), lambda i: (jnp.minimum(i, nl - 1), 0)),
                  pl.BlockSpec((tm, d), lambda i: (ctx_block0 + jnp.maximum(i - nl, 0), 0)),
                  pl.BlockSpec((1, d), lambda i: (0, 0)),
                  _mod_spec(rows, layer, 0, d),
                  _mod_spec(rows, layer, 1, d)],
        out_specs=pl.BlockSpec((tm, d), lambda i: (i, 0)),
        out_shape=jax.ShapeDtypeStruct((rows.n_all * tm, d), BF16),
        compiler_params=_cp(("parallel",)),
        name="norm_mod",
    )(x_lat, x_ctx, g.reshape(1, d), mods5, mods5)


def _mm_kernel(a_ref, wt_ref, o_ref):
    o_ref[...] = _dot_nt(a_ref[...], wt_ref[...].astype(BF16)).astype(o_ref.dtype)


def _matmul(a, wt3, layer, n_cols, tn, tm):
    m, k = a.shape
    return pl.pallas_call(
        _mm_kernel,
        grid=(m // tm, n_cols // tn),
        in_specs=[pl.BlockSpec((tm, k), lambda i, j: (i, 0)),
                  pl.BlockSpec((None, tn, k), lambda i, j: (layer, j, 0))],
        out_specs=pl.BlockSpec((tm, tn), lambda i, j: (i, j)),
        out_shape=jax.ShapeDtypeStruct((m, n_cols), BF16),
        compiler_params=_cp(("parallel", "arbitrary")),
        name="in_proj",
    )(a, wt3)


def _mm_tail_kernel(a_ref, wt_ref, o_ref):
    n = o_ref.shape[-1]
    acc = _dot_nt(a_ref[...], wt_ref[...].astype(BF16))
    o_ref[...] = acc[:, :n].astype(o_ref.dtype)


def _matmul_tail(a, wt3, layer, col0, n_cols, tm):
    m, k = a.shape
    lane = 128
    assert col0 % lane == 0 and n_cols <= lane and col0 + n_cols == wt3.shape[1]
    return pl.pallas_call(
        _mm_tail_kernel,
        grid=(m // tm,),
        in_specs=[pl.BlockSpec((tm, k), lambda i: (i, 0)),
                  pl.BlockSpec((None, lane, k), lambda i: (layer, col0 // lane, 0))],
        out_specs=pl.BlockSpec((tm, n_cols), lambda i: (i, 0)),
        out_shape=jax.ShapeDtypeStruct((m, n_cols), BF16),
        compiler_params=_cp(("parallel",)),
        name="in_proj_tail",
    )(a, wt3)


def _out_proj_kernel(ma_ref, mb_ref, w_hbm, xl_ref, xc_ref, g_ref, o_ref, w_s, stage, *, n_lat, widx):
    i = pl.program_id(0)
    ka = ma_ref.shape[-1]

    @pl.when(i == 0)
    def _():
        rows = stage.shape[0]
        for c in range(w_s.shape[0] // rows):
            pltpu.sync_copy(w_hbm.at[widx, pl.ds(c * rows, rows)], stage)
            w_s[c * rows:(c + 1) * rows, :] = stage[...].astype(BF16)

    acc = jnp.dot(ma_ref[...], w_s[:ka, :], preferred_element_type=F32)
    acc += jnp.dot(mb_ref[...], w_s[ka:, :], preferred_element_type=F32)
    upd = g_ref[...] * acc

    @pl.when(i < n_lat)
    def _():
        o_ref[...] = xl_ref[...] + upd

    @pl.when(i >= n_lat)
    def _():
        o_ref[...] = xc_ref[...] + upd


def _out_proj(mix_a, mix_b, w_out, widx, x_lat, x_ctx, ctx_block0, mods5, layer, rows, n_tiles):
    d = x_lat.shape[-1]
    ka, kb = mix_a.shape[-1], mix_b.shape[-1]
    tm = rows.tm
    nl = rows.n_lat
    stage_rows = min(512, ka + kb)
    return pl.pallas_call(
        functools.partial(_out_proj_kernel, n_lat=nl, widx=widx),
        grid=(n_tiles,),
        in_specs=[pl.BlockSpec((tm, ka), lambda i: (i, 0)),
                  pl.BlockSpec((tm, kb), lambda i: (i, 0)),
                  pl.BlockSpec(memory_space=pl.ANY),
                  pl.BlockSpec((tm, d), lambda i: (jnp.minimum(i, nl - 1), 0)),
                  pl.BlockSpec((tm, d), lambda i: (ctx_block0 + jnp.maximum(i - nl, 0), 0)),
                  _mod_spec(rows, layer, 2, d)],
        out_specs=pl.BlockSpec((tm, d), lambda i: (i, 0)),
        out_shape=jax.ShapeDtypeStruct((n_tiles * tm, d), F32),
        scratch_shapes=[pltpu.VMEM((ka + kb, d), BF16), pltpu.VMEM((stage_rows, d), F32)],
        compiler_params=_cp(("arbitrary",)),
        name="out_proj",
    )(mix_a, mix_b, w_out, x_lat, x_ctx, mods5)


def _tri(c, upper):
    r = lax.broadcasted_iota(jnp.int32, (c, c), 0)
    s = lax.broadcasted_iota(jnp.int32, (c, c), 1)
    return (s >= r) if upper else (r >= s)


def _scan_block(q, k, v, g, st, mask, forward):
    c = SCAN_C
    dk, dv = q.shape[-1], v.shape[-1]
    n = q.shape[0] // c
    mid, last = (c // 2 - 1, c - 1) if forward else (c // 2, 0)
    tri = jnp.broadcast_to(mask.astype(BF16)[None], (n, c, c))
    g3 = (g * LOG2E).reshape(n, c, dk)
    g_hi = g3.astype(BF16)
    g_lo = (g3 - g_hi.astype(F32)).astype(BF16)
    cum = (jnp.einsum('cts,csd->ctd', tri, g_hi, preferred_element_type=F32)
           + jnp.einsum('cts,csd->ctd', tri, g_lo, preferred_element_type=F32))
    m = cum[:, mid:mid + 1, :]
    tot = cum[:, last:last + 1, :]
    qe = (q.reshape(n, c, dk) * jnp.exp2(cum - m)).astype(BF16)
    ke = (k.reshape(n, c, dk) * jnp.exp2(m - cum)).astype(BF16)
    a = jnp.einsum('ctd,csd->cts', qe, ke, preferred_element_type=F32)
    a = jnp.where(mask[None], a, 0.0).astype(BF16)
    v3 = v.reshape(n, c, dv)
    o = jnp.einsum('cts,csv->ctv', a, v3, preferred_element_type=F32)
    u = jnp.einsum('csv,csd->cvd', v3, ke, preferred_element_type=F32)
    em = jnp.exp2(m)
    et = jnp.exp2(tot - m)
    states = [None] * n
    for ci in (range(n) if forward else reversed(range(n))):
        stp = st * em[ci]
        states[ci] = stp.astype(BF16)
        st = (stp + u[ci]) * et[ci]
    o = o + jnp.einsum('ctd,cvd->ctv', qe, jnp.stack(states), preferred_element_type=F32)
    return o.reshape(n * c, dv), st


def _scan_segments(segments, prep_f, prep_b, of_ref, ob_ref, dk, dv):
    low, up = _tri(SCAN_C, False), _tri(SCAN_C, True)
    carry = (jnp.zeros((dv, dk), F32), jnp.zeros((dv, dk), F32))
    for rows, off, seg in segments:
        rb_ = min(SCAN_BLOCK, rows)
        n = rows // rb_

        def body(i, carry, n=n, off=off, seg=seg, rb_=rb_):
            sf, sb = carry
            rf = pl.multiple_of(i * rb_, rb_)
            rb = pl.multiple_of((n - 1 - i) * rb_, rb_)
            q, k, v, g = prep_f(seg, rf, rb_)
            o, sf = _scan_block(q, k, v, g, sf, low, True)
            of_ref[pl.ds(off + rf, rb_), :] = o
            q, k, v, g = prep_b(seg, rb, rb_)
            o, sb = _scan_block(q, k, v, g, sb, up, False)
            ob_ref[pl.ds(off + rb, rb_), :] = o
            return sf, sb

        carry = lax.fori_loop(0, n, body, carry)


def _hgrn_kernel(ql, qc, f1l, f1c, f2l, f2c, vl, vc, gl, gc, lbf, lbb, ng, o_ref, of_s, ob_s, *, ctx_len, seq):
    rt = pl.program_id(2)
    n_ctx_tiles = ctx_len // TQ

    @pl.when(rt == 0)
    def _scan():
        refs = {0: (qc, f1c, f2c, vc), 1: (ql, f1l, f2l, vl)}
        scale = A_DK ** -0.5

        for hh in range(SCAN_HEADS):
            ks = slice(hh * A_DK, (hh + 1) * A_DK)
            vs = slice(hh * A_DV, (hh + 1) * A_DV)

            def prep(seg, r, nr, fi, lb_ref, ks=ks, vs=vs):
                x = refs[seg][0][pl.ds(r, nr), ks].astype(F32)
                q = x * jax.nn.sigmoid(x) * scale
                v = refs[seg][3][pl.ds(r, nr), vs]
                lb = lb_ref[:, ks]
                f = lb + (1.0 - lb) * jax.nn.sigmoid(refs[seg][fi][pl.ds(r, nr), ks].astype(F32))
                return q, 1.0 - f, v, jnp.log(f)

            _scan_segments(
                [(ctx_len, 0, 0), (seq, ctx_len, 1)],
                lambda seg, r, nr, prep=prep: prep(seg, r, nr, 1, lbf),
                lambda seg, r, nr, prep=prep: prep(seg, r, nr, 2, lbb),
                of_s.at[hh], ob_s.at[hh], A_DK, A_DV)

    r0 = pl.multiple_of(rt * TQ, TQ)
    ys = [_rms(of_s[hh, pl.ds(r0, TQ), :] + ob_s[hh, pl.ds(r0, TQ), :], ng[...]) for hh in range(SCAN_HEADS)]
    y = jnp.concatenate(ys, axis=-1)

    @pl.when(rt < n_ctx_tiles)
    def _():
        gate = gc[pl.ds(r0, TQ), :].astype(F32)
        o_ref[...] = (y * jax.nn.sigmoid(gate)).astype(o_ref.dtype)

    @pl.when(rt >= n_ctx_tiles)
    def _():
        gate = gl[pl.ds(pl.multiple_of(r0 - ctx_len, TQ), TQ), :].astype(F32)
        o_ref[...] = (y * jax.nn.sigmoid(gate)).astype(o_ref.dtype)


def _out_row_block(batch, seq, ctx_len):
    nct = ctx_len // TQ
    nlt = seq // TQ

    def f(b, rt):
        return jnp.where(rt < nct, batch * nlt + b * nct + rt, b * nlt + rt - nct)

    return f


def _hgrn(p, lb_f, lb_b, norm_g, batch, seq, ctx_len):
    h, dk, dv = A_HEADS, A_DK, A_DV
    hp = SCAN_HEADS
    hg = h // hp
    nct, nlt = ctx_len // TQ, seq // TQ
    ctx_blk0 = batch * seq // ctx_len
    row_block = _out_row_block(batch, seq, ctx_len)
    in_specs = []
    for kcol in range(5):
        in_specs.append(pl.BlockSpec((seq, hp * dk), lambda b, hh, rt, kcol=kcol: (b, kcol * hg + hh)))
        in_specs.append(pl.BlockSpec((ctx_len, hp * dk), lambda b, hh, rt, kcol=kcol: (ctx_blk0 + b, kcol * hg + hh)))
    vec = pl.BlockSpec((1, hp * dk), lambda b, hh, rt: (0, hh))
    in_specs += [vec, vec, pl.BlockSpec((1, dv), lambda b, hh, rt: (0, 0))]
    return pl.pallas_call(
        functools.partial(_hgrn_kernel, ctx_len=ctx_len, seq=seq),
        grid=(batch, hg, nct + nlt),
        in_specs=in_specs,
        out_specs=pl.BlockSpec((TQ, hp * dv), lambda b, hh, rt: (row_block(b, rt), hh)),
        out_shape=jax.ShapeDtypeStruct((batch * (seq + ctx_len), h * dv), BF16),
        scratch_shapes=[pltpu.VMEM((hp, seq + ctx_len, dv), F32), pltpu.VMEM((hp, seq + ctx_len, dv), F32)],
        compiler_params=_cp(("parallel", "parallel", "arbitrary")),
        name="hgrn_scan",
    )(*([p] * 10), lb_f.reshape(1, h * dk), lb_b.reshape(1, h * dk), norm_g.reshape(1, dv))


def _gla_kernel(ql, qc, kl, kc, vl, vc, gl, al, ac, wa, ba, ng, o_ref, of_s, ob_s, *, ctx_len, seq):
    refs = {0: (qc, kc, vc, ac), 1: (ql, kl, vl, al)}
    scale = D_DK ** -0.5
    r16 = D_GATE_RANK

    for hh in range(GLA_HEADS):
        ks = slice(hh * D_DK, (hh + 1) * D_DK)
        vs = slice(hh * D_DV, (hh + 1) * D_DV)

        def prep(seg, r, nr, d, ks=ks, vs=vs):
            q = refs[seg][0][pl.ds(r, nr), ks].astype(F32) * scale
            k = refs[seg][1][pl.ds(r, nr), ks].astype(F32)
            v = refs[seg][2][pl.ds(r, nr), vs]
            a = refs[seg][3][pl.ds(r, nr), :].astype(F32)[:, d * r16:(d + 1) * r16]
            z = jnp.dot(a, wa[d, :, ks], preferred_element_type=F32, precision=HI) + ba[d, :, ks]
            g = (jnp.minimum(z, 0.0) - jnp.log(1.0 + jnp.exp(-jnp.abs(z)))) * (1.0 / GLA_TAU)
            return q, k, v, g

        _scan_segments(
            [(ctx_len, 0, 0), (seq, ctx_len, 1)],
            lambda seg, r, nr, prep=prep: prep(seg, r, nr, 0),
            lambda seg, r, nr, prep=prep: prep(seg, r, nr, 1),
            of_s.at[hh], ob_s.at[hh], D_DK, D_DV)

    for rt in range(seq // TQ):
        rows = slice(rt * TQ, (rt + 1) * TQ)
        srows = slice(ctx_len + rt * TQ, ctx_len + (rt + 1) * TQ)
        ys = [_rms(of_s[hh, srows, :] + ob_s[hh, srows, :], ng[...]) for hh in range(GLA_HEADS)]
        gate = gl[rows, :].astype(F32)
        o_ref[rows, :] = (jnp.concatenate(ys, axis=-1) * gate * jax.nn.sigmoid(gate)).astype(o_ref.dtype)


def _gla(p, ga, w_a2, b_a, norm_g, batch, seq, ctx_len):
    h, dk, dv = D_HEADS, D_DK, D_DV
    hp = GLA_HEADS
    hg = h // hp
    nlt = seq // TQ
    ctx_blk0 = batch * seq // ctx_len
    wk, wv = hp * dk, hp * dv
    q0 = (C_HEADS + 2 * C_KV_HEADS) * C_DH // wk
    k0 = q0 + hg
    v0 = (k0 + hg) * wk // wv
    g0 = v0 + hg

    def pair(width, blk0):
        return [pl.BlockSpec((seq, width), lambda b, hh: (b, blk0 + hh)),
                pl.BlockSpec((ctx_len, width), lambda b, hh: (ctx_blk0 + b, blk0 + hh))]

    in_specs = pair(wk, q0) + pair(wk, k0) + pair(wv, v0)
    in_specs += [pl.BlockSpec((seq, wv), lambda b, hh: (b, g0 + hh)),
                 pl.BlockSpec((seq, 2 * D_GATE_RANK), lambda b, hh: (b, 0)),
                 pl.BlockSpec((ctx_len, 2 * D_GATE_RANK), lambda b, hh: (ctx_blk0 + b, 0)),
                 pl.BlockSpec((2, D_GATE_RANK, wk), lambda b, hh: (0, 0, hh)),
                 pl.BlockSpec((2, 1, wk), lambda b, hh: (0, 0, hh)),
                 pl.BlockSpec((1, dv), lambda b, hh: (0, 0))]
    return pl.pallas_call(
        functools.partial(_gla_kernel, ctx_len=ctx_len, seq=seq),
        grid=(batch, hg),
        in_specs=in_specs,
        out_specs=pl.BlockSpec((seq, wv), lambda b, hh: (b, hh)),
        out_shape=jax.ShapeDtypeStruct((batch * seq, h * dv), BF16),
        scratch_shapes=[pltpu.VMEM((hp, seq + ctx_len, dv), F32), pltpu.VMEM((hp, seq + ctx_len, dv), F32)],
        compiler_params=_cp(("parallel", "parallel")),
        name="gla_scan",
    )(p, p, p, p, p, p, p, ga, ga, w_a2, b_a.reshape(2, 1, h * dk), norm_g.reshape(1, dv))


def _softmax_pv(s, v):
    m = jnp.max(s, axis=-1, keepdims=True)
    p = jnp.exp2(s - m)
    l = jnp.sum(p, axis=-1, keepdims=True)
    return jnp.dot(p.astype(BF16), v, preferred_element_type=F32) / l


def _mla_kernel(ql_ref, kvl_ref, kvc_ref, krl_ref, krc_ref, wqn_ref, wqr_ref, wkv_ref, gq_ref, gkv_ref,
                cosq_ref, sinq_ref, cosk_ref, sink_ref, o_ref, k_s, v_s, *, ctx_len):
    qt = pl.program_id(1)
    n_ctx_tiles = ctx_len // TQ
    scale = (B_NOPE + B_ROPE) ** -0.5 * LOG2E
    dkv = B_NOPE + B_DV

    @pl.when(qt == 0)
    def _prep():
        kvc = _rms(kvc_ref[...].astype(F32), gkv_ref[...]).astype(BF16)
        kvl = _rms(kvl_ref[...].astype(F32), gkv_ref[...]).astype(BF16)
        kr_c = krc_ref[...]
        kr_l = _rope(krl_ref[...].astype(F32), cosk_ref[...], sink_ref[...]).astype(BF16)
        for h in range(B_HEADS):
            w = wkv_ref[:, h * dkv:(h + 1) * dkv].astype(BF16)
            up_c = jnp.dot(kvc, w, preferred_element_type=F32)
            k_s[h, 0:ctx_len, :] = jnp.concatenate([up_c[:, :B_NOPE].astype(BF16), kr_c], axis=-1)
            v_s[h, 0:ctx_len, :] = up_c[:, B_NOPE:].astype(BF16)
            up_l = jnp.dot(kvl, w, preferred_element_type=F32)
            k_s[h, ctx_len:, :] = jnp.concatenate([up_l[:, :B_NOPE].astype(BF16), kr_l], axis=-1)
            v_s[h, ctx_len:, :] = up_l[:, B_NOPE:].astype(BF16)

    xn = _rms(ql_ref[...].astype(F32), gq_ref[...]).astype(BF16)
    qn_all = jnp.dot(xn, wqn_ref[...].astype(BF16), preferred_element_type=F32) * scale
    qr_all = jnp.dot(xn, wqr_ref[...].astype(BF16), preferred_element_type=F32) * scale

    def heads(n_keys, rotate):
        outs = []
        for h in range(B_HEADS):
            qn = qn_all[:, h * B_NOPE:(h + 1) * B_NOPE]
            qr = qr_all[:, h * B_ROPE:(h + 1) * B_ROPE]
            if rotate:
                qr = _rope(qr, cosq_ref[...], sinq_ref[...])
            q = jnp.concatenate([qn, qr], axis=-1).astype(BF16)
            s = _dot_nt(q, k_s[h, 0:n_keys, :])
            outs.append(_softmax_pv(s, v_s[h, 0:n_keys, :]).astype(o_ref.dtype))
        o_ref[...] = jnp.concatenate(outs, axis=-1)

    @pl.when(qt < n_ctx_tiles)
    def _():
        heads(ctx_len, False)

    @pl.when(qt >= n_ctx_tiles)
    def _():
        heads(k_s.shape[1], True)


def _mla(p, kr, w_uq, w_ukv, gq, gkv, cos, sin, batch, seq, ctx_len):
    h = B_HEADS
    nct, nlt = ctx_len // TQ, seq // TQ
    ctx_blk0 = batch * seq // ctx_len
    row_block = _out_row_block(batch, seq, ctx_len)
    ql_blk = 5 * A_HEADS * A_DK // B_Q_LORA
    kv_blk = (5 * A_HEADS * A_DK + B_Q_LORA) // B_KV_LORA
    dq = B_NOPE + B_ROPE
    s_all = seq + ctx_len
    w3 = w_uq.reshape(B_Q_LORA, h, dq)
    wq_n = w3[:, :, :B_NOPE].reshape(B_Q_LORA, h * B_NOPE)
    wq_r = w3[:, :, B_NOPE:].reshape(B_Q_LORA, h * B_ROPE)
    in_specs = [
        pl.BlockSpec((TQ, B_Q_LORA), lambda b, qt: (row_block(b, qt), ql_blk)),
        pl.BlockSpec((seq, B_KV_LORA), lambda b, qt: (b, kv_blk)),
        pl.BlockSpec((ctx_len, B_KV_LORA), lambda b, qt: (ctx_blk0 + b, kv_blk)),
        pl.BlockSpec((seq, B_ROPE), lambda b, qt: (b, 0)),
        pl.BlockSpec((ctx_len, B_ROPE), lambda b, qt: (ctx_blk0 + b, 0)),
        pl.BlockSpec((B_Q_LORA, h * B_NOPE), lambda b, qt: (0, 0)),
        pl.BlockSpec((B_Q_LORA, h * B_ROPE), lambda b, qt: (0, 0)),
        pl.BlockSpec((B_KV_LORA, h * (B_NOPE + B_DV)), lambda b, qt: (0, 0)),
        pl.BlockSpec((1, B_Q_LORA), lambda b, qt: (0, 0)),
        pl.BlockSpec((1, B_KV_LORA), lambda b, qt: (0, 0)),
        pl.BlockSpec((TQ, B_ROPE), lambda b, qt: (jnp.maximum(qt - nct, 0), 0)),
        pl.BlockSpec((TQ, B_ROPE), lambda b, qt: (jnp.maximum(qt - nct, 0), 0)),
        pl.BlockSpec((seq, B_ROPE), lambda b, qt: (0, 0)),
        pl.BlockSpec((seq, B_ROPE), lambda b, qt: (0, 0)),
    ]
    return pl.pallas_call(
        functools.partial(_mla_kernel, ctx_len=ctx_len),
        grid=(batch, nct + nlt),
        in_specs=in_specs,
        out_specs=pl.BlockSpec((TQ, h * B_DV), lambda b, qt: (row_block(b, qt), 0)),
        out_shape=jax.ShapeDtypeStruct((batch * s_all, h * B_DV), BF16),
        scratch_shapes=[pltpu.VMEM((h, s_all, dq), BF16), pltpu.VMEM((h, s_all, B_DV), BF16)],
        compiler_params=_cp(("parallel", "arbitrary")),
        name="mla_attn",
    )(p, p, p, kr, kr, wq_n, wq_r, w_ukv, gq.reshape(1, -1), gkv.reshape(1, -1), cos, sin, cos, sin)


def _gqa_kernel(q_ref, kl_ref, kc_ref, vl_ref, vc_ref, gq_ref, gk_ref, cosq_ref, sinq_ref, cosk_ref, sink_ref,
                o_ref, k_s, v_s, *, ctx_len):
    qt = pl.program_id(1)
    scale = C_DH ** -0.5 * LOG2E
    dh = C_DH
    grp = C_HEADS // C_KV_HEADS

    @pl.when(qt == 0)
    def _prep():
        for kh in range(C_KV_HEADS):
            cols = slice(kh * dh, (kh + 1) * dh)
            k_s[kh, 0:ctx_len, :] = _rms(kc_ref[:, cols].astype(F32), gk_ref[...]).astype(BF16)
            kl = _rms(kl_ref[:, cols].astype(F32), gk_ref[...])
            k_s[kh, ctx_len:, :] = _rope(kl, cosk_ref[...], sink_ref[...]).astype(BF16)
            v_s[kh, 0:ctx_len, :] = vc_ref[:, cols]
            v_s[kh, ctx_len:, :] = vl_ref[:, cols]

    outs = []
    for hq in range(C_HEADS):
        kh = hq // grp
        q = _rms(q_ref[:, hq * dh:(hq + 1) * dh].astype(F32), gq_ref[...])
        q = _rope(q, cosq_ref[...], sinq_ref[...]) * scale
        s = _dot_nt(q.astype(BF16), k_s[kh])
        outs.append(_softmax_pv(s, v_s[kh]).astype(o_ref.dtype))
    o_ref[...] = jnp.concatenate(outs, axis=-1)


def _gqa(p, gq, gk, cos, sin, batch, seq, ctx_len):
    kvh, dh = C_KV_HEADS, C_DH
    nlt = seq // TQ
    ctx_blk0 = batch * seq // ctx_len
    wq, wkv = C_HEADS * dh, kvh * dh
    k0 = wq // wkv
    v0 = k0 + 1
    s_all = seq + ctx_len
    in_specs = [
        pl.BlockSpec((TQ, wq), lambda b, qt: (b * nlt + qt, 0)),
        pl.BlockSpec((seq, wkv), lambda b, qt: (b, k0)),
        pl.BlockSpec((ctx_len, wkv), lambda b, qt: (ctx_blk0 + b, k0)),
        pl.BlockSpec((seq, wkv), lambda b, qt: (b, v0)),
        pl.BlockSpec((ctx_len, wkv), lambda b, qt: (ctx_blk0 + b, v0)),
        pl.BlockSpec((1, dh), lambda b, qt: (0, 0)),
        pl.BlockSpec((1, dh), lambda b, qt: (0, 0)),
        pl.BlockSpec((TQ, dh), lambda b, qt: (qt, 0)),
        pl.BlockSpec((TQ, dh), lambda b, qt: (qt, 0)),
        pl.BlockSpec((seq, dh), lambda b, qt: (0, 0)),
        pl.BlockSpec((seq, dh), lambda b, qt: (0, 0)),
    ]
    return pl.pallas_call(
        functools.partial(_gqa_kernel, ctx_len=ctx_len),
        grid=(batch, nlt),
        in_specs=in_specs,
        out_specs=pl.BlockSpec((TQ, wq), lambda b, qt: (b * nlt + qt, 0)),
        out_shape=jax.ShapeDtypeStruct((batch * seq, wq), BF16),
        scratch_shapes=[pltpu.VMEM((kvh, s_all, dh), BF16), pltpu.VMEM((kvh, s_all, dh), BF16)],
        compiler_params=_cp(("parallel", "arbitrary")),
        name="gqa_attn",
    )(p, p, p, p, p, gq.reshape(1, dh), gk.reshape(1, dh), cos, sin, cos, sin)


def _router_kernel(x_ref, g_ref, sh_ref, sc_ref, rw_ref, rb_ref, h_ref, ri_ref, rf_ref, cnt_ref, base_s):
    i = pl.program_id(0)
    tm = x_ref.shape[0]
    ne = N_EXPERTS
    per = ne // N_GROUPS

    @pl.when(i == 0)
    def _():
        base_s[...] = jnp.zeros_like(base_s)

    h = _rms(x_ref[...], g_ref[...]) * (1.0 + sc_ref[...]) + sh_ref[...]
    nch = h.shape[1] // LANES
    for j in range(nch):
        h_ref[pl.ds(j, tm, stride=nch), :] = h[:, j * LANES:(j + 1) * LANES]
    h_hi = h.astype(BF16)
    h_lo = (h - h_hi.astype(F32)).astype(BF16)
    rw = rw_ref[...]
    w_hi = rw.astype(BF16)
    w_lo = (rw - w_hi.astype(F32)).astype(BF16)
    hw = jnp.dot(h_hi, jnp.concatenate([w_hi, w_lo], axis=1), preferred_element_type=F32)
    logits = hw[:, :ne] + hw[:, ne:] + jnp.dot(h_lo, w_hi, preferred_element_type=F32)
    scores = jax.nn.sigmoid(logits)
    sel = scores + rb_ref[...]
    lane = lax.broadcasted_iota(jnp.int32, (tm, ne), 1).astype(F32)
    neg = -jnp.inf
    big = float(ne)

    def top2(vals):
        m1 = jnp.max(vals, axis=1, keepdims=True)
        i1 = jnp.min(jnp.where(vals == m1, lane, big), axis=1, keepdims=True)
        rest = jnp.where(lane == i1, neg, vals)
        m2 = jnp.max(rest, axis=1, keepdims=True)
        i2 = jnp.min(jnp.where(rest == m2, lane, big), axis=1, keepdims=True)
        return m1 + m2, i1, i2

    best, e1, e2 = None, None, None
    for grp in range(N_GROUPS):
        in_grp = jnp.logical_and(lane >= float(grp * per), lane < float((grp + 1) * per))
        gsum, i1, i2 = top2(jnp.where(in_grp, sel, neg))
        if grp == 0:
            best, e1, e2 = gsum, i1, i2
        else:
            better = gsum > best
            best = jnp.where(better, gsum, best)
            e1 = jnp.where(better, i1, e1)
            e2 = jnp.where(better, i2, e2)

    hot1 = lane == e1
    hot2 = lane == e2
    w1 = jnp.sum(jnp.where(hot1, scores, 0.0), axis=1, keepdims=True)
    w2 = jnp.sum(jnp.where(hot2, scores, 0.0), axis=1, keepdims=True)
    wsum = w1 + w2
    assign = jnp.logical_or(hot1, hot2)
    r = lax.broadcasted_iota(jnp.int32, (tm, tm), 0)
    c = lax.broadcasted_iota(jnp.int32, (tm, tm), 1)
    before = (c < r).astype(BF16)
    excl = jnp.dot(before, assign.astype(BF16), preferred_element_type=F32) + base_s[...]
    rank1 = jnp.sum(jnp.where(hot1, excl, 0.0), axis=1, keepdims=True)
    rank2 = jnp.sum(jnp.where(hot2, excl, 0.0), axis=1, keepdims=True)
    base_s[...] = base_s[...] + jnp.sum(assign.astype(F32), axis=0, keepdims=True)

    l128 = lax.broadcasted_iota(jnp.int32, (tm, 128), 1)
    ri = jnp.where(l128 == 0, e1, jnp.where(l128 == 1, e2, jnp.where(l128 == 2, rank1, jnp.where(l128 == 3, rank2, 0.0))))
    ri_ref[...] = ri.T[0:8, :].astype(jnp.int32)
    rf_ref[...] = jnp.where(l128 == 0, w1 / wsum, jnp.where(l128 == 1, w2 / wsum, 0.0))
    cnt_ref[...] = jnp.broadcast_to(base_s[...], cnt_ref.shape)


def _router(x_all, g, mods5, layer, router_w, router_b, rows, n_tiles):
    d = x_all.shape[-1]
    tm = rows.tm
    n = n_tiles * tm
    ne = N_EXPERTS
    return pl.pallas_call(
        _router_kernel,
        grid=(n_tiles,),
        in_specs=[pl.BlockSpec((tm, d), lambda i: (i, 0)),
                  pl.BlockSpec((1, d), lambda i: (0, 0)),
                  _mod_spec(rows, layer, 3, d),
                  _mod_spec(rows, layer, 4, d),
                  pl.BlockSpec((d, ne), lambda i: (0, 0)),
                  pl.BlockSpec((1, ne), lambda i: (0, 0))],
        out_specs=[pl.BlockSpec((tm * (d // LANES), LANES), lambda i: (i, 0)),
                   pl.BlockSpec((8, tm), lambda i: (0, i)),
                   pl.BlockSpec((tm, 128), lambda i: (i, 0)),
                   pl.BlockSpec((8, ne), lambda i: (0, 0))],
        out_shape=[jax.ShapeDtypeStruct((n * (d // LANES), LANES), F32),
                   jax.ShapeDtypeStruct((8, n), jnp.int32),
                   jax.ShapeDtypeStruct((n, 128), F32),
                   jax.ShapeDtypeStruct((8, ne), F32)],
        scratch_shapes=[pltpu.VMEM((1, ne), F32)],
        compiler_params=_cp(("arbitrary",)),
        name="moe_router",
    )(x_all, g.reshape(1, d), mods5, mods5, router_w, router_b.reshape(1, ne))


def _expert_kernel(te_ref, nx_ref, nu_ref, pad_ref, dest_ref, h_hbm, w1_hbm, w3_hbm, w2_hbm, y_ref, xbuf, sem,
                   w1_s, w3_s, w2_s, wf1, wf3, wf2, wsem, wslot_ref, src_ref, *, n_tok, layer):
    r = pl.program_id(0)
    n_used = nu_ref[0]
    active = r < n_used
    changed = jnp.logical_or(r == 0, te_ref[r] != te_ref[jnp.maximum(r - 1, 0)])
    slot = lax.rem(r, 2)
    nch = xbuf.shape[1] // TE

    def weight_copies(e, ws):
        return (pltpu.make_async_copy(w1_hbm.at[layer, e], wf1.at[ws], wsem.at[ws]),
                pltpu.make_async_copy(w3_hbm.at[layer, e], wf3.at[ws], wsem.at[ws]),
                pltpu.make_async_copy(w2_hbm.at[layer, e], wf2.at[ws], wsem.at[ws]))

    @pl.when(r == 0)
    def _():
        wslot_ref[0] = 0
        for cp in weight_copies(te_ref[0], 0):
            cp.start()

        def clear(i, carry):
            src_ref[i] = 0
            return carry

        for e in range(N_EXPERTS):
            lax.fori_loop(pad_ref[e], pad_ref[N_EXPERTS + e], clear, 0)

        def invert(i, carry):
            src_ref[dest_ref[i]] = i
            src_ref[dest_ref[n_tok + i]] = i
            return carry

        lax.fori_loop(0, n_tok, invert, 0, unroll=DMA_UNROLL)

    def gather(tile, dst_slot):
        base = tile * TE
        for t in range(TE):
            row0 = pl.multiple_of(src_ref[base + t] * nch, nch)
            pltpu.make_async_copy(h_hbm.at[pl.ds(row0, nch)], xbuf.at[dst_slot, pl.ds(t * nch, nch)],
                                  sem.at[dst_slot]).start(priority=0)

    @pl.when(jnp.logical_and(r == 0, active))
    def _():
        gather(0, 0)

    @pl.when(r + 1 < n_used)
    def _():
        gather(r + 1, 1 - slot)

    @pl.when(jnp.logical_and(active, changed))
    def _():
        ws = wslot_ref[0]
        for cp in weight_copies(te_ref[r], ws):
            cp.wait()

        @pl.when(nx_ref[r] != te_ref[r])
        def _():
            for cp in weight_copies(nx_ref[r], 1 - ws):
                cp.start(priority=1)

        w1_s[...] = wf1[ws].astype(BF16)
        w3_s[...] = wf3[ws].astype(BF16)
        w2_s[...] = wf2[ws].astype(BF16)
        wslot_ref[0] = 1 - ws

    @pl.when(active)
    def _():
        pltpu.make_async_copy(h_hbm.at[pl.ds(0, TE * nch)], xbuf.at[slot], sem.at[slot]).wait()
        x = jnp.concatenate([xbuf[slot, pl.ds(j, TE, stride=nch), :] for j in range(nch)], axis=1).astype(BF16)
        a = jnp.dot(x, w1_s[...], preferred_element_type=F32)
        b = jnp.dot(x, w3_s[...], preferred_element_type=F32)
        hid = (a * jax.nn.sigmoid(a) * b).astype(BF16)
        y_ref[...] = jnp.dot(hid, w2_s[...], preferred_element_type=F32)

    @pl.when(jnp.logical_not(active))
    def _():
        y_ref[...] = jnp.zeros_like(y_ref)


def _experts(tile_expert, next_expert, n_used, pad, dest, hp, w1, w3, w2, layer, p_max):
    d = w1.shape[-2]
    nch = d // LANES
    n_tok = hp.shape[0] // nch
    f = w1.shape[-1]
    any_spec = pl.BlockSpec(memory_space=pl.ANY)
    grid_spec = pltpu.PrefetchScalarGridSpec(
        num_scalar_prefetch=5,
        grid=(p_max // TE,),
        in_specs=[any_spec, any_spec, any_spec, any_spec],
        out_specs=pl.BlockSpec((TE, d), lambda r, te, nx, nu, pd, sr: (r, 0)),
        scratch_shapes=[pltpu.VMEM((2, TE * nch, LANES), F32), pltpu.SemaphoreType.DMA((2,)),
                        pltpu.VMEM((d, f), BF16), pltpu.VMEM((d, f), BF16), pltpu.VMEM((f, d), BF16),
                        pltpu.VMEM((2, d, f), F32), pltpu.VMEM((2, d, f), F32), pltpu.VMEM((2, f, d), F32),
                        pltpu.SemaphoreType.DMA((2,)), pltpu.SMEM((1,), jnp.int32),
                        pltpu.SMEM((p_max,), jnp.int32)],
    )
    return pl.pallas_call(
        functools.partial(_expert_kernel, n_tok=n_tok, layer=layer),
        grid_spec=grid_spec,
        out_shape=jax.ShapeDtypeStruct((p_max, d), F32),
        compiler_params=_cp(("arbitrary",)),
        name="moe_experts",
    )(tile_expert, next_expert, n_used, pad, dest, hp, w1, w3, w2)


def _combine_kernel(dest_ref, x_ref, rf_ref, g_ref, pg_ref, psh_ref, psc_ref, ys_hbm, *rest, tm, n_tok, n_tiles, final):
    o_ref = rest[0]
    buf, sem = rest[-2], rest[-1]
    i = pl.program_id(0)
    slot = lax.rem(i, 2)

    def gather(tile, dst_slot):
        base = tile * tm
        for t in range(tm):
            pltpu.make_async_copy(ys_hbm.at[pl.ds(dest_ref[base + t], 1)], buf.at[dst_slot, 0, pl.ds(t, 1)],
                                  sem.at[dst_slot]).start(priority=0)
            pltpu.make_async_copy(ys_hbm.at[pl.ds(dest_ref[n_tok + base + t], 1)], buf.at[dst_slot, 1, pl.ds(t, 1)],
                                  sem.at[dst_slot]).start(priority=1)

    @pl.when(i == 0)
    def _():
        gather(0, 0)

    @pl.when(i + 1 < n_tiles)
    def _():
        gather(i + 1, 1 - slot)

    pltpu.make_async_copy(ys_hbm.at[pl.ds(0, tm)], buf.at[slot, 0], sem.at[slot]).wait()
    pltpu.make_async_copy(ys_hbm.at[pl.ds(0, tm)], buf.at[slot, 1], sem.at[slot]).wait()
    w = rf_ref[...]
    y = w[:, 0:1] * buf[slot, 0] + w[:, 1:2] * buf[slot, 1]
    x2 = x_ref[...] + g_ref[...] * y
    z = _rms(x2, pg_ref[...])
    if final:
        o_ref[...] = z
    else:
        o_ref[...] = x2
        rest[1][...] = (z * (1.0 + psc_ref[...]) + psh_ref[...]).astype(rest[1].dtype)


def _combine(dest, x_all, rf, mods5, layer, post_g, ys, rows, n_tiles, n_tok_total, final):
    d = x_all.shape[-1]
    tm = rows.tm
    nxt = min(layer + 1, mods5.shape[0] - 1)
    row_spec = pl.BlockSpec((tm, d), lambda i, dr: (i, 0))
    grid_spec = pltpu.PrefetchScalarGridSpec(
        num_scalar_prefetch=1,
        grid=(n_tiles,),
        in_specs=[row_spec,
                  pl.BlockSpec((tm, 128), lambda i, dr: (i, 0)),
                  _mod_spec(rows, layer, 5, d),
                  pl.BlockSpec((1, d), lambda i, dr: (0, 0)),
                  _mod_spec(rows, nxt, 0, d),
                  _mod_spec(rows, nxt, 1, d),
                  pl.BlockSpec(memory_space=pl.ANY)],
        out_specs=row_spec if final else [row_spec, row_spec],
        scratch_shapes=[pltpu.VMEM((2, 2, tm, d), F32), pltpu.SemaphoreType.DMA((2,))],
    )
    stream = jax.ShapeDtypeStruct((n_tiles * tm, d), F32)
    return pl.pallas_call(
        functools.partial(_combine_kernel, tm=tm, n_tok=n_tok_total, n_tiles=n_tiles, final=final),
        grid_spec=grid_spec,
        out_shape=stream if final else [stream, jax.ShapeDtypeStruct((n_tiles * tm, d), BF16)],
        compiler_params=_cp(("arbitrary",)),
        name="moe_combine",
    )(dest, x_all, rf, mods5, post_g.reshape(1, d), mods5, mods5, ys)


def _moe(x_all, n_tok, norm_g, mods5, layer, router_w, router_b, w1, w3, w2, final_g, final, batch, seq, ctx_len):
    rows_r = _Rows(batch, seq, ctx_len, TROUTE)
    rows_c = _Rows(batch, seq, ctx_len, TCOMB)
    h, ri, rf, cnt = _router(x_all, norm_g, mods5, layer, router_w, router_b, rows_r, n_tok // TROUTE)
    counts = cnt[0].astype(jnp.int32)
    padded = ((counts + TE - 1) // TE) * TE
    ends = jnp.cumsum(padded)
    starts = ends - padded
    e1, e2, r1, r2 = ri[0], ri[1], ri[2], ri[3]
    dest = jnp.concatenate([starts[e1] + r1, starts[e2] + r2]).astype(jnp.int32)
    p_max = 2 * n_tok + N_EXPERTS * TE
    n_tiles = p_max // TE
    n_used = (ends[-1] // TE).astype(jnp.int32)
    tile_start = jnp.arange(n_tiles, dtype=jnp.int32) * TE
    tile_expert = jnp.sum((tile_start[:, None] >= ends[None, :]).astype(jnp.int32), axis=1)
    last_expert = jnp.sum((jnp.maximum(ends[-1] - 1, 0) >= ends).astype(jnp.int32))
    tile_expert = jnp.minimum(jnp.where(tile_start < ends[-1], tile_expert, last_expert), N_EXPERTS - 1).astype(jnp.int32)
    eid = jnp.arange(N_EXPERTS, dtype=jnp.int32)
    later_used = jnp.logical_and(eid[None, :] > eid[:, None], (padded > 0)[None, :])
    next_used = jnp.min(jnp.where(later_used, eid[None, :], N_EXPERTS), axis=1)
    next_used = jnp.where(next_used == N_EXPERTS, eid, next_used)
    next_expert = jnp.sum(jnp.where(tile_expert[:, None] == eid[None, :], next_used[None, :], 0), axis=1).astype(jnp.int32)
    pad = jnp.concatenate([starts + counts, ends]).astype(jnp.int32)
    ys = _experts(tile_expert, next_expert, n_used.reshape(1), pad, dest, h, w1, w3, w2, layer, p_max)
    return _combine(dest, x_all, rf, mods5, layer, final_g, ys, rows_c, n_tok // TCOMB, n_tok, final)


def _rope_tables(t_len, d_rope):
    rows = t_len // GRID_W
    quarter = d_rope // 4
    freqs = ROPE_THETA ** (-jnp.arange(quarter, dtype=F32) / quarter)
    row = jnp.repeat(jnp.arange(rows, dtype=F32), GRID_W)
    col = jnp.tile(jnp.arange(GRID_W, dtype=F32), rows)
    ang = jnp.concatenate([row[:, None] * freqs, col[:, None] * freqs], axis=-1)
    cos, sin = jnp.cos(ang), jnp.sin(ang)
    return jnp.concatenate([cos, cos], axis=-1), jnp.concatenate([-sin, sin], axis=-1)


def kernel(x, c, ctx, c_ctx, mod_w, mod_b, norm_attn_g, norm_ffn_g, final_norm_g, ab_w_in, ab_w_out, hgrn_lb_logits, hgrn_norm_g, mla_q_norm_g, mla_w_uq, mla_kv_norm_g, mla_w_ukv, cd_w_in, cd_w_out, gqa_q_norm_g, gqa_k_norm_g, gla_w_a2, gla_b_a, gla_norm_g, router_w, router_b, moe_w1, moe_w3, moe_w2):
    batch, seq, d = x.shape
    ctx_len = ctx.shape[1]
    n_lat, n_ctx = batch * seq, batch * ctx_len
    assert ctx_len % TQ == 0 and seq % TQ == 0 and seq % ctx_len == 0 and batch < 8
    tm = min(1024, seq, n_ctx)
    rows = _Rows(batch, seq, ctx_len, tm)

    cvec = jnp.concatenate([c, c_ctx[None, :], jnp.zeros((8 - batch - 1, d), F32)], axis=0)
    mods = _modvec(cvec, mod_w, mod_b)
    mods5 = mods.reshape(mods.shape[0], 8, 6, 1, d)

    cos_b, sin_b = _rope_tables(seq, B_ROPE)
    cos_c, sin_c = _rope_tables(seq, C_DH)
    lb = jnp.cumsum(jax.nn.softmax(hgrn_lb_logits.astype(F32), axis=1), axis=1)

    x_lat = x.reshape(n_lat, d)
    x_ctx = ctx.reshape(n_ctx, d)

    h0 = _norm_mod(x_lat, x_ctx, 0, norm_attn_g[0], mods5, 0, rows)
    ab_main = 5 * A_HEADS * A_DK + B_Q_LORA + B_KV_LORA
    tm_mm = next(t for t in (2304, 2048, 1536, 1024, 512, 256) if (n_lat + n_ctx) % t == 0)
    ab_wt = jnp.swapaxes(ab_w_in, 1, 2)
    p0 = _matmul(h0, ab_wt, 0, ab_main, 256, tm_mm)
    kr0 = _matmul_tail(h0, ab_wt, 0, ab_main, B_ROPE, tm_mm)
    mix_a = _hgrn(p0, lb[0, 0], lb[1, 0], hgrn_norm_g[0], batch, seq, ctx_len)
    mix_b = _mla(p0, kr0, mla_w_uq[0], mla_w_ukv[0], mla_q_norm_g[0], mla_kv_norm_g[0], cos_b, sin_b,
                 batch, seq, ctx_len)
    rows_o = _Rows(batch, seq, ctx_len, min(512, tm))
    x1 = _out_proj(mix_a, mix_b, ab_w_out, 0, x_lat, x_ctx, 0, mods5, 0, rows_o, rows_o.n_all)
    x2, h1 = _moe(x1, n_lat + n_ctx, norm_ffn_g[0], mods5, 0, router_w, router_b, moe_w1, moe_w3, moe_w2,
                  norm_attn_g[1], False, batch, seq, ctx_len)

    cd_main = (C_HEADS + 2 * C_KV_HEADS) * C_DH + 2 * D_HEADS * D_DK + 2 * D_HEADS * D_DV
    cd_wt = jnp.swapaxes(cd_w_in, 1, 2)
    p1 = _matmul(h1, cd_wt, 0, cd_main, 512, tm_mm)
    ga1 = _matmul_tail(h1, cd_wt, 0, cd_main, 2 * D_GATE_RANK, tm_mm)
    mix_c = _gqa(p1, gqa_q_norm_g[0], gqa_k_norm_g[0], cos_c, sin_c, batch, seq, ctx_len)
    mix_d = _gla(p1, ga1, gla_w_a2[0], gla_b_a[0], gla_norm_g[0], batch, seq, ctx_len)
    x3 = _out_proj(mix_c, mix_d, cd_w_out, 0, x2, x2, rows_o.n_lat, mods5, 1, rows_o, rows_o.n_lat)
    out = _moe(x3, n_lat, norm_ffn_g[1], mods5, 1, router_w, router_b, moe_w1, moe_w3, moe_w2,
               final_norm_g, True, batch, seq, ctx_len)
    return out.reshape(batch, seq, d)
```

```python
import functools

import jax
import jax.numpy as jnp
from jax import lax
from jax.experimental import pallas as pl
from jax.experimental.pallas import tpu as pltpu

F32 = jnp.float32
BF16 = jnp.bfloat16
HI = lax.Precision.HIGHEST

GRID_W = 64
ROPE_THETA = 10000.0
NORM_EPS = 1e-6
A_HEADS, A_DK, A_DV = 8, 128, 128
B_HEADS, B_Q_LORA, B_KV_LORA, B_NOPE, B_ROPE, B_DV = 8, 512, 256, 128, 64, 128
C_HEADS, C_KV_HEADS, C_DH = 8, 2, 128
D_HEADS, D_DK, D_DV, D_GATE_RANK = 4, 128, 256, 16
GLA_TAU = 16.0
N_EXPERTS, N_GROUPS = 16, 4

TQ = 256
SCAN_C = 64
SCAN_HEADS = 4
GLA_HEADS = 2
SCAN_BLOCK = 2048
TE = 256
TROUTE = 512
TCOMB = 256
DMA_UNROLL = 8
LANES = 128
LOG2E = 1.4426950408889634
VMEM_MIB = 56


def _cp(sem):
    return pltpu.CompilerParams(dimension_semantics=sem, vmem_limit_bytes=VMEM_MIB * 1024 * 1024)


def _rms(x, g):
    return x * lax.rsqrt(jnp.mean(x * x, axis=-1, keepdims=True) + NORM_EPS) * g


def _rope(x, cos, sin):
    half = x.shape[-1] // 2
    swapped = jnp.concatenate([x[:, half:], x[:, :half]], axis=-1)
    return x * cos + swapped * sin


def _dot_nt(a, b):
    return lax.dot_general(a, b, (((1,), (1,)), ((), ())), preferred_element_type=F32)


def _dot_tn(a, b):
    return lax.dot_general(a, b, (((0,), (0,)), ((), ())), preferred_element_type=F32)


def _modvec_kernel(c_ref, w_ref, b_ref, o_ref):
    c = c_ref[...]
    a = c * jax.nn.sigmoid(c)
    a_hi = a.astype(BF16)
    a_lo = (a - a_hi.astype(F32)).astype(BF16)
    w = w_ref[...]
    w_hi = w.astype(BF16)
    w_lo = (w - w_hi.astype(F32)).astype(BF16)
    acc = jnp.dot(a_hi, w_hi, preferred_element_type=F32) + jnp.dot(a_lo, w_hi, preferred_element_type=F32)
    o_ref[...] = acc + jnp.dot(a_hi, w_lo, preferred_element_type=F32) + b_ref[...]


def _modvec(cvec, mod_w, mod_b):
    n_layers, d, n6 = mod_w.shape
    tn = min(1024, n6)
    return pl.pallas_call(
        _modvec_kernel,
        grid=(n_layers, n6 // tn),
        in_specs=[pl.BlockSpec((8, d), lambda l, j: (0, 0)),
                  pl.BlockSpec((None, d, tn), lambda l, j: (l, 0, j)),
                  pl.BlockSpec((None, 1, tn), lambda l, j: (l, 0, j))],
        out_specs=pl.BlockSpec((None, 8, tn), lambda l, j: (l, 0, j)),
        out_shape=jax.ShapeDtypeStruct((n_layers, 8, n6), F32),
        compiler_params=_cp(("parallel", "parallel")),
        name="modvec",
    )(cvec, mod_w, mod_b.reshape(n_layers, 1, n6))


class _Rows:
    def __init__(self, batch, seq, ctx_len, tm):
        assert seq % tm == 0 and (batch * ctx_len) % tm == 0
        self.tm = tm
        self.batch = batch
        self.per_batch = seq // tm
        self.n_lat = batch * seq // tm
        self.n_ctx = batch * ctx_len // tm
        self.n_all = self.n_lat + self.n_ctx

    def mod_row(self, i):
        return jnp.where(i < self.n_lat, i // self.per_batch, self.batch)


def _mod_spec(rows, layer, chunk, d):
    return pl.BlockSpec((None, None, None, 1, d), lambda i, *_: (layer, rows.mod_row(i), chunk, 0, 0))


def _norm_mod_kernel(xl_ref, xc_ref, g_ref, sh_ref, sc_ref, o_ref, *, n_lat):
    i = pl.program_id(0)

    def body(x_ref):
        y = _rms(x_ref[...], g_ref[...])
        o_ref[...] = (y * (1.0 + sc_ref[...]) + sh_ref[...]).astype(o_ref.dtype)

    @pl.when(i < n_lat)
    def _():
        body(xl_ref)

    @pl.when(i >= n_lat)
    def _():
        body(xc_ref)


def _norm_mod(x_lat, x_ctx, ctx_block0, g, mods5, layer, rows):
    d = x_lat.shape[-1]
    tm = rows.tm
    nl = rows.n_lat
    return pl.pallas_call(
        functools.partial(_norm_mod_kernel, n_lat=nl),
        grid=(rows.n_all,),
        in_specs=[pl.BlockSpec((tm, d), lambda i: (jnp.minimum(i, nl - 1), 0)),
                  pl.BlockSpec((tm, d), lambda i: (ctx_block0 + jnp.maximum(i - nl, 0), 0)),
                  pl.BlockSpec((1, d), lambda i: (0, 0)),
                  _mod_spec(rows, layer, 0, d),
                  _mod_spec(rows, layer, 1, d)],
        out_specs=pl.BlockSpec((tm, d), lambda i: (i, 0)),
        out_shape=jax.ShapeDtypeStruct((rows.n_all * tm, d), BF16),
        compiler_params=_cp(("parallel",)),
        name="norm_mod",
    )(x_lat, x_ctx, g.reshape(1, d), mods5, mods5)


def _mm_kernel(a_ref, wt_ref, o_ref):
    o_ref[...] = _dot_nt(a_ref[...], wt_ref[...].astype(BF16)).astype(o_ref.dtype)


def _matmul(a, wt3, layer, n_cols, tn, tm):
    m, k = a.shape
    return pl.pallas_call(
        _mm_kernel,
        grid=(m // tm, n_cols // tn),
        in_specs=[pl.BlockSpec((tm, k), lambda i, j: (i, 0)),
                  pl.BlockSpec((None, tn, k), lambda i, j: (layer, j, 0))],
        out_specs=pl.BlockSpec((tm, tn), lambda i, j: (i, j)),
        out_shape=jax.ShapeDtypeStruct((m, n_cols), BF16),
        compiler_params=_cp(("parallel", "arbitrary")),
        name="in_proj",
    )(a, wt3)


def _mm_tail_kernel(a_ref, wt_ref, o_ref):
    n = o_ref.shape[-1]
    acc = _dot_nt(a_ref[...], wt_ref[...].astype(BF16))
    o_ref[...] = acc[:, :n].astype(o_ref.dtype)


def _matmul_tail(a, wt3, layer, col0, n_cols, tm):
    m, k = a.shape
    lane = 128
    assert col0 % lane == 0 and n_cols <= lane and col0 + n_cols == wt3.shape[1]
    return pl.pallas_call(
        _mm_tail_kernel,
        grid=(m // tm,),
        in_specs=[pl.BlockSpec((tm, k), lambda i: (i, 0)),
                  pl.BlockSpec((None, lane, k), lambda i: (layer, col0 // lane, 0))],
        out_specs=pl.BlockSpec((tm, n_cols), lambda i: (i, 0)),
        out_shape=jax.ShapeDtypeStruct((m, n_cols), BF16),
        compiler_params=_cp(("parallel",)),
        name="in_proj_tail",
    )(a, wt3)


def _out_proj_kernel(mal_ref, mac_ref, mb_ref, w_hbm, xl_ref, xc_ref, g_ref, o_ref, w_s, stage, *, n_lat, widx):
    i = pl.program_id(0)
    ka = mal_ref.shape[-1]

    @pl.when(i == 0)
    def _():
        rows = stage.shape[0]
        for c in range(w_s.shape[0] // rows):
            pltpu.sync_copy(w_hbm.at[widx, pl.ds(c * rows, rows)], stage)
            w_s[c * rows:(c + 1) * rows, :] = stage[...].astype(BF16)

    part_b = jnp.dot(mb_ref[...], w_s[ka:, :], preferred_element_type=F32)

    def finish(ma_ref, x_ref):
        acc = part_b + jnp.dot(ma_ref[...], w_s[:ka, :], preferred_element_type=F32)
        o_ref[...] = x_ref[...] + g_ref[...] * acc

    @pl.when(i < n_lat)
    def _():
        finish(mal_ref, xl_ref)

    @pl.when(i >= n_lat)
    def _():
        finish(mac_ref, xc_ref)


def _out_proj(mix_a_lat, mix_a_ctx, mix_b, w_out, widx, x_lat, x_ctx, ctx_block0, mods5, layer, rows, n_tiles):
    d = x_lat.shape[-1]
    ka, kb = mix_a_lat.shape[-1], mix_b.shape[-1]
    tm = rows.tm
    nl = rows.n_lat
    stage_rows = min(512, ka + kb)
    return pl.pallas_call(
        functools.partial(_out_proj_kernel, n_lat=nl, widx=widx),
        grid=(n_tiles,),
        in_specs=[pl.BlockSpec((tm, ka), lambda i: (jnp.minimum(i, nl - 1), 0)),
                  pl.BlockSpec((tm, ka), lambda i: (jnp.maximum(i - nl, 0), 0)),
                  pl.BlockSpec((tm, kb), lambda i: (i, 0)),
                  pl.BlockSpec(memory_space=pl.ANY),
                  pl.BlockSpec((tm, d), lambda i: (jnp.minimum(i, nl - 1), 0)),
                  pl.BlockSpec((tm, d), lambda i: (ctx_block0 + jnp.maximum(i - nl, 0), 0)),
                  _mod_spec(rows, layer, 2, d)],
        out_specs=pl.BlockSpec((tm, d), lambda i: (i, 0)),
        out_shape=jax.ShapeDtypeStruct((n_tiles * tm, d), F32),
        scratch_shapes=[pltpu.VMEM((ka + kb, d), BF16), pltpu.VMEM((stage_rows, d), F32)],
        compiler_params=_cp(("arbitrary",)),
        name="out_proj",
    )(mix_a_lat, mix_a_ctx, mix_b, w_out, x_lat, x_ctx, mods5)


def _tri(c, upper):
    r = lax.broadcasted_iota(jnp.int32, (c, c), 0)
    s = lax.broadcasted_iota(jnp.int32, (c, c), 1)
    return (s >= r) if upper else (r >= s)


def _scan_block(q, k, v, g, st, mask, forward):
    c = SCAN_C
    dk, dv = q.shape[-1], v.shape[-1]
    n = q.shape[0] // c
    mid, last = (c // 2 - 1, c - 1) if forward else (c // 2, 0)
    tri = jnp.broadcast_to(mask.astype(BF16)[None], (n, c, c))
    g3 = (g * LOG2E).reshape(n, c, dk)
    g_hi = g3.astype(BF16)
    g_lo = (g3 - g_hi.astype(F32)).astype(BF16)
    cum = (jnp.einsum('cts,csd->ctd', tri, g_hi, preferred_element_type=F32)
           + jnp.einsum('cts,csd->ctd', tri, g_lo, preferred_element_type=F32))
    m = cum[:, mid:mid + 1, :]
    tot = cum[:, last:last + 1, :]
    qe = (q.reshape(n, c, dk) * jnp.exp2(cum - m)).astype(BF16)
    ke = (k.reshape(n, c, dk) * jnp.exp2(m - cum)).astype(BF16)
    a = jnp.einsum('ctd,csd->cts', qe, ke, preferred_element_type=F32)
    a = jnp.where(mask[None], a, 0.0).astype(BF16)
    v3 = v.reshape(n, c, dv)
    o = jnp.einsum('cts,csv->ctv', a, v3, preferred_element_type=F32)
    u = jnp.einsum('csv,csd->cvd', v3, ke, preferred_element_type=F32)
    em = jnp.exp2(m)
    et = jnp.exp2(tot - m)
    states = [None] * n
    for ci in (range(n) if forward else reversed(range(n))):
        stp = st * em[ci]
        states[ci] = stp.astype(BF16)
        st = (stp + u[ci]) * et[ci]
    o = o + jnp.einsum('ctd,cvd->ctv', qe, jnp.stack(states), preferred_element_type=F32)
    return o.reshape(n * c, dv), st


def _scan_segments(segments, prep_f, prep_b, of_ref, ob_ref, dk, dv):
    low, up = _tri(SCAN_C, False), _tri(SCAN_C, True)
    carry = (jnp.zeros((dv, dk), F32), jnp.zeros((dv, dk), F32))
    for rows, off, seg in segments:
        rb_ = min(SCAN_BLOCK, rows)
        n = rows // rb_

        def body(i, carry, n=n, off=off, seg=seg, rb_=rb_):
            sf, sb = carry
            rf = pl.multiple_of(i * rb_, rb_)
            rb = pl.multiple_of((n - 1 - i) * rb_, rb_)
            q, k, v, g = prep_f(seg, rf, rb_)
            o, sf = _scan_block(q, k, v, g, sf, low, True)
            of_ref[pl.ds(off + rf, rb_), :] = o
            q, k, v, g = prep_b(seg, rb, rb_)
            o, sb = _scan_block(q, k, v, g, sb, up, False)
            ob_ref[pl.ds(off + rb, rb_), :] = o
            return sf, sb

        carry = lax.fori_loop(0, n, body, carry)


def _hgrn_kernel(ql, qc, f1l, f1c, f2l, f2c, vl, vc, gl, gc, lbf, lbb, ng, ol_ref, oc_ref, of_s, ob_s, *,
                 ctx_len, seq):
    refs = {0: (qc, f1c, f2c, vc), 1: (ql, f1l, f2l, vl)}
    scale = A_DK ** -0.5

    for hh in range(SCAN_HEADS):
        ks = slice(hh * A_DK, (hh + 1) * A_DK)
        vs = slice(hh * A_DV, (hh + 1) * A_DV)

        def prep(seg, r, nr, fi, lb_ref, ks=ks, vs=vs):
            x = refs[seg][0][pl.ds(r, nr), ks].astype(F32)
            q = x * jax.nn.sigmoid(x) * scale
            v = refs[seg][3][pl.ds(r, nr), vs]
            lb = lb_ref[:, ks]
            f = lb + (1.0 - lb) * jax.nn.sigmoid(refs[seg][fi][pl.ds(r, nr), ks].astype(F32))
            return q, 1.0 - f, v, jnp.log(f)

        _scan_segments(
            [(ctx_len, 0, 0), (seq, ctx_len, 1)],
            lambda seg, r, nr, prep=prep: prep(seg, r, nr, 1, lbf),
            lambda seg, r, nr, prep=prep: prep(seg, r, nr, 2, lbb),
            of_s.at[hh], ob_s.at[hh], A_DK, A_DV)

    def emit(gate_ref, out_ref, n_rows, row0):
        for rt in range(n_rows // TQ):
            rows = slice(rt * TQ, (rt + 1) * TQ)
            srows = slice(row0 + rt * TQ, row0 + (rt + 1) * TQ)
            ys = [_rms(of_s[hh, srows, :] + ob_s[hh, srows, :], ng[...]) for hh in range(SCAN_HEADS)]
            gate = gate_ref[rows, :].astype(F32)
            out_ref[rows, :] = (jnp.concatenate(ys, axis=-1) * jax.nn.sigmoid(gate)).astype(out_ref.dtype)

    emit(gc, oc_ref, ctx_len, 0)
    emit(gl, ol_ref, seq, ctx_len)


def _out_row_block(batch, seq, ctx_len):
    nct = ctx_len // TQ
    nlt = seq // TQ

    def f(b, rt):
        return jnp.where(rt < nct, batch * nlt + b * nct + rt, b * nlt + rt - nct)

    return f


def _hgrn(p, lb_f, lb_b, norm_g, batch, seq, ctx_len):
    h, dk, dv = A_HEADS, A_DK, A_DV
    hp = SCAN_HEADS
    hg = h // hp
    ctx_blk0 = batch * seq // ctx_len
    in_specs = []
    for kcol in range(5):
        in_specs.append(pl.BlockSpec((seq, hp * dk), lambda b, hh, kcol=kcol: (b, kcol * hg + hh)))
        in_specs.append(pl.BlockSpec((ctx_len, hp * dk), lambda b, hh, kcol=kcol: (ctx_blk0 + b, kcol * hg + hh)))
    vec = pl.BlockSpec((1, hp * dk), lambda b, hh: (0, hh))
    in_specs += [vec, vec, pl.BlockSpec((1, dv), lambda b, hh: (0, 0))]
    return pl.pallas_call(
        functools.partial(_hgrn_kernel, ctx_len=ctx_len, seq=seq),
        grid=(batch, hg),
        in_specs=in_specs,
        out_specs=[pl.BlockSpec((seq, hp * dv), lambda b, hh: (b, hh)),
                   pl.BlockSpec((ctx_len, hp * dv), lambda b, hh: (b, hh))],
        out_shape=[jax.ShapeDtypeStruct((batch * seq, h * dv), BF16),
                   jax.ShapeDtypeStruct((batch * ctx_len, h * dv), BF16)],
        scratch_shapes=[pltpu.VMEM((hp, seq + ctx_len, dv), F32), pltpu.VMEM((hp, seq + ctx_len, dv), F32)],
        compiler_params=_cp(("parallel", "parallel")),
        name="hgrn_scan",
    )(*([p] * 10), lb_f.reshape(1, h * dk), lb_b.reshape(1, h * dk), norm_g.reshape(1, dv))


def _gla_kernel(ql, qc, kl, kc, vl, vc, gl, al, ac, wa, ba, ng, o_ref, of_s, ob_s, *, ctx_len, seq):
    refs = {0: (qc, kc, vc, ac), 1: (ql, kl, vl, al)}
    scale = D_DK ** -0.5
    r16 = D_GATE_RANK

    for hh in range(GLA_HEADS):
        ks = slice(hh * D_DK, (hh + 1) * D_DK)
        vs = slice(hh * D_DV, (hh + 1) * D_DV)

        def prep(seg, r, nr, d, ks=ks, vs=vs):
            q = refs[seg][0][pl.ds(r, nr), ks].astype(F32) * scale
            k = refs[seg][1][pl.ds(r, nr), ks].astype(F32)
            v = refs[seg][2][pl.ds(r, nr), vs]
            a = refs[seg][3][pl.ds(r, nr), :].astype(F32)[:, d * r16:(d + 1) * r16]
            z = jnp.dot(a, wa[d, :, ks], preferred_element_type=F32, precision=HI) + ba[d, :, ks]
            g = (jnp.minimum(z, 0.0) - jnp.log(1.0 + jnp.exp(-jnp.abs(z)))) * (1.0 / GLA_TAU)
            return q, k, v, g

        _scan_segments(
            [(ctx_len, 0, 0), (seq, ctx_len, 1)],
            lambda seg, r, nr, prep=prep: prep(seg, r, nr, 0),
            lambda seg, r, nr, prep=prep: prep(seg, r, nr, 1),
            of_s.at[hh], ob_s.at[hh], D_DK, D_DV)

    for rt in range(seq // TQ):
        rows = slice(rt * TQ, (rt + 1) * TQ)
        srows = slice(ctx_len + rt * TQ, ctx_len + (rt + 1) * TQ)
        ys = [_rms(of_s[hh, srows, :] + ob_s[hh, srows, :], ng[...]) for hh in range(GLA_HEADS)]
        gate = gl[rows, :].astype(F32)
        o_ref[rows, :] = (jnp.concatenate(ys, axis=-1) * gate * jax.nn.sigmoid(gate)).astype(o_ref.dtype)


def _gla(p, ga, w_a2, b_a, norm_g, batch, seq, ctx_len):
    h, dk, dv = D_HEADS, D_DK, D_DV
    hp = GLA_HEADS
    hg = h // hp
    nlt = seq // TQ
    ctx_blk0 = batch * seq // ctx_len
    wk, wv = hp * dk, hp * dv
    q0 = (C_HEADS + 2 * C_KV_HEADS) * C_DH // wk
    k0 = q0 + hg
    v0 = (k0 + hg) * wk // wv
    g0 = v0 + hg

    def pair(width, blk0):
        return [pl.BlockSpec((seq, width), lambda b, hh: (b, blk0 + hh)),
                pl.BlockSpec((ctx_len, width), lambda b, hh: (ctx_blk0 + b, blk0 + hh))]

    in_specs = pair(wk, q0) + pair(wk, k0) + pair(wv, v0)
    in_specs += [pl.BlockSpec((seq, wv), lambda b, hh: (b, g0 + hh)),
                 pl.BlockSpec((seq, 2 * D_GATE_RANK), lambda b, hh: (b, 0)),
                 pl.BlockSpec((ctx_len, 2 * D_GATE_RANK), lambda b, hh: (ctx_blk0 + b, 0)),
                 pl.BlockSpec((2, D_GATE_RANK, wk), lambda b, hh: (0, 0, hh)),
                 pl.BlockSpec((2, 1, wk), lambda b, hh: (0, 0, hh)),
                 pl.BlockSpec((1, dv), lambda b, hh: (0, 0))]
    return pl.pallas_call(
        functools.partial(_gla_kernel, ctx_len=ctx_len, seq=seq),
        grid=(batch, hg),
        in_specs=in_specs,
        out_specs=pl.BlockSpec((seq, wv), lambda b, hh: (b, hh)),
        out_shape=jax.ShapeDtypeStruct((batch * seq, h * dv), BF16),
        scratch_shapes=[pltpu.VMEM((hp, seq + ctx_len, dv), F32), pltpu.VMEM((hp, seq + ctx_len, dv), F32)],
        compiler_params=_cp(("parallel", "parallel")),
        name="gla_scan",
    )(p, p, p, p, p, p, p, ga, ga, w_a2, b_a.reshape(2, 1, h * dk), norm_g.reshape(1, dv))


def _softmax_pv(s, v):
    m = jnp.max(s, axis=-1, keepdims=True)
    p = jnp.exp2(s - m)
    l = jnp.sum(p, axis=-1, keepdims=True)
    return jnp.dot(p.astype(BF16), v, preferred_element_type=F32) / l


def _mla_kernel(ql_ref, kvl_ref, kvc_ref, krl_ref, krc_ref, wqn_ref, wqr_ref, wkv_ref, gq_ref, gkv_ref,
                cosq_ref, sinq_ref, cosk_ref, sink_ref, o_ref, k_s, v_s, *, ctx_len):
    qt = pl.program_id(1)
    n_ctx_tiles = ctx_len // TQ
    scale = (B_NOPE + B_ROPE) ** -0.5 * LOG2E
    dkv = B_NOPE + B_DV

    @pl.when(qt == 0)
    def _prep():
        kvc = _rms(kvc_ref[...].astype(F32), gkv_ref[...]).astype(BF16)
        kvl = _rms(kvl_ref[...].astype(F32), gkv_ref[...]).astype(BF16)
        kr_c = krc_ref[...]
        kr_l = _rope(krl_ref[...].astype(F32), cosk_ref[...], sink_ref[...]).astype(BF16)
        for h in range(B_HEADS):
            w = wkv_ref[:, h * dkv:(h + 1) * dkv].astype(BF16)
            up_c = jnp.dot(kvc, w, preferred_element_type=F32)
            k_s[h, 0:ctx_len, :] = jnp.concatenate([up_c[:, :B_NOPE].astype(BF16), kr_c], axis=-1)
            v_s[h, 0:ctx_len, :] = up_c[:, B_NOPE:].astype(BF16)
            up_l = jnp.dot(kvl, w, preferred_element_type=F32)
            k_s[h, ctx_len:, :] = jnp.concatenate([up_l[:, :B_NOPE].astype(BF16), kr_l], axis=-1)
            v_s[h, ctx_len:, :] = up_l[:, B_NOPE:].astype(BF16)

    xn = _rms(ql_ref[...].astype(F32), gq_ref[...]).astype(BF16)
    qn_all = jnp.dot(xn, wqn_ref[...].astype(BF16), preferred_element_type=F32) * scale
    qr_all = jnp.dot(xn, wqr_ref[...].astype(BF16), preferred_element_type=F32) * scale

    def heads(n_keys, rotate):
        outs = []
        for h in range(B_HEADS):
            qn = qn_all[:, h * B_NOPE:(h + 1) * B_NOPE]
            qr = qr_all[:, h * B_ROPE:(h + 1) * B_ROPE]
            if rotate:
                qr = _rope(qr, cosq_ref[...], sinq_ref[...])
            q = jnp.concatenate([qn, qr], axis=-1).astype(BF16)
            s = _dot_nt(q, k_s[h, 0:n_keys, :])
            outs.append(_softmax_pv(s, v_s[h, 0:n_keys, :]).astype(o_ref.dtype))
        o_ref[...] = jnp.concatenate(outs, axis=-1)

    @pl.when(qt < n_ctx_tiles)
    def _():
        heads(ctx_len, False)

    @pl.when(qt >= n_ctx_tiles)
    def _():
        heads(k_s.shape[1], True)


def _mla(p, kr, w_uq, w_ukv, gq, gkv, cos, sin, batch, seq, ctx_len):
    h = B_HEADS
    nct, nlt = ctx_len // TQ, seq // TQ
    ctx_blk0 = batch * seq // ctx_len
    row_block = _out_row_block(batch, seq, ctx_len)
    ql_blk = 5 * A_HEADS * A_DK // B_Q_LORA
    kv_blk = (5 * A_HEADS * A_DK + B_Q_LORA) // B_KV_LORA
    dq = B_NOPE + B_ROPE
    s_all = seq + ctx_len
    w3 = w_uq.reshape(B_Q_LORA, h, dq)
    wq_n = w3[:, :, :B_NOPE].reshape(B_Q_LORA, h * B_NOPE)
    wq_r = w3[:, :, B_NOPE:].reshape(B_Q_LORA, h * B_ROPE)
    in_specs = [
        pl.BlockSpec((TQ, B_Q_LORA), lambda b, qt: (row_block(b, qt), ql_blk)),
        pl.BlockSpec((seq, B_KV_LORA), lambda b, qt: (b, kv_blk)),
        pl.BlockSpec((ctx_len, B_KV_LORA), lambda b, qt: (ctx_blk0 + b, kv_blk)),
        pl.BlockSpec((seq, B_ROPE), lambda b, qt: (b, 0)),
        pl.BlockSpec((ctx_len, B_ROPE), lambda b, qt: (ctx_blk0 + b, 0)),
        pl.BlockSpec((B_Q_LORA, h * B_NOPE), lambda b, qt: (0, 0)),
        pl.BlockSpec((B_Q_LORA, h * B_ROPE), lambda b, qt: (0, 0)),
        pl.BlockSpec((B_KV_LORA, h * (B_NOPE + B_DV)), lambda b, qt: (0, 0)),
        pl.BlockSpec((1, B_Q_LORA), lambda b, qt: (0, 0)),
        pl.BlockSpec((1, B_KV_LORA), lambda b, qt: (0, 0)),
        pl.BlockSpec((TQ, B_ROPE), lambda b, qt: (jnp.maximum(qt - nct, 0), 0)),
        pl.BlockSpec((TQ, B_ROPE), lambda b, qt: (jnp.maximum(qt - nct, 0), 0)),
        pl.BlockSpec((seq, B_ROPE), lambda b, qt: (0, 0)),
        pl.BlockSpec((seq, B_ROPE), lambda b, qt: (0, 0)),
    ]
    return pl.pallas_call(
        functools.partial(_mla_kernel, ctx_len=ctx_len),
        grid=(batch, nct + nlt),
        in_specs=in_specs,
        out_specs=pl.BlockSpec((TQ, h * B_DV), lambda b, qt: (row_block(b, qt), 0)),
        out_shape=jax.ShapeDtypeStruct((batch * s_all, h * B_DV), BF16),
        scratch_shapes=[pltpu.VMEM((h, s_all, dq), BF16), pltpu.VMEM((h, s_all, B_DV), BF16)],
        compiler_params=_cp(("parallel", "arbitrary")),
        name="mla_attn",
    )(p, p, p, kr, kr, wq_n, wq_r, w_ukv, gq.reshape(1, -1), gkv.reshape(1, -1), cos, sin, cos, sin)


def _gqa_kernel(q_ref, kl_ref, kc_ref, vl_ref, vc_ref, gq_ref, gk_ref, cosq_ref, sinq_ref, cosk_ref, sink_ref,
                o_ref, k_s, v_s, *, ctx_len):
    qt = pl.program_id(1)
    scale = C_DH ** -0.5 * LOG2E
    dh = C_DH
    grp = C_HEADS // C_KV_HEADS

    @pl.when(qt == 0)
    def _prep():
        for kh in range(C_KV_HEADS):
            cols = slice(kh * dh, (kh + 1) * dh)
            k_s[kh, 0:ctx_len, :] = _rms(kc_ref[:, cols].astype(F32), gk_ref[...]).astype(BF16)
            kl = _rms(kl_ref[:, cols].astype(F32), gk_ref[...])
            k_s[kh, ctx_len:, :] = _rope(kl, cosk_ref[...], sink_ref[...]).astype(BF16)
            v_s[kh, 0:ctx_len, :] = vc_ref[:, cols]
            v_s[kh, ctx_len:, :] = vl_ref[:, cols]

    outs = []
    for hq in range(C_HEADS):
        kh = hq // grp
        q = _rms(q_ref[:, hq * dh:(hq + 1) * dh].astype(F32), gq_ref[...])
        q = _rope(q, cosq_ref[...], sinq_ref[...]) * scale
        s = _dot_nt(q.astype(BF16), k_s[kh])
        outs.append(_softmax_pv(s, v_s[kh]).astype(o_ref.dtype))
    o_ref[...] = jnp.concatenate(outs, axis=-1)


def _gqa(p, gq, gk, cos, sin, batch, seq, ctx_len):
    kvh, dh = C_KV_HEADS, C_DH
    nlt = seq // TQ
    ctx_blk0 = batch * seq // ctx_len
    wq, wkv = C_HEADS * dh, kvh * dh
    k0 = wq // wkv
    v0 = k0 + 1
    s_all = seq + ctx_len
    in_specs = [
        pl.BlockSpec((TQ, wq), lambda b, qt: (b * nlt + qt, 0)),
        pl.BlockSpec((seq, wkv), lambda b, qt: (b, k0)),
        pl.BlockSpec((ctx_len, wkv), lambda b, qt: (ctx_blk0 + b, k0)),
        pl.BlockSpec((seq, wkv), lambda b, qt: (b, v0)),
        pl.BlockSpec((ctx_len, wkv), lambda b, qt: (ctx_blk0 + b, v0)),
        pl.BlockSpec((1, dh), lambda b, qt: (0, 0)),
        pl.BlockSpec((1, dh), lambda b, qt: (0, 0)),
        pl.BlockSpec((TQ, dh), lambda b, qt: (qt, 0)),
        pl.BlockSpec((TQ, dh), lambda b, qt: (qt, 0)),
        pl.BlockSpec((seq, dh), lambda b, qt: (0, 0)),
        pl.BlockSpec((seq, dh), lambda b, qt: (0, 0)),
    ]
    return pl.pallas_call(
        functools.partial(_gqa_kernel, ctx_len=ctx_len),
        grid=(batch, nlt),
        in_specs=in_specs,
        out_specs=pl.BlockSpec((TQ, wq), lambda b, qt: (b * nlt + qt, 0)),
        out_shape=jax.ShapeDtypeStruct((batch * seq, wq), BF16),
        scratch_shapes=[pltpu.VMEM((kvh, s_all, dh), BF16), pltpu.VMEM((kvh, s_all, dh), BF16)],
        compiler_params=_cp(("parallel", "arbitrary")),
        name="gqa_attn",
    )(p, p, p, p, p, gq.reshape(1, dh), gk.reshape(1, dh), cos, sin, cos, sin)


def _router_kernel(x_ref, g_ref, sh_ref, sc_ref, rw_ref, rb_ref, h_ref, ri_ref, rf_ref, cnt_ref, base_s):
    i = pl.program_id(0)
    tm = x_ref.shape[0]
    ne = N_EXPERTS
    per = ne // N_GROUPS

    @pl.when(i == 0)
    def _():
        base_s[...] = jnp.zeros_like(base_s)

    h = _rms(x_ref[...], g_ref[...]) * (1.0 + sc_ref[...]) + sh_ref[...]
    nch = h.shape[1] // LANES
    for j in range(nch):
        h_ref[pl.ds(j, tm, stride=nch), :] = h[:, j * LANES:(j + 1) * LANES]
    h_hi = h.astype(BF16)
    h_lo = (h - h_hi.astype(F32)).astype(BF16)
    rw = rw_ref[...]
    w_hi = rw.astype(BF16)
    w_lo = (rw - w_hi.astype(F32)).astype(BF16)
    hw = jnp.dot(h_hi, jnp.concatenate([w_hi, w_lo], axis=1), preferred_element_type=F32)
    logits = hw[:, :ne] + hw[:, ne:] + jnp.dot(h_lo, w_hi, preferred_element_type=F32)
    scores = jax.nn.sigmoid(logits)
    sel = scores + rb_ref[...]
    lane = lax.broadcasted_iota(jnp.int32, (tm, ne), 1).astype(F32)
    neg = -jnp.inf
    big = float(ne)

    def top2(vals):
        m1 = jnp.max(vals, axis=1, keepdims=True)
        i1 = jnp.min(jnp.where(vals == m1, lane, big), axis=1, keepdims=True)
        rest = jnp.where(lane == i1, neg, vals)
        m2 = jnp.max(rest, axis=1, keepdims=True)
        i2 = jnp.min(jnp.where(rest == m2, lane, big), axis=1, keepdims=True)
        return m1 + m2, i1, i2

    best, e1, e2 = None, None, None
    for grp in range(N_GROUPS):
        in_grp = jnp.logical_and(lane >= float(grp * per), lane < float((grp + 1) * per))
        gsum, i1, i2 = top2(jnp.where(in_grp, sel, neg))
        if grp == 0:
            best, e1, e2 = gsum, i1, i2
        else:
            better = gsum > best
            best = jnp.where(better, gsum, best)
            e1 = jnp.where(better, i1, e1)
            e2 = jnp.where(better, i2, e2)

    hot1 = lane == e1
    hot2 = lane == e2
    w1 = jnp.sum(jnp.where(hot1, scores, 0.0), axis=1, keepdims=True)
    w2 = jnp.sum(jnp.where(hot2, scores, 0.0), axis=1, keepdims=True)
    wsum = w1 + w2
    assign = jnp.logical_or(hot1, hot2)
    r = lax.broadcasted_iota(jnp.int32, (tm, tm), 0)
    c = lax.broadcasted_iota(jnp.int32, (tm, tm), 1)
    before = (c < r).astype(BF16)
    excl = jnp.dot(before, assign.astype(BF16), preferred_element_type=F32) + base_s[...]
    rank1 = jnp.sum(jnp.where(hot1, excl, 0.0), axis=1, keepdims=True)
    rank2 = jnp.sum(jnp.where(hot2, excl, 0.0), axis=1, keepdims=True)
    base_s[...] = base_s[...] + jnp.sum(assign.astype(F32), axis=0, keepdims=True)

    l128 = lax.broadcasted_iota(jnp.int32, (tm, 128), 1)
    ri = jnp.where(l128 == 0, e1, jnp.where(l128 == 1, e2, jnp.where(l128 == 2, rank1, jnp.where(l128 == 3, rank2, 0.0))))
    ri_ref[...] = ri.T[0:8, :].astype(jnp.int32)
    rf_ref[...] = jnp.where(l128 == 0, w1 / wsum, jnp.where(l128 == 1, w2 / wsum, 0.0))
    cnt_ref[...] = jnp.broadcast_to(base_s[...], cnt_ref.shape)


def _router(x_all, g, mods5, layer, router_w, router_b, rows, n_tiles):
    d = x_all.shape[-1]
    tm = rows.tm
    n = n_tiles * tm
    ne = N_EXPERTS
    return pl.pallas_call(
        _router_kernel,
        grid=(n_tiles,),
        in_specs=[pl.BlockSpec((tm, d), lambda i: (i, 0)),
                  pl.BlockSpec((1, d), lambda i: (0, 0)),
                  _mod_spec(rows, layer, 3, d),
                  _mod_spec(rows, layer, 4, d),
                  pl.BlockSpec((d, ne), lambda i: (0, 0)),
                  pl.BlockSpec((1, ne), lambda i: (0, 0))],
        out_specs=[pl.BlockSpec((tm * (d // LANES), LANES), lambda i: (i, 0)),
                   pl.BlockSpec((8, tm), lambda i: (0, i)),
                   pl.BlockSpec((tm, 128), lambda i: (i, 0)),
                   pl.BlockSpec((8, ne), lambda i: (0, 0))],
        out_shape=[jax.ShapeDtypeStruct((n * (d // LANES), LANES), F32),
                   jax.ShapeDtypeStruct((8, n), jnp.int32),
                   jax.ShapeDtypeStruct((n, 128), F32),
                   jax.ShapeDtypeStruct((8, ne), F32)],
        scratch_shapes=[pltpu.VMEM((1, ne), F32)],
        compiler_params=_cp(("arbitrary",)),
        name="moe_router",
    )(x_all, g.reshape(1, d), mods5, mods5, router_w, router_b.reshape(1, ne))


def _expert_kernel(te_ref, nx_ref, nu_ref, pad_ref, dest_ref, h_hbm, w1_hbm, w3_hbm, w2_hbm, y_ref, xbuf, sem,
                   w1_s, w3_s, w2_s, wf1, wf3, wf2, wsem, wslot_ref, src_ref, *, n_tok, layer):
    r = pl.program_id(0)
    n_used = nu_ref[0]
    active = r < n_used
    changed = jnp.logical_or(r == 0, te_ref[r] != te_ref[jnp.maximum(r - 1, 0)])
    slot = lax.rem(r, 2)
    nch = xbuf.shape[1] // TE

    def weight_copies(e, ws):
        return (pltpu.make_async_copy(w1_hbm.at[layer, e], wf1.at[ws], wsem.at[ws]),
                pltpu.make_async_copy(w3_hbm.at[layer, e], wf3.at[ws], wsem.at[ws]),
                pltpu.make_async_copy(w2_hbm.at[layer, e], wf2.at[ws], wsem.at[ws]))

    @pl.when(r == 0)
    def _():
        wslot_ref[0] = 0
        for cp in weight_copies(te_ref[0], 0):
            cp.start()

        def clear(i, carry):
            src_ref[i] = 0
            return carry

        for e in range(N_EXPERTS):
            lax.fori_loop(pad_ref[e], pad_ref[N_EXPERTS + e], clear, 0)

        def invert(i, carry):
            src_ref[dest_ref[i]] = i
            src_ref[dest_ref[n_tok + i]] = i
            return carry

        lax.fori_loop(0, n_tok, invert, 0, unroll=DMA_UNROLL)

    def gather(tile, dst_slot):
        base = tile * TE
        for t in range(TE):
            row0 = pl.multiple_of(src_ref[base + t] * nch, nch)
            pltpu.make_async_copy(h_hbm.at[pl.ds(row0, nch)], xbuf.at[dst_slot, pl.ds(t * nch, nch)],
                                  sem.at[dst_slot]).start(priority=0)

    @pl.when(jnp.logical_and(r == 0, active))
    def _():
        gather(0, 0)

    @pl.when(r + 1 < n_used)
    def _():
        gather(r + 1, 1 - slot)

    @pl.when(jnp.logical_and(active, changed))
    def _():
        ws = wslot_ref[0]
        for cp in weight_copies(te_ref[r], ws):
            cp.wait()

        @pl.when(nx_ref[r] != te_ref[r])
        def _():
            for cp in weight_copies(nx_ref[r], 1 - ws):
                cp.start(priority=1)

        w1_s[...] = wf1[ws].astype(BF16)
        w3_s[...] = wf3[ws].astype(BF16)
        w2_s[...] = wf2[ws].astype(BF16)
        wslot_ref[0] = 1 - ws

    @pl.when(active)
    def _():
        pltpu.make_async_copy(h_hbm.at[pl.ds(0, TE * nch)], xbuf.at[slot], sem.at[slot]).wait()
        x = jnp.concatenate([xbuf[slot, pl.ds(j, TE, stride=nch), :] for j in range(nch)], axis=1).astype(BF16)
        a = jnp.dot(x, w1_s[...], preferred_element_type=F32)
        b = jnp.dot(x, w3_s[...], preferred_element_type=F32)
        hid = (a * jax.nn.sigmoid(a) * b).astype(BF16)
        y_ref[...] = jnp.dot(hid, w2_s[...], preferred_element_type=F32)

    @pl.when(jnp.logical_not(active))
    def _():
        y_ref[...] = jnp.zeros_like(y_ref)


def _experts(tile_expert, next_expert, n_used, pad, dest, hp, w1, w3, w2, layer, p_max):
    d = w1.shape[-2]
    nch = d // LANES
    n_tok = hp.shape[0] // nch
    f = w1.shape[-1]
    any_spec = pl.BlockSpec(memory_space=pl.ANY)
    grid_spec = pltpu.PrefetchScalarGridSpec(
        num_scalar_prefetch=5,
        grid=(p_max // TE,),
        in_specs=[any_spec, any_spec, any_spec, any_spec],
        out_specs=pl.BlockSpec((TE, d), lambda r, te, nx, nu, pd, sr: (r, 0)),
        scratch_shapes=[pltpu.VMEM((2, TE * nch, LANES), F32), pltpu.SemaphoreType.DMA((2,)),
                        pltpu.VMEM((d, f), BF16), pltpu.VMEM((d, f), BF16), pltpu.VMEM((f, d), BF16),
                        pltpu.VMEM((2, d, f), F32), pltpu.VMEM((2, d, f), F32), pltpu.VMEM((2, f, d), F32),
                        pltpu.SemaphoreType.DMA((2,)), pltpu.SMEM((1,), jnp.int32),
                        pltpu.SMEM((p_max,), jnp.int32)],
    )
    return pl.pallas_call(
        functools.partial(_expert_kernel, n_tok=n_tok, layer=layer),
        grid_spec=grid_spec,
        out_shape=jax.ShapeDtypeStruct((p_max, d), F32),
        compiler_params=_cp(("arbitrary",)),
        name="moe_experts",
    )(tile_expert, next_expert, n_used, pad, dest, hp, w1, w3, w2)


def _combine_kernel(dest_ref, x_ref, rf_ref, g_ref, pg_ref, psh_ref, psc_ref, ys_hbm, *rest, tm, n_tok, n_tiles, final):
    o_ref = rest[0]
    buf, sem = rest[-2], rest[-1]
    i = pl.program_id(0)
    slot = lax.rem(i, 2)

    def gather(tile, dst_slot):
        base = tile * tm
        for t in range(tm):
            pltpu.make_async_copy(ys_hbm.at[pl.ds(dest_ref[base + t], 1)], buf.at[dst_slot, 0, pl.ds(t, 1)],
                                  sem.at[dst_slot]).start(priority=0)
            pltpu.make_async_copy(ys_hbm.at[pl.ds(dest_ref[n_tok + base + t], 1)], buf.at[dst_slot, 1, pl.ds(t, 1)],
                                  sem.at[dst_slot]).start(priority=1)

    @pl.when(i == 0)
    def _():
        gather(0, 0)

    @pl.when(i + 1 < n_tiles)
    def _():
        gather(i + 1, 1 - slot)

    pltpu.make_async_copy(ys_hbm.at[pl.ds(0, tm)], buf.at[slot, 0], sem.at[slot]).wait()
    pltpu.make_async_copy(ys_hbm.at[pl.ds(0, tm)], buf.at[slot, 1], sem.at[slot]).wait()
    w = rf_ref[...]
    y = w[:, 0:1] * buf[slot, 0] + w[:, 1:2] * buf[slot, 1]
    x2 = x_ref[...] + g_ref[...] * y
    z = _rms(x2, pg_ref[...])
    if final:
        o_ref[...] = z
    else:
        o_ref[...] = x2
        rest[1][...] = (z * (1.0 + psc_ref[...]) + psh_ref[...]).astype(rest[1].dtype)


def _combine(dest, x_all, rf, mods5, layer, post_g, ys, rows, n_tiles, n_tok_total, final):
    d = x_all.shape[-1]
    tm = rows.tm
    nxt = min(layer + 1, mods5.shape[0] - 1)
    row_spec = pl.BlockSpec((tm, d), lambda i, dr: (i, 0))
    grid_spec = pltpu.PrefetchScalarGridSpec(
        num_scalar_prefetch=1,
        grid=(n_tiles,),
        in_specs=[row_spec,
                  pl.BlockSpec((tm, 128), lambda i, dr: (i, 0)),
                  _mod_spec(rows, layer, 5, d),
                  pl.BlockSpec((1, d), lambda i, dr: (0, 0)),
                  _mod_spec(rows, nxt, 0, d),
                  _mod_spec(rows, nxt, 1, d),
                  pl.BlockSpec(memory_space=pl.ANY)],
        out_specs=row_spec if final else [row_spec, row_spec],
        scratch_shapes=[pltpu.VMEM((2, 2, tm, d), F32), pltpu.SemaphoreType.DMA((2,))],
    )
    stream = jax.ShapeDtypeStruct((n_tiles * tm, d), F32)
    return pl.pallas_call(
        functools.partial(_combine_kernel, tm=tm, n_tok=n_tok_total, n_tiles=n_tiles, final=final),
        grid_spec=grid_spec,
        out_shape=stream if final else [stream, jax.ShapeDtypeStruct((n_tiles * tm, d), BF16)],
        compiler_params=_cp(("arbitrary",)),
        name="moe_combine",
    )(dest, x_all, rf, mods5, post_g.reshape(1, d), mods5, mods5, ys)


def _moe(x_all, n_tok, norm_g, mods5, layer, router_w, router_b, w1, w3, w2, final_g, final, batch, seq, ctx_len):
    rows_r = _Rows(batch, seq, ctx_len, TROUTE)
    rows_c = _Rows(batch, seq, ctx_len, TCOMB)
    h, ri, rf, cnt = _router(x_all, norm_g, mods5, layer, router_w, router_b, rows_r, n_tok // TROUTE)
    counts = cnt[0].astype(jnp.int32)
    padded = ((counts + TE - 1) // TE) * TE
    ends = jnp.cumsum(padded)
    starts = ends - padded
    e1, e2, r1, r2 = ri[0], ri[1], ri[2], ri[3]
    dest = jnp.concatenate([starts[e1] + r1, starts[e2] + r2]).astype(jnp.int32)
    p_max = 2 * n_tok + N_EXPERTS * TE
    n_tiles = p_max // TE
    n_used = (ends[-1] // TE).astype(jnp.int32)
    tile_start = jnp.arange(n_tiles, dtype=jnp.int32) * TE
    tile_expert = jnp.sum((tile_start[:, None] >= ends[None, :]).astype(jnp.int32), axis=1)
    last_expert = jnp.sum((jnp.maximum(ends[-1] - 1, 0) >= ends).astype(jnp.int32))
    tile_expert = jnp.minimum(jnp.where(tile_start < ends[-1], tile_expert, last_expert), N_EXPERTS - 1).astype(jnp.int32)
    eid = jnp.arange(N_EXPERTS, dtype=jnp.int32)
    later_used = jnp.logical_and(eid[None, :] > eid[:, None], (padded > 0)[None, :])
    next_used = jnp.min(jnp.where(later_used, eid[None, :], N_EXPERTS), axis=1)
    next_used = jnp.where(next_used == N_EXPERTS, eid, next_used)
    next_expert = jnp.sum(jnp.where(tile_expert[:, None] == eid[None, :], next_used[None, :], 0), axis=1).astype(jnp.int32)
    pad = jnp.concatenate([starts + counts, ends]).astype(jnp.int32)
    ys = _experts(tile_expert, next_expert, n_used.reshape(1), pad, dest, h, w1, w3, w2, layer, p_max)
    return _combine(dest, x_all, rf, mods5, layer, final_g, ys, rows_c, n_tok // TCOMB, n_tok, final)


def _rope_tables(t_len, d_rope):
    rows = t_len // GRID_W
    quarter = d_rope // 4
    freqs = ROPE_THETA ** (-jnp.arange(quarter, dtype=F32) / quarter)
    row = jnp.repeat(jnp.arange(rows, dtype=F32), GRID_W)
    col = jnp.tile(jnp.arange(GRID_W, dtype=F32), rows)
    ang = jnp.concatenate([row[:, None] * freqs, col[:, None] * freqs], axis=-1)
    cos, sin = jnp.cos(ang), jnp.sin(ang)
    return jnp.concatenate([cos, cos], axis=-1), jnp.concatenate([-sin, sin], axis=-1)


def kernel(x, c, ctx, c_ctx, mod_w, mod_b, norm_attn_g, norm_ffn_g, final_norm_g, ab_w_in, ab_w_out, hgrn_lb_logits, hgrn_norm_g, mla_q_norm_g, mla_w_uq, mla_kv_norm_g, mla_w_ukv, cd_w_in, cd_w_out, gqa_q_norm_g, gqa_k_norm_g, gla_w_a2, gla_b_a, gla_norm_g, router_w, router_b, moe_w1, moe_w3, moe_w2):
    batch, seq, d = x.shape
    ctx_len = ctx.shape[1]
    n_lat, n_ctx = batch * seq, batch * ctx_len
    assert ctx_len % TQ == 0 and seq % TQ == 0 and seq % ctx_len == 0 and batch < 8
    tm = min(1024, seq, n_ctx)
    rows = _Rows(batch, seq, ctx_len, tm)

    cvec = jnp.concatenate([c, c_ctx[None, :], jnp.zeros((8 - batch - 1, d), F32)], axis=0)
    mods = _modvec(cvec, mod_w, mod_b)
    mods5 = mods.reshape(mods.shape[0], 8, 6, 1, d)

    cos_b, sin_b = _rope_tables(seq, B_ROPE)
    cos_c, sin_c = _rope_tables(seq, C_DH)
    lb = jnp.cumsum(jax.nn.softmax(hgrn_lb_logits.astype(F32), axis=1), axis=1)

    x_lat = x.reshape(n_lat, d)
    x_ctx = ctx.reshape(n_ctx, d)

    h0 = _norm_mod(x_lat, x_ctx, 0, norm_attn_g[0], mods5, 0, rows)
    ab_main = 5 * A_HEADS * A_DK + B_Q_LORA + B_KV_LORA
    tm_mm = next(t for t in (2304, 2048, 1536, 1024, 512, 256) if (n_lat + n_ctx) % t == 0)
    ab_wt = jnp.swapaxes(ab_w_in, 1, 2)
    p0 = _matmul(h0, ab_wt, 0, ab_main, 256, tm_mm)
    kr0 = _matmul_tail(h0, ab_wt, 0, ab_main, B_ROPE, tm_mm)
    mix_a_lat, mix_a_ctx = _hgrn(p0, lb[0, 0], lb[1, 0], hgrn_norm_g[0], batch, seq, ctx_len)
    mix_b = _mla(p0, kr0, mla_w_uq[0], mla_w_ukv[0], mla_q_norm_g[0], mla_kv_norm_g[0], cos_b, sin_b,
                 batch, seq, ctx_len)
    rows_o = _Rows(batch, seq, ctx_len, min(512, tm))
    x1 = _out_proj(mix_a_lat, mix_a_ctx, mix_b, ab_w_out, 0, x_lat, x_ctx, 0, mods5, 0, rows_o, rows_o.n_all)
    x2, h1 = _moe(x1, n_lat + n_ctx, norm_ffn_g[0], mods5, 0, router_w, router_b, moe_w1, moe_w3, moe_w2,
                  norm_attn_g[1], False, batch, seq, ctx_len)

    cd_main = (C_HEADS + 2 * C_KV_HEADS) * C_DH + 2 * D_HEADS * D_DK + 2 * D_HEADS * D_DV
    cd_wt = jnp.swapaxes(cd_w_in, 1, 2)
    p1 = _matmul(h1, cd_wt, 0, cd_main, 512, tm_mm)
    ga1 = _matmul_tail(h1, cd_wt, 0, cd_main, 2 * D_GATE_RANK, tm_mm)
    mix_c = _gqa(p1, gqa_q_norm_g[0], gqa_k_norm_g[0], cos_c, sin_c, batch, seq, ctx_len)
    mix_d = _gla(p1, ga1, gla_w_a2[0], gla_b_a[0], gla_norm_g[0], batch, seq, ctx_len)
    x3 = _out_proj(mix_c, mix_c, mix_d, cd_w_out, 0, x2, x2, rows_o.n_lat, mods5, 1, rows_o, rows_o.n_lat)
    out = _moe(x3, n_lat, norm_ffn_g[1], mods5, 1, router_w, router_b, moe_w1, moe_w3, moe_w2,
               final_norm_g, True, batch, seq, ctx_len)
    return out.reshape(batch, seq, d)
```

```python
import functools

import jax
import jax.numpy as jnp
from jax import lax
from jax.experimental import pallas as pl
from jax.experimental.pallas import tpu as pltpu

F32 = jnp.float32
BF16 = jnp.bfloat16
HI = lax.Precision.HIGHEST

GRID_W = 64
ROPE_THETA = 10000.0
NORM_EPS = 1e-6
A_HEADS, A_DK, A_DV = 8, 128, 128
B_HEADS, B_Q_LORA, B_KV_LORA, B_NOPE, B_ROPE, B_DV = 8, 512, 256, 128, 64, 128
C_HEADS, C_KV_HEADS, C_DH = 8, 2, 128
D_HEADS, D_DK, D_DV, D_GATE_RANK = 4, 128, 256, 16
GLA_TAU = 16.0
N_EXPERTS, N_GROUPS = 16, 4

TQ = 256
SCAN_C = 64
SCAN_HEADS = 4
GLA_HEADS = 2
SCAN_BLOCK = 2048
TE = 256
TROUTE = 512
TCOMB = 256
DMA_UNROLL = 8
LANES = 128
LOG2E = 1.4426950408889634
VMEM_MIB = 56


def _cp(sem):
    return pltpu.CompilerParams(dimension_semantics=sem, vmem_limit_bytes=VMEM_MIB * 1024 * 1024)


def _rms(x, g):
    return x * lax.rsqrt(jnp.mean(x * x, axis=-1, keepdims=True) + NORM_EPS) * g


def _rope(x, cos, sin):
    half = x.shape[-1] // 2
    swapped = jnp.concatenate([x[:, half:], x[:, :half]], axis=-1)
    return x * cos + swapped * sin


def _dot_nt(a, b):
    return lax.dot_general(a, b, (((1,), (1,)), ((), ())), preferred_element_type=F32)


def _dot_tn(a, b):
    return lax.dot_general(a, b, (((0,), (0,)), ((), ())), preferred_element_type=F32)


def _modvec_kernel(c_ref, w_ref, b_ref, o_ref):
    c = c_ref[...]
    a = c * jax.nn.sigmoid(c)
    a_hi = a.astype(BF16)
    a_lo = (a - a_hi.astype(F32)).astype(BF16)
    w = w_ref[...]
    w_hi = w.astype(BF16)
    w_lo = (w - w_hi.astype(F32)).astype(BF16)
    acc = jnp.dot(a_hi, w_hi, preferred_element_type=F32) + jnp.dot(a_lo, w_hi, preferred_element_type=F32)
    o_ref[...] = acc + jnp.dot(a_hi, w_lo, preferred_element_type=F32) + b_ref[...]


def _modvec(cvec, mod_w, mod_b):
    n_layers, d, n6 = mod_w.shape
    tn = min(1024, n6)
    return pl.pallas_call(
        _modvec_kernel,
        grid=(n_layers, n6 // tn),
        in_specs=[pl.BlockSpec((8, d), lambda l, j: (0, 0)),
                  pl.BlockSpec((None, d, tn), lambda l, j: (l, 0, j)),
                  pl.BlockSpec((None, 1, tn), lambda l, j: (l, 0, j))],
        out_specs=pl.BlockSpec((None, 8, tn), lambda l, j: (l, 0, j)),
        out_shape=jax.ShapeDtypeStruct((n_layers, 8, n6), F32),
        compiler_params=_cp(("parallel", "parallel")),
        name="modvec",
    )(cvec, mod_w, mod_b.reshape(n_layers, 1, n6))


class _Rows:
    def __init__(self, batch, seq, ctx_len, tm):
        assert seq % tm == 0 and (batch * ctx_len) % tm == 0
        self.tm = tm
        self.batch = batch
        self.per_batch = seq // tm
        self.n_lat = batch * seq // tm
        self.n_ctx = batch * ctx_len // tm
        self.n_all = self.n_lat + self.n_ctx

    def mod_row(self, i):
        return jnp.where(i < self.n_lat, i // self.per_batch, self.batch)


def _mod_spec(rows, layer, chunk, d):
    return pl.BlockSpec((None, None, None, 1, d), lambda i, *_: (layer, rows.mod_row(i), chunk, 0, 0))


def _norm_mod_kernel(xl_ref, xc_ref, g_ref, sh_ref, sc_ref, wt_ref, o_ref, t_ref, *, n_lat):
    i = pl.program_id(0)

    def body(x_ref):
        y = _rms(x_ref[...], g_ref[...])
        h = (y * (1.0 + sc_ref[...]) + sh_ref[...]).astype(o_ref.dtype)
        o_ref[...] = h
        t_ref[...] = _dot_nt(h, wt_ref[...].astype(BF16))[:, :t_ref.shape[-1]].astype(t_ref.dtype)

    @pl.when(i < n_lat)
    def _():
        body(xl_ref)

    @pl.when(i >= n_lat)
    def _():
        body(xc_ref)


def _tail_spec(wt3, widx, col0, n_cols):
    assert col0 % LANES == 0 and n_cols <= LANES and col0 + n_cols == wt3.shape[1]
    return pl.BlockSpec((None, LANES, wt3.shape[2]), lambda i, *_: (widx, col0 // LANES, 0))


def _norm_mod(x_lat, x_ctx, ctx_block0, g, mods5, layer, rows, wt3, widx, col0, n_tail):
    d = x_lat.shape[-1]
    tm = rows.tm
    nl = rows.n_lat
    n = rows.n_all * tm
    return pl.pallas_call(
        functools.partial(_norm_mod_kernel, n_lat=nl),
        grid=(rows.n_all,),
        in_specs=[pl.BlockSpec((tm, d), lambda i: (jnp.minimum(i, nl - 1), 0)),
                  pl.BlockSpec((tm, d), lambda i: (ctx_block0 + jnp.maximum(i - nl, 0), 0)),
                  pl.BlockSpec((1, d), lambda i: (0, 0)),
                  _mod_spec(rows, layer, 0, d),
                  _mod_spec(rows, layer, 1, d),
                  _tail_spec(wt3, widx, col0, n_tail)],
        out_specs=[pl.BlockSpec((tm, d), lambda i: (i, 0)), pl.BlockSpec((tm, n_tail), lambda i: (i, 0))],
        out_shape=[jax.ShapeDtypeStruct((n, d), BF16), jax.ShapeDtypeStruct((n, n_tail), BF16)],
        compiler_params=_cp(("parallel",)),
        name="norm_mod",
    )(x_lat, x_ctx, g.reshape(1, d), mods5, mods5, wt3)


def _mm_kernel(a_ref, wt_ref, o_ref):
    o_ref[...] = _dot_nt(a_ref[...], wt_ref[...].astype(BF16)).astype(o_ref.dtype)


def _matmul(a, wt3, layer, n_cols, tn, tm):
    m, k = a.shape
    return pl.pallas_call(
        _mm_kernel,
        grid=(m // tm, n_cols // tn),
        in_specs=[pl.BlockSpec((tm, k), lambda i, j: (i, 0)),
                  pl.BlockSpec((None, tn, k), lambda i, j: (layer, j, 0))],
        out_specs=pl.BlockSpec((tm, tn), lambda i, j: (i, j)),
        out_shape=jax.ShapeDtypeStruct((m, n_cols), BF16),
        compiler_params=_cp(("parallel", "arbitrary")),
        name="in_proj",
    )(a, wt3)


def _out_proj_kernel(mal_ref, mac_ref, mb_ref, w_hbm, xl_ref, xc_ref, g_ref, o_ref, w_s, stage, *, n_lat, widx):
    i = pl.program_id(0)
    ka = mal_ref.shape[-1]

    @pl.when(i == 0)
    def _():
        rows = stage.shape[0]
        for c in range(w_s.shape[0] // rows):
            pltpu.sync_copy(w_hbm.at[widx, pl.ds(c * rows, rows)], stage)
            w_s[c * rows:(c + 1) * rows, :] = stage[...].astype(BF16)

    part_b = jnp.dot(mb_ref[...], w_s[ka:, :], preferred_element_type=F32)

    def finish(ma_ref, x_ref):
        acc = part_b + jnp.dot(ma_ref[...], w_s[:ka, :], preferred_element_type=F32)
        o_ref[...] = x_ref[...] + g_ref[...] * acc

    @pl.when(i < n_lat)
    def _():
        finish(mal_ref, xl_ref)

    @pl.when(i >= n_lat)
    def _():
        finish(mac_ref, xc_ref)


def _out_proj(mix_a_lat, mix_a_ctx, mix_b, w_out, widx, x_lat, x_ctx, ctx_block0, mods5, layer, rows, n_tiles):
    d = x_lat.shape[-1]
    ka, kb = mix_a_lat.shape[-1], mix_b.shape[-1]
    tm = rows.tm
    nl = rows.n_lat
    stage_rows = min(512, ka + kb)
    return pl.pallas_call(
        functools.partial(_out_proj_kernel, n_lat=nl, widx=widx),
        grid=(n_tiles,),
        in_specs=[pl.BlockSpec((tm, ka), lambda i: (jnp.minimum(i, nl - 1), 0)),
                  pl.BlockSpec((tm, ka), lambda i: (jnp.maximum(i - nl, 0), 0)),
                  pl.BlockSpec((tm, kb), lambda i: (i, 0)),
                  pl.BlockSpec(memory_space=pl.ANY),
                  pl.BlockSpec((tm, d), lambda i: (jnp.minimum(i, nl - 1), 0)),
                  pl.BlockSpec((tm, d), lambda i: (ctx_block0 + jnp.maximum(i - nl, 0), 0)),
                  _mod_spec(rows, layer, 2, d)],
        out_specs=pl.BlockSpec((tm, d), lambda i: (i, 0)),
        out_shape=jax.ShapeDtypeStruct((n_tiles * tm, d), F32),
        scratch_shapes=[pltpu.VMEM((ka + kb, d), BF16), pltpu.VMEM((stage_rows, d), F32)],
        compiler_params=_cp(("arbitrary",)),
        name="out_proj",
    )(mix_a_lat, mix_a_ctx, mix_b, w_out, x_lat, x_ctx, mods5)


def _tri(c, upper):
    r = lax.broadcasted_iota(jnp.int32, (c, c), 0)
    s = lax.broadcasted_iota(jnp.int32, (c, c), 1)
    return (s >= r) if upper else (r >= s)


def _scan_block(q, k, v, g, st, mask, forward):
    c = SCAN_C
    dk, dv = q.shape[-1], v.shape[-1]
    n = q.shape[0] // c
    mid, last = (c // 2 - 1, c - 1) if forward else (c // 2, 0)
    tri = jnp.broadcast_to(mask.astype(BF16)[None], (n, c, c))
    g3 = (g * LOG2E).reshape(n, c, dk)
    g_hi = g3.astype(BF16)
    g_lo = (g3 - g_hi.astype(F32)).astype(BF16)
    cum = (jnp.einsum('cts,csd->ctd', tri, g_hi, preferred_element_type=F32)
           + jnp.einsum('cts,csd->ctd', tri, g_lo, preferred_element_type=F32))
    m = cum[:, mid:mid + 1, :]
    tot = cum[:, last:last + 1, :]
    qe = (q.reshape(n, c, dk) * jnp.exp2(cum - m)).astype(BF16)
    ke = (k.reshape(n, c, dk) * jnp.exp2(m - cum)).astype(BF16)
    a = jnp.einsum('ctd,csd->cts', qe, ke, preferred_element_type=F32)
    a = jnp.where(mask[None], a, 0.0).astype(BF16)
    v3 = v.reshape(n, c, dv)
    o = jnp.einsum('cts,csv->ctv', a, v3, preferred_element_type=F32)
    u = jnp.einsum('csv,csd->cvd', v3, ke, preferred_element_type=F32)
    em = jnp.exp2(m)
    et = jnp.exp2(tot - m)
    states = [None] * n
    for ci in (range(n) if forward else reversed(range(n))):
        stp = st * em[ci]
        states[ci] = stp.astype(BF16)
        st = (stp + u[ci]) * et[ci]
    o = o + jnp.einsum('ctd,cvd->ctv', qe, jnp.stack(states), preferred_element_type=F32)
    return o.reshape(n * c, dv), st


def _scan_segments(segments, prep_f, prep_b, of_ref, ob_ref, dk, dv):
    low, up = _tri(SCAN_C, False), _tri(SCAN_C, True)
    carry = (jnp.zeros((dv, dk), F32), jnp.zeros((dv, dk), F32))
    for rows, off, seg in segments:
        rb_ = min(SCAN_BLOCK, rows)
        n = rows // rb_

        def body(i, carry, n=n, off=off, seg=seg, rb_=rb_):
            sf, sb = carry
            rf = pl.multiple_of(i * rb_, rb_)
            rb = pl.multiple_of((n - 1 - i) * rb_, rb_)
            q, k, v, g = prep_f(seg, rf, rb_)
            o, sf = _scan_block(q, k, v, g, sf, low, True)
            of_ref[pl.ds(off + rf, rb_), :] = o
            q, k, v, g = prep_b(seg, rb, rb_)
            o, sb = _scan_block(q, k, v, g, sb, up, False)
            ob_ref[pl.ds(off + rb, rb_), :] = o
            return sf, sb

        carry = lax.fori_loop(0, n, body, carry)


def _hgrn_kernel(ql, qc, f1l, f1c, f2l, f2c, vl, vc, gl, gc, lbf, lbb, ng, ol_ref, oc_ref, of_s, ob_s, *,
                 ctx_len, seq):
    refs = {0: (qc, f1c, f2c, vc), 1: (ql, f1l, f2l, vl)}
    scale = A_DK ** -0.5

    for hh in range(SCAN_HEADS):
        ks = slice(hh * A_DK, (hh + 1) * A_DK)
        vs = slice(hh * A_DV, (hh + 1) * A_DV)

        def prep(seg, r, nr, fi, lb_ref, ks=ks, vs=vs):
            x = refs[seg][0][pl.ds(r, nr), ks].astype(F32)
            q = x * jax.nn.sigmoid(x) * scale
            v = refs[seg][3][pl.ds(r, nr), vs]
            lb = lb_ref[:, ks]
            f = lb + (1.0 - lb) * jax.nn.sigmoid(refs[seg][fi][pl.ds(r, nr), ks].astype(F32))
            return q, 1.0 - f, v, jnp.log(f)

        _scan_segments(
            [(ctx_len, 0, 0), (seq, ctx_len, 1)],
            lambda seg, r, nr, prep=prep: prep(seg, r, nr, 1, lbf),
            lambda seg, r, nr, prep=prep: prep(seg, r, nr, 2, lbb),
            of_s.at[hh], ob_s.at[hh], A_DK, A_DV)

    def emit(gate_ref, out_ref, n_rows, row0):
        for rt in range(n_rows // TQ):
            rows = slice(rt * TQ, (rt + 1) * TQ)
            srows = slice(row0 + rt * TQ, row0 + (rt + 1) * TQ)
            ys = [_rms(of_s[hh, srows, :] + ob_s[hh, srows, :], ng[...]) for hh in range(SCAN_HEADS)]
            gate = gate_ref[rows, :].astype(F32)
            out_ref[rows, :] = (jnp.concatenate(ys, axis=-1) * jax.nn.sigmoid(gate)).astype(out_ref.dtype)

    emit(gc, oc_ref, ctx_len, 0)
    emit(gl, ol_ref, seq, ctx_len)


def _out_row_block(batch, seq, ctx_len):
    nct = ctx_len // TQ
    nlt = seq // TQ

    def f(b, rt):
        return jnp.where(rt < nct, batch * nlt + b * nct + rt, b * nlt + rt - nct)

    return f


def _hgrn(p, lb_f, lb_b, norm_g, batch, seq, ctx_len):
    h, dk, dv = A_HEADS, A_DK, A_DV
    hp = SCAN_HEADS
    hg = h // hp
    ctx_blk0 = batch * seq // ctx_len
    in_specs = []
    for kcol in range(5):
        in_specs.append(pl.BlockSpec((seq, hp * dk), lambda b, hh, kcol=kcol: (b, kcol * hg + hh)))
        in_specs.append(pl.BlockSpec((ctx_len, hp * dk), lambda b, hh, kcol=kcol: (ctx_blk0 + b, kcol * hg + hh)))
    vec = pl.BlockSpec((1, hp * dk), lambda b, hh: (0, hh))
    in_specs += [vec, vec, pl.BlockSpec((1, dv), lambda b, hh: (0, 0))]
    return pl.pallas_call(
        functools.partial(_hgrn_kernel, ctx_len=ctx_len, seq=seq),
        grid=(batch, hg),
        in_specs=in_specs,
        out_specs=[pl.BlockSpec((seq, hp * dv), lambda b, hh: (b, hh)),
                   pl.BlockSpec((ctx_len, hp * dv), lambda b, hh: (b, hh))],
        out_shape=[jax.ShapeDtypeStruct((batch * seq, h * dv), BF16),
                   jax.ShapeDtypeStruct((batch * ctx_len, h * dv), BF16)],
        scratch_shapes=[pltpu.VMEM((hp, seq + ctx_len, dv), F32), pltpu.VMEM((hp, seq + ctx_len, dv), F32)],
        compiler_params=_cp(("parallel", "parallel")),
        name="hgrn_scan",
    )(*([p] * 10), lb_f.reshape(1, h * dk), lb_b.reshape(1, h * dk), norm_g.reshape(1, dv))


def _gla_kernel(ql, qc, kl, kc, vl, vc, gl, al, ac, wa, ba, ng, o_ref, of_s, ob_s, *, ctx_len, seq):
    refs = {0: (qc, kc, vc, ac), 1: (ql, kl, vl, al)}
    scale = D_DK ** -0.5
    r16 = D_GATE_RANK

    for hh in range(GLA_HEADS):
        ks = slice(hh * D_DK, (hh + 1) * D_DK)
        vs = slice(hh * D_DV, (hh + 1) * D_DV)

        def prep(seg, r, nr, d, ks=ks, vs=vs):
            q = refs[seg][0][pl.ds(r, nr), ks].astype(F32) * scale
            k = refs[seg][1][pl.ds(r, nr), ks].astype(F32)
            v = refs[seg][2][pl.ds(r, nr), vs]
            a = refs[seg][3][pl.ds(r, nr), :].astype(F32)[:, d * r16:(d + 1) * r16]
            z = jnp.dot(a, wa[d, :, ks], preferred_element_type=F32, precision=HI) + ba[d, :, ks]
            g = (jnp.minimum(z, 0.0) - jnp.log(1.0 + jnp.exp(-jnp.abs(z)))) * (1.0 / GLA_TAU)
            return q, k, v, g

        _scan_segments(
            [(ctx_len, 0, 0), (seq, ctx_len, 1)],
            lambda seg, r, nr, prep=prep: prep(seg, r, nr, 0),
            lambda seg, r, nr, prep=prep: prep(seg, r, nr, 1),
            of_s.at[hh], ob_s.at[hh], D_DK, D_DV)

    for rt in range(seq // TQ):
        rows = slice(rt * TQ, (rt + 1) * TQ)
        srows = slice(ctx_len + rt * TQ, ctx_len + (rt + 1) * TQ)
        ys = [_rms(of_s[hh, srows, :] + ob_s[hh, srows, :], ng[...]) for hh in range(GLA_HEADS)]
        gate = gl[rows, :].astype(F32)
        o_ref[rows, :] = (jnp.concatenate(ys, axis=-1) * gate * jax.nn.sigmoid(gate)).astype(o_ref.dtype)


def _gla(p, ga, w_a2, b_a, norm_g, batch, seq, ctx_len):
    h, dk, dv = D_HEADS, D_DK, D_DV
    hp = GLA_HEADS
    hg = h // hp
    nlt = seq // TQ
    ctx_blk0 = batch * seq // ctx_len
    wk, wv = hp * dk, hp * dv
    q0 = (C_HEADS + 2 * C_KV_HEADS) * C_DH // wk
    k0 = q0 + hg
    v0 = (k0 + hg) * wk // wv
    g0 = v0 + hg

    def pair(width, blk0):
        return [pl.BlockSpec((seq, width), lambda b, hh: (b, blk0 + hh)),
                pl.BlockSpec((ctx_len, width), lambda b, hh: (ctx_blk0 + b, blk0 + hh))]

    in_specs = pair(wk, q0) + pair(wk, k0) + pair(wv, v0)
    in_specs += [pl.BlockSpec((seq, wv), lambda b, hh: (b, g0 + hh)),
                 pl.BlockSpec((seq, 2 * D_GATE_RANK), lambda b, hh: (b, 0)),
                 pl.BlockSpec((ctx_len, 2 * D_GATE_RANK), lambda b, hh: (ctx_blk0 + b, 0)),
                 pl.BlockSpec((2, D_GATE_RANK, wk), lambda b, hh: (0, 0, hh)),
                 pl.BlockSpec((2, 1, wk), lambda b, hh: (0, 0, hh)),
                 pl.BlockSpec((1, dv), lambda b, hh: (0, 0))]
    return pl.pallas_call(
        functools.partial(_gla_kernel, ctx_len=ctx_len, seq=seq),
        grid=(batch, hg),
        in_specs=in_specs,
        out_specs=pl.BlockSpec((seq, wv), lambda b, hh: (b, hh)),
        out_shape=jax.ShapeDtypeStruct((batch * seq, h * dv), BF16),
        scratch_shapes=[pltpu.VMEM((hp, seq + ctx_len, dv), F32), pltpu.VMEM((hp, seq + ctx_len, dv), F32)],
        compiler_params=_cp(("parallel", "parallel")),
        name="gla_scan",
    )(p, p, p, p, p, p, p, ga, ga, w_a2, b_a.reshape(2, 1, h * dk), norm_g.reshape(1, dv))


def _softmax_pv(s, v):
    m = jnp.max(s, axis=-1, keepdims=True)
    p = jnp.exp2(s - m)
    l = jnp.sum(p, axis=-1, keepdims=True)
    return jnp.dot(p.astype(BF16), v, preferred_element_type=F32) / l


def _mla_kernel(ql_ref, kvl_ref, kvc_ref, krl_ref, krc_ref, wqn_ref, wqr_ref, wkv_ref, gq_ref, gkv_ref,
                cosq_ref, sinq_ref, cosk_ref, sink_ref, o_ref, k_s, v_s, *, ctx_len):
    qt = pl.program_id(1)
    n_ctx_tiles = ctx_len // TQ
    scale = (B_NOPE + B_ROPE) ** -0.5 * LOG2E
    dkv = B_NOPE + B_DV

    @pl.when(qt == 0)
    def _prep():
        kvc = _rms(kvc_ref[...].astype(F32), gkv_ref[...]).astype(BF16)
        kvl = _rms(kvl_ref[...].astype(F32), gkv_ref[...]).astype(BF16)
        kr_c = krc_ref[...]
        kr_l = _rope(krl_ref[...].astype(F32), cosk_ref[...], sink_ref[...]).astype(BF16)
        for h in range(B_HEADS):
            w = wkv_ref[:, h * dkv:(h + 1) * dkv].astype(BF16)
            up_c = jnp.dot(kvc, w, preferred_element_type=F32)
            k_s[h, 0:ctx_len, :] = jnp.concatenate([up_c[:, :B_NOPE].astype(BF16), kr_c], axis=-1)
            v_s[h, 0:ctx_len, :] = up_c[:, B_NOPE:].astype(BF16)
            up_l = jnp.dot(kvl, w, preferred_element_type=F32)
            k_s[h, ctx_len:, :] = jnp.concatenate([up_l[:, :B_NOPE].astype(BF16), kr_l], axis=-1)
            v_s[h, ctx_len:, :] = up_l[:, B_NOPE:].astype(BF16)

    xn = _rms(ql_ref[...].astype(F32), gq_ref[...]).astype(BF16)
    qn_all = jnp.dot(xn, wqn_ref[...].astype(BF16), preferred_element_type=F32) * scale
    qr_all = jnp.dot(xn, wqr_ref[...].astype(BF16), preferred_element_type=F32) * scale

    def heads(n_keys, rotate):
        outs = []
        for h in range(B_HEADS):
            qn = qn_all[:, h * B_NOPE:(h + 1) * B_NOPE]
            qr = qr_all[:, h * B_ROPE:(h + 1) * B_ROPE]
            if rotate:
                qr = _rope(qr, cosq_ref[...], sinq_ref[...])
            q = jnp.concatenate([qn, qr], axis=-1).astype(BF16)
            s = _dot_nt(q, k_s[h, 0:n_keys, :])
            outs.append(_softmax_pv(s, v_s[h, 0:n_keys, :]).astype(o_ref.dtype))
        o_ref[...] = jnp.concatenate(outs, axis=-1)

    @pl.when(qt < n_ctx_tiles)
    def _():
        heads(ctx_len, False)

    @pl.when(qt >= n_ctx_tiles)
    def _():
        heads(k_s.shape[1], True)


def _mla(p, kr, w_uq, w_ukv, gq, gkv, cos, sin, batch, seq, ctx_len):
    h = B_HEADS
    nct, nlt = ctx_len // TQ, seq // TQ
    ctx_blk0 = batch * seq // ctx_len
    row_block = _out_row_block(batch, seq, ctx_len)
    ql_blk = 5 * A_HEADS * A_DK // B_Q_LORA
    kv_blk = (5 * A_HEADS * A_DK + B_Q_LORA) // B_KV_LORA
    dq = B_NOPE + B_ROPE
    s_all = seq + ctx_len
    w3 = w_uq.reshape(B_Q_LORA, h, dq)
    wq_n = w3[:, :, :B_NOPE].reshape(B_Q_LORA, h * B_NOPE)
    wq_r = w3[:, :, B_NOPE:].reshape(B_Q_LORA, h * B_ROPE)
    in_specs = [
        pl.BlockSpec((TQ, B_Q_LORA), lambda b, qt: (row_block(b, qt), ql_blk)),
        pl.BlockSpec((seq, B_KV_LORA), lambda b, qt: (b, kv_blk)),
        pl.BlockSpec((ctx_len, B_KV_LORA), lambda b, qt: (ctx_blk0 + b, kv_blk)),
        pl.BlockSpec((seq, B_ROPE), lambda b, qt: (b, 0)),
        pl.BlockSpec((ctx_len, B_ROPE), lambda b, qt: (ctx_blk0 + b, 0)),
        pl.BlockSpec((B_Q_LORA, h * B_NOPE), lambda b, qt: (0, 0)),
        pl.BlockSpec((B_Q_LORA, h * B_ROPE), lambda b, qt: (0, 0)),
        pl.BlockSpec((B_KV_LORA, h * (B_NOPE + B_DV)), lambda b, qt: (0, 0)),
        pl.BlockSpec((1, B_Q_LORA), lambda b, qt: (0, 0)),
        pl.BlockSpec((1, B_KV_LORA), lambda b, qt: (0, 0)),
        pl.BlockSpec((TQ, B_ROPE), lambda b, qt: (jnp.maximum(qt - nct, 0), 0)),
        pl.BlockSpec((TQ, B_ROPE), lambda b, qt: (jnp.maximum(qt - nct, 0), 0)),
        pl.BlockSpec((seq, B_ROPE), lambda b, qt: (0, 0)),
        pl.BlockSpec((seq, B_ROPE), lambda b, qt: (0, 0)),
    ]
    return pl.pallas_call(
        functools.partial(_mla_kernel, ctx_len=ctx_len),
        grid=(batch, nct + nlt),
        in_specs=in_specs,
        out_specs=pl.BlockSpec((TQ, h * B_DV), lambda b, qt: (row_block(b, qt), 0)),
        out_shape=jax.ShapeDtypeStruct((batch * s_all, h * B_DV), BF16),
        scratch_shapes=[pltpu.VMEM((h, s_all, dq), BF16), pltpu.VMEM((h, s_all, B_DV), BF16)],
        compiler_params=_cp(("parallel", "arbitrary")),
        name="mla_attn",
    )(p, p, p, kr, kr, wq_n, wq_r, w_ukv, gq.reshape(1, -1), gkv.reshape(1, -1), cos, sin, cos, sin)


def _gqa_kernel(q_ref, kl_ref, kc_ref, vl_ref, vc_ref, gq_ref, gk_ref, cosq_ref, sinq_ref, cosk_ref, sink_ref,
                o_ref, k_s, v_s, *, ctx_len):
    qt = pl.program_id(1)
    scale = C_DH ** -0.5 * LOG2E
    dh = C_DH
    grp = C_HEADS // C_KV_HEADS

    @pl.when(qt == 0)
    def _prep():
        for kh in range(C_KV_HEADS):
            cols = slice(kh * dh, (kh + 1) * dh)
            k_s[kh, 0:ctx_len, :] = _rms(kc_ref[:, cols].astype(F32), gk_ref[...]).astype(BF16)
            kl = _rms(kl_ref[:, cols].astype(F32), gk_ref[...])
            k_s[kh, ctx_len:, :] = _rope(kl, cosk_ref[...], sink_ref[...]).astype(BF16)
            v_s[kh, 0:ctx_len, :] = vc_ref[:, cols]
            v_s[kh, ctx_len:, :] = vl_ref[:, cols]

    outs = []
    for hq in range(C_HEADS):
        kh = hq // grp
        q = _rms(q_ref[:, hq * dh:(hq + 1) * dh].astype(F32), gq_ref[...])
        q = _rope(q, cosq_ref[...], sinq_ref[...]) * scale
        s = _dot_nt(q.astype(BF16), k_s[kh])
        outs.append(_softmax_pv(s, v_s[kh]).astype(o_ref.dtype))
    o_ref[...] = jnp.concatenate(outs, axis=-1)


def _gqa(p, gq, gk, cos, sin, batch, seq, ctx_len):
    kvh, dh = C_KV_HEADS, C_DH
    nlt = seq // TQ
    ctx_blk0 = batch * seq // ctx_len
    wq, wkv = C_HEADS * dh, kvh * dh
    k0 = wq // wkv
    v0 = k0 + 1
    s_all = seq + ctx_len
    in_specs = [
        pl.BlockSpec((TQ, wq), lambda b, qt: (b * nlt + qt, 0)),
        pl.BlockSpec((seq, wkv), lambda b, qt: (b, k0)),
        pl.BlockSpec((ctx_len, wkv), lambda b, qt: (ctx_blk0 + b, k0)),
        pl.BlockSpec((seq, wkv), lambda b, qt: (b, v0)),
        pl.BlockSpec((ctx_len, wkv), lambda b, qt: (ctx_blk0 + b, v0)),
        pl.BlockSpec((1, dh), lambda b, qt: (0, 0)),
        pl.BlockSpec((1, dh), lambda b, qt: (0, 0)),
        pl.BlockSpec((TQ, dh), lambda b, qt: (qt, 0)),
        pl.BlockSpec((TQ, dh), lambda b, qt: (qt, 0)),
        pl.BlockSpec((seq, dh), lambda b, qt: (0, 0)),
        pl.BlockSpec((seq, dh), lambda b, qt: (0, 0)),
    ]
    return pl.pallas_call(
        functools.partial(_gqa_kernel, ctx_len=ctx_len),
        grid=(batch, nlt),
        in_specs=in_specs,
        out_specs=pl.BlockSpec((TQ, wq), lambda b, qt: (b * nlt + qt, 0)),
        out_shape=jax.ShapeDtypeStruct((batch * seq, wq), BF16),
        scratch_shapes=[pltpu.VMEM((kvh, s_all, dh), BF16), pltpu.VMEM((kvh, s_all, dh), BF16)],
        compiler_params=_cp(("parallel", "arbitrary")),
        name="gqa_attn",
    )(p, p, p, p, p, gq.reshape(1, dh), gk.reshape(1, dh), cos, sin, cos, sin)


def _router_kernel(x_ref, g_ref, sh_ref, sc_ref, rw_ref, rb_ref, h_ref, ri_ref, rf_ref, cnt_ref, base_s):
    i = pl.program_id(0)
    tm = x_ref.shape[0]
    ne = N_EXPERTS
    per = ne // N_GROUPS

    @pl.when(i == 0)
    def _():
        base_s[...] = jnp.zeros_like(base_s)

    h = _rms(x_ref[...], g_ref[...]) * (1.0 + sc_ref[...]) + sh_ref[...]
    nch = h.shape[1] // LANES
    for j in range(nch):
        h_ref[pl.ds(j, tm, stride=nch), :] = h[:, j * LANES:(j + 1) * LANES]
    h_hi = h.astype(BF16)
    h_lo = (h - h_hi.astype(F32)).astype(BF16)
    rw = rw_ref[...]
    w_hi = rw.astype(BF16)
    w_lo = (rw - w_hi.astype(F32)).astype(BF16)
    hw = jnp.dot(h_hi, jnp.concatenate([w_hi, w_lo], axis=1), preferred_element_type=F32)
    logits = hw[:, :ne] + hw[:, ne:] + jnp.dot(h_lo, w_hi, preferred_element_type=F32)
    scores = jax.nn.sigmoid(logits)
    sel = scores + rb_ref[...]
    lane = lax.broadcasted_iota(jnp.int32, (tm, ne), 1).astype(F32)
    neg = -jnp.inf
    big = float(ne)

    def top2(vals):
        m1 = jnp.max(vals, axis=1, keepdims=True)
        i1 = jnp.min(jnp.where(vals == m1, lane, big), axis=1, keepdims=True)
        rest = jnp.where(lane == i1, neg, vals)
        m2 = jnp.max(rest, axis=1, keepdims=True)
        i2 = jnp.min(jnp.where(rest == m2, lane, big), axis=1, keepdims=True)
        return m1 + m2, i1, i2

    best, e1, e2 = None, None, None
    for grp in range(N_GROUPS):
        in_grp = jnp.logical_and(lane >= float(grp * per), lane < float((grp + 1) * per))
        gsum, i1, i2 = top2(jnp.where(in_grp, sel, neg))
        if grp == 0:
            best, e1, e2 = gsum, i1, i2
        else:
            better = gsum > best
            best = jnp.where(better, gsum, best)
            e1 = jnp.where(better, i1, e1)
            e2 = jnp.where(better, i2, e2)

    hot1 = lane == e1
    hot2 = lane == e2
    w1 = jnp.sum(jnp.where(hot1, scores, 0.0), axis=1, keepdims=True)
    w2 = jnp.sum(jnp.where(hot2, scores, 0.0), axis=1, keepdims=True)
    wsum = w1 + w2
    assign = jnp.logical_or(hot1, hot2)
    r = lax.broadcasted_iota(jnp.int32, (tm, tm), 0)
    c = lax.broadcasted_iota(jnp.int32, (tm, tm), 1)
    before = (c < r).astype(BF16)
    excl = jnp.dot(before, assign.astype(BF16), preferred_element_type=F32) + base_s[...]
    rank1 = jnp.sum(jnp.where(hot1, excl, 0.0), axis=1, keepdims=True)
    rank2 = jnp.sum(jnp.where(hot2, excl, 0.0), axis=1, keepdims=True)
    base_s[...] = base_s[...] + jnp.sum(assign.astype(F32), axis=0, keepdims=True)

    l128 = lax.broadcasted_iota(jnp.int32, (tm, 128), 1)
    ri = jnp.where(l128 == 0, e1, jnp.where(l128 == 1, e2, jnp.where(l128 == 2, rank1, jnp.where(l128 == 3, rank2, 0.0))))
    ri_ref[...] = ri.T[0:8, :].astype(jnp.int32)
    rf_ref[...] = jnp.where(l128 == 0, w1 / wsum, jnp.where(l128 == 1, w2 / wsum, 0.0))
    cnt_ref[...] = jnp.broadcast_to(base_s[...], cnt_ref.shape)


def _router(x_all, g, mods5, layer, router_w, router_b, rows, n_tiles):
    d = x_all.shape[-1]
    tm = rows.tm
    n = n_tiles * tm
    ne = N_EXPERTS
    return pl.pallas_call(
        _router_kernel,
        grid=(n_tiles,),
        in_specs=[pl.BlockSpec((tm, d), lambda i: (i, 0)),
                  pl.BlockSpec((1, d), lambda i: (0, 0)),
                  _mod_spec(rows, layer, 3, d),
                  _mod_spec(rows, layer, 4, d),
                  pl.BlockSpec((d, ne), lambda i: (0, 0)),
                  pl.BlockSpec((1, ne), lambda i: (0, 0))],
        out_specs=[pl.BlockSpec((tm * (d // LANES), LANES), lambda i: (i, 0)),
                   pl.BlockSpec((8, tm), lambda i: (0, i)),
                   pl.BlockSpec((tm, 128), lambda i: (i, 0)),
                   pl.BlockSpec((8, ne), lambda i: (0, 0))],
        out_shape=[jax.ShapeDtypeStruct((n * (d // LANES), LANES), F32),
                   jax.ShapeDtypeStruct((8, n), jnp.int32),
                   jax.ShapeDtypeStruct((n, 128), F32),
                   jax.ShapeDtypeStruct((8, ne), F32)],
        scratch_shapes=[pltpu.VMEM((1, ne), F32)],
        compiler_params=_cp(("arbitrary",)),
        name="moe_router",
    )(x_all, g.reshape(1, d), mods5, mods5, router_w, router_b.reshape(1, ne))


def _expert_kernel(te_ref, nx_ref, nu_ref, pad_ref, dest_ref, h_hbm, w1_hbm, w3_hbm, w2_hbm, y_ref, xbuf, sem,
                   w1_s, w3_s, w2_s, wf1, wf3, wf2, wsem, wslot_ref, src_ref, *, n_tok, layer):
    r = pl.program_id(0)
    n_used = nu_ref[0]
    active = r < n_used
    changed = jnp.logical_or(r == 0, te_ref[r] != te_ref[jnp.maximum(r - 1, 0)])
    slot = lax.rem(r, 2)
    nch = xbuf.shape[1] // TE

    def weight_copies(e, ws):
        return (pltpu.make_async_copy(w1_hbm.at[layer, e], wf1.at[ws], wsem.at[ws]),
                pltpu.make_async_copy(w3_hbm.at[layer, e], wf3.at[ws], wsem.at[ws]),
                pltpu.make_async_copy(w2_hbm.at[layer, e], wf2.at[ws], wsem.at[ws]))

    @pl.when(r == 0)
    def _():
        wslot_ref[0] = 0
        for cp in weight_copies(te_ref[0], 0):
            cp.start()

        def clear(i, carry):
            src_ref[i] = 0
            return carry

        for e in range(N_EXPERTS):
            lax.fori_loop(pad_ref[e], pad_ref[N_EXPERTS + e], clear, 0)

        def invert(i, carry):
            src_ref[dest_ref[i]] = i
            src_ref[dest_ref[n_tok + i]] = i
            return carry

        lax.fori_loop(0, n_tok, invert, 0, unroll=DMA_UNROLL)

    def gather(tile, dst_slot):
        base = tile * TE
        for t in range(TE):
            row0 = pl.multiple_of(src_ref[base + t] * nch, nch)
            pltpu.make_async_copy(h_hbm.at[pl.ds(row0, nch)], xbuf.at[dst_slot, pl.ds(t * nch, nch)],
                                  sem.at[dst_slot]).start(priority=0)

    @pl.when(jnp.logical_and(r == 0, active))
    def _():
        gather(0, 0)

    @pl.when(r + 1 < n_used)
    def _():
        gather(r + 1, 1 - slot)

    @pl.when(jnp.logical_and(active, changed))
    def _():
        ws = wslot_ref[0]
        for cp in weight_copies(te_ref[r], ws):
            cp.wait()

        @pl.when(nx_ref[r] != te_ref[r])
        def _():
            for cp in weight_copies(nx_ref[r], 1 - ws):
                cp.start(priority=1)

        w1_s[...] = wf1[ws].astype(BF16)
        w3_s[...] = wf3[ws].astype(BF16)
        w2_s[...] = wf2[ws].astype(BF16)
        wslot_ref[0] = 1 - ws

    @pl.when(active)
    def _():
        pltpu.make_async_copy(h_hbm.at[pl.ds(0, TE * nch)], xbuf.at[slot], sem.at[slot]).wait()
        x = jnp.concatenate([xbuf[slot, pl.ds(j, TE, stride=nch), :] for j in range(nch)], axis=1).astype(BF16)
        a = jnp.dot(x, w1_s[...], preferred_element_type=F32)
        b = jnp.dot(x, w3_s[...], preferred_element_type=F32)
        hid = (a * jax.nn.sigmoid(a) * b).astype(BF16)
        y_ref[...] = jnp.dot(hid, w2_s[...], preferred_element_type=F32)

    @pl.when(jnp.logical_not(active))
    def _():
        y_ref[...] = jnp.zeros_like(y_ref)


def _experts(tile_expert, next_expert, n_used, pad, dest, hp, w1, w3, w2, layer, p_max):
    d = w1.shape[-2]
    nch = d // LANES
    n_tok = hp.shape[0] // nch
    f = w1.shape[-1]
    any_spec = pl.BlockSpec(memory_space=pl.ANY)
    grid_spec = pltpu.PrefetchScalarGridSpec(
        num_scalar_prefetch=5,
        grid=(p_max // TE,),
        in_specs=[any_spec, any_spec, any_spec, any_spec],
        out_specs=pl.BlockSpec((TE, d), lambda r, te, nx, nu, pd, sr: (r, 0)),
        scratch_shapes=[pltpu.VMEM((2, TE * nch, LANES), F32), pltpu.SemaphoreType.DMA((2,)),
                        pltpu.VMEM((d, f), BF16), pltpu.VMEM((d, f), BF16), pltpu.VMEM((f, d), BF16),
                        pltpu.VMEM((2, d, f), F32), pltpu.VMEM((2, d, f), F32), pltpu.VMEM((2, f, d), F32),
                        pltpu.SemaphoreType.DMA((2,)), pltpu.SMEM((1,), jnp.int32),
                        pltpu.SMEM((p_max,), jnp.int32)],
    )
    return pl.pallas_call(
        functools.partial(_expert_kernel, n_tok=n_tok, layer=layer),
        grid_spec=grid_spec,
        out_shape=jax.ShapeDtypeStruct((p_max, d), F32),
        compiler_params=_cp(("arbitrary",)),
        name="moe_experts",
    )(tile_expert, next_expert, n_used, pad, dest, hp, w1, w3, w2)


def _combine_kernel(dest_ref, x_ref, rf_ref, g_ref, pg_ref, psh_ref, psc_ref, ys_hbm, *rest, tm, n_tok, n_tiles, final):
    o_ref = rest[0] if final else rest[1]
    buf, sem = rest[-2], rest[-1]
    i = pl.program_id(0)
    slot = lax.rem(i, 2)

    def gather(tile, dst_slot):
        base = tile * tm
        for t in range(tm):
            pltpu.make_async_copy(ys_hbm.at[pl.ds(dest_ref[base + t], 1)], buf.at[dst_slot, 0, pl.ds(t, 1)],
                                  sem.at[dst_slot]).start(priority=0)
            pltpu.make_async_copy(ys_hbm.at[pl.ds(dest_ref[n_tok + base + t], 1)], buf.at[dst_slot, 1, pl.ds(t, 1)],
                                  sem.at[dst_slot]).start(priority=1)

    @pl.when(i == 0)
    def _():
        gather(0, 0)

    @pl.when(i + 1 < n_tiles)
    def _():
        gather(i + 1, 1 - slot)

    pltpu.make_async_copy(ys_hbm.at[pl.ds(0, tm)], buf.at[slot, 0], sem.at[slot]).wait()
    pltpu.make_async_copy(ys_hbm.at[pl.ds(0, tm)], buf.at[slot, 1], sem.at[slot]).wait()
    w = rf_ref[...]
    y = w[:, 0:1] * buf[slot, 0] + w[:, 1:2] * buf[slot, 1]
    x2 = x_ref[...] + g_ref[...] * y
    z = _rms(x2, pg_ref[...])
    if final:
        o_ref[...] = z
    else:
        wt_ref, _, h_ref, t_ref = rest[:4]
        o_ref[...] = x2
        h = (z * (1.0 + psc_ref[...]) + psh_ref[...]).astype(h_ref.dtype)
        h_ref[...] = h
        t_ref[...] = _dot_nt(h, wt_ref[...].astype(BF16))[:, :t_ref.shape[-1]].astype(t_ref.dtype)


def _combine(dest, x_all, rf, mods5, layer, post_g, ys, rows, n_tiles, n_tok_total, final, tail=None):
    d = x_all.shape[-1]
    tm = rows.tm
    n = n_tiles * tm
    nxt = min(layer + 1, mods5.shape[0] - 1)
    row_spec = pl.BlockSpec((tm, d), lambda i, dr: (i, 0))
    in_specs = [row_spec,
                pl.BlockSpec((tm, 128), lambda i, dr: (i, 0)),
                _mod_spec(rows, layer, 5, d),
                pl.BlockSpec((1, d), lambda i, dr: (0, 0)),
                _mod_spec(rows, nxt, 0, d),
                _mod_spec(rows, nxt, 1, d),
                pl.BlockSpec(memory_space=pl.ANY)]
    args = [dest, x_all, rf, mods5, post_g.reshape(1, d), mods5, mods5, ys]
    stream = jax.ShapeDtypeStruct((n, d), F32)
    if final:
        out_specs, out_shape = row_spec, stream
    else:
        wt3, widx, col0, n_tail = tail
        in_specs.append(_tail_spec(wt3, widx, col0, n_tail))
        args.append(wt3)
        out_specs = [row_spec, row_spec, pl.BlockSpec((tm, n_tail), lambda i, dr: (i, 0))]
        out_shape = [stream, jax.ShapeDtypeStruct((n, d), BF16), jax.ShapeDtypeStruct((n, n_tail), BF16)]
    grid_spec = pltpu.PrefetchScalarGridSpec(
        num_scalar_prefetch=1,
        grid=(n_tiles,),
        in_specs=in_specs,
        out_specs=out_specs,
        scratch_shapes=[pltpu.VMEM((2, 2, tm, d), F32), pltpu.SemaphoreType.DMA((2,))],
    )
    return pl.pallas_call(
        functools.partial(_combine_kernel, tm=tm, n_tok=n_tok_total, n_tiles=n_tiles, final=final),
        grid_spec=grid_spec,
        out_shape=out_shape,
        compiler_params=_cp(("arbitrary",)),
        name="moe_combine",
    )(*args)


def _moe(x_all, n_tok, norm_g, mods5, layer, router_w, router_b, w1, w3, w2, final_g, final, batch, seq, ctx_len,
         tail=None):
    rows_r = _Rows(batch, seq, ctx_len, TROUTE)
    rows_c = _Rows(batch, seq, ctx_len, TCOMB)
    h, ri, rf, cnt = _router(x_all, norm_g, mods5, layer, router_w, router_b, rows_r, n_tok // TROUTE)
    counts = cnt[0].astype(jnp.int32)
    padded = ((counts + TE - 1) // TE) * TE
    ends = jnp.cumsum(padded)
    starts = ends - padded
    e1, e2, r1, r2 = ri[0], ri[1], ri[2], ri[3]
    dest = jnp.concatenate([starts[e1] + r1, starts[e2] + r2]).astype(jnp.int32)
    p_max = 2 * n_tok + N_EXPERTS * TE
    n_tiles = p_max // TE
    n_used = (ends[-1] // TE).astype(jnp.int32)
    tile_start = jnp.arange(n_tiles, dtype=jnp.int32) * TE
    tile_expert = jnp.sum((tile_start[:, None] >= ends[None, :]).astype(jnp.int32), axis=1)
    last_expert = jnp.sum((jnp.maximum(ends[-1] - 1, 0) >= ends).astype(jnp.int32))
    tile_expert = jnp.minimum(jnp.where(tile_start < ends[-1], tile_expert, last_expert), N_EXPERTS - 1).astype(jnp.int32)
    eid = jnp.arange(N_EXPERTS, dtype=jnp.int32)
    later_used = jnp.logical_and(eid[None, :] > eid[:, None], (padded > 0)[None, :])
    next_used = jnp.min(jnp.where(later_used, eid[None, :], N_EXPERTS), axis=1)
    next_used = jnp.where(next_used == N_EXPERTS, eid, next_used)
    next_expert = jnp.sum(jnp.where(tile_expert[:, None] == eid[None, :], next_used[None, :], 0), axis=1).astype(jnp.int32)
    pad = jnp.concatenate([starts + counts, ends]).astype(jnp.int32)
    ys = _experts(tile_expert, next_expert, n_used.reshape(1), pad, dest, h, w1, w3, w2, layer, p_max)
    return _combine(dest, x_all, rf, mods5, layer, final_g, ys, rows_c, n_tok // TCOMB, n_tok, final, tail)


def _rope_tables(t_len, d_rope):
    rows = t_len // GRID_W
    quarter = d_rope // 4
    freqs = ROPE_THETA ** (-jnp.arange(quarter, dtype=F32) / quarter)
    row = jnp.repeat(jnp.arange(rows, dtype=F32), GRID_W)
    col = jnp.tile(jnp.arange(GRID_W, dtype=F32), rows)
    ang = jnp.concatenate([row[:, None] * freqs, col[:, None] * freqs], axis=-1)
    cos, sin = jnp.cos(ang), jnp.sin(ang)
    return jnp.concatenate([cos, cos], axis=-1), jnp.concatenate([-sin, sin], axis=-1)


def kernel(x, c, ctx, c_ctx, mod_w, mod_b, norm_attn_g, norm_ffn_g, final_norm_g, ab_w_in, ab_w_out, hgrn_lb_logits, hgrn_norm_g, mla_q_norm_g, mla_w_uq, mla_kv_norm_g, mla_w_ukv, cd_w_in, cd_w_out, gqa_q_norm_g, gqa_k_norm_g, gla_w_a2, gla_b_a, gla_norm_g, router_w, router_b, moe_w1, moe_w3, moe_w2):
    batch, seq, d = x.shape
    ctx_len = ctx.shape[1]
    n_lat, n_ctx = batch * seq, batch * ctx_len
    assert ctx_len % TQ == 0 and seq % TQ == 0 and seq % ctx_len == 0 and batch < 8
    tm = min(1024, seq, n_ctx)
    rows = _Rows(batch, seq, ctx_len, tm)

    cvec = jnp.concatenate([c, c_ctx[None, :], jnp.zeros((8 - batch - 1, d), F32)], axis=0)
    mods = _modvec(cvec, mod_w, mod_b)
    mods5 = mods.reshape(mods.shape[0], 8, 6, 1, d)

    cos_b, sin_b = _rope_tables(seq, B_ROPE)
    cos_c, sin_c = _rope_tables(seq, C_DH)
    lb = jnp.cumsum(jax.nn.softmax(hgrn_lb_logits.astype(F32), axis=1), axis=1)

    x_lat = x.reshape(n_lat, d)
    x_ctx = ctx.reshape(n_ctx, d)

    ab_main = 5 * A_HEADS * A_DK + B_Q_LORA + B_KV_LORA
    tm_mm = next(t for t in (2304, 2048, 1536, 1024, 512, 256) if (n_lat + n_ctx) % t == 0)
    ab_wt = jnp.swapaxes(ab_w_in, 1, 2)
    cd_wt = jnp.swapaxes(cd_w_in, 1, 2)
    h0, kr0 = _norm_mod(x_lat, x_ctx, 0, norm_attn_g[0], mods5, 0, rows, ab_wt, 0, ab_main, B_ROPE)
    p0 = _matmul(h0, ab_wt, 0, ab_main, 256, tm_mm)
    mix_a_lat, mix_a_ctx = _hgrn(p0, lb[0, 0], lb[1, 0], hgrn_norm_g[0], batch, seq, ctx_len)
    mix_b = _mla(p0, kr0, mla_w_uq[0], mla_w_ukv[0], mla_q_norm_g[0], mla_kv_norm_g[0], cos_b, sin_b,
                 batch, seq, ctx_len)
    rows_o = _Rows(batch, seq, ctx_len, min(512, tm))
    x1 = _out_proj(mix_a_lat, mix_a_ctx, mix_b, ab_w_out, 0, x_lat, x_ctx, 0, mods5, 0, rows_o, rows_o.n_all)
    cd_main = (C_HEADS + 2 * C_KV_HEADS) * C_DH + 2 * D_HEADS * D_DK + 2 * D_HEADS * D_DV
    x2, h1, ga1 = _moe(x1, n_lat + n_ctx, norm_ffn_g[0], mods5, 0, router_w, router_b, moe_w1, moe_w3, moe_w2,
                       norm_attn_g[1], False, batch, seq, ctx_len, tail=(cd_wt, 0, cd_main, 2 * D_GATE_RANK))

    p1 = _matmul(h1, cd_wt, 0, cd_main, 512, tm_mm)
    mix_c = _gqa(p1, gqa_q_norm_g[0], gqa_k_norm_g[0], cos_c, sin_c, batch, seq, ctx_len)
    mix_d = _gla(p1, ga1, gla_w_a2[0], gla_b_a[0], gla_norm_g[0], batch, seq, ctx_len)
    x3 = _out_proj(mix_c, mix_c, mix_d, cd_w_out, 0, x2, x2, rows_o.n_lat, mods5, 1, rows_o, rows_o.n_lat)
    out = _moe(x3, n_lat, norm_ffn_g[1], mods5, 1, router_w, router_b, moe_w1, moe_w3, moe_w2,
               final_norm_g, True, batch, seq, ctx_len)
    return out.reshape(batch, seq, d)
```

```python
import functools

import jax
import jax.numpy as jnp
from jax import lax
from jax.experimental import pallas as pl
from jax.experimental.pallas import tpu as pltpu

F32 = jnp.float32
BF16 = jnp.bfloat16
HI = lax.Precision.HIGHEST

GRID_W = 64
ROPE_THETA = 10000.0
NORM_EPS = 1e-6
A_HEADS, A_DK, A_DV = 8, 128, 128
B_HEADS, B_Q_LORA, B_KV_LORA, B_NOPE, B_ROPE, B_DV = 8, 512, 256, 128, 64, 128
C_HEADS, C_KV_HEADS, C_DH = 8, 2, 128
D_HEADS, D_DK, D_DV, D_GATE_RANK = 4, 128, 256, 16
GLA_TAU = 16.0
N_EXPERTS, N_GROUPS = 16, 4

TQ = 256
SCAN_C = 64
SCAN_HEADS = 4
GLA_HEADS = 2
SCAN_BLOCK = 2048
TE = 256
TROUTE = 512
TCOMB = 256
DMA_UNROLL = 8
LANES = 128
LOG2E = 1.4426950408889634
VMEM_MIB = 56


def _cp(sem):
    return pltpu.CompilerParams(dimension_semantics=sem, vmem_limit_bytes=VMEM_MIB * 1024 * 1024)


def _rms(x, g):
    return x * lax.rsqrt(jnp.mean(x * x, axis=-1, keepdims=True) + NORM_EPS) * g


def _rope(x, cos, sin):
    half = x.shape[-1] // 2
    swapped = jnp.concatenate([x[:, half:], x[:, :half]], axis=-1)
    return x * cos + swapped * sin


def _dot_nt(a, b):
    return lax.dot_general(a, b, (((1,), (1,)), ((), ())), preferred_element_type=F32)


def _dot_tn(a, b):
    return lax.dot_general(a, b, (((0,), (0,)), ((), ())), preferred_element_type=F32)


def _modvec_kernel(c_ref, w_ref, b_ref, o_ref):
    c = c_ref[...]
    a = c * jax.nn.sigmoid(c)
    a_hi = a.astype(BF16)
    a_lo = (a - a_hi.astype(F32)).astype(BF16)
    w = w_ref[...]
    w_hi = w.astype(BF16)
    w_lo = (w - w_hi.astype(F32)).astype(BF16)
    acc = jnp.dot(a_hi, w_hi, preferred_element_type=F32) + jnp.dot(a_lo, w_hi, preferred_element_type=F32)
    o_ref[...] = acc + jnp.dot(a_hi, w_lo, preferred_element_type=F32) + b_ref[...]


def _modvec(cvec, mod_w, mod_b):
    n_layers, d, n6 = mod_w.shape
    tn = min(1024, n6)
    return pl.pallas_call(
        _modvec_kernel,
        grid=(n_layers, n6 // tn),
        in_specs=[pl.BlockSpec((8, d), lambda l, j: (0, 0)),
                  pl.BlockSpec((None, d, tn), lambda l, j: (l, 0, j)),
                  pl.BlockSpec((None, 1, tn), lambda l, j: (l, 0, j))],
        out_specs=pl.BlockSpec((None, 8, tn), lambda l, j: (l, 0, j)),
        out_shape=jax.ShapeDtypeStruct((n_layers, 8, n6), F32),
        compiler_params=_cp(("parallel", "parallel")),
        name="modvec",
    )(cvec, mod_w, mod_b.reshape(n_layers, 1, n6))


class _Rows:
    def __init__(self, batch, seq, ctx_len, tm):
        assert seq % tm == 0 and (batch * ctx_len) % tm == 0
        self.tm = tm
        self.batch = batch
        self.per_batch = seq // tm
        self.n_lat = batch * seq // tm
        self.n_ctx = batch * ctx_len // tm
        self.n_all = self.n_lat + self.n_ctx

    def mod_row(self, i):
        return jnp.where(i < self.n_lat, i // self.per_batch, self.batch)


def _mod_spec(rows, layer, chunk, d):
    return pl.BlockSpec((None, None, None, 1, d), lambda i, *_: (layer, rows.mod_row(i), chunk, 0, 0))


def _norm_mod_kernel(xl_ref, xc_ref, g_ref, sh_ref, sc_ref, wt_ref, o_ref, t_ref, *, n_lat):
    i = pl.program_id(0)

    def body(x_ref):
        y = _rms(x_ref[...], g_ref[...])
        h = (y * (1.0 + sc_ref[...]) + sh_ref[...]).astype(o_ref.dtype)
        o_ref[...] = h
        t_ref[...] = _dot_nt(h, wt_ref[...].astype(BF16))[:, :t_ref.shape[-1]].astype(t_ref.dtype)

    @pl.when(i < n_lat)
    def _():
        body(xl_ref)

    @pl.when(i >= n_lat)
    def _():
        body(xc_ref)


def _tail_spec(wt3, widx, col0, n_cols):
    assert col0 % LANES == 0 and n_cols <= LANES and col0 + n_cols == wt3.shape[1]
    return pl.BlockSpec((None, LANES, wt3.shape[2]), lambda i, *_: (widx, col0 // LANES, 0))


def _norm_mod(x_lat, x_ctx, ctx_block0, g, mods5, layer, rows, wt3, widx, col0, n_tail):
    d = x_lat.shape[-1]
    tm = rows.tm
    nl = rows.n_lat
    n = rows.n_all * tm
    return pl.pallas_call(
        functools.partial(_norm_mod_kernel, n_lat=nl),
        grid=(rows.n_all,),
        in_specs=[pl.BlockSpec((tm, d), lambda i: (jnp.minimum(i, nl - 1), 0)),
                  pl.BlockSpec((tm, d), lambda i: (ctx_block0 + jnp.maximum(i - nl, 0), 0)),
                  pl.BlockSpec((1, d), lambda i: (0, 0)),
                  _mod_spec(rows, layer, 0, d),
                  _mod_spec(rows, layer, 1, d),
                  _tail_spec(wt3, widx, col0, n_tail)],
        out_specs=[pl.BlockSpec((tm, d), lambda i: (i, 0)), pl.BlockSpec((tm, n_tail), lambda i: (i, 0))],
        out_shape=[jax.ShapeDtypeStruct((n, d), BF16), jax.ShapeDtypeStruct((n, n_tail), BF16)],
        compiler_params=_cp(("parallel",)),
        name="norm_mod",
    )(x_lat, x_ctx, g.reshape(1, d), mods5, mods5, wt3)


def _mm_kernel(a_ref, wt_ref, o_ref):
    o_ref[...] = _dot_nt(a_ref[...], wt_ref[...].astype(BF16)).astype(o_ref.dtype)


def _matmul(a, wt3, layer, n_cols, tn, tm):
    m, k = a.shape
    return pl.pallas_call(
        _mm_kernel,
        grid=(m // tm, n_cols // tn),
        in_specs=[pl.BlockSpec((tm, k), lambda i, j: (i, 0)),
                  pl.BlockSpec((None, tn, k), lambda i, j: (layer, j, 0))],
        out_specs=pl.BlockSpec((tm, tn), lambda i, j: (i, j)),
        out_shape=jax.ShapeDtypeStruct((m, n_cols), BF16),
        compiler_params=_cp(("parallel", "arbitrary")),
        name="in_proj",
    )(a, wt3)


def _out_proj_kernel(mal_ref, mac_ref, mb_ref, w_hbm, xl_ref, xc_ref, g_ref, o_ref, w_s, stage, *, n_lat, widx):
    i = pl.program_id(0)
    ka = mal_ref.shape[-1]

    @pl.when(i == 0)
    def _():
        rows = stage.shape[0]
        for c in range(w_s.shape[0] // rows):
            pltpu.sync_copy(w_hbm.at[widx, pl.ds(c * rows, rows)], stage)
            w_s[c * rows:(c + 1) * rows, :] = stage[...].astype(BF16)

    part_b = jnp.dot(mb_ref[...], w_s[ka:, :], preferred_element_type=F32)

    def finish(ma_ref, x_ref):
        acc = part_b + jnp.dot(ma_ref[...], w_s[:ka, :], preferred_element_type=F32)
        o_ref[...] = x_ref[...] + g_ref[...] * acc

    @pl.when(i < n_lat)
    def _():
        finish(mal_ref, xl_ref)

    @pl.when(i >= n_lat)
    def _():
        finish(mac_ref, xc_ref)


def _out_proj(mix_a_lat, mix_a_ctx, mix_b, w_out, widx, x_lat, x_ctx, ctx_block0, mods5, layer, rows, n_tiles):
    d = x_lat.shape[-1]
    ka, kb = mix_a_lat.shape[-1], mix_b.shape[-1]
    tm = rows.tm
    nl = rows.n_lat
    stage_rows = min(512, ka + kb)
    return pl.pallas_call(
        functools.partial(_out_proj_kernel, n_lat=nl, widx=widx),
        grid=(n_tiles,),
        in_specs=[pl.BlockSpec((tm, ka), lambda i: (jnp.minimum(i, nl - 1), 0)),
                  pl.BlockSpec((tm, ka), lambda i: (jnp.maximum(i - nl, 0), 0)),
                  pl.BlockSpec((tm, kb), lambda i: (i, 0)),
                  pl.BlockSpec(memory_space=pl.ANY),
                  pl.BlockSpec((tm, d), lambda i: (jnp.minimum(i, nl - 1), 0)),
                  pl.BlockSpec((tm, d), lambda i: (ctx_block0 + jnp.maximum(i - nl, 0), 0)),
                  _mod_spec(rows, layer, 2, d)],
        out_specs=pl.BlockSpec((tm, d), lambda i: (i, 0)),
        out_shape=jax.ShapeDtypeStruct((n_tiles * tm, d), F32),
        scratch_shapes=[pltpu.VMEM((ka + kb, d), BF16), pltpu.VMEM((stage_rows, d), F32)],
        compiler_params=_cp(("arbitrary",)),
        name="out_proj",
    )(mix_a_lat, mix_a_ctx, mix_b, w_out, x_lat, x_ctx, mods5)


def _tri(c, upper):
    r = lax.broadcasted_iota(jnp.int32, (c, c), 0)
    s = lax.broadcasted_iota(jnp.int32, (c, c), 1)
    return (s >= r) if upper else (r >= s)


def _scan_block(q, k, v, g, st, mask, forward):
    c = SCAN_C
    dk, dv = q.shape[-1], v.shape[-1]
    n = q.shape[0] // c
    mid, last = (c // 2 - 1, c - 1) if forward else (c // 2, 0)
    tri = jnp.broadcast_to(mask.astype(BF16)[None], (n, c, c))
    g3 = (g * LOG2E).reshape(n, c, dk)
    g_hi = g3.astype(BF16)
    g_lo = (g3 - g_hi.astype(F32)).astype(BF16)
    cum = (jnp.einsum('cts,csd->ctd', tri, g_hi, preferred_element_type=F32)
           + jnp.einsum('cts,csd->ctd', tri, g_lo, preferred_element_type=F32))
    m = cum[:, mid:mid + 1, :]
    tot = cum[:, last:last + 1, :]
    qe = (q.reshape(n, c, dk) * jnp.exp2(cum - m)).astype(BF16)
    ke = (k.reshape(n, c, dk) * jnp.exp2(m - cum)).astype(BF16)
    a = jnp.einsum('ctd,csd->cts', qe, ke, preferred_element_type=F32)
    a = jnp.where(mask[None], a, 0.0).astype(BF16)
    v3 = v.reshape(n, c, dv)
    o = jnp.einsum('cts,csv->ctv', a, v3, preferred_element_type=F32)
    u = jnp.einsum('csv,csd->cvd', v3, ke, preferred_element_type=F32)
    em = jnp.exp2(m)
    et = jnp.exp2(tot - m)
    states = [None] * n
    for ci in (range(n) if forward else reversed(range(n))):
        stp = st * em[ci]
        states[ci] = stp.astype(BF16)
        st = (stp + u[ci]) * et[ci]
    o = o + jnp.einsum('ctd,cvd->ctv', qe, jnp.stack(states), preferred_element_type=F32)
    return o.reshape(n * c, dv), st


def _scan_segments(segments, prep_f, prep_b, of_ref, ob_ref, dk, dv):
    low, up = _tri(SCAN_C, False), _tri(SCAN_C, True)
    carry = (jnp.zeros((dv, dk), F32), jnp.zeros((dv, dk), F32))
    for rows, off, seg in segments:
        rb_ = min(SCAN_BLOCK, rows)
        n = rows // rb_

        def body(i, carry, n=n, off=off, seg=seg, rb_=rb_):
            sf, sb = carry
            rf = pl.multiple_of(i * rb_, rb_)
            rb = pl.multiple_of((n - 1 - i) * rb_, rb_)
            q, k, v, g = prep_f(seg, rf, rb_)
            shared = (q, k, v) if n == 1 else None
            o, sf = _scan_block(q, k, v, g, sf, low, True)
            of_ref[pl.ds(off + rf, rb_), :] = o
            q, k, v, g = prep_b(seg, rb, rb_, shared)
            o, sb = _scan_block(q, k, v, g, sb, up, False)
            ob_ref[pl.ds(off + rb, rb_), :] = o
            return sf, sb

        carry = lax.fori_loop(0, n, body, carry)


def _hgrn_kernel(ql, qc, f1l, f1c, f2l, f2c, vl, vc, gl, gc, lbf, lbb, ng, ol_ref, oc_ref, of_s, ob_s, *,
                 ctx_len, seq):
    refs = {0: (qc, f1c, f2c, vc), 1: (ql, f1l, f2l, vl)}
    scale = A_DK ** -0.5

    for hh in range(SCAN_HEADS):
        ks = slice(hh * A_DK, (hh + 1) * A_DK)
        vs = slice(hh * A_DV, (hh + 1) * A_DV)

        def prep(seg, r, nr, fi, lb_ref, shared=None, ks=ks, vs=vs):
            if shared is None:
                x = refs[seg][0][pl.ds(r, nr), ks].astype(F32)
                q = x * jax.nn.sigmoid(x) * scale
                v = refs[seg][3][pl.ds(r, nr), vs]
            else:
                q, _, v = shared
            lb = lb_ref[:, ks]
            f = lb + (1.0 - lb) * jax.nn.sigmoid(refs[seg][fi][pl.ds(r, nr), ks].astype(F32))
            return q, 1.0 - f, v, jnp.log(f)

        _scan_segments(
            [(ctx_len, 0, 0), (seq, ctx_len, 1)],
            lambda seg, r, nr, prep=prep: prep(seg, r, nr, 1, lbf),
            lambda seg, r, nr, shared=None, prep=prep: prep(seg, r, nr, 2, lbb, shared),
            of_s.at[hh], ob_s.at[hh], A_DK, A_DV)

    def emit(gate_ref, out_ref, n_rows, row0):
        for rt in range(n_rows // TQ):
            rows = slice(rt * TQ, (rt + 1) * TQ)
            srows = slice(row0 + rt * TQ, row0 + (rt + 1) * TQ)
            ys = [_rms(of_s[hh, srows, :] + ob_s[hh, srows, :], ng[...]) for hh in range(SCAN_HEADS)]
            gate = gate_ref[rows, :].astype(F32)
            out_ref[rows, :] = (jnp.concatenate(ys, axis=-1) * jax.nn.sigmoid(gate)).astype(out_ref.dtype)

    emit(gc, oc_ref, ctx_len, 0)
    emit(gl, ol_ref, seq, ctx_len)


def _out_row_block(batch, seq, ctx_len):
    nct = ctx_len // TQ
    nlt = seq // TQ

    def f(b, rt):
        return jnp.where(rt < nct, batch * nlt + b * nct + rt, b * nlt + rt - nct)

    return f


def _hgrn(p, lb_f, lb_b, norm_g, batch, seq, ctx_len):
    h, dk, dv = A_HEADS, A_DK, A_DV
    hp = SCAN_HEADS
    hg = h // hp
    ctx_blk0 = batch * seq // ctx_len
    in_specs = []
    for kcol in range(5):
        in_specs.append(pl.BlockSpec((seq, hp * dk), lambda b, hh, kcol=kcol: (b, kcol * hg + hh)))
        in_specs.append(pl.BlockSpec((ctx_len, hp * dk), lambda b, hh, kcol=kcol: (ctx_blk0 + b, kcol * hg + hh)))
    vec = pl.BlockSpec((1, hp * dk), lambda b, hh: (0, hh))
    in_specs += [vec, vec, pl.BlockSpec((1, dv), lambda b, hh: (0, 0))]
    return pl.pallas_call(
        functools.partial(_hgrn_kernel, ctx_len=ctx_len, seq=seq),
        grid=(batch, hg),
        in_specs=in_specs,
        out_specs=[pl.BlockSpec((seq, hp * dv), lambda b, hh: (b, hh)),
                   pl.BlockSpec((ctx_len, hp * dv), lambda b, hh: (b, hh))],
        out_shape=[jax.ShapeDtypeStruct((batch * seq, h * dv), BF16),
                   jax.ShapeDtypeStruct((batch * ctx_len, h * dv), BF16)],
        scratch_shapes=[pltpu.VMEM((hp, seq + ctx_len, dv), F32), pltpu.VMEM((hp, seq + ctx_len, dv), F32)],
        compiler_params=_cp(("parallel", "parallel")),
        name="hgrn_scan",
    )(*([p] * 10), lb_f.reshape(1, h * dk), lb_b.reshape(1, h * dk), norm_g.reshape(1, dv))


def _gla_kernel(ql, qc, kl, kc, vl, vc, gl, al, ac, wa, ba, ng, o_ref, of_s, ob_s, *, ctx_len, seq):
    refs = {0: (qc, kc, vc, ac), 1: (ql, kl, vl, al)}
    scale = D_DK ** -0.5
    r16 = D_GATE_RANK

    for hh in range(GLA_HEADS):
        ks = slice(hh * D_DK, (hh + 1) * D_DK)
        vs = slice(hh * D_DV, (hh + 1) * D_DV)

        def prep(seg, r, nr, d, shared=None, ks=ks, vs=vs):
            if shared is None:
                q = refs[seg][0][pl.ds(r, nr), ks].astype(F32) * scale
                k = refs[seg][1][pl.ds(r, nr), ks].astype(F32)
                v = refs[seg][2][pl.ds(r, nr), vs]
            else:
                q, k, v = shared
            a = refs[seg][3][pl.ds(r, nr), :].astype(F32)[:, d * r16:(d + 1) * r16]
            z = jnp.dot(a, wa[d, :, ks], preferred_element_type=F32, precision=HI) + ba[d, :, ks]
            g = (jnp.minimum(z, 0.0) - jnp.log(1.0 + jnp.exp(-jnp.abs(z)))) * (1.0 / GLA_TAU)
            return q, k, v, g

        _scan_segments(
            [(ctx_len, 0, 0), (seq, ctx_len, 1)],
            lambda seg, r, nr, prep=prep: prep(seg, r, nr, 0),
            lambda seg, r, nr, shared=None, prep=prep: prep(seg, r, nr, 1, shared),
            of_s.at[hh], ob_s.at[hh], D_DK, D_DV)

    for rt in range(seq // TQ):
        rows = slice(rt * TQ, (rt + 1) * TQ)
        srows = slice(ctx_len + rt * TQ, ctx_len + (rt + 1) * TQ)
        ys = [_rms(of_s[hh, srows, :] + ob_s[hh, srows, :], ng[...]) for hh in range(GLA_HEADS)]
        gate = gl[rows, :].astype(F32)
        o_ref[rows, :] = (jnp.concatenate(ys, axis=-1) * gate * jax.nn.sigmoid(gate)).astype(o_ref.dtype)


def _gla(p, ga, w_a2, b_a, norm_g, batch, seq, ctx_len):
    h, dk, dv = D_HEADS, D_DK, D_DV
    hp = GLA_HEADS
    hg = h // hp
    nlt = seq // TQ
    ctx_blk0 = batch * seq // ctx_len
    wk, wv = hp * dk, hp * dv
    q0 = (C_HEADS + 2 * C_KV_HEADS) * C_DH // wk
    k0 = q0 + hg
    v0 = (k0 + hg) * wk // wv
    g0 = v0 + hg

    def pair(width, blk0):
        return [pl.BlockSpec((seq, width), lambda b, hh: (b, blk0 + hh)),
                pl.BlockSpec((ctx_len, width), lambda b, hh: (ctx_blk0 + b, blk0 + hh))]

    in_specs = pair(wk, q0) + pair(wk, k0) + pair(wv, v0)
    in_specs += [pl.BlockSpec((seq, wv), lambda b, hh: (b, g0 + hh)),
                 pl.BlockSpec((seq, 2 * D_GATE_RANK), lambda b, hh: (b, 0)),
                 pl.BlockSpec((ctx_len, 2 * D_GATE_RANK), lambda b, hh: (ctx_blk0 + b, 0)),
                 pl.BlockSpec((2, D_GATE_RANK, wk), lambda b, hh: (0, 0, hh)),
                 pl.BlockSpec((2, 1, wk), lambda b, hh: (0, 0, hh)),
                 pl.BlockSpec((1, dv), lambda b, hh: (0, 0))]
    return pl.pallas_call(
        functools.partial(_gla_kernel, ctx_len=ctx_len, seq=seq),
        grid=(batch, hg),
        in_specs=in_specs,
        out_specs=pl.BlockSpec((seq, wv), lambda b, hh: (b, hh)),
        out_shape=jax.ShapeDtypeStruct((batch * seq, h * dv), BF16),
        scratch_shapes=[pltpu.VMEM((hp, seq + ctx_len, dv), F32), pltpu.VMEM((hp, seq + ctx_len, dv), F32)],
        compiler_params=_cp(("parallel", "parallel")),
        name="gla_scan",
    )(p, p, p, p, p, p, p, ga, ga, w_a2, b_a.reshape(2, 1, h * dk), norm_g.reshape(1, dv))


def _softmax_pv(s, v):
    m = jnp.max(s, axis=-1, keepdims=True)
    p = jnp.exp2(s - m)
    l = jnp.sum(p, axis=-1, keepdims=True)
    return jnp.dot(p.astype(BF16), v, preferred_element_type=F32) / l


def _mla_kernel(ql_ref, kvl_ref, kvc_ref, krl_ref, krc_ref, wqn_ref, wqr_ref, wkv_ref, gq_ref, gkv_ref,
                cosq_ref, sinq_ref, cosk_ref, sink_ref, o_ref, k_s, v_s, *, ctx_len):
    qt = pl.program_id(1)
    n_ctx_tiles = ctx_len // TQ
    scale = (B_NOPE + B_ROPE) ** -0.5 * LOG2E
    dkv = B_NOPE + B_DV

    @pl.when(qt == 0)
    def _prep():
        kvc = _rms(kvc_ref[...].astype(F32), gkv_ref[...]).astype(BF16)
        kvl = _rms(kvl_ref[...].astype(F32), gkv_ref[...]).astype(BF16)
        kr_c = krc_ref[...]
        kr_l = _rope(krl_ref[...].astype(F32), cosk_ref[...], sink_ref[...]).astype(BF16)
        for h in range(B_HEADS):
            w = wkv_ref[:, h * dkv:(h + 1) * dkv].astype(BF16)
            up_c = jnp.dot(kvc, w, preferred_element_type=F32)
            k_s[h, 0:ctx_len, :] = jnp.concatenate([up_c[:, :B_NOPE].astype(BF16), kr_c], axis=-1)
            v_s[h, 0:ctx_len, :] = up_c[:, B_NOPE:].astype(BF16)
            up_l = jnp.dot(kvl, w, preferred_element_type=F32)
            k_s[h, ctx_len:, :] = jnp.concatenate([up_l[:, :B_NOPE].astype(BF16), kr_l], axis=-1)
            v_s[h, ctx_len:, :] = up_l[:, B_NOPE:].astype(BF16)

    xn = _rms(ql_ref[...].astype(F32), gq_ref[...]).astype(BF16)
    qn_all = jnp.dot(xn, wqn_ref[...].astype(BF16), preferred_element_type=F32) * scale
    qr_all = jnp.dot(xn, wqr_ref[...].astype(BF16), preferred_element_type=F32) * scale

    def heads(n_keys, rotate):
        outs = []
        for h in range(B_HEADS):
            qn = qn_all[:, h * B_NOPE:(h + 1) * B_NOPE]
            qr = qr_all[:, h * B_ROPE:(h + 1) * B_ROPE]
            if rotate:
                qr = _rope(qr, cosq_ref[...], sinq_ref[...])
            q = jnp.concatenate([qn, qr], axis=-1).astype(BF16)
            s = _dot_nt(q, k_s[h, 0:n_keys, :])
            outs.append(_softmax_pv(s, v_s[h, 0:n_keys, :]).astype(o_ref.dtype))
        o_ref[...] = jnp.concatenate(outs, axis=-1)

    @pl.when(qt < n_ctx_tiles)
    def _():
        heads(ctx_len, False)

    @pl.when(qt >= n_ctx_tiles)
    def _():
        heads(k_s.shape[1], True)


def _mla(p, kr, w_uq, w_ukv, gq, gkv, cos, sin, batch, seq, ctx_len):
    h = B_HEADS
    nct, nlt = ctx_len // TQ, seq // TQ
    ctx_blk0 = batch * seq // ctx_len
    row_block = _out_row_block(batch, seq, ctx_len)
    ql_blk = 5 * A_HEADS * A_DK // B_Q_LORA
    kv_blk = (5 * A_HEADS * A_DK + B_Q_LORA) // B_KV_LORA
    dq = B_NOPE + B_ROPE
    s_all = seq + ctx_len
    w3 = w_uq.reshape(B_Q_LORA, h, dq)
    wq_n = w3[:, :, :B_NOPE].reshape(B_Q_LORA, h * B_NOPE)
    wq_r = w3[:, :, B_NOPE:].reshape(B_Q_LORA, h * B_ROPE)
    in_specs = [
        pl.BlockSpec((TQ, B_Q_LORA), lambda b, qt: (row_block(b, qt), ql_blk)),
        pl.BlockSpec((seq, B_KV_LORA), lambda b, qt: (b, kv_blk)),
        pl.BlockSpec((ctx_len, B_KV_LORA), lambda b, qt: (ctx_blk0 + b, kv_blk)),
        pl.BlockSpec((seq, B_ROPE), lambda b, qt: (b, 0)),
        pl.BlockSpec((ctx_len, B_ROPE), lambda b, qt: (ctx_blk0 + b, 0)),
        pl.BlockSpec((B_Q_LORA, h * B_NOPE), lambda b, qt: (0, 0)),
        pl.BlockSpec((B_Q_LORA, h * B_ROPE), lambda b, qt: (0, 0)),
        pl.BlockSpec((B_KV_LORA, h * (B_NOPE + B_DV)), lambda b, qt: (0, 0)),
        pl.BlockSpec((1, B_Q_LORA), lambda b, qt: (0, 0)),
        pl.BlockSpec((1, B_KV_LORA), lambda b, qt: (0, 0)),
        pl.BlockSpec((TQ, B_ROPE), lambda b, qt: (jnp.maximum(qt - nct, 0), 0)),
        pl.BlockSpec((TQ, B_ROPE), lambda b, qt: (jnp.maximum(qt - nct, 0), 0)),
        pl.BlockSpec((seq, B_ROPE), lambda b, qt: (0, 0)),
        pl.BlockSpec((seq, B_ROPE), lambda b, qt: (0, 0)),
    ]
    return pl.pallas_call(
        functools.partial(_mla_kernel, ctx_len=ctx_len),
        grid=(batch, nct + nlt),
        in_specs=in_specs,
        out_specs=pl.BlockSpec((TQ, h * B_DV), lambda b, qt: (row_block(b, qt), 0)),
        out_shape=jax.ShapeDtypeStruct((batch * s_all, h * B_DV), BF16),
        scratch_shapes=[pltpu.VMEM((h, s_all, dq), BF16), pltpu.VMEM((h, s_all, B_DV), BF16)],
        compiler_params=_cp(("parallel", "arbitrary")),
        name="mla_attn",
    )(p, p, p, kr, kr, wq_n, wq_r, w_ukv, gq.reshape(1, -1), gkv.reshape(1, -1), cos, sin, cos, sin)


def _gqa_kernel(q_ref, kl_ref, kc_ref, vl_ref, vc_ref, gq_ref, gk_ref, cosq_ref, sinq_ref, cosk_ref, sink_ref,
                o_ref, k_s, v_s, *, ctx_len):
    qt = pl.program_id(1)
    scale = C_DH ** -0.5 * LOG2E
    dh = C_DH
    grp = C_HEADS // C_KV_HEADS

    @pl.when(qt == 0)
    def _prep():
        for kh in range(C_KV_HEADS):
            cols = slice(kh * dh, (kh + 1) * dh)
            k_s[kh, 0:ctx_len, :] = _rms(kc_ref[:, cols].astype(F32), gk_ref[...]).astype(BF16)
            kl = _rms(kl_ref[:, cols].astype(F32), gk_ref[...])
            k_s[kh, ctx_len:, :] = _rope(kl, cosk_ref[...], sink_ref[...]).astype(BF16)
            v_s[kh, 0:ctx_len, :] = vc_ref[:, cols]
            v_s[kh, ctx_len:, :] = vl_ref[:, cols]

    outs = []
    for hq in range(C_HEADS):
        kh = hq // grp
        q = _rms(q_ref[:, hq * dh:(hq + 1) * dh].astype(F32), gq_ref[...])
        q = _rope(q, cosq_ref[...], sinq_ref[...]) * scale
        s = _dot_nt(q.astype(BF16), k_s[kh])
        outs.append(_softmax_pv(s, v_s[kh]).astype(o_ref.dtype))
    o_ref[...] = jnp.concatenate(outs, axis=-1)


def _gqa(p, gq, gk, cos, sin, batch, seq, ctx_len):
    kvh, dh = C_KV_HEADS, C_DH
    nlt = seq // TQ
    ctx_blk0 = batch * seq // ctx_len
    wq, wkv = C_HEADS * dh, kvh * dh
    k0 = wq // wkv
    v0 = k0 + 1
    s_all = seq + ctx_len
    in_specs = [
        pl.BlockSpec((TQ, wq), lambda b, qt: (b * nlt + qt, 0)),
        pl.BlockSpec((seq, wkv), lambda b, qt: (b, k0)),
        pl.BlockSpec((ctx_len, wkv), lambda b, qt: (ctx_blk0 + b, k0)),
        pl.BlockSpec((seq, wkv), lambda b, qt: (b, v0)),
        pl.BlockSpec((ctx_len, wkv), lambda b, qt: (ctx_blk0 + b, v0)),
        pl.BlockSpec((1, dh), lambda b, qt: (0, 0)),
        pl.BlockSpec((1, dh), lambda b, qt: (0, 0)),
        pl.BlockSpec((TQ, dh), lambda b, qt: (qt, 0)),
        pl.BlockSpec((TQ, dh), lambda b, qt: (qt, 0)),
        pl.BlockSpec((seq, dh), lambda b, qt: (0, 0)),
        pl.BlockSpec((seq, dh), lambda b, qt: (0, 0)),
    ]
    return pl.pallas_call(
        functools.partial(_gqa_kernel, ctx_len=ctx_len),
        grid=(batch, nlt),
        in_specs=in_specs,
        out_specs=pl.BlockSpec((TQ, wq), lambda b, qt: (b * nlt + qt, 0)),
        out_shape=jax.ShapeDtypeStruct((batch * seq, wq), BF16),
        scratch_shapes=[pltpu.VMEM((kvh, s_all, dh), BF16), pltpu.VMEM((kvh, s_all, dh), BF16)],
        compiler_params=_cp(("parallel", "arbitrary")),
        name="gqa_attn",
    )(p, p, p, p, p, gq.reshape(1, dh), gk.reshape(1, dh), cos, sin, cos, sin)


def _router_kernel(x_ref, g_ref, sh_ref, sc_ref, rw_ref, rb_ref, h_ref, ri_ref, rf_ref, cnt_ref, base_s):
    i = pl.program_id(0)
    tm = x_ref.shape[0]
    ne = N_EXPERTS
    per = ne // N_GROUPS

    @pl.when(i == 0)
    def _():
        base_s[...] = jnp.zeros_like(base_s)

    h = _rms(x_ref[...], g_ref[...]) * (1.0 + sc_ref[...]) + sh_ref[...]
    nch = h.shape[1] // LANES
    for j in range(nch):
        h_ref[pl.ds(j, tm, stride=nch), :] = h[:, j * LANES:(j + 1) * LANES]
    h_hi = h.astype(BF16)
    h_lo = (h - h_hi.astype(F32)).astype(BF16)
    rw = rw_ref[...]
    w_hi = rw.astype(BF16)
    w_lo = (rw - w_hi.astype(F32)).astype(BF16)
    hw = jnp.dot(h_hi, jnp.concatenate([w_hi, w_lo], axis=1), preferred_element_type=F32)
    logits = hw[:, :ne] + hw[:, ne:] + jnp.dot(h_lo, w_hi, preferred_element_type=F32)
    scores = jax.nn.sigmoid(logits)
    sel = scores + rb_ref[...]
    lane = lax.broadcasted_iota(jnp.int32, (tm, ne), 1).astype(F32)
    neg = -jnp.inf
    big = float(ne)

    def top2(vals):
        m1 = jnp.max(vals, axis=1, keepdims=True)
        i1 = jnp.min(jnp.where(vals == m1, lane, big), axis=1, keepdims=True)
        rest = jnp.where(lane == i1, neg, vals)
        m2 = jnp.max(rest, axis=1, keepdims=True)
        i2 = jnp.min(jnp.where(rest == m2, lane, big), axis=1, keepdims=True)
        return m1 + m2, i1, i2

    best, e1, e2 = None, None, None
    for grp in range(N_GROUPS):
        in_grp = jnp.logical_and(lane >= float(grp * per), lane < float((grp + 1) * per))
        gsum, i1, i2 = top2(jnp.where(in_grp, sel, neg))
        if grp == 0:
            best, e1, e2 = gsum, i1, i2
        else:
            better = gsum > best
            best = jnp.where(better, gsum, best)
            e1 = jnp.where(better, i1, e1)
            e2 = jnp.where(better, i2, e2)

    hot1 = lane == e1
    hot2 = lane == e2
    w1 = jnp.sum(jnp.where(hot1, scores, 0.0), axis=1, keepdims=True)
    w2 = jnp.sum(jnp.where(hot2, scores, 0.0), axis=1, keepdims=True)
    wsum = w1 + w2
    assign = jnp.logical_or(hot1, hot2)
    r = lax.broadcasted_iota(jnp.int32, (tm, tm), 0)
    c = lax.broadcasted_iota(jnp.int32, (tm, tm), 1)
    before = (c < r).astype(BF16)
    excl = jnp.dot(before, assign.astype(BF16), preferred_element_type=F32) + base_s[...]
    rank1 = jnp.sum(jnp.where(hot1, excl, 0.0), axis=1, keepdims=True)
    rank2 = jnp.sum(jnp.where(hot2, excl, 0.0), axis=1, keepdims=True)
    base_s[...] = base_s[...] + jnp.sum(assign.astype(F32), axis=0, keepdims=True)

    l128 = lax.broadcasted_iota(jnp.int32, (tm, 128), 1)
    ri = jnp.where(l128 == 0, e1, jnp.where(l128 == 1, e2, jnp.where(l128 == 2, rank1, jnp.where(l128 == 3, rank2, 0.0))))
    ri_ref[...] = ri.T[0:8, :].astype(jnp.int32)
    rf_ref[...] = jnp.where(l128 == 0, w1 / wsum, jnp.where(l128 == 1, w2 / wsum, 0.0))
    cnt_ref[...] = jnp.broadcast_to(base_s[...], cnt_ref.shape)


def _router(x_all, g, mods5, layer, router_w, router_b, rows, n_tiles):
    d = x_all.shape[-1]
    tm = rows.tm
    n = n_tiles * tm
    ne = N_EXPERTS
    return pl.pallas_call(
        _router_kernel,
        grid=(n_tiles,),
        in_specs=[pl.BlockSpec((tm, d), lambda i: (i, 0)),
                  pl.BlockSpec((1, d), lambda i: (0, 0)),
                  _mod_spec(rows, layer, 3, d),
                  _mod_spec(rows, layer, 4, d),
                  pl.BlockSpec((d, ne), lambda i: (0, 0)),
                  pl.BlockSpec((1, ne), lambda i: (0, 0))],
        out_specs=[pl.BlockSpec((tm * (d // LANES), LANES), lambda i: (i, 0)),
                   pl.BlockSpec((8, tm), lambda i: (0, i)),
                   pl.BlockSpec((tm, 128), lambda i: (i, 0)),
                   pl.BlockSpec((8, ne), lambda i: (0, 0))],
        out_shape=[jax.ShapeDtypeStruct((n * (d // LANES), LANES), F32),
                   jax.ShapeDtypeStruct((8, n), jnp.int32),
                   jax.ShapeDtypeStruct((n, 128), F32),
                   jax.ShapeDtypeStruct((8, ne), F32)],
        scratch_shapes=[pltpu.VMEM((1, ne), F32)],
        compiler_params=_cp(("arbitrary",)),
        name="moe_router",
    )(x_all, g.reshape(1, d), mods5, mods5, router_w, router_b.reshape(1, ne))


def _expert_kernel(te_ref, nx_ref, nu_ref, pad_ref, dest_ref, h_hbm, w1_hbm, w3_hbm, w2_hbm, y_ref, xbuf, sem,
                   w1_s, w3_s, w2_s, wf1, wf3, wf2, wsem, wslot_ref, src_ref, *, n_tok, layer):
    r = pl.program_id(0)
    n_used = nu_ref[0]
    active = r < n_used
    changed = jnp.logical_or(r == 0, te_ref[r] != te_ref[jnp.maximum(r - 1, 0)])
    slot = lax.rem(r, 2)
    nch = xbuf.shape[1] // TE

    def weight_copies(e, ws):
        return (pltpu.make_async_copy(w1_hbm.at[layer, e], wf1.at[ws], wsem.at[ws]),
                pltpu.make_async_copy(w3_hbm.at[layer, e], wf3.at[ws], wsem.at[ws]),
                pltpu.make_async_copy(w2_hbm.at[layer, e], wf2.at[ws], wsem.at[ws]))

    @pl.when(r == 0)
    def _():
        wslot_ref[0] = 0
        for cp in weight_copies(te_ref[0], 0):
            cp.start()

        def clear(i, carry):
            src_ref[i] = 0
            return carry

        for e in range(N_EXPERTS):
            lax.fori_loop(pad_ref[e], pad_ref[N_EXPERTS + e], clear, 0)

        def invert(i, carry):
            src_ref[dest_ref[i]] = i
            src_ref[dest_ref[n_tok + i]] = i
            return carry

        lax.fori_loop(0, n_tok, invert, 0, unroll=DMA_UNROLL)

    def gather(tile, dst_slot):
        base = tile * TE
        for t in range(TE):
            row0 = pl.multiple_of(src_ref[base + t] * nch, nch)
            pltpu.make_async_copy(h_hbm.at[pl.ds(row0, nch)], xbuf.at[dst_slot, pl.ds(t * nch, nch)],
                                  sem.at[dst_slot]).start(priority=0)

    @pl.when(jnp.logical_and(r == 0, active))
    def _():
        gather(0, 0)

    @pl.when(r + 1 < n_used)
    def _():
        gather(r + 1, 1 - slot)

    @pl.when(jnp.logical_and(active, changed))
    def _():
        ws = wslot_ref[0]
        for cp in weight_copies(te_ref[r], ws):
            cp.wait()

        @pl.when(nx_ref[r] != te_ref[r])
        def _():
            for cp in weight_copies(nx_ref[r], 1 - ws):
                cp.start(priority=1)

        w1_s[...] = wf1[ws].astype(BF16)
        w3_s[...] = wf3[ws].astype(BF16)
        w2_s[...] = wf2[ws].astype(BF16)
        wslot_ref[0] = 1 - ws

    @pl.when(active)
    def _():
        pltpu.make_async_copy(h_hbm.at[pl.ds(0, TE * nch)], xbuf.at[slot], sem.at[slot]).wait()
        x = jnp.concatenate([xbuf[slot, pl.ds(j, TE, stride=nch), :] for j in range(nch)], axis=1).astype(BF16)
        a = jnp.dot(x, w1_s[...], preferred_element_type=F32)
        b = jnp.dot(x, w3_s[...], preferred_element_type=F32)
        hid = (a * jax.nn.sigmoid(a) * b).astype(BF16)
        y_ref[...] = jnp.dot(hid, w2_s[...], preferred_element_type=F32)

    @pl.when(jnp.logical_not(active))
    def _():
        y_ref[...] = jnp.zeros_like(y_ref)


def _experts(tile_expert, next_expert, n_used, pad, dest, hp, w1, w3, w2, layer, p_max):
    d = w1.shape[-2]
    nch = d // LANES
    n_tok = hp.shape[0] // nch
    f = w1.shape[-1]
    any_spec = pl.BlockSpec(memory_space=pl.ANY)
    grid_spec = pltpu.PrefetchScalarGridSpec(
        num_scalar_prefetch=5,
        grid=(p_max // TE,),
        in_specs=[any_spec, any_spec, any_spec, any_spec],
        out_specs=pl.BlockSpec((TE, d), lambda r, te, nx, nu, pd, sr: (r, 0)),
        scratch_shapes=[pltpu.VMEM((2, TE * nch, LANES), F32), pltpu.SemaphoreType.DMA((2,)),
                        pltpu.VMEM((d, f), BF16), pltpu.VMEM((d, f), BF16), pltpu.VMEM((f, d), BF16),
                        pltpu.VMEM((2, d, f), F32), pltpu.VMEM((2, d, f), F32), pltpu.VMEM((2, f, d), F32),
                        pltpu.SemaphoreType.DMA((2,)), pltpu.SMEM((1,), jnp.int32),
                        pltpu.SMEM((p_max,), jnp.int32)],
    )
    return pl.pallas_call(
        functools.partial(_expert_kernel, n_tok=n_tok, layer=layer),
        grid_spec=grid_spec,
        out_shape=jax.ShapeDtypeStruct((p_max, d), F32),
        compiler_params=_cp(("arbitrary",)),
        name="moe_experts",
    )(tile_expert, next_expert, n_used, pad, dest, hp, w1, w3, w2)


def _combine_kernel(dest_ref, x_ref, rf_ref, g_ref, pg_ref, psh_ref, psc_ref, ys_hbm, *rest, tm, n_tok, n_tiles, final):
    o_ref = rest[0] if final else rest[1]
    buf, sem = rest[-2], rest[-1]
    i = pl.program_id(0)
    slot = lax.rem(i, 2)

    def gather(tile, dst_slot):
        base = tile * tm
        for t in range(tm):
            pltpu.make_async_copy(ys_hbm.at[pl.ds(dest_ref[base + t], 1)], buf.at[dst_slot, 0, pl.ds(t, 1)],
                                  sem.at[dst_slot]).start(priority=0)
            pltpu.make_async_copy(ys_hbm.at[pl.ds(dest_ref[n_tok + base + t], 1)], buf.at[dst_slot, 1, pl.ds(t, 1)],
                                  sem.at[dst_slot]).start(priority=1)

    @pl.when(i == 0)
    def _():
        gather(0, 0)

    @pl.when(i + 1 < n_tiles)
    def _():
        gather(i + 1, 1 - slot)

    pltpu.make_async_copy(ys_hbm.at[pl.ds(0, tm)], buf.at[slot, 0], sem.at[slot]).wait()
    pltpu.make_async_copy(ys_hbm.at[pl.ds(0, tm)], buf.at[slot, 1], sem.at[slot]).wait()
    w = rf_ref[...]
    y = w[:, 0:1] * buf[slot, 0] + w[:, 1:2] * buf[slot, 1]
    x2 = x_ref[...] + g_ref[...] * y
    z = _rms(x2, pg_ref[...])
    if final:
        o_ref[...] = z
    else:
        wt_ref, _, h_ref, t_ref = rest[:4]
        o_ref[...] = x2
        h = (z * (1.0 + psc_ref[...]) + psh_ref[...]).astype(h_ref.dtype)
        h_ref[...] = h
        t_ref[...] = _dot_nt(h, wt_ref[...].astype(BF16))[:, :t_ref.shape[-1]].astype(t_ref.dtype)


def _combine(dest, x_all, rf, mods5, layer, post_g, ys, rows, n_tiles, n_tok_total, final, tail=None):
    d = x_all.shape[-1]
    tm = rows.tm
    n = n_tiles * tm
    nxt = min(layer + 1, mods5.shape[0] - 1)
    row_spec = pl.BlockSpec((tm, d), lambda i, dr: (i, 0))
    in_specs = [row_spec,
                pl.BlockSpec((tm, 128), lambda i, dr: (i, 0)),
                _mod_spec(rows, layer, 5, d),
                pl.BlockSpec((1, d), lambda i, dr: (0, 0)),
                _mod_spec(rows, nxt, 0, d),
                _mod_spec(rows, nxt, 1, d),
                pl.BlockSpec(memory_space=pl.ANY)]
    args = [dest, x_all, rf, mods5, post_g.reshape(1, d), mods5, mods5, ys]
    stream = jax.ShapeDtypeStruct((n, d), F32)
    if final:
        out_specs, out_shape = row_spec, stream
    else:
        wt3, widx, col0, n_tail = tail
        in_specs.append(_tail_spec(wt3, widx, col0, n_tail))
        args.append(wt3)
        out_specs = [row_spec, row_spec, pl.BlockSpec((tm, n_tail), lambda i, dr: (i, 0))]
        out_shape = [stream, jax.ShapeDtypeStruct((n, d), BF16), jax.ShapeDtypeStruct((n, n_tail), BF16)]
    grid_spec = pltpu.PrefetchScalarGridSpec(
        num_scalar_prefetch=1,
        grid=(n_tiles,),
        in_specs=in_specs,
        out_specs=out_specs,
        scratch_shapes=[pltpu.VMEM((2, 2, tm, d), F32), pltpu.SemaphoreType.DMA((2,))],
    )
    return pl.pallas_call(
        functools.partial(_combine_kernel, tm=tm, n_tok=n_tok_total, n_tiles=n_tiles, final=final),
        grid_spec=grid_spec,
        out_shape=out_shape,
        compiler_params=_cp(("arbitrary",)),
        name="moe_combine",
    )(*args)


def _moe(x_all, n_tok, norm_g, mods5, layer, router_w, router_b, w1, w3, w2, final_g, final, batch, seq, ctx_len,
         tail=None):
    rows_r = _Rows(batch, seq, ctx_len, TROUTE)
    rows_c = _Rows(batch, seq, ctx_len, TCOMB)
    h, ri, rf, cnt = _router(x_all, norm_g, mods5, layer, router_w, router_b, rows_r, n_tok // TROUTE)
    counts = cnt[0].astype(jnp.int32)
    padded = ((counts + TE - 1) // TE) * TE
    ends = jnp.cumsum(padded)
    starts = ends - padded
    e1, e2, r1, r2 = ri[0], ri[1], ri[2], ri[3]
    dest = jnp.concatenate([starts[e1] + r1, starts[e2] + r2]).astype(jnp.int32)
    p_max = 2 * n_tok + N_EXPERTS * TE
    n_tiles = p_max // TE
    n_used = (ends[-1] // TE).astype(jnp.int32)
    tile_start = jnp.arange(n_tiles, dtype=jnp.int32) * TE
    tile_expert = jnp.sum((tile_start[:, None] >= ends[None, :]).astype(jnp.int32), axis=1)
    last_expert = jnp.sum((jnp.maximum(ends[-1] - 1, 0) >= ends).astype(jnp.int32))
    tile_expert = jnp.minimum(jnp.where(tile_start < ends[-1], tile_expert, last_expert), N_EXPERTS - 1).astype(jnp.int32)
    eid = jnp.arange(N_EXPERTS, dtype=jnp.int32)
    later_used = jnp.logical_and(eid[None, :] > eid[:, None], (padded > 0)[None, :])
    next_used = jnp.min(jnp.where(later_used, eid[None, :], N_EXPERTS), axis=1)
    next_used = jnp.where(next_used == N_EXPERTS, eid, next_used)
    next_expert = jnp.sum(jnp.where(tile_expert[:, None] == eid[None, :], next_used[None, :], 0), axis=1).astype(jnp.int32)
    pad = jnp.concatenate([starts + counts, ends]).astype(jnp.int32)
    ys = _experts(tile_expert, next_expert, n_used.reshape(1), pad, dest, h, w1, w3, w2, layer, p_max)
    return _combine(dest, x_all, rf, mods5, layer, final_g, ys, rows_c, n_tok // TCOMB, n_tok, final, tail)


def _rope_tables(t_len, d_rope):
    rows = t_len // GRID_W
    quarter = d_rope // 4
    freqs = ROPE_THETA ** (-jnp.arange(quarter, dtype=F32) / quarter)
    row = jnp.repeat(jnp.arange(rows, dtype=F32), GRID_W)
    col = jnp.tile(jnp.arange(GRID_W, dtype=F32), rows)
    ang = jnp.concatenate([row[:, None] * freqs, col[:, None] * freqs], axis=-1)
    cos, sin = jnp.cos(ang), jnp.sin(ang)
    return jnp.concatenate([cos, cos], axis=-1), jnp.concatenate([-sin, sin], axis=-1)


def kernel(x, c, ctx, c_ctx, mod_w, mod_b, norm_attn_g, norm_ffn_g, final_norm_g, ab_w_in, ab_w_out, hgrn_lb_logits, hgrn_norm_g, mla_q_norm_g, mla_w_uq, mla_kv_norm_g, mla_w_ukv, cd_w_in, cd_w_out, gqa_q_norm_g, gqa_k_norm_g, gla_w_a2, gla_b_a, gla_norm_g, router_w, router_b, moe_w1, moe_w3, moe_w2):
    batch, seq, d = x.shape
    ctx_len = ctx.shape[1]
    n_lat, n_ctx = batch * seq, batch * ctx_len
    assert ctx_len % TQ == 0 and seq % TQ == 0 and seq % ctx_len == 0 and batch < 8
    tm = min(1024, seq, n_ctx)
    rows = _Rows(batch, seq, ctx_len, tm)

    cvec = jnp.concatenate([c, c_ctx[None, :], jnp.zeros((8 - batch - 1, d), F32)], axis=0)
    mods = _modvec(cvec, mod_w, mod_b)
    mods5 = mods.reshape(mods.shape[0], 8, 6, 1, d)

    cos_b, sin_b = _rope_tables(seq, B_ROPE)
    cos_c, sin_c = _rope_tables(seq, C_DH)
    lb = jnp.cumsum(jax.nn.softmax(hgrn_lb_logits.astype(F32), axis=1), axis=1)

    x_lat = x.reshape(n_lat, d)
    x_ctx = ctx.reshape(n_ctx, d)

    ab_main = 5 * A_HEADS * A_DK + B_Q_LORA + B_KV_LORA
    tm_mm = next(t for t in (2304, 2048, 1536, 1024, 512, 256) if (n_lat + n_ctx) % t == 0)
    ab_wt = jnp.swapaxes(ab_w_in, 1, 2)
    cd_wt = jnp.swapaxes(cd_w_in, 1, 2)
    h0, kr0 = _norm_mod(x_lat, x_ctx, 0, norm_attn_g[0], mods5, 0, rows, ab_wt, 0, ab_main, B_ROPE)
    p0 = _matmul(h0, ab_wt, 0, ab_main, 256, tm_mm)
    mix_a_lat, mix_a_ctx = _hgrn(p0, lb[0, 0], lb[1, 0], hgrn_norm_g[0], batch, seq, ctx_len)
    mix_b = _mla(p0, kr0, mla_w_uq[0], mla_w_ukv[0], mla_q_norm_g[0], mla_kv_norm_g[0], cos_b, sin_b,
                 batch, seq, ctx_len)
    rows_o = _Rows(batch, seq, ctx_len, min(512, tm))
    x1 = _out_proj(mix_a_lat, mix_a_ctx, mix_b, ab_w_out, 0, x_lat, x_ctx, 0, mods5, 0, rows_o, rows_o.n_all)
    cd_main = (C_HEADS + 2 * C_KV_HEADS) * C_DH + 2 * D_HEADS * D_DK + 2 * D_HEADS * D_DV
    x2, h1, ga1 = _moe(x1, n_lat + n_ctx, norm_ffn_g[0], mods5, 0, router_w, router_b, moe_w1, moe_w3, moe_w2,
                       norm_attn_g[1], False, batch, seq, ctx_len, tail=(cd_wt, 0, cd_main, 2 * D_GATE_RANK))

    p1 = _matmul(h1, cd_wt, 0, cd_main, 512, tm_mm)
    mix_c = _gqa(p1, gqa_q_norm_g[0], gqa_k_norm_g[0], cos_c, sin_c, batch, seq, ctx_len)
    mix_d = _gla(p1, ga1, gla_w_a2[0], gla_b_a[0], gla_norm_g[0], batch, seq, ctx_len)
    x3 = _out_proj(mix_c, mix_c, mix_d, cd_w_out, 0, x2, x2, rows_o.n_lat, mods5, 1, rows_o, rows_o.n_lat)
    out = _moe(x3, n_lat, norm_ffn_g[1], mods5, 1, router_w, router_b, moe_w1, moe_w3, moe_w2,
               final_norm_g, True, batch, seq, ctx_len)
    return out.reshape(batch, seq, d)
```
